```python
import jax, jax.numpy as jnp
from jax import lax
import numpy as np

D_MODEL = 2048
BATCH = 4
SEQ = 2048
DEPTH = 2

ATT_HEAD_DIM = 64
ATT_Q_HEADS = 16
ATT_KV_HEADS = 4
ATT_GROUP = ATT_Q_HEADS // ATT_KV_HEADS
WINDOW = 128
ATT_BLOCK = 128
RET_HEADS = 8
RET_HEAD_DIM = 128
RET_V_DIM = 128
RET_CHUNK = 128
ROPE_THETA = 10000.0
N_GROUPS = 4
EXPERTS_PER_GROUP = 4
N_EXPERTS = N_GROUPS * EXPERTS_PER_GROUP
TOP_K = 2
D_FF = 1024
EPS = 1e-6

ATT_Q_W = ATT_Q_HEADS * ATT_HEAD_DIM
ATT_KV_W = ATT_KV_HEADS * ATT_HEAD_DIM
RET_QK_W = RET_HEADS * RET_HEAD_DIM
RET_V_W = RET_HEADS * RET_V_DIM
IN_SPLITS = (ATT_Q_W, ATT_KV_W, ATT_KV_W, RET_QK_W, RET_QK_W, RET_V_W, RET_V_W, D_MODEL, D_MODEL)
IN_W = ATT_Q_W + 2 * ATT_KV_W + 2 * RET_QK_W + 2 * RET_V_W + 2 * D_MODEL

kernel_name = "hybrid_swa_retention_grouped_moe_adaln"


def rms_norm(x, g):
    xf = x.astype(jnp.float32)
    y = xf * lax.rsqrt(jnp.mean(xf * xf, axis=-1, keepdims=True) + EPS)
    return (y * g.astype(jnp.float32)).astype(x.dtype)


def modulate(h, shift, scale):
    return h * (1.0 + scale[:, None, :]) + shift[:, None, :]


def rope(x, pos):
    d = x.shape[-1]
    half = d // 2
    inv_freq = ROPE_THETA ** (-2.0 * jnp.arange(half, dtype=jnp.float32) / d)
    ang = pos.astype(jnp.float32)[..., None] * inv_freq
    cos = jnp.cos(ang)[:, :, None, :]
    sin = jnp.sin(ang)[:, :, None, :]
    xf = x.astype(jnp.float32)
    x1, x2 = xf[..., :half], xf[..., half:]
    out = jnp.concatenate([x1 * cos - x2 * sin, x2 * cos + x1 * sin], axis=-1)
    return out.astype(x.dtype)


def sliding_window_attention(q, k, v, sinks):
    B, S = q.shape[0], q.shape[1]
    W = ATT_BLOCK
    N = S // W
    qb = q.reshape(B, N, W, ATT_KV_HEADS, ATT_GROUP, ATT_HEAD_DIM)
    kb = k.reshape(B, N, W, ATT_KV_HEADS, ATT_HEAD_DIM)
    vb = v.reshape(B, N, W, ATT_KV_HEADS, ATT_HEAD_DIM)
    pad = ((0, 0), (1, 0), (0, 0), (0, 0), (0, 0))
    kcat = jnp.concatenate([jnp.pad(kb, pad)[:, :-1], kb], axis=2)
    vcat = jnp.concatenate([jnp.pad(vb, pad)[:, :-1], vb], axis=2)
    s = jnp.einsum('bnqhgd,bnkhd->bhgnqk', qb, kcat).astype(jnp.float32) * (ATT_HEAD_DIM ** -0.5)
    qi = jnp.arange(W)[:, None]
    kj = jnp.arange(2 * W)[None, :]
    rel = qi + W - kj
    band = (rel >= 0) & (rel < WINDOW)
    valid = (jnp.arange(N)[:, None, None] > 0) | (kj >= W)[None]
    mask = band[None] & valid
    s = jnp.where(mask, s, -jnp.inf)
    sink = jnp.broadcast_to(
        sinks.astype(jnp.float32).reshape(ATT_KV_HEADS, ATT_GROUP)[None, :, :, None, None, None],
        s.shape[:-1] + (1,))
    p = jax.nn.softmax(jnp.concatenate([s, sink], axis=-1), axis=-1)[..., :-1]
    o = jnp.einsum('bhgnqk,bnkhd->bnqhgd', p.astype(v.dtype), vcat)
    return o.reshape(B, S, ATT_Q_W)


def chunk_retention(q, k, v):
    B, S = q.shape[0], q.shape[1]
    C = RET_CHUNK
    N = S // C
    log_g = jnp.log1p(-jnp.exp2(-5.0 - jnp.arange(RET_HEADS, dtype=jnp.float32)))
    qc = q.reshape(B, N, C, RET_HEADS, RET_HEAD_DIM)
    kc = k.reshape(B, N, C, RET_HEADS, RET_HEAD_DIM)
    vc = v.reshape(B, N, C, RET_HEADS, RET_V_DIM)
    i = jnp.arange(C, dtype=jnp.float32)
    diff = i[:, None] - i[None, :]
    intra_decay = jnp.where(diff[None] >= 0,
                            jnp.exp(jnp.maximum(diff, 0.0)[None] * log_g[:, None, None]), 0.0)
    scores = jnp.einsum('bnihd,bnjhd->bnhij', qc, kc) * intra_decay
    intra = jnp.einsum('bnhij,bnjhe->bnihe', scores, vc)
    k_decay = jnp.exp((C - 1.0 - i)[None, :] * log_g[:, None])
    kv = jnp.einsum('bnjhd,bnjhe,hj->nbhde', kc, vc, k_decay)
    chunk_decay = jnp.exp(C * log_g)[None, :, None, None]

    def step(state, kv_n):
        return chunk_decay * state + kv_n, state

    init = jnp.zeros((B, RET_HEADS, RET_HEAD_DIM, RET_V_DIM), jnp.float32)
    _, prev = lax.scan(step, init, kv)
    q_decay = jnp.exp((i + 1.0)[None, :] * log_g[:, None])
    cross = jnp.einsum('bnihd,hi,nbhde->bnihe', qc, q_decay, prev)
    return (intra + cross).reshape(B, S, RET_HEADS, RET_V_DIM)


def grouped_moe(h, w_router, b_router, w_gate, w_up, w_down):
    B, S, D = h.shape
    t = h.reshape(-1, D)
    logits = (t @ w_router).astype(jnp.float32) + b_router.astype(jnp.float32)
    probs = jax.nn.softmax(logits, axis=-1)
    pg = probs.reshape(-1, N_GROUPS, EXPERTS_PER_GROUP)
    group_score = lax.top_k(pg, TOP_K)[0].sum(-1)
    gsel = jnp.argmax(group_score, axis=-1)
    p_in = jnp.take_along_axis(pg, gsel[:, None, None], axis=1)[:, 0]
    vals, idx = lax.top_k(p_in, TOP_K)
    wts = vals / jnp.sum(vals, axis=-1, keepdims=True)
    eid = gsel[:, None] * EXPERTS_PER_GROUP + idx
    comb = jnp.sum(jax.nn.one_hot(eid, N_EXPERTS, dtype=jnp.float32) * wts[..., None], axis=1)
    out = jnp.zeros(t.shape, jnp.float32)
    for e in range(N_EXPERTS):
        y = (jax.nn.silu(t @ w_gate[e]) * (t @ w_up[e])) @ w_down[e]
        out = out + comb[:, e:e + 1] * y.astype(jnp.float32)
    return out.astype(h.dtype).reshape(B, S, D)


def setup_inputs(seed: int = 0) -> dict:
    key = jax.random.key(seed)
    ks = jax.random.split(key, 20)
    f32 = jnp.float32
    D = D_MODEL

    def nrm(k, shape, scale):
        return jax.random.normal(k, shape, f32) * scale

    x = nrm(ks[0], (BATCH, SEQ, D), 1.0)
    c = nrm(ks[1], (BATCH, D), 1.0)
    positions = (jnp.arange(SEQ, dtype=jnp.int32)[None, :]
                 + jax.random.randint(ks[2], (BATCH, 1), 0, 1024, dtype=jnp.int32))
    return {
        "x": x,
        "c": c,
        "positions": positions,
        "w_ada": nrm(ks[3], (DEPTH, D, 6 * D), 0.5 * D ** -0.5),
        "b_ada": nrm(ks[4], (DEPTH, 6 * D), 0.01),
        "norm1_g": 1.0 + nrm(ks[5], (DEPTH, D), 0.05),
        "norm2_g": 1.0 + nrm(ks[6], (DEPTH, D), 0.05),
        "w_in": nrm(ks[7], (DEPTH, D, IN_W), D ** -0.5),
        "attn_sinks": nrm(ks[8], (DEPTH, ATT_Q_HEADS), 1.0),
        "w_attn_out": nrm(ks[9], (DEPTH, ATT_Q_W, D), ATT_Q_W ** -0.5),
        "ret_norm_g": 1.0 + nrm(ks[10], (DEPTH, RET_V_W), 0.05),
        "w_ret_out": nrm(ks[11], (DEPTH, RET_V_W, D), RET_V_W ** -0.5),
        "w_o": nrm(ks[12], (DEPTH, D, D), D ** -0.5),
        "w_router": nrm(ks[13], (D, N_EXPERTS), D ** -0.5),
        "b_router": nrm(ks[14], (N_EXPERTS,), 0.01),
        "w_gate": nrm(ks[15], (DEPTH, N_EXPERTS, D, D_FF), D ** -0.5),
        "w_up": nrm(ks[16], (DEPTH, N_EXPERTS, D, D_FF), D ** -0.5),
        "w_down": nrm(ks[17], (DEPTH, N_EXPERTS, D_FF, D), D_FF ** -0.5),
        "final_g": 1.0 + nrm(ks[18], (D,), 0.05),
    }


def reference(x, c, positions, w_ada, b_ada, norm1_g, norm2_g, w_in, attn_sinks,
              w_attn_out, ret_norm_g, w_ret_out, w_o, w_router, b_router,
              w_gate, w_up, w_down, final_g):
    B, S, D = x.shape
    offsets = [int(o) for o in np.cumsum(IN_SPLITS)[:-1]]
    c_act = jax.nn.silu(c)
    for l in range(DEPTH):
        mod = c_act @ w_ada[l] + b_ada[l]
        shift1, scale1, gate1, shift2, scale2, gate2 = jnp.split(mod, 6, axis=-1)

        h = modulate(rms_norm(x, norm1_g[l]), shift1, scale1)
        proj = h @ w_in[l]
        qa, ka, va, qr, kr, vr, gr, ga, gb = jnp.split(proj, offsets, axis=-1)

        qa = rope(qa.reshape(B, S, ATT_Q_HEADS, ATT_HEAD_DIM), positions)
        ka = rope(ka.reshape(B, S, ATT_KV_HEADS, ATT_HEAD_DIM), positions)
        va = va.reshape(B, S, ATT_KV_HEADS, ATT_HEAD_DIM)
        ya = sliding_window_attention(qa, ka, va, attn_sinks[l]) @ w_attn_out[l]

        qr = rope(qr.reshape(B, S, RET_HEADS, RET_HEAD_DIM), positions).astype(jnp.float32)
        kr = rope(kr.reshape(B, S, RET_HEADS, RET_HEAD_DIM), positions).astype(jnp.float32) * (RET_HEAD_DIM ** -0.5)
        vr = vr.reshape(B, S, RET_HEADS, RET_V_DIM).astype(jnp.float32)
        yr = chunk_retention(qr, kr, vr)
        yr = yr * lax.rsqrt(jnp.mean(yr * yr, axis=-1, keepdims=True) + EPS)
        yr = yr * ret_norm_g[l].astype(jnp.float32).reshape(RET_HEADS, RET_V_DIM)
        yr = (jax.nn.silu(gr.astype(jnp.float32)) * yr.reshape(B, S, RET_V_W)).astype(x.dtype)
        yr = yr @ w_ret_out[l]

        merged = jax.nn.sigmoid(ga) * ya + jax.nn.sigmoid(gb) * yr
        x = x + gate1[:, None, :] * (merged @ w_o[l])

        h2 = modulate(rms_norm(x, norm2_g[l]), shift2, scale2)
        x = x + gate2[:, None, :] * grouped_moe(h2, w_router, b_router, w_gate[l], w_up[l], w_down[l])
    return rms_norm(x, final_g)
```

```python
import functools

import numpy as np
import jax
import jax.numpy as jnp
from jax import lax
from jax.experimental import pallas as pl
from jax.experimental.pallas import tpu as pltpu

F32 = jnp.float32
BF16 = jnp.bfloat16
U32 = jnp.uint32
I32 = jnp.int32

D_MODEL = 2048
ATT_HEAD_DIM = 64
ATT_Q_HEADS = 16
ATT_KV_HEADS = 4
WINDOW = 128
RET_HEADS = 8
RET_HEAD_DIM = 128
RET_CHUNK = 128
ROPE_THETA = 10000.0
N_GROUPS = 4
EXPERTS_PER_GROUP = 4
N_EXPERTS = 16
D_FF = 1024
EPS = 1e-6

ATT_Q_W = ATT_Q_HEADS * ATT_HEAD_DIM
ATT_KV_W = ATT_KV_HEADS * ATT_HEAD_DIM
RET_W = RET_HEADS * RET_HEAD_DIM
OFF_QA = 0
OFF_KA = OFF_QA + ATT_Q_W
OFF_VA = OFF_KA + ATT_KV_W
OFF_QR = OFF_VA + ATT_KV_W
OFF_KR = OFF_QR + RET_W
OFF_VR = OFF_KR + RET_W
OFF_GR = OFF_VR + RET_W
OFF_GA = OFF_GR + RET_W
OFF_GB = OFF_GA + D_MODEL
IN_W = OFF_GB + D_MODEL

LANES = 128
PROJ_TN = 512
ROW_CHUNK = 256
MOE_SUB = 256
MOE_ROWS = 2048
MOE_FC = 256
VMEM_LIMIT = 56 * 1024 * 1024
HI_MASK = np.uint32(0xFFFF0000)


def _pick(n, cands):
    for c in cands:
        if n % c == 0:
            return c
    raise ValueError(f"no tile in {cands} divides {n}")


def _sigmoid(x):
    return 1.0 / (1.0 + jnp.exp(-x))


def _pack_bf16_pair(lo, hi):
    lo_b = lax.bitcast_convert_type(lo.astype(BF16).astype(F32), U32)
    hi_b = lax.bitcast_convert_type(hi.astype(BF16).astype(F32), U32)
    return (lo_b >> 16) | (hi_b & HI_MASK)


def _unpack_bf16_pair(u):
    lo = lax.bitcast_convert_type(u << 16, F32)
    hi = lax.bitcast_convert_type(u & HI_MASK, F32)
    return lo, hi


def _cparams(sem, vmem=VMEM_LIMIT):
    return pltpu.CompilerParams(dimension_semantics=sem, vmem_limit_bytes=vmem)


def _rope_consts():
    lane = np.arange(LANES)
    inv_att = ROPE_THETA ** (-2.0 * (lane % 32).astype(np.float64) / ATT_HEAD_DIM)
    inv_ret = ROPE_THETA ** (-2.0 * (lane % 64).astype(np.float64) / RET_HEAD_DIM)
    sgn_att = np.where((lane % 64) < 32, -1.0, 1.0)
    sgn_ret = np.where(lane < 64, -1.0, 1.0)
    inv = np.stack([np.tile(inv_att, (8, 1)), np.tile(inv_ret, (8, 1))]).astype(np.float32)
    sgn = np.stack([np.tile(sgn_att, (8, 1)), np.tile(sgn_ret, (8, 1))]).astype(np.float32)
    return jnp.asarray(inv), jnp.asarray(sgn)


def _rope_kernel(pos_ref, inv_ref, sgn_ref, tab_ref):
    pos = pos_ref[...]
    for kind in range(2):
        ang = pos * inv_ref[kind, 0:1, :]
        tab_ref[kind, :, 0:LANES] = jnp.cos(ang)
        tab_ref[kind, :, LANES:2 * LANES] = jnp.sin(ang) * sgn_ref[kind, 0:1, :]


def rope_tables(positions):
    T = positions.size
    posb = jnp.broadcast_to(positions.reshape(T, 1).astype(F32), (T, LANES))
    inv, sgn = _rope_consts()
    tm = _pick(T, (1024, 512, 256, 128))
    return pl.pallas_call(
        _rope_kernel,
        out_shape=jax.ShapeDtypeStruct((2, T, 2 * LANES), F32),
        grid=(T // tm,),
        in_specs=[pl.BlockSpec((tm, LANES), lambda i: (i, 0)),
                  pl.BlockSpec((2, 8, LANES), lambda i: (0, 0, 0)),
                  pl.BlockSpec((2, 8, LANES), lambda i: (0, 0, 0))],
        out_specs=pl.BlockSpec((2, tm, 2 * LANES), lambda i: (0, i, 0)),
        compiler_params=_cparams(("arbitrary",)),
        name="rope_tables",
    )(posb, inv, sgn)


def _adaln_kernel(c_ref, w_ref, b_ref, o_ref):
    c = c_ref[...]
    ca = c * _sigmoid(c)
    hi = ca.astype(BF16).astype(F32)
    lhs = jnp.concatenate([hi, ca - hi], axis=0).astype(BF16)
    r = jnp.dot(lhs, w_ref[...].astype(BF16), preferred_element_type=F32)
    o_ref[...] = r[0:8] + r[8:16] + b_ref[...]


def adaln_mod(c, w_ada, b_ada):
    L, D, N = w_ada.shape
    B = c.shape[0]
    cp = jnp.pad(c, ((0, 8 - B), (0, 0)))
    tn = _pick(N, (1024, 512))
    return pl.pallas_call(
        _adaln_kernel,
        out_shape=jax.ShapeDtypeStruct((L, 8, N), F32),
        grid=(L, N // tn),
        in_specs=[pl.BlockSpec((8, D), lambda l, j: (0, 0)),
                  pl.BlockSpec((None, D, tn), lambda l, j: (l, 0, j)),
                  pl.BlockSpec((None, 1, tn), lambda l, j: (l, 0, j))],
        out_specs=pl.BlockSpec((None, 8, tn), lambda l, j: (l, 0, j)),
        compiler_params=_cparams(("arbitrary", "arbitrary")),
        name="adaln_mod",
    )(cp, w_ada, b_ada.reshape(L, 1, N))


def _norm_mod(x, g, shift, scale):
    ms = jnp.mean(x * x, axis=-1, keepdims=True)
    return (x * lax.rsqrt(ms + EPS) * g) * (1.0 + scale) + shift


def _norm_mod_kernel(x_ref, g_ref, sh_ref, sc_ref, h_ref):
    h_ref[...] = _norm_mod(x_ref[...], g_ref[...], sh_ref[...], sc_ref[...]).astype(BF16)


def _mod_spec(l, k, S, tm):
    return pl.BlockSpec((None, None, None, 1, D_MODEL), lambda i, *_: (l, k, (i * tm) // S, 0, 0))


def norm_modulate(x2, g, modr, l, S):
    T, D = x2.shape
    tm = _pick(S, (512, 256, 128))
    return pl.pallas_call(
        _norm_mod_kernel,
        out_shape=jax.ShapeDtypeStruct((T, D), BF16),
        grid=(T // tm,),
        in_specs=[pl.BlockSpec((tm, D), lambda i: (i, 0)),
                  pl.BlockSpec((None, 1, D), lambda i: (l, 0, 0)),
                  _mod_spec(l, 0, S, tm), _mod_spec(l, 1, S, tm)],
        out_specs=pl.BlockSpec((tm, D), lambda i: (i, 0)),
        compiler_params=_cparams(("arbitrary",)),
        name="norm1_modulate",
    )(x2, g, modr, modr)


def _rope_att(blk, cos, sin, scale):
    lane = lax.broadcasted_iota(I32, blk.shape, 1)
    rot = jnp.where((lane % 64) < 32, pltpu.roll(blk, 96, 1), pltpu.roll(blk, 32, 1))
    out = blk * cos + rot * sin
    return out * scale if scale != 1.0 else out


def _rope_ret(blk, cos, sin, scale):
    out = blk * cos + pltpu.roll(blk, 64, 1) * sin
    return out * scale if scale != 1.0 else out


def _inproj_kernel(h_ref, w_ref, tab_ref, o_ref, wbf_ref, *, tm):
    j = pl.program_id(0)

    @pl.when(pl.program_id(1) == 0)
    def _():
        wbf_ref[...] = w_ref[...].astype(BF16)

    ngrp = PROJ_TN // LANES

    def run(epilogue):
        def body(r, carry):
            r0 = pl.multiple_of(r * ROW_CHUNK, ROW_CHUNK)
            acc = jnp.dot(h_ref[pl.ds(r0, ROW_CHUNK), :], wbf_ref[...], preferred_element_type=F32)
            cos = tab_ref[pl.ds(r0, ROW_CHUNK), 0:LANES]
            sin = tab_ref[pl.ds(r0, ROW_CHUNK), LANES:2 * LANES]
            for g in range(ngrp):
                blk = acc[:, g * LANES:(g + 1) * LANES]
                o_ref[pl.ds(r0, ROW_CHUNK), g * LANES:(g + 1) * LANES] = epilogue(g, blk, cos, sin).astype(BF16)
            return carry
        lax.fori_loop(0, tm // ROW_CHUNK, body, 0)

    t = lambda off: off // PROJ_TN
    q_scale = float(ATT_HEAD_DIM) ** -0.5
    k_scale = float(RET_HEAD_DIM) ** -0.5

    @pl.when(j < t(OFF_KA))
    def _():
        run(lambda g, b, c, s: _rope_att(b, c, s, q_scale))

    @pl.when(j == t(OFF_KA))
    def _():
        run(lambda g, b, c, s: _rope_att(b, c, s, 1.0) if g < ATT_KV_W // LANES else b)

    @pl.when((j >= t(OFF_QR)) & (j < t(OFF_KR)))
    def _():
        run(lambda g, b, c, s: _rope_ret(b, c, s, 1.0))

    @pl.when((j >= t(OFF_KR)) & (j < t(OFF_VR)))
    def _():
        run(lambda g, b, c, s: _rope_ret(b, c, s, k_scale))

    @pl.when((j >= t(OFF_VR)) & (j < t(OFF_GR)))
    def _():
        run(lambda g, b, c, s: b)

    @pl.when((j >= t(OFF_GR)) & (j < t(OFF_GA)))
    def _():
        run(lambda g, b, c, s: b * _sigmoid(b))

    @pl.when(j >= t(OFF_GA))
    def _():
        run(lambda g, b, c, s: _sigmoid(b))


def in_projection(h, w_in, l, tab):
    T, D = h.shape
    tm = _pick(T, (1024, 768, 512, 256))
    nj = IN_W // PROJ_TN
    t_qr, t_vr = OFF_QR // PROJ_TN, OFF_VR // PROJ_TN

    def tab_map(j, i):
        kind = jnp.where(j < t_qr, 0, 1)
        return (kind, jnp.where(j < t_vr, i, 0), 0)

    return pl.pallas_call(
        functools.partial(_inproj_kernel, tm=tm),
        out_shape=jax.ShapeDtypeStruct((T, IN_W), BF16),
        grid=(nj, T // tm),
        in_specs=[pl.BlockSpec((tm, D), lambda j, i: (i, 0)),
                  pl.BlockSpec((None, D, PROJ_TN), lambda j, i: (l, 0, j)),
                  pl.BlockSpec((None, tm, 2 * LANES), tab_map)],
        out_specs=pl.BlockSpec((tm, PROJ_TN), lambda j, i: (i, j)),
        scratch_shapes=[pltpu.VMEM((D, PROJ_TN), BF16)],
        compiler_params=_cparams(("arbitrary", "arbitrary")),
        name="in_projection",
    )(h, w_in, tab)


def _attn_kernel(sinks_ref, q_ref, kvc_ref, kvp_ref, o_ref, *, tq):
    W = WINDOW
    nsub = tq // W
    is_first = pl.program_id(1) == 0
    qi = lax.broadcasted_iota(I32, (W, 2 * W), 0)
    kj = lax.broadcasted_iota(I32, (W, 2 * W), 1)
    rel = qi + W - kj
    band = (rel >= 0) & (rel < WINDOW)
    band0 = band & (jnp.logical_not(is_first) | (kj >= W))
    lane = lax.broadcasted_iota(I32, (tq + W, LANES), 1)
    lo = lane < ATT_HEAD_DIM

    def split_pair(raw, parity):
        x = raw.astype(F32)
        xr = pltpu.roll(x, ATT_HEAD_DIM, 1)
        if parity == 0:
            even, odd = jnp.where(lo, x, 0.0), jnp.where(lo, 0.0, xr)
        else:
            even, odd = jnp.where(lo, xr, 0.0), jnp.where(lo, 0.0, x)
        return even.astype(BF16), odd.astype(BF16)

    for hk in range(ATT_KV_HEADS):
        grp, par = hk // 2, hk % 2
        kc = slice(grp * LANES, (grp + 1) * LANES)
        vc = slice(ATT_KV_W + grp * LANES, ATT_KV_W + (grp + 1) * LANES)
        k_pair = split_pair(jnp.concatenate([kvp_ref[:, kc], kvc_ref[:, kc]], axis=0), par)
        v_pair = split_pair(jnp.concatenate([kvp_ref[:, vc], kvc_ref[:, vc]], axis=0), par)
        for m in range(nsub):
            rows = slice(m * W, (m + 1) * W)
            keys = slice(m * W, m * W + 2 * W)
            mask = band0 if m == 0 else band
            for jq in range(2):
                cols = slice(hk * 4 * ATT_HEAD_DIM + jq * LANES, hk * 4 * ATT_HEAD_DIM + (jq + 1) * LANES)
                q = q_ref[rows, cols]
                o = jnp.zeros((W, LANES), F32)
                for e in range(2):
                    s = lax.dot_general(q, k_pair[e][keys], (((1,), (1,)), ((), ())),
                                        preferred_element_type=F32)
                    s = jnp.where(mask, s, -1e30)
                    sink = sinks_ref[hk * 4 + jq * 2 + e]
                    mx = jnp.maximum(jnp.max(s, axis=1, keepdims=True), sink)
                    p = jnp.exp(s - mx)
                    den = jnp.sum(p, axis=1, keepdims=True) + jnp.exp(sink - mx)
                    p = (p * (1.0 / den)).astype(BF16)
                    o = o + jnp.dot(p, v_pair[e][keys], preferred_element_type=F32)
                o_ref[rows, cols] = o.astype(BF16)


def swa_attention(proj, sinks, B, S):
    T = proj.shape[0]
    tq = _pick(S, (512, 384, 256, 128))
    nq = S // tq
    kvw = 2 * ATT_KV_W
    kv_blk = OFF_KA // kvw

    def prev_map(b, i):
        return (jnp.maximum(b * (S // WINDOW) + i * (tq // WINDOW) - 1, 0), kv_blk)

    return pl.pallas_call(
        functools.partial(_attn_kernel, tq=tq),
        out_shape=jax.ShapeDtypeStruct((T, ATT_Q_W), BF16),
        grid=(B, nq),
        in_specs=[pl.BlockSpec(memory_space=pltpu.SMEM),
                  pl.BlockSpec((tq, ATT_Q_W), lambda b, i: (b * nq + i, 0)),
                  pl.BlockSpec((tq, kvw), lambda b, i: (b * nq + i, kv_blk)),
                  pl.BlockSpec((WINDOW, kvw), prev_map)],
        out_specs=pl.BlockSpec((tq, ATT_Q_W), lambda b, i: (b * nq + i, 0)),
        compiler_params=_cparams(("arbitrary", "arbitrary")),
        name="swa_attention",
    )(sinks, proj, proj, proj)


def _ret_consts():
    C = RET_CHUNK
    log_g = np.log1p(-np.exp2(-5.0 - np.arange(RET_HEADS, dtype=np.float64)))
    i = np.arange(C, dtype=np.float64)
    diff = i[:, None] - i[None, :]
    dm = np.where(diff[None] >= 0, np.exp(np.maximum(diff, 0.0)[None] * log_g[:, None, None]), 0.0)
    qd = np.exp((i + 1.0)[None, :] * log_g[:, None])
    kd = np.exp((C - 1.0 - i)[None, :] * log_g[:, None])
    cd = np.exp(C * log_g)
    bc = lambda v: np.broadcast_to(v[:, :, None], (RET_HEADS, C, LANES))
    cdb = np.broadcast_to(cd[:, None, None], (RET_HEADS, 8, LANES))
    f = lambda a: jnp.asarray(np.ascontiguousarray(a).astype(np.float32))
    return f(dm), f(bc(qd)), f(bc(kd)), f(cdb)


def _ret_kernel(q_ref, k_ref, v_ref, g_ref, gn_ref, dm_ref, qd_ref, kd_ref, cd_ref, o_ref, *, nchunk):
    C = RET_CHUNK
    dm, qd, kd = dm_ref[...], qd_ref[...], kd_ref[...]
    cd = cd_ref[0:1, :]
    gn = gn_ref[...]
    state = jnp.zeros((RET_HEAD_DIM, RET_HEAD_DIM), F32)
    for c in range(nchunk):
        rows = slice(c * C, (c + 1) * C)
        q, k, v = q_ref[rows, :], k_ref[rows, :], v_ref[rows, :]
        s = lax.dot_general(q, k, (((1,), (1,)), ((), ())), preferred_element_type=F32) * dm
        y = jnp.dot(s.astype(BF16), v, preferred_element_type=F32)
        y = y + jnp.dot(q, state.astype(BF16), preferred_element_type=F32) * qd
        vk = (v.astype(F32) * kd).astype(BF16)
        kv = lax.dot_general(k, vk, (((0,), (0,)), ((), ())), preferred_element_type=F32)
        state = cd * state + kv
        ms = jnp.mean(y * y, axis=-1, keepdims=True)
        yn = y * lax.rsqrt(ms + EPS) * gn
        o_ref[rows, :] = (g_ref[rows, :].astype(F32) * yn).astype(BF16)


def retention(proj, ret_norm_g, l, B, S):
    T = proj.shape[0]
    dm, qd, kd, cd = _ret_consts()
    blk = lambda off: (lambda b, h: (b, off // LANES + h))
    hconst = lambda rows: pl.BlockSpec((None, rows, LANES), lambda b, h: (h, 0, 0))
    return pl.pallas_call(
        functools.partial(_ret_kernel, nchunk=S // RET_CHUNK),
        out_shape=jax.ShapeDtypeStruct((T, RET_W), BF16),
        grid=(B, RET_HEADS),
        in_specs=[pl.BlockSpec((S, LANES), blk(OFF_QR)),
                  pl.BlockSpec((S, LANES), blk(OFF_KR)),
                  pl.BlockSpec((S, LANES), blk(OFF_VR)),
                  pl.BlockSpec((S, LANES), blk(OFF_GR)),
                  pl.BlockSpec((None, 1, LANES), lambda b, h: (l, 0, h)),
                  hconst(RET_CHUNK), hconst(RET_CHUNK), hconst(RET_CHUNK), hconst(8)],
        out_specs=pl.BlockSpec((S, LANES), lambda b, h: (b, h)),
        compiler_params=_cparams(("arbitrary", "arbitrary")),
        name="retention",
    )(proj, proj, proj, proj, ret_norm_g.reshape(-1, 1, RET_W), dm, qd, kd, cd)


def _load_cast(w_hbm, dst, stage, sem):
    rows = stage.shape[0]
    n = w_hbm.shape[0] // rows

    def body(i, carry):
        r0 = pl.multiple_of(i * rows, rows)
        cp = pltpu.make_async_copy(w_hbm.at[pl.ds(r0, rows), :], stage, sem)
        cp.start()
        cp.wait()
        dst[pl.ds(r0, rows), :] = stage[...].astype(BF16)
        return carry
    lax.fori_loop(0, n, body, 0)


def _mixer_out_kernel(att_ref, ret_ref, ga0, ga1, ga2, ga3, gb0, gb1, gb2, gb3, x_ref,
                      wa_hbm, wr_hbm, wo_hbm, gate1_ref, g2_ref, sh2_ref, sc2_ref,
                      wrt_ref, brt_ref, tril_ref,
                      xn_ref, h2p_ref, rti_ref, rtw_ref, cnt_ref,
                      wa, wr, wo, wrt_hi, wrt_lo, mrg, carry, stage, sem, *, tm, l):
    i = pl.program_id(0)
    D = D_MODEL

    @pl.when(i == 0)
    def _():
        _load_cast(wa_hbm.at[l], wa, stage, sem)
        _load_cast(wr_hbm.at[l], wr, stage, sem)
        _load_cast(wo_hbm.at[l], wo, stage, sem)
        w = wrt_ref[...]
        hi = w.astype(BF16)
        wrt_hi[...] = hi
        wrt_lo[...] = (w - hi.astype(F32)).astype(BF16)
        carry[...] = jnp.zeros_like(carry)

    a = att_ref[...]
    r = ret_ref[...]
    ga = (ga0, ga1, ga2, ga3)
    gb = (gb0, gb1, gb2, gb3)
    for n in range(D // PROJ_TN):
        cols = slice(n * PROJ_TN, (n + 1) * PROJ_TN)
        ya = jnp.dot(a, wa[:, cols], preferred_element_type=F32)
        yr = jnp.dot(r, wr[:, cols], preferred_element_type=F32)
        mrg[:, cols] = (ga[n][...].astype(F32) * ya + gb[n][...].astype(F32) * yr).astype(BF16)
    o = jnp.dot(mrg[...], wo[...], preferred_element_type=F32)
    xn = x_ref[...] + gate1_ref[...] * o
    xn_ref[...] = xn

    h2 = _norm_mod(xn, g2_ref[...], sh2_ref[...], sc2_ref[...])
    h2b = h2.astype(BF16)
    h2p_ref[...] = _pack_bf16_pair(h2[:, 0:D // 2], h2[:, D // 2:D])

    h2l = (h2 - h2b.astype(F32)).astype(BF16)
    logits = (jnp.dot(h2b, wrt_hi[...], preferred_element_type=F32)
              + jnp.dot(h2l, wrt_hi[...], preferred_element_type=F32)
              + jnp.dot(h2b, wrt_lo[...], preferred_element_type=F32)) + brt_ref[...]
    lane = lax.broadcasted_iota(I32, (tm, LANES), 1)
    valid = lane < N_EXPERTS
    mx = jnp.max(logits, axis=-1, keepdims=True)
    p = jnp.where(valid, jnp.exp(logits - mx), 0.0)
    pos_in_grp = lane % EXPERTS_PER_GROUP
    grp_of = lane // EXPERTS_PER_GROUP

    def member(k):
        wrapped = pos_in_grp + k >= EXPERTS_PER_GROUP
        return jnp.where(wrapped, pltpu.roll(p, EXPERTS_PER_GROUP - k, 1), pltpu.roll(p, LANES - k, 1)), wrapped

    (b1, w1), (b2, w2), (b3, w3) = member(1), member(2), member(3)
    m_ab, n_ab = jnp.maximum(p, b1), jnp.minimum(p, b1)
    m_cd, n_cd = jnp.maximum(b2, b3), jnp.minimum(b2, b3)
    gscore = jnp.maximum(m_ab, m_cd) + jnp.maximum(jnp.minimum(m_ab, m_cd), jnp.maximum(n_ab, n_cd))
    gscore = jnp.where(valid, gscore, -1.0)
    gmax = jnp.max(gscore, axis=-1, keepdims=True)
    gsel = jnp.min(jnp.where(gscore == gmax, grp_of, N_GROUPS), axis=-1, keepdims=True)
    in_sel = grp_of == gsel
    beats = lambda b, w: ((b > p) | ((b == p) & w)).astype(I32)
    rank_in_grp = beats(b1, w1) + beats(b2, w2) + beats(b3, w3)
    sel0 = in_sel & (rank_in_grp == 0)
    sel1 = in_sel & (rank_in_grp == 1)
    lsum = lambda m, v: jnp.sum(jnp.where(m, v, 0.0), axis=-1, keepdims=True)
    v0, v1 = lsum(sel0, p), lsum(sel1, p)
    lanef = lane.astype(F32)
    e0, e1 = lsum(sel0, lanef), lsum(sel1, lanef)
    inv = 1.0 / (v0 + v1)
    onehot = jnp.where(sel0 | sel1, 1.0, 0.0)
    prefix = jnp.dot(tril_ref[...], onehot.astype(BF16), preferred_element_type=F32) + carry[0:1, :]
    r0, r1 = lsum(sel0, prefix), lsum(sel1, prefix)
    carry[...] = carry[...] + jnp.sum(onehot, axis=0, keepdims=True)
    cnt_ref[...] = carry[...].astype(I32)
    sel4 = lambda a0, a1, a2, a3: jnp.where(lane == 0, a0, jnp.where(lane == 1, a1, jnp.where(lane == 2, a2, a3)))
    rti_ref[...] = sel4(e0, e1, r0, r1).astype(I32)
    rtw_ref[...] = jnp.where(lane == 0, v0 * inv, jnp.where(lane == 1, v1 * inv, 0.0))


def mixer_out(att, ret, proj, x2, w_attn_out, w_ret_out, w_o, modr, norm2_g, w_router, b_router, l, S):
    T, D = x2.shape
    tm = 256
    assert S % tm == 0
    nga, ngb = OFF_GA // PROJ_TN, OFF_GB // PROJ_TN
    gate_spec = lambda blk: pl.BlockSpec((tm, PROJ_TN), lambda i: (i, blk))
    row = lambda: pl.BlockSpec((tm, D), lambda i: (i, 0))
    wrt = jnp.pad(w_router, ((0, 0), (0, LANES - N_EXPERTS)))
    brt = jnp.pad(b_router.astype(F32), (0, LANES - N_EXPERTS), constant_values=-1e30).reshape(1, LANES)
    tril = jnp.asarray(np.tril(np.ones((tm, tm), np.float32), -1), BF16)
    any_spec = pl.BlockSpec(memory_space=pl.ANY)
    outs = pl.pallas_call(
        functools.partial(_mixer_out_kernel, tm=tm, l=l),
        out_shape=(jax.ShapeDtypeStruct((T, D), F32),
                   jax.ShapeDtypeStruct((T, D // 2), U32),
                   jax.ShapeDtypeStruct((T, LANES), I32),
                   jax.ShapeDtypeStruct((T, LANES), F32),
                   jax.ShapeDtypeStruct((8, LANES), I32)),
        grid=(T // tm,),
        in_specs=[pl.BlockSpec((tm, ATT_Q_W), lambda i: (i, 0)),
                  pl.BlockSpec((tm, RET_W), lambda i: (i, 0)),
                  *[gate_spec(nga + n) for n in range(4)],
                  *[gate_spec(ngb + n) for n in range(4)],
                  row(), any_spec, any_spec, any_spec,
                  _mod_spec(l, 2, S, tm),
                  pl.BlockSpec((None, 1, D), lambda i: (l, 0, 0)),
                  _mod_spec(l, 3, S, tm), _mod_spec(l, 4, S, tm),
                  pl.BlockSpec((D, LANES), lambda i: (0, 0)),
                  pl.BlockSpec((1, LANES), lambda i: (0, 0)),
                  pl.BlockSpec((tm, tm), lambda i: (0, 0))],
        out_specs=(row(),
                   pl.BlockSpec((tm, D // 2), lambda i: (i, 0)),
                   pl.BlockSpec((tm, LANES), lambda i: (i, 0)),
                   pl.BlockSpec((tm, LANES), lambda i: (i, 0)),
                   pl.BlockSpec((8, LANES), lambda i: (0, 0))),
        scratch_shapes=[pltpu.VMEM((ATT_Q_W, D), BF16), pltpu.VMEM((RET_W, D), BF16),
                        pltpu.VMEM((D, D), BF16),
                        pltpu.VMEM((D, LANES), BF16), pltpu.VMEM((D, LANES), BF16),
                        pltpu.VMEM((tm, D), BF16), pltpu.VMEM((8, LANES), F32),
                        pltpu.VMEM((256, D), F32), pltpu.SemaphoreType.DMA],
        compiler_params=_cparams(("arbitrary",)),
        name="mixer_out",
    )(att, ret, *([proj] * 8), x2, w_attn_out, w_ret_out, w_o,
      modr, norm2_g.reshape(-1, 1, D), modr, modr, wrt, brt, tril)
    return outs


def moe_plan(counts, n_work):
    sub_per = MOE_ROWS // MOE_SUB
    seg = ((counts + MOE_SUB - 1) // MOE_SUB) * MOE_SUB
    off = jnp.cumsum(seg) - seg
    off17 = jnp.concatenate([off, off[-1:] + seg[-1:]]).astype(I32)
    nb = (counts + MOE_ROWS - 1) // MOE_ROWS
    cum = jnp.cumsum(nb)
    total = cum[-1]
    w = jnp.arange(n_work, dtype=I32)
    wc = jnp.minimum(w, total - 1)
    e_w = jnp.searchsorted(cum, wc, side="right").astype(I32)
    blk = wc - (cum[e_w] - nb[e_w])
    start = off[e_w] + blk * MOE_ROWS
    nsub = jnp.clip(seg[e_w] // MOE_SUB - blk * sub_per, 0, sub_per)
    nsub = jnp.where(w < total, nsub, 0)
    return off17, e_w, start.astype(I32), nsub.astype(I32)


def _dispatch_kernel(e_ref, r_ref, off_ref, h_ref, xs_ref, zero_ref, sem, *, td, T):
    i = pl.program_id(0)

    @pl.when(i == 0)
    def _():
        zero_ref[...] = jnp.zeros_like(zero_ref)
        used = off_ref[N_EXPERTS]

        def zero_copy(row):
            row = pl.multiple_of(row, MOE_SUB)
            return pltpu.make_async_copy(zero_ref, xs_ref.at[pl.ds(row, MOE_SUB), :], sem)

        def fill(op):
            def seg_tail(e, carry):
                @pl.when(off_ref[e + 1] > off_ref[e])
                def _():
                    getattr(zero_copy(off_ref[e + 1] - MOE_SUB), op)()
                return carry

            def buf_tail(n, carry):
                getattr(zero_copy(used + n * MOE_SUB), op)()
                return carry
            lax.fori_loop(0, N_EXPERTS, seg_tail, 0)
            lax.fori_loop(0, (xs_ref.shape[0] - used) // MOE_SUB, buf_tail, 0)
        fill("start")
        fill("wait")

    def copy(r, k):
        t = k * T + i * td + r
        p = off_ref[e_ref[t]] + r_ref[t]
        return pltpu.make_async_copy(h_ref.at[pl.ds(r, 1), :], xs_ref.at[pl.ds(p, 1), :], sem)

    def start(r, carry):
        copy(r, 0).start()
        copy(r, 1).start()
        return carry

    def wait(r, carry):
        copy(r, 0).wait()
        copy(r, 1).wait()
        return carry
    lax.fori_loop(0, td, start, 0)
    lax.fori_loop(0, td, wait, 0)


def dispatch(h2p, e_flat, r_flat, off17, n_rows):
    T, DW = h2p.shape
    td = 256
    return pl.pallas_call(
        functools.partial(_dispatch_kernel, td=td, T=T),
        out_shape=jax.ShapeDtypeStruct((n_rows, DW), U32),
        grid_spec=pltpu.PrefetchScalarGridSpec(
            num_scalar_prefetch=3, grid=(T // td,),
            in_specs=[pl.BlockSpec((td, DW), lambda i, *_: (i, 0))],
            out_specs=pl.BlockSpec(memory_space=pl.ANY),
            scratch_shapes=[pltpu.VMEM((MOE_SUB, DW), U32), pltpu.SemaphoreType.DMA]),
        compiler_params=_cparams(("arbitrary",)),
        name="moe_dispatch",
    )(e_flat, r_flat, off17, h2p)


def _moe_kernel(we_ref, ws_ref, wn_ref, xs_ref, wg_ref, wu_ref, wd_ref, ys_ref,
                xraw, acc, wgb, wub, wdb, sem_in, sem_out, *, nfc):
    w = pl.program_id(0)
    c = pl.program_id(1)
    n = wn_ref[w]
    s0 = ws_ref[w]
    half = D_MODEL // 2

    def sub_rows(i):
        return pl.multiple_of(i * MOE_SUB, MOE_SUB), pl.multiple_of(s0 + i * MOE_SUB, MOE_SUB)

    def for_subs(fn):
        def body(i, carry):
            fn(i)
            return carry
        lax.fori_loop(0, n, body, 0)

    def in_copy(i):
        r0, g0 = sub_rows(i)
        return pltpu.make_async_copy(xs_ref.at[pl.ds(g0, MOE_SUB), :], xraw.at[pl.ds(r0, MOE_SUB), :], sem_in)

    def out_copy(i):
        r0, g0 = sub_rows(i)
        return pltpu.make_async_copy(xraw.at[pl.ds(r0, MOE_SUB), :], ys_ref.at[pl.ds(g0, MOE_SUB), :], sem_out)

    @pl.when((c == 0) & (n > 0))
    def _():
        for_subs(lambda i: in_copy(i).start())

        def zero(i):
            r0, _ = sub_rows(i)
            acc[pl.ds(r0, MOE_SUB), :] = jnp.zeros((MOE_SUB, D_MODEL), F32)
        for_subs(zero)
        for_subs(lambda i: in_copy(i).wait())

    @pl.when(n > 0)
    def _():
        wgb[...] = wg_ref[...].astype(BF16)
        wub[...] = wu_ref[...].astype(BF16)
        wdb[...] = wd_ref[...].astype(BF16)

        def compute(i):
            r0, _ = sub_rows(i)
            lo, hi = _unpack_bf16_pair(xraw[pl.ds(r0, MOE_SUB), :])
            xlo, xhi = lo.astype(BF16), hi.astype(BF16)
            g = (jnp.dot(xlo, wgb[0:half, :], preferred_element_type=F32)
                 + jnp.dot(xhi, wgb[half:, :], preferred_element_type=F32))
            u = (jnp.dot(xlo, wub[0:half, :], preferred_element_type=F32)
                 + jnp.dot(xhi, wub[half:, :], preferred_element_type=F32))
            a = (g * _sigmoid(g) * u).astype(BF16)
            acc[pl.ds(r0, MOE_SUB), :] += jnp.dot(a, wdb[...], preferred_element_type=F32)
        for_subs(compute)

    @pl.when((c == nfc - 1) & (n > 0))
    def _():
        def pack_and_send(i):
            r0, _ = sub_rows(i)
            y = acc[pl.ds(r0, MOE_SUB), :]
            xraw[pl.ds(r0, MOE_SUB), :] = _pack_bf16_pair(y[:, 0:half], y[:, half:])
            out_copy(i).start()
        for_subs(pack_and_send)
        for_subs(lambda i: out_copy(i).wait())


def moe_experts(xs, w_gate, w_up, w_down, l, e_w, start_w, nsub_w):
    n_rows, DW = xs.shape
    D = D_MODEL
    nfc = D_FF // MOE_FC
    n_work = e_w.shape[0]

    def chunk(c, wn, w):
        return jnp.where(wn[w] > 0, c, nfc - 1)

    return pl.pallas_call(
        functools.partial(_moe_kernel, nfc=nfc),
        out_shape=jax.ShapeDtypeStruct((n_rows, DW), U32),
        grid_spec=pltpu.PrefetchScalarGridSpec(
            num_scalar_prefetch=3, grid=(n_work, nfc),
            in_specs=[pl.BlockSpec(memory_space=pl.ANY),
                      pl.BlockSpec((None, None, D, MOE_FC), lambda w, c, we, ws, wn: (l, we[w], 0, chunk(c, wn, w))),
                      pl.BlockSpec((None, None, D, MOE_FC), lambda w, c, we, ws, wn: (l, we[w], 0, chunk(c, wn, w))),
                      pl.BlockSpec((None, None, MOE_FC, D), lambda w, c, we, ws, wn: (l, we[w], chunk(c, wn, w), 0))],
            out_specs=pl.BlockSpec(memory_space=pl.ANY),
            scratch_shapes=[pltpu.VMEM((MOE_ROWS, DW), U32), pltpu.VMEM((MOE_ROWS, D), F32),
                            pltpu.VMEM((D, MOE_FC), BF16), pltpu.VMEM((D, MOE_FC), BF16),
                            pltpu.VMEM((MOE_FC, D), BF16),
                            pltpu.SemaphoreType.DMA, pltpu.SemaphoreType.DMA]),
        input_output_aliases={3: 0},
        compiler_params=_cparams(("arbitrary", "arbitrary")),
        name="moe_experts",
    )(e_w, start_w, nsub_w, xs, w_gate, w_up, w_down)


def _combine_kernel(e_ref, r_ref, off_ref, ys_ref, x_ref, rtw_ref, gate2_ref, g_ref, *rest, tc, T, final):
    if final:
        out_ref, ybuf, sem = rest
    else:
        sh_ref, sc_ref, xo_ref, h_ref, ybuf, sem = rest
    i = pl.program_id(0)
    half = D_MODEL // 2

    def copy(r, k):
        t = k * T + i * tc + r
        p = off_ref[e_ref[t]] + r_ref[t]
        return pltpu.make_async_copy(ys_ref.at[pl.ds(p, 1), :], ybuf.at[k, pl.ds(r, 1), :], sem)

    def start(r, carry):
        copy(r, 0).start()
        copy(r, 1).start()
        return carry

    def wait(r, carry):
        copy(r, 0).wait()
        copy(r, 1).wait()
        return carry
    lax.fori_loop(0, tc, start, 0)
    lax.fori_loop(0, tc, wait, 0)

    w0 = rtw_ref[:, 0:1]
    w1 = rtw_ref[:, 1:2]
    y0lo, y0hi = _unpack_bf16_pair(ybuf[0])
    y1lo, y1hi = _unpack_bf16_pair(ybuf[1])
    gate2 = gate2_ref[...]
    xlo = x_ref[:, 0:half] + gate2[:, 0:half] * (w0 * y0lo + w1 * y1lo)
    xhi = x_ref[:, half:] + gate2[:, half:] * (w0 * y0hi + w1 * y1hi)
    ms = (jnp.sum(xlo * xlo, axis=-1, keepdims=True) + jnp.sum(xhi * xhi, axis=-1, keepdims=True)) / D_MODEL
    rs = lax.rsqrt(ms + EPS)
    g = g_ref[...]
    if final:
        out_ref[:, 0:half] = xlo * rs * g[:, 0:half]
        out_ref[:, half:] = xhi * rs * g[:, half:]
    else:
        xo_ref[:, 0:half] = xlo
        xo_ref[:, half:] = xhi
        sh, sc = sh_ref[...], sc_ref[...]
        h_ref[:, 0:half] = ((xlo * rs * g[:, 0:half]) * (1.0 + sc[:, 0:half]) + sh[:, 0:half]).astype(BF16)
        h_ref[:, half:] = ((xhi * rs * g[:, half:]) * (1.0 + sc[:, half:]) + sh[:, half:]).astype(BF16)


def combine(ys, xn, rtw, e_flat, r_flat, off17, modr, l, S, g_next, final):
    T, D = xn.shape
    tc = 256
    assert S % tc == 0
    row = lambda: pl.BlockSpec((tc, D), lambda i, *_: (i, 0))
    in_specs = [pl.BlockSpec(memory_space=pl.ANY), row(),
                pl.BlockSpec((tc, LANES), lambda i, *_: (i, 0)),
                _mod_spec(l, 5, S, tc)]
    args = [ys, xn, rtw, modr]
    if final:
        in_specs.append(pl.BlockSpec((1, D), lambda i, *_: (0, 0)))
        args.append(g_next.reshape(1, D))
        out_shape = jax.ShapeDtypeStruct((T, D), F32)
        out_specs = row()
    else:
        in_specs += [pl.BlockSpec((None, 1, D), lambda i, *_: (l + 1, 0, 0)),
                     _mod_spec(l + 1, 0, S, tc), _mod_spec(l + 1, 1, S, tc)]
        args += [g_next.reshape(-1, 1, D), modr, modr]
        out_shape = (jax.ShapeDtypeStruct((T, D), F32), jax.ShapeDtypeStruct((T, D), BF16))
        out_specs = (row(), row())
    return pl.pallas_call(
        functools.partial(_combine_kernel, tc=tc, T=T, final=final),
        out_shape=out_shape,
        grid_spec=pltpu.PrefetchScalarGridSpec(
            num_scalar_prefetch=3, grid=(T // tc,),
            in_specs=in_specs, out_specs=out_specs,
            scratch_shapes=[pltpu.VMEM((2, tc, D // 2), U32), pltpu.SemaphoreType.DMA]),
        compiler_params=_cparams(("arbitrary",)),
        name="moe_combine_final" if final else "moe_combine",
    )(e_flat, r_flat, off17, *args)


def kernel(x, c, positions, w_ada, b_ada, norm1_g, norm2_g, w_in, attn_sinks, w_attn_out, ret_norm_g,
           w_ret_out, w_o, w_router, b_router, w_gate, w_up, w_down, final_g):
    B, S, D = x.shape
    L = w_ada.shape[0]
    T = B * S
    assert D == D_MODEL and w_in.shape[-1] == IN_W and S % RET_CHUNK == 0

    tab = rope_tables(positions)
    mod = adaln_mod(c, w_ada, b_ada)
    modr = mod[:, :B].reshape(L, B, 6, D).transpose(0, 2, 1, 3).reshape(L, 6, B, 1, D)

    x2 = x.reshape(T, D)
    h = norm_modulate(x2, norm1_g.reshape(L, 1, D), modr, 0, S)
    n_work = N_EXPERTS + (2 * T + MOE_ROWS - 1) // MOE_ROWS
    n_rows = 2 * T + N_EXPERTS * MOE_SUB
    out = None
    for l in range(L):
        proj = in_projection(h, w_in, l, tab)
        att = swa_attention(proj, attn_sinks[l], B, S)
        ret = retention(proj, ret_norm_g, l, B, S)
        xn, h2p, rti, rtw, cnt = mixer_out(att, ret, proj, x2, w_attn_out, w_ret_out, w_o, modr,
                                           norm2_g, w_router, b_router, l, S)
        off17, e_w, start_w, nsub_w = moe_plan(cnt[0, :N_EXPERTS], n_work)
        e_flat = rti[:, 0:2].T.reshape(2 * T)
        r_flat = rti[:, 2:4].T.reshape(2 * T)
        xs = dispatch(h2p, e_flat, r_flat, off17, n_rows)
        ys = moe_experts(xs, w_gate, w_up, w_down, l, e_w, start_w, nsub_w)
        if l + 1 < L:
            x2, h = combine(ys, xn, rtw, e_flat, r_flat, off17, modr, l, S, norm1_g, final=False)
        else:
            out = combine(ys, xn, rtw, e_flat, r_flat, off17, modr, l, S, final_g, final=True)
    return out.reshape(B, S, D)
```

```python
import functools

import numpy as np
import jax
import jax.numpy as jnp
from jax import lax
from jax.experimental import pallas as pl
from jax.experimental.pallas import tpu as pltpu

F32 = jnp.float32
BF16 = jnp.bfloat16
U32 = jnp.uint32
I32 = jnp.int32

D_MODEL = 2048
ATT_HEAD_DIM = 64
ATT_Q_HEADS = 16
ATT_KV_HEADS = 4
WINDOW = 128
RET_HEADS = 8
RET_HEAD_DIM = 128
RET_CHUNK = 128
ROPE_THETA = 10000.0
N_GROUPS = 4
EXPERTS_PER_GROUP = 4
N_EXPERTS = 16
D_FF = 1024
EPS = 1e-6

ATT_Q_W = ATT_Q_HEADS * ATT_HEAD_DIM
ATT_KV_W = ATT_KV_HEADS * ATT_HEAD_DIM
RET_W = RET_HEADS * RET_HEAD_DIM
OFF_QA = 0
OFF_KA = OFF_QA + ATT_Q_W
OFF_VA = OFF_KA + ATT_KV_W
OFF_QR = OFF_VA + ATT_KV_W
OFF_KR = OFF_QR + RET_W
OFF_VR = OFF_KR + RET_W
OFF_GR = OFF_VR + RET_W
OFF_GA = OFF_GR + RET_W
OFF_GB = OFF_GA + D_MODEL
IN_W = OFF_GB + D_MODEL

LANES = 128
PROJ_TN = 512
ROW_GROUP = 8
MOE_SUB = 256
MOE_ROWS = 2048
MOE_FC = 256
VMEM_LIMIT = 56 * 1024 * 1024
HI_MASK = np.uint32(0xFFFF0000)


def _pick(n, cands):
    for c in cands:
        if n % c == 0:
            return c
    raise ValueError(f"no tile in {cands} divides {n}")


def _sigmoid(x):
    return 1.0 / (1.0 + jnp.exp(-x))


def _pack_bf16_pair(lo, hi):
    lo_b = lax.bitcast_convert_type(lo.astype(BF16).astype(F32), U32)
    hi_b = lax.bitcast_convert_type(hi.astype(BF16).astype(F32), U32)
    return (lo_b >> 16) | (hi_b & HI_MASK)


def _unpack_bf16_pair(u):
    lo = lax.bitcast_convert_type(u << 16, F32)
    hi = lax.bitcast_convert_type(u & HI_MASK, F32)
    return lo, hi


def _cparams(sem, vmem=VMEM_LIMIT):
    return pltpu.CompilerParams(dimension_semantics=sem, vmem_limit_bytes=vmem)


def _rope_consts():
    lane = np.arange(LANES)
    inv_att = ROPE_THETA ** (-2.0 * (lane % 32).astype(np.float64) / ATT_HEAD_DIM)
    inv_ret = ROPE_THETA ** (-2.0 * (lane % 64).astype(np.float64) / RET_HEAD_DIM)
    sgn_att = np.where((lane % 64) < 32, -1.0, 1.0)
    sgn_ret = np.where(lane < 64, -1.0, 1.0)
    inv = np.stack([np.tile(inv_att, (8, 1)), np.tile(inv_ret, (8, 1))]).astype(np.float32)
    sgn = np.stack([np.tile(sgn_att, (8, 1)), np.tile(sgn_ret, (8, 1))]).astype(np.float32)
    return jnp.asarray(inv), jnp.asarray(sgn)


def _rope_kernel(pos_ref, inv_ref, sgn_ref, tab_ref):
    pos = pos_ref[...]
    for kind in range(2):
        ang = pos * inv_ref[kind, 0:1, :]
        tab_ref[kind, :, 0:LANES] = jnp.cos(ang)
        tab_ref[kind, :, LANES:2 * LANES] = jnp.sin(ang) * sgn_ref[kind, 0:1, :]


def rope_tables(positions):
    T = positions.size
    posb = jnp.broadcast_to(positions.reshape(T, 1).astype(F32), (T, LANES))
    inv, sgn = _rope_consts()
    tm = _pick(T, (1024, 512, 256, 128))
    return pl.pallas_call(
        _rope_kernel,
        out_shape=jax.ShapeDtypeStruct((2, T, 2 * LANES), F32),
        grid=(T // tm,),
        in_specs=[pl.BlockSpec((tm, LANES), lambda i: (i, 0)),
                  pl.BlockSpec((2, 8, LANES), lambda i: (0, 0, 0)),
                  pl.BlockSpec((2, 8, LANES), lambda i: (0, 0, 0))],
        out_specs=pl.BlockSpec((2, tm, 2 * LANES), lambda i: (0, i, 0)),
        compiler_params=_cparams(("arbitrary",)),
        name="rope_tables",
    )(posb, inv, sgn)


def _adaln_kernel(c_ref, w_ref, b_ref, o_ref):
    c = c_ref[...]
    ca = c * _sigmoid(c)
    hi = ca.astype(BF16).astype(F32)
    lhs = jnp.concatenate([hi, ca - hi], axis=0).astype(BF16)
    r = jnp.dot(lhs, w_ref[...].astype(BF16), preferred_element_type=F32)
    o_ref[...] = r[0:8] + r[8:16] + b_ref[...]


def adaln_mod(c, w_ada, b_ada):
    L, D, N = w_ada.shape
    B = c.shape[0]
    cp = jnp.pad(c, ((0, 8 - B), (0, 0)))
    tn = _pick(N, (1024, 512))
    return pl.pallas_call(
        _adaln_kernel,
        out_shape=jax.ShapeDtypeStruct((L, 8, N), F32),
        grid=(L, N // tn),
        in_specs=[pl.BlockSpec((8, D), lambda l, j: (0, 0)),
                  pl.BlockSpec((None, D, tn), lambda l, j: (l, 0, j)),
                  pl.BlockSpec((None, 1, tn), lambda l, j: (l, 0, j))],
        out_specs=pl.BlockSpec((None, 8, tn), lambda l, j: (l, 0, j)),
        compiler_params=_cparams(("arbitrary", "arbitrary")),
        name="adaln_mod",
    )(cp, w_ada, b_ada.reshape(L, 1, N))


def _norm_mod(x, g, shift, scale):
    ms = jnp.mean(x * x, axis=-1, keepdims=True)
    return (x * lax.rsqrt(ms + EPS) * g) * (1.0 + scale) + shift


def _norm_mod_kernel(x_ref, g_ref, sh_ref, sc_ref, h_ref):
    h_ref[...] = _norm_mod(x_ref[...], g_ref[...], sh_ref[...], sc_ref[...]).astype(BF16)


def _mod_spec(l, k, S, tm):
    return pl.BlockSpec((None, None, None, 1, D_MODEL), lambda i, *_: (l, k, (i * tm) // S, 0, 0))


def norm_modulate(x2, g, modr, l, S):
    T, D = x2.shape
    tm = _pick(S, (512, 256, 128))
    return pl.pallas_call(
        _norm_mod_kernel,
        out_shape=jax.ShapeDtypeStruct((T, D), BF16),
        grid=(T // tm,),
        in_specs=[pl.BlockSpec((tm, D), lambda i: (i, 0)),
                  pl.BlockSpec((None, 1, D), lambda i: (l, 0, 0)),
                  _mod_spec(l, 0, S, tm), _mod_spec(l, 1, S, tm)],
        out_specs=pl.BlockSpec((tm, D), lambda i: (i, 0)),
        compiler_params=_cparams(("arbitrary",)),
        name="norm1_modulate",
    )(x2, g, modr, modr)


def _rope_att(blk, cos, sin, scale):
    lane = lax.broadcasted_iota(I32, blk.shape, 1)
    rot = jnp.where((lane % 64) < 32, pltpu.roll(blk, 96, 1), pltpu.roll(blk, 32, 1))
    out = blk * cos + rot * sin
    return out * scale if scale != 1.0 else out


def _rope_ret(blk, cos, sin, scale):
    out = blk * cos + pltpu.roll(blk, 64, 1) * sin
    return out * scale if scale != 1.0 else out


def _inproj_kernel(h_ref, w_ref, tab_ref, o_ref, wbf_ref):
    j = pl.program_id(0)

    @pl.when(pl.program_id(1) == 0)
    def _():
        wbf_ref[...] = w_ref[...].astype(BF16)

    ngrp = PROJ_TN // LANES

    def run(epilogue):
        acc = jnp.dot(h_ref[...], wbf_ref[...], preferred_element_type=F32)
        cos = tab_ref[:, 0:LANES]
        sin = tab_ref[:, LANES:2 * LANES]
        for g in range(ngrp):
            blk = acc[:, g * LANES:(g + 1) * LANES]
            o_ref[:, g * LANES:(g + 1) * LANES] = epilogue(g, blk, cos, sin).astype(BF16)

    t = lambda off: off // PROJ_TN
    q_scale = float(ATT_HEAD_DIM) ** -0.5
    k_scale = float(RET_HEAD_DIM) ** -0.5

    @pl.when(j < t(OFF_KA))
    def _():
        run(lambda g, b, c, s: _rope_att(b, c, s, q_scale))

    @pl.when(j == t(OFF_KA))
    def _():
        run(lambda g, b, c, s: _rope_att(b, c, s, 1.0) if g < ATT_KV_W // LANES else b)

    @pl.when((j >= t(OFF_QR)) & (j < t(OFF_KR)))
    def _():
        run(lambda g, b, c, s: _rope_ret(b, c, s, 1.0))

    @pl.when((j >= t(OFF_KR)) & (j < t(OFF_VR)))
    def _():
        run(lambda g, b, c, s: _rope_ret(b, c, s, k_scale))

    @pl.when((j >= t(OFF_VR)) & (j < t(OFF_GR)))
    def _():
        run(lambda g, b, c, s: b)

    @pl.when((j >= t(OFF_GR)) & (j < t(OFF_GA)))
    def _():
        run(lambda g, b, c, s: b * _sigmoid(b))

    @pl.when(j >= t(OFF_GA))
    def _():
        run(lambda g, b, c, s: _sigmoid(b))


def in_projection(h, w_in, l, tab):
    T, D = h.shape
    tm = _pick(T, (1024, 768, 512, 256))
    nj = IN_W // PROJ_TN
    t_qr, t_vr = OFF_QR // PROJ_TN, OFF_VR // PROJ_TN

    def tab_map(j, i):
        kind = jnp.where(j < t_qr, 0, 1)
        return (kind, jnp.where(j < t_vr, i, 0), 0)

    return pl.pallas_call(
        _inproj_kernel,
        out_shape=jax.ShapeDtypeStruct((T, IN_W), BF16),
        grid=(nj, T // tm),
        in_specs=[pl.BlockSpec((tm, D), lambda j, i: (i, 0)),
                  pl.BlockSpec((None, D, PROJ_TN), lambda j, i: (l, 0, j)),
                  pl.BlockSpec((None, tm, 2 * LANES), tab_map)],
        out_specs=pl.BlockSpec((tm, PROJ_TN), lambda j, i: (i, j)),
        scratch_shapes=[pltpu.VMEM((D, PROJ_TN), BF16)],
        compiler_params=_cparams(("arbitrary", "arbitrary")),
        name="in_projection",
    )(h, w_in, tab)


def _attn_kernel(sinks_ref, q_ref, kvc_ref, kvp_ref, o_ref, *, tq):
    W = WINDOW
    nsub = tq // W
    is_first = pl.program_id(1) == 0
    qi = lax.broadcasted_iota(I32, (W, 2 * W), 0)
    kj = lax.broadcasted_iota(I32, (W, 2 * W), 1)
    rel = qi + W - kj
    band = (rel >= 0) & (rel < WINDOW)
    band0 = band & (jnp.logical_not(is_first) | (kj >= W))
    lane = lax.broadcasted_iota(I32, (tq + W, LANES), 1)
    lo = lane < ATT_HEAD_DIM

    def split_pair(raw, parity):
        x = raw.astype(F32)
        xr = pltpu.roll(x, ATT_HEAD_DIM, 1)
        if parity == 0:
            return jnp.where(lo, x, 0.0), jnp.where(lo, 0.0, xr)
        return jnp.where(lo, xr, 0.0), jnp.where(lo, 0.0, x)

    for hk in range(ATT_KV_HEADS):
        grp, par = hk // 2, hk % 2
        kc = slice(grp * LANES, (grp + 1) * LANES)
        vc = slice(ATT_KV_W + grp * LANES, ATT_KV_W + (grp + 1) * LANES)
        k_pair = split_pair(jnp.concatenate([kvp_ref[:, kc], kvc_ref[:, kc]], axis=0), par)
        v_pair = split_pair(jnp.concatenate([kvp_ref[:, vc], kvc_ref[:, vc]], axis=0), par)
        kt_pair = [k.T.astype(BF16) for k in k_pair]
        v_pair = [v.astype(BF16) for v in v_pair]
        qcols = [slice(hk * 4 * ATT_HEAD_DIM + jq * LANES, hk * 4 * ATT_HEAD_DIM + (jq + 1) * LANES)
                 for jq in range(2)]
        for m in range(nsub):
            rows = slice(m * W, (m + 1) * W)
            keys = slice(m * W, m * W + 2 * W)
            mask = band0 if m == 0 else band
            kcat = jnp.concatenate([kt_pair[0][:, keys], kt_pair[1][:, keys]], axis=1)
            vcat = jnp.concatenate([v_pair[0][keys], v_pair[1][keys]], axis=0)
            q = jnp.concatenate([q_ref[rows, qcols[0]], q_ref[rows, qcols[1]]], axis=0)
            s_all = jnp.dot(q, kcat, preferred_element_type=F32)
            p_rows = []
            for jq in range(2):
                p_cols = []
                for e in range(2):
                    s = s_all[jq * W:(jq + 1) * W, e * 2 * W:(e + 1) * 2 * W]
                    s = jnp.where(mask, s, -1e30)
                    sink = sinks_ref[hk * 4 + jq * 2 + e]
                    mx = jnp.maximum(jnp.max(s, axis=1, keepdims=True), sink)
                    p = jnp.exp(s - mx)
                    den = jnp.sum(p, axis=1, keepdims=True) + jnp.exp(sink - mx)
                    p_cols.append((p * (1.0 / den)).astype(BF16))
                p_rows.append(jnp.concatenate(p_cols, axis=1))
            o = jnp.dot(jnp.concatenate(p_rows, axis=0), vcat, preferred_element_type=F32)
            o_ref[rows, qcols[0]] = o[0:W].astype(BF16)
            o_ref[rows, qcols[1]] = o[W:2 * W].astype(BF16)


def swa_attention(proj, sinks, B, S):
    T = proj.shape[0]
    tq = _pick(S, (512, 384, 256, 128))
    nq = S // tq
    kvw = 2 * ATT_KV_W
    kv_blk = OFF_KA // kvw

    def prev_map(b, i):
        return (jnp.maximum(b * (S // WINDOW) + i * (tq // WINDOW) - 1, 0), kv_blk)

    return pl.pallas_call(
        functools.partial(_attn_kernel, tq=tq),
        out_shape=jax.ShapeDtypeStruct((T, ATT_Q_W), BF16),
        grid=(B, nq),
        in_specs=[pl.BlockSpec(memory_space=pltpu.SMEM),
                  pl.BlockSpec((tq, ATT_Q_W), lambda b, i: (b * nq + i, 0)),
                  pl.BlockSpec((tq, kvw), lambda b, i: (b * nq + i, kv_blk)),
                  pl.BlockSpec((WINDOW, kvw), prev_map)],
        out_specs=pl.BlockSpec((tq, ATT_Q_W), lambda b, i: (b * nq + i, 0)),
        compiler_params=_cparams(("arbitrary", "arbitrary")),
        name="swa_attention",
    )(sinks, proj, proj, proj)


def _ret_consts():
    C = RET_CHUNK
    log_g = np.log1p(-np.exp2(-5.0 - np.arange(RET_HEADS, dtype=np.float64)))
    i = np.arange(C, dtype=np.float64)
    diff = i[:, None] - i[None, :]
    dm = np.where(diff[None] >= 0, np.exp(np.maximum(diff, 0.0)[None] * log_g[:, None, None]), 0.0)
    qd = np.exp((i + 1.0)[None, :] * log_g[:, None])
    kd = np.exp((C - 1.0 - i)[None, :] * log_g[:, None])
    cd = np.exp(C * log_g)
    bc = lambda v: np.broadcast_to(v[:, :, None], (RET_HEADS, C, LANES))
    cdb = np.broadcast_to(cd[:, None, None], (RET_HEADS, 8, LANES))
    f = lambda a: jnp.asarray(np.ascontiguousarray(a).astype(np.float32))
    return f(dm), f(bc(qd)), f(bc(kd)), f(cdb)


def _ret_kernel(q_ref, k_ref, v_ref, g_ref, gn_ref, dm_ref, qd_ref, kd_ref, cd_ref, o_ref, *, nchunk):
    C = RET_CHUNK
    dm, qd, kd = dm_ref[...], qd_ref[...], kd_ref[...]
    cd = cd_ref[0:1, :]
    gn = gn_ref[...]
    state = jnp.zeros((RET_HEAD_DIM, RET_HEAD_DIM), F32)
    for c in range(nchunk):
        rows = slice(c * C, (c + 1) * C)
        q, k, v = q_ref[rows, :], k_ref[rows, :], v_ref[rows, :]
        s = lax.dot_general(q, k, (((1,), (1,)), ((), ())), preferred_element_type=F32) * dm
        y = jnp.dot(s.astype(BF16), v, preferred_element_type=F32)
        y = y + jnp.dot(q, state.astype(BF16), preferred_element_type=F32) * qd
        vk = (v.astype(F32) * kd).astype(BF16)
        kv = lax.dot_general(k, vk, (((0,), (0,)), ((), ())), preferred_element_type=F32)
        state = cd * state + kv
        ms = jnp.mean(y * y, axis=-1, keepdims=True)
        yn = y * lax.rsqrt(ms + EPS) * gn
        o_ref[rows, :] = (g_ref[rows, :].astype(F32) * yn).astype(BF16)


def retention(proj, ret_norm_g, l, B, S):
    T = proj.shape[0]
    dm, qd, kd, cd = _ret_consts()
    blk = lambda off: (lambda b, h: (b, off // LANES + h))
    hconst = lambda rows: pl.BlockSpec((None, rows, LANES), lambda b, h: (h, 0, 0))
    return pl.pallas_call(
        functools.partial(_ret_kernel, nchunk=S // RET_CHUNK),
        out_shape=jax.ShapeDtypeStruct((T, RET_W), BF16),
        grid=(B, RET_HEADS),
        in_specs=[pl.BlockSpec((S, LANES), blk(OFF_QR)),
                  pl.BlockSpec((S, LANES), blk(OFF_KR)),
                  pl.BlockSpec((S, LANES), blk(OFF_VR)),
                  pl.BlockSpec((S, LANES), blk(OFF_GR)),
                  pl.BlockSpec((None, 1, LANES), lambda b, h: (l, 0, h)),
                  hconst(RET_CHUNK), hconst(RET_CHUNK), hconst(RET_CHUNK), hconst(8)],
        out_specs=pl.BlockSpec((S, LANES), lambda b, h: (b, h)),
        compiler_params=_cparams(("arbitrary", "arbitrary")),
        name="retention",
    )(proj, proj, proj, proj, ret_norm_g.reshape(-1, 1, RET_W), dm, qd, kd, cd)


def _load_cast(w_hbm, dst, stage, sem):
    rows = stage.shape[0]
    n = w_hbm.shape[0] // rows

    def body(i, carry):
        r0 = pl.multiple_of(i * rows, rows)
        cp = pltpu.make_async_copy(w_hbm.at[pl.ds(r0, rows), :], stage, sem)
        cp.start()
        cp.wait()
        dst[pl.ds(r0, rows), :] = stage[...].astype(BF16)
        return carry
    lax.fori_loop(0, n, body, 0)


def _mixer_out_kernel(att_ref, ret_ref, ga0, ga1, ga2, ga3, gb0, gb1, gb2, gb3, x_ref,
                      wa_hbm, wr_hbm, wo_hbm, gate1_ref, g2_ref, sh2_ref, sc2_ref,
                      wrt_ref, brt_ref, tril_ref,
                      xn_ref, h2p_ref, rti_ref, rtw_ref, cnt_ref,
                      wa, wr, wo, wrt_hi, wrt_lo, mrg, carry, stage, sem, *, tm, l):
    i = pl.program_id(0)
    D = D_MODEL

    @pl.when(i == 0)
    def _():
        _load_cast(wa_hbm.at[l], wa, stage, sem)
        _load_cast(wr_hbm.at[l], wr, stage, sem)
        _load_cast(wo_hbm.at[l], wo, stage, sem)
        w = wrt_ref[...]
        hi = w.astype(BF16)
        wrt_hi[...] = hi
        wrt_lo[...] = (w - hi.astype(F32)).astype(BF16)
        carry[...] = jnp.zeros_like(carry)

    a = att_ref[...]
    r = ret_ref[...]
    ga = (ga0, ga1, ga2, ga3)
    gb = (gb0, gb1, gb2, gb3)
    for n in range(D // PROJ_TN):
        cols = slice(n * PROJ_TN, (n + 1) * PROJ_TN)
        ya = jnp.dot(a, wa[:, cols], preferred_element_type=F32)
        yr = jnp.dot(r, wr[:, cols], preferred_element_type=F32)
        mrg[:, cols] = (ga[n][...].astype(F32) * ya + gb[n][...].astype(F32) * yr).astype(BF16)
    o = jnp.dot(mrg[...], wo[...], preferred_element_type=F32)
    xn = x_ref[...] + gate1_ref[...] * o
    xn_ref[...] = xn

    h2 = _norm_mod(xn, g2_ref[...], sh2_ref[...], sc2_ref[...])
    h2b = h2.astype(BF16)
    h2p_ref[...] = _pack_bf16_pair(h2[:, 0:D // 2], h2[:, D // 2:D])

    h2l = (h2 - h2b.astype(F32)).astype(BF16)
    logits = (jnp.dot(h2b, wrt_hi[...], preferred_element_type=F32)
              + jnp.dot(h2l, wrt_hi[...], preferred_element_type=F32)
              + jnp.dot(h2b, wrt_lo[...], preferred_element_type=F32)) + brt_ref[...]
    lane = lax.broadcasted_iota(I32, (tm, LANES), 1)
    valid = lane < N_EXPERTS
    mx = jnp.max(logits, axis=-1, keepdims=True)
    p = jnp.where(valid, jnp.exp(logits - mx), 0.0)
    pos_in_grp = lane % EXPERTS_PER_GROUP
    grp_of = lane // EXPERTS_PER_GROUP

    def member(k):
        wrapped = pos_in_grp + k >= EXPERTS_PER_GROUP
        return jnp.where(wrapped, pltpu.roll(p, EXPERTS_PER_GROUP - k, 1), pltpu.roll(p, LANES - k, 1)), wrapped

    (b1, w1), (b2, w2), (b3, w3) = member(1), member(2), member(3)
    m_ab, n_ab = jnp.maximum(p, b1), jnp.minimum(p, b1)
    m_cd, n_cd = jnp.maximum(b2, b3), jnp.minimum(b2, b3)
    gscore = jnp.maximum(m_ab, m_cd) + jnp.maximum(jnp.minimum(m_ab, m_cd), jnp.maximum(n_ab, n_cd))
    gscore = jnp.where(valid, gscore, -1.0)
    gmax = jnp.max(gscore, axis=-1, keepdims=True)
    gsel = jnp.min(jnp.where(gscore == gmax, grp_of, N_GROUPS), axis=-1, keepdims=True)
    in_sel = grp_of == gsel
    beats = lambda b, w: ((b > p) | ((b == p) & w)).astype(I32)
    rank_in_grp = beats(b1, w1) + beats(b2, w2) + beats(b3, w3)
    sel0 = in_sel & (rank_in_grp == 0)
    sel1 = in_sel & (rank_in_grp == 1)
    lsum = lambda m, v: jnp.sum(jnp.where(m, v, 0.0), axis=-1, keepdims=True)
    v0, v1 = lsum(sel0, p), lsum(sel1, p)
    lanef = lane.astype(F32)
    e0, e1 = lsum(sel0, lanef), lsum(sel1, lanef)
    inv = 1.0 / (v0 + v1)
    onehot = jnp.where(sel0 | sel1, 1.0, 0.0)
    prefix = jnp.dot(tril_ref[...], onehot.astype(BF16), preferred_element_type=F32) + carry[0:1, :]
    r0, r1 = lsum(sel0, prefix), lsum(sel1, prefix)
    carry[...] = carry[...] + jnp.sum(onehot, axis=0, keepdims=True)
    cnt_ref[...] = carry[...].astype(I32)
    sel4 = lambda a0, a1, a2, a3: jnp.where(lane == 0, a0, jnp.where(lane == 1, a1, jnp.where(lane == 2, a2, a3)))
    rti_ref[...] = sel4(e0, e1, r0, r1).astype(I32)
    rtw_ref[...] = jnp.where(lane == 0, v0 * inv, jnp.where(lane == 1, v1 * inv, 0.0))


def mixer_out(att, ret, proj, x2, w_attn_out, w_ret_out, w_o, modr, norm2_g, w_router, b_router, l, S):
    T, D = x2.shape
    tm = 256
    assert S % tm == 0
    nga, ngb = OFF_GA // PROJ_TN, OFF_GB // PROJ_TN
    gate_spec = lambda blk: pl.BlockSpec((tm, PROJ_TN), lambda i: (i, blk))
    row = lambda: pl.BlockSpec((tm, D), lambda i: (i, 0))
    wrt = jnp.pad(w_router, ((0, 0), (0, LANES - N_EXPERTS)))
    brt = jnp.pad(b_router.astype(F32), (0, LANES - N_EXPERTS), constant_values=-1e30).reshape(1, LANES)
    tril = jnp.asarray(np.tril(np.ones((tm, tm), np.float32), -1), BF16)
    any_spec = pl.BlockSpec(memory_space=pl.ANY)
    outs = pl.pallas_call(
        functools.partial(_mixer_out_kernel, tm=tm, l=l),
        out_shape=(jax.ShapeDtypeStruct((T, D), F32),
                   jax.ShapeDtypeStruct((T, D // 2), U32),
                   jax.ShapeDtypeStruct((T, LANES), I32),
                   jax.ShapeDtypeStruct((T, LANES), F32),
                   jax.ShapeDtypeStruct((8, LANES), I32)),
        grid=(T // tm,),
        in_specs=[pl.BlockSpec((tm, ATT_Q_W), lambda i: (i, 0)),
                  pl.BlockSpec((tm, RET_W), lambda i: (i, 0)),
                  *[gate_spec(nga + n) for n in range(4)],
                  *[gate_spec(ngb + n) for n in range(4)],
                  row(), any_spec, any_spec, any_spec,
                  _mod_spec(l, 2, S, tm),
                  pl.BlockSpec((None, 1, D), lambda i: (l, 0, 0)),
                  _mod_spec(l, 3, S, tm), _mod_spec(l, 4, S, tm),
                  pl.BlockSpec((D, LANES), lambda i: (0, 0)),
                  pl.BlockSpec((1, LANES), lambda i: (0, 0)),
                  pl.BlockSpec((tm, tm), lambda i: (0, 0))],
        out_specs=(row(),
                   pl.BlockSpec((tm, D // 2), lambda i: (i, 0)),
                   pl.BlockSpec((tm, LANES), lambda i: (i, 0)),
                   pl.BlockSpec((tm, LANES), lambda i: (i, 0)),
                   pl.BlockSpec((8, LANES), lambda i: (0, 0))),
        scratch_shapes=[pltpu.VMEM((ATT_Q_W, D), BF16), pltpu.VMEM((RET_W, D), BF16),
                        pltpu.VMEM((D, D), BF16),
                        pltpu.VMEM((D, LANES), BF16), pltpu.VMEM((D, LANES), BF16),
                        pltpu.VMEM((tm, D), BF16), pltpu.VMEM((8, LANES), F32),
                        pltpu.VMEM((256, D), F32), pltpu.SemaphoreType.DMA],
        compiler_params=_cparams(("arbitrary",)),
        name="mixer_out",
    )(att, ret, *([proj] * 8), x2, w_attn_out, w_ret_out, w_o,
      modr, norm2_g.reshape(-1, 1, D), modr, modr, wrt, brt, tril)
    return outs


def moe_plan(counts, n_work):
    sub_per = MOE_ROWS // MOE_SUB
    seg = ((counts + MOE_SUB - 1) // MOE_SUB) * MOE_SUB
    off = jnp.cumsum(seg) - seg
    off17 = jnp.concatenate([off, off[-1:] + seg[-1:]]).astype(I32)
    nb = (counts + MOE_ROWS - 1) // MOE_ROWS
    cum = jnp.cumsum(nb)
    total = cum[-1]
    w = jnp.arange(n_work, dtype=I32)
    wc = jnp.minimum(w, total - 1)
    e_w = jnp.sum((cum[None, :] <= wc[:, None]).astype(I32), axis=1)
    blk = wc - (cum[e_w] - nb[e_w])
    start = off[e_w] + blk * MOE_ROWS
    nsub = jnp.clip(seg[e_w] // MOE_SUB - blk * sub_per, 0, sub_per)
    nsub = jnp.where(w < total, nsub, 0)
    return off17, e_w, start.astype(I32), nsub.astype(I32)


def _row_dma_loops(n_rows, make_copy):
    def run(op):
        def body(g, carry):
            r0 = pl.multiple_of(g * ROW_GROUP, ROW_GROUP)
            for u in range(ROW_GROUP):
                for k in range(2):
                    getattr(make_copy(r0 + u, k), op)()
            return carry
        lax.fori_loop(0, n_rows // ROW_GROUP, body, 0)
    run("start")
    run("wait")


def _dispatch_kernel(pos_ref, off_ref, h_ref, xs_ref, zero_ref, sem, *, td, T):
    i = pl.program_id(0)

    @pl.when(i == 0)
    def _():
        zero_ref[...] = jnp.zeros_like(zero_ref)
        used = off_ref[N_EXPERTS]

        def zero_copy(row):
            row = pl.multiple_of(row, MOE_SUB)
            return pltpu.make_async_copy(zero_ref, xs_ref.at[pl.ds(row, MOE_SUB), :], sem)

        def fill(op):
            def seg_tail(e, carry):
                @pl.when(off_ref[e + 1] > off_ref[e])
                def _():
                    getattr(zero_copy(off_ref[e + 1] - MOE_SUB), op)()
                return carry

            def buf_tail(n, carry):
                getattr(zero_copy(used + n * MOE_SUB), op)()
                return carry
            lax.fori_loop(0, N_EXPERTS, seg_tail, 0)
            lax.fori_loop(0, (xs_ref.shape[0] - used) // MOE_SUB, buf_tail, 0)
        fill("start")
        fill("wait")

    def copy(r, k):
        p = pos_ref[k * T + i * td + r]
        return pltpu.make_async_copy(h_ref.at[pl.ds(r, 1), :], xs_ref.at[pl.ds(p, 1), :], sem)
    _row_dma_loops(td, copy)


def dispatch(h2p, pos_flat, off17, n_rows):
    T, DW = h2p.shape
    td = 256
    return pl.pallas_call(
        functools.partial(_dispatch_kernel, td=td, T=T),
        out_shape=jax.ShapeDtypeStruct((n_rows, DW), U32),
        grid_spec=pltpu.PrefetchScalarGridSpec(
            num_scalar_prefetch=2, grid=(T // td,),
            in_specs=[pl.BlockSpec((td, DW), lambda i, *_: (i, 0))],
            out_specs=pl.BlockSpec(memory_space=pl.ANY),
            scratch_shapes=[pltpu.VMEM((MOE_SUB, DW), U32), pltpu.SemaphoreType.DMA]),
        compiler_params=_cparams(("arbitrary",)),
        name="moe_dispatch",
    )(pos_flat, off17, h2p)


def _moe_kernel(we_ref, ws_ref, wn_ref, xs_ref, wg_ref, wu_ref, wd_ref, ys_ref,
                xraw, acc, wgb, wub, wdb, sem_in, sem_out, *, nfc):
    w = pl.program_id(0)
    c = pl.program_id(1)
    n = wn_ref[w]
    s0 = ws_ref[w]
    half = D_MODEL // 2

    def sub_rows(i):
        return pl.multiple_of(i * MOE_SUB, MOE_SUB), pl.multiple_of(s0 + i * MOE_SUB, MOE_SUB)

    def for_subs(fn):
        def body(i, carry):
            fn(i)
            return carry
        lax.fori_loop(0, n, body, 0)

    def in_copy(i):
        r0, g0 = sub_rows(i)
        return pltpu.make_async_copy(xs_ref.at[pl.ds(g0, MOE_SUB), :], xraw.at[pl.ds(r0, MOE_SUB), :], sem_in)

    def out_copy(i):
        r0, g0 = sub_rows(i)
        return pltpu.make_async_copy(xraw.at[pl.ds(r0, MOE_SUB), :], ys_ref.at[pl.ds(g0, MOE_SUB), :], sem_out)

    @pl.when((c == 0) & (n > 0))
    def _():
        for_subs(lambda i: in_copy(i).start())

        def zero(i):
            r0, _ = sub_rows(i)
            acc[pl.ds(r0, MOE_SUB), :] = jnp.zeros((MOE_SUB, D_MODEL), F32)
        for_subs(zero)
        for_subs(lambda i: in_copy(i).wait())

    @pl.when(n > 0)
    def _():
        wgb[...] = wg_ref[...].astype(BF16)
        wub[...] = wu_ref[...].astype(BF16)
        wdb[...] = wd_ref[...].astype(BF16)

        def compute(i):
            r0, _ = sub_rows(i)
            lo, hi = _unpack_bf16_pair(xraw[pl.ds(r0, MOE_SUB), :])
            xlo, xhi = lo.astype(BF16), hi.astype(BF16)
            g = (jnp.dot(xlo, wgb[0:half, :], preferred_element_type=F32)
                 + jnp.dot(xhi, wgb[half:, :], preferred_element_type=F32))
            u = (jnp.dot(xlo, wub[0:half, :], preferred_element_type=F32)
                 + jnp.dot(xhi, wub[half:, :], preferred_element_type=F32))
            a = (g * _sigmoid(g) * u).astype(BF16)
            acc[pl.ds(r0, MOE_SUB), :] += jnp.dot(a, wdb[...], preferred_element_type=F32)
        for_subs(compute)

    @pl.when((c == nfc - 1) & (n > 0))
    def _():
        def pack_and_send(i):
            r0, _ = sub_rows(i)
            y = acc[pl.ds(r0, MOE_SUB), :]
            xraw[pl.ds(r0, MOE_SUB), :] = _pack_bf16_pair(y[:, 0:half], y[:, half:])
            out_copy(i).start()
        for_subs(pack_and_send)
        for_subs(lambda i: out_copy(i).wait())


def moe_experts(xs, w_gate, w_up, w_down, l, e_w, start_w, nsub_w):
    n_rows, DW = xs.shape
    D = D_MODEL
    nfc = D_FF // MOE_FC
    n_work = e_w.shape[0]

    def chunk(c, wn, w):
        return jnp.where(wn[w] > 0, c, nfc - 1)

    return pl.pallas_call(
        functools.partial(_moe_kernel, nfc=nfc),
        out_shape=jax.ShapeDtypeStruct((n_rows, DW), U32),
        grid_spec=pltpu.PrefetchScalarGridSpec(
            num_scalar_prefetch=3, grid=(n_work, nfc),
            in_specs=[pl.BlockSpec(memory_space=pl.ANY),
                      pl.BlockSpec((None, None, D, MOE_FC), lambda w, c, we, ws, wn: (l, we[w], 0, chunk(c, wn, w))),
                      pl.BlockSpec((None, None, D, MOE_FC), lambda w, c, we, ws, wn: (l, we[w], 0, chunk(c, wn, w))),
                      pl.BlockSpec((None, None, MOE_FC, D), lambda w, c, we, ws, wn: (l, we[w], chunk(c, wn, w), 0))],
            out_specs=pl.BlockSpec(memory_space=pl.ANY),
            scratch_shapes=[pltpu.VMEM((MOE_ROWS, DW), U32), pltpu.VMEM((MOE_ROWS, D), F32),
                            pltpu.VMEM((D, MOE_FC), BF16), pltpu.VMEM((D, MOE_FC), BF16),
                            pltpu.VMEM((MOE_FC, D), BF16),
                            pltpu.SemaphoreType.DMA, pltpu.SemaphoreType.DMA]),
        input_output_aliases={3: 0},
        compiler_params=_cparams(("arbitrary", "arbitrary")),
        name="moe_experts",
    )(e_w, start_w, nsub_w, xs, w_gate, w_up, w_down)


def _combine_kernel(pos_ref, ys_ref, x_ref, rtw_ref, gate2_ref, g_ref, *rest, tc, T, final):
    if final:
        out_ref, ybuf, sem = rest
    else:
        sh_ref, sc_ref, xo_ref, h_ref, ybuf, sem = rest
    i = pl.program_id(0)
    half = D_MODEL // 2

    def copy(r, k):
        p = pos_ref[k * T + i * tc + r]
        return pltpu.make_async_copy(ys_ref.at[pl.ds(p, 1), :], ybuf.at[k, pl.ds(r, 1), :], sem)
    _row_dma_loops(tc, copy)

    w0 = rtw_ref[:, 0:1]
    w1 = rtw_ref[:, 1:2]
    y0lo, y0hi = _unpack_bf16_pair(ybuf[0])
    y1lo, y1hi = _unpack_bf16_pair(ybuf[1])
    gate2 = gate2_ref[...]
    xlo = x_ref[:, 0:half] + gate2[:, 0:half] * (w0 * y0lo + w1 * y1lo)
    xhi = x_ref[:, half:] + gate2[:, half:] * (w0 * y0hi + w1 * y1hi)
    ms = (jnp.sum(xlo * xlo, axis=-1, keepdims=True) + jnp.sum(xhi * xhi, axis=-1, keepdims=True)) / D_MODEL
    rs = lax.rsqrt(ms + EPS)
    g = g_ref[...]
    if final:
        out_ref[:, 0:half] = xlo * rs * g[:, 0:half]
        out_ref[:, half:] = xhi * rs * g[:, half:]
    else:
        xo_ref[:, 0:half] = xlo
        xo_ref[:, half:] = xhi
        sh, sc = sh_ref[...], sc_ref[...]
        h_ref[:, 0:half] = ((xlo * rs * g[:, 0:half]) * (1.0 + sc[:, 0:half]) + sh[:, 0:half]).astype(BF16)
        h_ref[:, half:] = ((xhi * rs * g[:, half:]) * (1.0 + sc[:, half:]) + sh[:, half:]).astype(BF16)


def combine(ys, xn, rtw, pos_flat, modr, l, S, g_next, final):
    T, D = xn.shape
    tc = 256
    assert S % tc == 0
    row = lambda: pl.BlockSpec((tc, D), lambda i, *_: (i, 0))
    in_specs = [pl.BlockSpec(memory_space=pl.ANY), row(),
                pl.BlockSpec((tc, LANES), lambda i, *_: (i, 0)),
                _mod_spec(l, 5, S, tc)]
    args = [ys, xn, rtw, modr]
    if final:
        in_specs.append(pl.BlockSpec((1, D), lambda i, *_: (0, 0)))
        args.append(g_next.reshape(1, D))
        out_shape = jax.ShapeDtypeStruct((T, D), F32)
        out_specs = row()
    else:
        in_specs += [pl.BlockSpec((None, 1, D), lambda i, *_: (l + 1, 0, 0)),
                     _mod_spec(l + 1, 0, S, tc), _mod_spec(l + 1, 1, S, tc)]
        args += [g_next.reshape(-1, 1, D), modr, modr]
        out_shape = (jax.ShapeDtypeStruct((T, D), F32), jax.ShapeDtypeStruct((T, D), BF16))
        out_specs = (row(), row())
    return pl.pallas_call(
        functools.partial(_combine_kernel, tc=tc, T=T, final=final),
        out_shape=out_shape,
        grid_spec=pltpu.PrefetchScalarGridSpec(
            num_scalar_prefetch=1, grid=(T // tc,),
            in_specs=in_specs, out_specs=out_specs,
            scratch_shapes=[pltpu.VMEM((2, tc, D // 2), U32), pltpu.SemaphoreType.DMA]),
        compiler_params=_cparams(("arbitrary",)),
        name="moe_combine_final" if final else "moe_combine",
    )(pos_flat, *args)


def kernel(x, c, positions, w_ada, b_ada, norm1_g, norm2_g, w_in, attn_sinks, w_attn_out, ret_norm_g,
           w_ret_out, w_o, w_router, b_router, w_gate, w_up, w_down, final_g):
    B, S, D = x.shape
    L = w_ada.shape[0]
    T = B * S
    assert D == D_MODEL and w_in.shape[-1] == IN_W and S % RET_CHUNK == 0

    tab = rope_tables(positions)
    mod = adaln_mod(c, w_ada, b_ada)
    modr = mod[:, :B].reshape(L, B, 6, D).transpose(0, 2, 1, 3).reshape(L, 6, B, 1, D)

    x2 = x.reshape(T, D)
    h = norm_modulate(x2, norm1_g.reshape(L, 1, D), modr, 0, S)
    n_work = N_EXPERTS + (2 * T + MOE_ROWS - 1) // MOE_ROWS
    n_rows = 2 * T + N_EXPERTS * MOE_SUB
    out = None
    for l in range(L):
        proj = in_projection(h, w_in, l, tab)
        att = swa_attention(proj, attn_sinks[l], B, S)
        ret = retention(proj, ret_norm_g, l, B, S)
        xn, h2p, rti, rtw, cnt = mixer_out(att, ret, proj, x2, w_attn_out, w_ret_out, w_o, modr,
                                           norm2_g, w_router, b_router, l, S)
        off17, e_w, start_w, nsub_w = moe_plan(cnt[0, :N_EXPERTS], n_work)
        pos_flat = (off17[rti[:, 0:2]] + rti[:, 2:4]).T.reshape(2 * T)
        xs = dispatch(h2p, pos_flat, off17, n_rows)
        ys = moe_experts(xs, w_gate, w_up, w_down, l, e_w, start_w, nsub_w)
        if l + 1 < L:
            x2, h = combine(ys, xn, rtw, pos_flat, modr, l, S, norm1_g, final=False)
        else:
            out = combine(ys, xn, rtw, pos_flat, modr, l, S, final_g, final=True)
    return out.reshape(B, S, D)
```

```python
import functools

import numpy as np
import jax
import jax.numpy as jnp
from jax import lax
from jax.experimental import pallas as pl
from jax.experimental.pallas import tpu as pltpu

F32 = jnp.float32
BF16 = jnp.bfloat16
I32 = jnp.int32

D_MODEL = 2048
ATT_HEAD_DIM = 64
ATT_Q_HEADS = 16
ATT_KV_HEADS = 4
WINDOW = 128
RET_HEADS = 8
RET_HEAD_DIM = 128
RET_CHUNK = 128
ROPE_THETA = 10000.0
N_GROUPS = 4
EXPERTS_PER_GROUP = 4
N_EXPERTS = 16
D_FF = 1024
EPS = 1e-6

ATT_Q_W = ATT_Q_HEADS * ATT_HEAD_DIM
ATT_KV_W = ATT_KV_HEADS * ATT_HEAD_DIM
RET_W = RET_HEADS * RET_HEAD_DIM
OFF_QA = 0
OFF_KA = OFF_QA + ATT_Q_W
OFF_VA = OFF_KA + ATT_KV_W
OFF_QR = OFF_VA + ATT_KV_W
OFF_KR = OFF_QR + RET_W
OFF_VR = OFF_KR + RET_W
OFF_GR = OFF_VR + RET_W
OFF_GA = OFF_GR + RET_W
OFF_GB = OFF_GA + D_MODEL
IN_W = OFF_GB + D_MODEL

LANES = 128
PROJ_TN = 512
ROW_GROUP = 8
MOE_SUB = 256
MOE_ROWS = 2048
MOE_FC = 256
VMEM_LIMIT = 56 * 1024 * 1024


def _pick(n, cands):
    for c in cands:
        if n % c == 0:
            return c
    raise ValueError(f"no tile in {cands} divides {n}")


def _sigmoid(x):
    return 1.0 / (1.0 + jnp.exp(-x))


def _cparams(sem, vmem=VMEM_LIMIT):
    return pltpu.CompilerParams(dimension_semantics=sem, vmem_limit_bytes=vmem)


def _rope_consts():
    def inv_freq(head_dim):
        half = head_dim // 2
        inv = ROPE_THETA ** (-2.0 * jnp.arange(half, dtype=F32) / head_dim)
        return jnp.broadcast_to(jnp.tile(inv, LANES // half), (8, LANES))
    lane = np.arange(LANES)
    sgn_att = np.where((lane % ATT_HEAD_DIM) < ATT_HEAD_DIM // 2, -1.0, 1.0)
    sgn_ret = np.where(lane < RET_HEAD_DIM // 2, -1.0, 1.0)
    sgn = np.stack([np.tile(sgn_att, (8, 1)), np.tile(sgn_ret, (8, 1))]).astype(np.float32)
    return jnp.stack([inv_freq(ATT_HEAD_DIM), inv_freq(RET_HEAD_DIM)]), jnp.asarray(sgn)


def _rope_kernel(pos_ref, inv_ref, sgn_ref, tab_ref):
    pos = pos_ref[...]
    for kind in range(2):
        ang = pos * inv_ref[kind, 0:1, :]
        tab_ref[kind, :, 0:LANES] = jnp.cos(ang)
        tab_ref[kind, :, LANES:2 * LANES] = jnp.sin(ang) * sgn_ref[kind, 0:1, :]


def rope_tables(positions):
    T = positions.size
    posb = jnp.broadcast_to(positions.reshape(T, 1).astype(F32), (T, LANES))
    inv, sgn = _rope_consts()
    tm = _pick(T, (1024, 512, 256, 128))
    return pl.pallas_call(
        _rope_kernel,
        out_shape=jax.ShapeDtypeStruct((2, T, 2 * LANES), F32),
        grid=(T // tm,),
        in_specs=[pl.BlockSpec((tm, LANES), lambda i: (i, 0)),
                  pl.BlockSpec((2, 8, LANES), lambda i: (0, 0, 0)),
                  pl.BlockSpec((2, 8, LANES), lambda i: (0, 0, 0))],
        out_specs=pl.BlockSpec((2, tm, 2 * LANES), lambda i: (0, i, 0)),
        compiler_params=_cparams(("arbitrary",)),
        name="rope_tables",
    )(posb, inv, sgn)


def _adaln_kernel(c_ref, w_ref, b_ref, o_ref):
    c = c_ref[...]
    ca = c * _sigmoid(c)
    hi = ca.astype(BF16).astype(F32)
    lhs = jnp.concatenate([hi, ca - hi], axis=0).astype(BF16)
    r = jnp.dot(lhs, w_ref[...].astype(BF16), preferred_element_type=F32)
    o_ref[...] = r[0:8] + r[8:16] + b_ref[...]


def adaln_mod(c, w_ada, b_ada):
    L, D, N = w_ada.shape
    B = c.shape[0]
    cp = jnp.pad(c, ((0, 8 - B), (0, 0)))
    tn = _pick(N, (1024, 512))
    return pl.pallas_call(
        _adaln_kernel,
        out_shape=jax.ShapeDtypeStruct((L, 8, N), F32),
        grid=(L, N // tn),
        in_specs=[pl.BlockSpec((8, D), lambda l, j: (0, 0)),
                  pl.BlockSpec((None, D, tn), lambda l, j: (l, 0, j)),
                  pl.BlockSpec((None, 1, tn), lambda l, j: (l, 0, j))],
        out_specs=pl.BlockSpec((None, 8, tn), lambda l, j: (l, 0, j)),
        compiler_params=_cparams(("arbitrary", "arbitrary")),
        name="adaln_mod",
    )(cp, w_ada, b_ada.reshape(L, 1, N))


def _norm_mod(x, g, shift, scale):
    ms = jnp.mean(x * x, axis=-1, keepdims=True)
    return (x * lax.rsqrt(ms + EPS) * g) * (1.0 + scale) + shift


def _norm_mod_kernel(x_ref, g_ref, sh_ref, sc_ref, h_ref):
    h_ref[...] = _norm_mod(x_ref[...], g_ref[...], sh_ref[...], sc_ref[...]).astype(BF16)


def _mod_spec(l, k, S, tm):
    return pl.BlockSpec((None, None, None, 1, D_MODEL), lambda i, *_: (l, k, (i * tm) // S, 0, 0))


def norm_modulate(x2, g, modr, l, S):
    T, D = x2.shape
    tm = _pick(S, (512, 256, 128))
    return pl.pallas_call(
        _norm_mod_kernel,
        out_shape=jax.ShapeDtypeStruct((T, D), BF16),
        grid=(T // tm,),
        in_specs=[pl.BlockSpec((tm, D), lambda i: (i, 0)),
                  pl.BlockSpec((None, 1, D), lambda i: (l, 0, 0)),
                  _mod_spec(l, 0, S, tm), _mod_spec(l, 1, S, tm)],
        out_specs=pl.BlockSpec((tm, D), lambda i: (i, 0)),
        compiler_params=_cparams(("arbitrary",)),
        name="norm1_modulate",
    )(x2, g, modr, modr)


def _rope_att(blk, cos, sin, scale):
    lane = lax.broadcasted_iota(I32, blk.shape, 1)
    rot = jnp.where((lane % 64) < 32, pltpu.roll(blk, 96, 1), pltpu.roll(blk, 32, 1))
    out = blk * cos + rot * sin
    return out * scale if scale != 1.0 else out


def _rope_ret(blk, cos, sin, scale):
    out = blk * cos + pltpu.roll(blk, 64, 1) * sin
    return out * scale if scale != 1.0 else out


def _inproj_kernel(h_ref, w_ref, tab_ref, o_ref, wbf_ref):
    j = pl.program_id(0)

    @pl.when(pl.program_id(1) == 0)
    def _():
        wbf_ref[...] = w_ref[...].astype(BF16)

    ngrp = PROJ_TN // LANES

    def run(epilogue):
        acc = jnp.dot(h_ref[...], wbf_ref[...], preferred_element_type=F32)
        cos = tab_ref[:, 0:LANES]
        sin = tab_ref[:, LANES:2 * LANES]
        for g in range(ngrp):
            blk = acc[:, g * LANES:(g + 1) * LANES]
            o_ref[:, g * LANES:(g + 1) * LANES] = epilogue(g, blk, cos, sin).astype(BF16)

    t = lambda off: off // PROJ_TN
    q_scale = float(ATT_HEAD_DIM) ** -0.5
    k_scale = float(RET_HEAD_DIM) ** -0.5

    @pl.when(j < t(OFF_KA))
    def _():
        run(lambda g, b, c, s: _rope_att(b, c, s, q_scale))

    @pl.when(j == t(OFF_KA))
    def _():
        run(lambda g, b, c, s: _rope_att(b, c, s, 1.0) if g < ATT_KV_W // LANES else b)

    @pl.when((j >= t(OFF_QR)) & (j < t(OFF_KR)))
    def _():
        run(lambda g, b, c, s: _rope_ret(b, c, s, 1.0))

    @pl.when((j >= t(OFF_KR)) & (j < t(OFF_VR)))
    def _():
        run(lambda g, b, c, s: _rope_ret(b, c, s, k_scale))

    @pl.when((j >= t(OFF_VR)) & (j < t(OFF_GR)))
    def _():
        run(lambda g, b, c, s: b)

    @pl.when((j >= t(OFF_GR)) & (j < t(OFF_GA)))
    def _():
        run(lambda g, b, c, s: b * _sigmoid(b))

    @pl.when(j >= t(OFF_GA))
    def _():
        run(lambda g, b, c, s: _sigmoid(b))


def in_projection(h, w_in, l, tab):
    T, D = h.shape
    tm = _pick(T, (1024, 768, 512, 256))
    nj = IN_W // PROJ_TN
    t_qr, t_vr = OFF_QR // PROJ_TN, OFF_VR // PROJ_TN

    def tab_map(j, i):
        kind = jnp.where(j < t_qr, 0, 1)
        return (kind, jnp.where(j < t_vr, i, 0), 0)

    return pl.pallas_call(
        _inproj_kernel,
        out_shape=jax.ShapeDtypeStruct((T, IN_W), BF16),
        grid=(nj, T // tm),
        in_specs=[pl.BlockSpec((tm, D), lambda j, i: (i, 0)),
                  pl.BlockSpec((None, D, PROJ_TN), lambda j, i: (l, 0, j)),
                  pl.BlockSpec((None, tm, 2 * LANES), tab_map)],
        out_specs=pl.BlockSpec((tm, PROJ_TN), lambda j, i: (i, j)),
        scratch_shapes=[pltpu.VMEM((D, PROJ_TN), BF16)],
        compiler_params=_cparams(("arbitrary", "arbitrary")),
        name="in_projection",
    )(h, w_in, tab)


def _attn_kernel(sinks_ref, q_ref, kvc_ref, kvp_ref, o_ref, *, tq):
    W = WINDOW
    nsub = tq // W
    is_first = pl.program_id(1) == 0
    qi = lax.broadcasted_iota(I32, (W, 2 * W), 0)
    kj = lax.broadcasted_iota(I32, (W, 2 * W), 1)
    rel = qi + W - kj
    band = (rel >= 0) & (rel < WINDOW)
    band0 = band & (jnp.logical_not(is_first) | (kj >= W))
    lane = lax.broadcasted_iota(I32, (tq + W, LANES), 1)
    lo = lane < ATT_HEAD_DIM

    def split_pair(raw, parity):
        x = raw.astype(F32)
        xr = pltpu.roll(x, ATT_HEAD_DIM, 1)
        if parity == 0:
            return jnp.where(lo, x, 0.0), jnp.where(lo, 0.0, xr)
        return jnp.where(lo, xr, 0.0), jnp.where(lo, 0.0, x)

    for hk in range(ATT_KV_HEADS):
        grp, par = hk // 2, hk % 2
        kc = slice(grp * LANES, (grp + 1) * LANES)
        vc = slice(ATT_KV_W + grp * LANES, ATT_KV_W + (grp + 1) * LANES)
        k_pair = split_pair(jnp.concatenate([kvp_ref[:, kc], kvc_ref[:, kc]], axis=0), par)
        v_pair = split_pair(jnp.concatenate([kvp_ref[:, vc], kvc_ref[:, vc]], axis=0), par)
        kt_pair = [k.T.astype(BF16) for k in k_pair]
        v_pair = [v.astype(BF16) for v in v_pair]
        qcols = [slice(hk * 4 * ATT_HEAD_DIM + jq * LANES, hk * 4 * ATT_HEAD_DIM + (jq + 1) * LANES)
                 for jq in range(2)]
        for m in range(nsub):
            rows = slice(m * W, (m + 1) * W)
            keys = slice(m * W, m * W + 2 * W)
            mask = band0 if m == 0 else band
            kcat = jnp.concatenate([kt_pair[0][:, keys], kt_pair[1][:, keys]], axis=1)
            vcat = jnp.concatenate([v_pair[0][keys], v_pair[1][keys]], axis=0)
            q = jnp.concatenate([q_ref[rows, qcols[0]], q_ref[rows, qcols[1]]], axis=0)
            s_all = jnp.dot(q, kcat, preferred_element_type=F32)
            p_rows = []
            for jq in range(2):
                p_cols = []
                for e in range(2):
                    s = s_all[jq * W:(jq + 1) * W, e * 2 * W:(e + 1) * 2 * W]
                    s = jnp.where(mask, s, -1e30)
                    sink = sinks_ref[hk * 4 + jq * 2 + e]
                    mx = jnp.maximum(jnp.max(s, axis=1, keepdims=True), sink)
                    p = jnp.exp(s - mx)
                    den = jnp.sum(p, axis=1, keepdims=True) + jnp.exp(sink - mx)
                    p_cols.append((p * (1.0 / den)).astype(BF16))
                p_rows.append(jnp.concatenate(p_cols, axis=1))
            o = jnp.dot(jnp.concatenate(p_rows, axis=0), vcat, preferred_element_type=F32)
            o_ref[rows, qcols[0]] = o[0:W].astype(BF16)
            o_ref[rows, qcols[1]] = o[W:2 * W].astype(BF16)


def swa_attention(proj, sinks, B, S):
    T = proj.shape[0]
    tq = _pick(S, (512, 384, 256, 128))
    nq = S // tq
    kvw = 2 * ATT_KV_W
    kv_blk = OFF_KA // kvw

    def prev_map(b, i):
        return (jnp.maximum(b * (S // WINDOW) + i * (tq // WINDOW) - 1, 0), kv_blk)

    return pl.pallas_call(
        functools.partial(_attn_kernel, tq=tq),
        out_shape=jax.ShapeDtypeStruct((T, ATT_Q_W), BF16),
        grid=(B, nq),
        in_specs=[pl.BlockSpec(memory_space=pltpu.SMEM),
                  pl.BlockSpec((tq, ATT_Q_W), lambda b, i: (b * nq + i, 0)),
                  pl.BlockSpec((tq, kvw), lambda b, i: (b * nq + i, kv_blk)),
                  pl.BlockSpec((WINDOW, kvw), prev_map)],
        out_specs=pl.BlockSpec((tq, ATT_Q_W), lambda b, i: (b * nq + i, 0)),
        compiler_params=_cparams(("arbitrary", "arbitrary")),
        name="swa_attention",
    )(sinks, proj, proj, proj)


def _ret_consts():
    C = RET_CHUNK
    log_g = jnp.log1p(-jnp.exp2(-5.0 - jnp.arange(RET_HEADS, dtype=F32)))
    i = jnp.arange(C, dtype=F32)
    diff = i[:, None] - i[None, :]
    dm = jnp.where(diff[None] >= 0, jnp.exp(jnp.maximum(diff, 0.0)[None] * log_g[:, None, None]), 0.0)
    qd = jnp.exp((i + 1.0)[None, :] * log_g[:, None])
    kd = jnp.exp((C - 1.0 - i)[None, :] * log_g[:, None])
    cd = jnp.exp(C * log_g)
    bc = lambda v: jnp.broadcast_to(v[:, :, None], (RET_HEADS, C, LANES))
    return dm, bc(qd), bc(kd), jnp.broadcast_to(cd[:, None, None], (RET_HEADS, 8, LANES))


def _ret_kernel(q_ref, k_ref, v_ref, g_ref, gn_ref, dm_ref, qd_ref, kd_ref, cd_ref, o_ref, *, nchunk):
    C = RET_CHUNK
    dm, qd, kd = dm_ref[...], qd_ref[...], kd_ref[...]
    cd = cd_ref[0:1, :]
    gn = gn_ref[...]
    state = jnp.zeros((RET_HEAD_DIM, RET_HEAD_DIM), F32)
    for c in range(nchunk):
        rows = slice(c * C, (c + 1) * C)
        q, k, v = q_ref[rows, :], k_ref[rows, :], v_ref[rows, :]
        s = lax.dot_general(q, k, (((1,), (1,)), ((), ())), preferred_element_type=F32) * dm
        y = jnp.dot(s.astype(BF16), v, preferred_element_type=F32)
        y = y + jnp.dot(q, state.astype(BF16), preferred_element_type=F32) * qd
        vk = (v.astype(F32) * kd).astype(BF16)
        kv = lax.dot_general(k, vk, (((0,), (0,)), ((), ())), preferred_element_type=F32)
        state = cd * state + kv
        ms = jnp.mean(y * y, axis=-1, keepdims=True)
        yn = y * lax.rsqrt(ms + EPS) * gn
        o_ref[rows, :] = (g_ref[rows, :].astype(F32) * yn).astype(BF16)


def retention(proj, ret_norm_g, l, B, S):
    T = proj.shape[0]
    dm, qd, kd, cd = _ret_consts()
    blk = lambda off: (lambda b, h: (b, off // LANES + h))
    hconst = lambda rows: pl.BlockSpec((None, rows, LANES), lambda b, h: (h, 0, 0))
    return pl.pallas_call(
        functools.partial(_ret_kernel, nchunk=S // RET_CHUNK),
        out_shape=jax.ShapeDtypeStruct((T, RET_W), BF16),
        grid=(B, RET_HEADS),
        in_specs=[pl.BlockSpec((S, LANES), blk(OFF_QR)),
                  pl.BlockSpec((S, LANES), blk(OFF_KR)),
                  pl.BlockSpec((S, LANES), blk(OFF_VR)),
                  pl.BlockSpec((S, LANES), blk(OFF_GR)),
                  pl.BlockSpec((None, 1, LANES), lambda b, h: (l, 0, h)),
                  hconst(RET_CHUNK), hconst(RET_CHUNK), hconst(RET_CHUNK), hconst(8)],
        out_specs=pl.BlockSpec((S, LANES), lambda b, h: (b, h)),
        compiler_params=_cparams(("arbitrary", "arbitrary")),
        name="retention",
    )(proj, proj, proj, proj, ret_norm_g.reshape(-1, 1, RET_W), dm, qd, kd, cd)


def _load_cast(w_hbm, dst, stage, sem):
    rows = stage.shape[0]
    n = w_hbm.shape[0] // rows

    def body(i, carry):
        r0 = pl.multiple_of(i * rows, rows)
        cp = pltpu.make_async_copy(w_hbm.at[pl.ds(r0, rows), :], stage, sem)
        cp.start()
        cp.wait()
        dst[pl.ds(r0, rows), :] = stage[...].astype(BF16)
        return carry
    lax.fori_loop(0, n, body, 0)


def _mixer_out_kernel(att_ref, ret_ref, ga0, ga1, ga2, ga3, gb0, gb1, gb2, gb3, x_ref,
                      wa_hbm, wr_hbm, wo_hbm, gate1_ref, g2_ref, sh2_ref, sc2_ref,
                      wrt_ref, brt_ref, tril_ref,
                      xn_ref, h2_ref, rti_ref, rtw_ref, cnt_ref,
                      wa, wr, wo, wrt_hi, wrt_lo, mrg, carry, stage, sem, *, tm, l):
    i = pl.program_id(0)
    D = D_MODEL

    @pl.when(i == 0)
    def _():
        _load_cast(wa_hbm.at[l], wa, stage, sem)
        _load_cast(wr_hbm.at[l], wr, stage, sem)
        _load_cast(wo_hbm.at[l], wo, stage, sem)
        w = wrt_ref[...]
        hi = w.astype(BF16)
        wrt_hi[...] = hi
        wrt_lo[...] = (w - hi.astype(F32)).astype(BF16)
        carry[...] = jnp.zeros_like(carry)

    a = att_ref[...]
    r = ret_ref[...]
    ga = (ga0, ga1, ga2, ga3)
    gb = (gb0, gb1, gb2, gb3)
    for n in range(D // PROJ_TN):
        cols = slice(n * PROJ_TN, (n + 1) * PROJ_TN)
        ya = jnp.dot(a, wa[:, cols], preferred_element_type=F32)
        yr = jnp.dot(r, wr[:, cols], preferred_element_type=F32)
        mrg[:, cols] = (ga[n][...].astype(F32) * ya + gb[n][...].astype(F32) * yr).astype(BF16)
    o = jnp.dot(mrg[...], wo[...], preferred_element_type=F32)
    xn = x_ref[...] + gate1_ref[...] * o
    xn_ref[...] = xn

    h2 = _norm_mod(xn, g2_ref[...], sh2_ref[...], sc2_ref[...])
    h2b = h2.astype(BF16)
    h2_ref[...] = h2b

    h2l = (h2 - h2b.astype(F32)).astype(BF16)
    logits = (jnp.dot(h2b, wrt_hi[...], preferred_element_type=F32)
              + jnp.dot(h2l, wrt_hi[...], preferred_element_type=F32)
              + jnp.dot(h2b, wrt_lo[...], preferred_element_type=F32)) + brt_ref[...]
    lane = lax.broadcasted_iota(I32, (tm, LANES), 1)
    valid = lane < N_EXPERTS
    mx = jnp.max(logits, axis=-1, keepdims=True)
    p = jnp.where(valid, jnp.exp(logits - mx), 0.0)
    pos_in_grp = lane % EXPERTS_PER_GROUP
    grp_of = lane // EXPERTS_PER_GROUP

    def member(k):
        wrapped = pos_in_grp + k >= EXPERTS_PER_GROUP
        return jnp.where(wrapped, pltpu.roll(p, EXPERTS_PER_GROUP - k, 1), pltpu.roll(p, LANES - k, 1)), wrapped

    (b1, w1), (b2, w2), (b3, w3) = member(1), member(2), member(3)
    m_ab, n_ab = jnp.maximum(p, b1), jnp.minimum(p, b1)
    m_cd, n_cd = jnp.maximum(b2, b3), jnp.minimum(b2, b3)
    gscore = jnp.maximum(m_ab, m_cd) + jnp.maximum(jnp.minimum(m_ab, m_cd), jnp.maximum(n_ab, n_cd))
    gscore = jnp.where(valid, gscore, -1.0)
    gmax = jnp.max(gscore, axis=-1, keepdims=True)
    gsel = jnp.min(jnp.where(gscore == gmax, grp_of, N_GROUPS), axis=-1, keepdims=True)
    in_sel = grp_of == gsel
    beats = lambda b, w: ((b > p) | ((b == p) & w)).astype(I32)
    rank_in_grp = beats(b1, w1) + beats(b2, w2) + beats(b3, w3)
    sel0 = in_sel & (rank_in_grp == 0)
    sel1 = in_sel & (rank_in_grp == 1)
    lsum = lambda m, v: jnp.sum(jnp.where(m, v, 0.0), axis=-1, keepdims=True)
    v0, v1 = lsum(sel0, p), lsum(sel1, p)
    lanef = lane.astype(F32)
    e0, e1 = lsum(sel0, lanef), lsum(sel1, lanef)
    inv = 1.0 / (v0 + v1)
    onehot = jnp.where(sel0 | sel1, 1.0, 0.0)
    prefix = jnp.dot(tril_ref[...], onehot.astype(BF16), preferred_element_type=F32) + carry[0:1, :]
    r0, r1 = lsum(sel0, prefix), lsum(sel1, prefix)
    carry[...] = carry[...] + jnp.sum(onehot, axis=0, keepdims=True)
    cnt_ref[...] = carry[...].astype(I32)
    sel4 = lambda a0, a1, a2, a3: jnp.where(lane == 0, a0, jnp.where(lane == 1, a1, jnp.where(lane == 2, a2, a3)))
    rti_ref[...] = sel4(e0, e1, r0, r1).astype(I32)
    rtw_ref[...] = jnp.where(lane == 0, v0 * inv, jnp.where(lane == 1, v1 * inv, 0.0))


def mixer_out(att, ret, proj, x2, w_attn_out, w_ret_out, w_o, modr, norm2_g, w_router, b_router, l, S):
    T, D = x2.shape
    tm = 256
    assert S % tm == 0
    nga, ngb = OFF_GA // PROJ_TN, OFF_GB // PROJ_TN
    gate_spec = lambda blk: pl.BlockSpec((tm, PROJ_TN), lambda i: (i, blk))
    row = lambda: pl.BlockSpec((tm, D), lambda i: (i, 0))
    wrt = jnp.pad(w_router, ((0, 0), (0, LANES - N_EXPERTS)))
    brt = jnp.pad(b_router.astype(F32), (0, LANES - N_EXPERTS), constant_values=-1e30).reshape(1, LANES)
    tril = jnp.asarray(np.tril(np.ones((tm, tm), np.float32), -1), BF16)
    any_spec = pl.BlockSpec(memory_space=pl.ANY)
    outs = pl.pallas_call(
        functools.partial(_mixer_out_kernel, tm=tm, l=l),
        out_shape=(jax.ShapeDtypeStruct((T, D), F32),
                   jax.ShapeDtypeStruct((T, D), BF16),
                   jax.ShapeDtypeStruct((T, LANES), I32),
                   jax.ShapeDtypeStruct((T, LANES), F32),
                   jax.ShapeDtypeStruct((8, LANES), I32)),
        grid=(T // tm,),
        in_specs=[pl.BlockSpec((tm, ATT_Q_W), lambda i: (i, 0)),
                  pl.BlockSpec((tm, RET_W), lambda i: (i, 0)),
                  *[gate_spec(nga + n) for n in range(4)],
                  *[gate_spec(ngb + n) for n in range(4)],
                  row(), any_spec, any_spec, any_spec,
                  _mod_spec(l, 2, S, tm),
                  pl.BlockSpec((None, 1, D), lambda i: (l, 0, 0)),
                  _mod_spec(l, 3, S, tm), _mod_spec(l, 4, S, tm),
                  pl.BlockSpec((D, LANES), lambda i: (0, 0)),
                  pl.BlockSpec((1, LANES), lambda i: (0, 0)),
                  pl.BlockSpec((tm, tm), lambda i: (0, 0))],
        out_specs=(row(),
                   row(),
                   pl.BlockSpec((tm, LANES), lambda i: (i, 0)),
                   pl.BlockSpec((tm, LANES), lambda i: (i, 0)),
                   pl.BlockSpec((8, LANES), lambda i: (0, 0))),
        scratch_shapes=[pltpu.VMEM((ATT_Q_W, D), BF16), pltpu.VMEM((RET_W, D), BF16),
                        pltpu.VMEM((D, D), BF16),
                        pltpu.VMEM((D, LANES), BF16), pltpu.VMEM((D, LANES), BF16),
                        pltpu.VMEM((tm, D), BF16), pltpu.VMEM((8, LANES), F32),
                        pltpu.VMEM((256, D), F32), pltpu.SemaphoreType.DMA],
        compiler_params=_cparams(("arbitrary",)),
        name="mixer_out",
    )(att, ret, *([proj] * 8), x2, w_attn_out, w_ret_out, w_o,
      modr, norm2_g.reshape(-1, 1, D), modr, modr, wrt, brt, tril)
    return outs


def moe_plan(counts, n_work):
    sub_per = MOE_ROWS // MOE_SUB
    seg = ((counts + MOE_SUB - 1) // MOE_SUB) * MOE_SUB
    off = jnp.cumsum(seg) - seg
    off17 = jnp.concatenate([off, off[-1:] + seg[-1:]]).astype(I32)
    nb = (counts + MOE_ROWS - 1) // MOE_ROWS
    cum = jnp.cumsum(nb)
    total = cum[-1]
    w = jnp.arange(n_work, dtype=I32)
    wc = jnp.minimum(w, total - 1)
    e_w = jnp.sum((cum[None, :] <= wc[:, None]).astype(I32), axis=1)
    blk = wc - (cum[e_w] - nb[e_w])
    start = off[e_w] + blk * MOE_ROWS
    nsub = jnp.clip(seg[e_w] // MOE_SUB - blk * sub_per, 0, sub_per)
    nsub = jnp.where(w < total, nsub, 0)
    return off17, e_w, start.astype(I32), nsub.astype(I32)


def _row_dma_loops(n_rows, make_copy):
    def run(op):
        def body(g, carry):
            r0 = pl.multiple_of(g * ROW_GROUP, ROW_GROUP)
            for u in range(ROW_GROUP):
                for k in range(2):
                    getattr(make_copy(r0 + u, k), op)()
            return carry
        lax.fori_loop(0, n_rows // ROW_GROUP, body, 0)
    run("start")
    run("wait")


def _dispatch_kernel(pos_ref, off_ref, h_ref, xs_ref, rows_ref, sem, *, td, T):
    i = pl.program_id(0)
    zero_ref = rows_ref

    @pl.when(i == 0)
    def _():
        zero_ref[...] = jnp.zeros_like(zero_ref)
        used = off_ref[N_EXPERTS]

        def zero_copy(row):
            row = pl.multiple_of(row, MOE_SUB)
            return pltpu.make_async_copy(zero_ref, xs_ref.at[pl.ds(row, MOE_SUB), :], sem)

        def fill(op):
            def seg_tail(e, carry):
                @pl.when(off_ref[e + 1] > off_ref[e])
                def _():
                    getattr(zero_copy(off_ref[e + 1] - MOE_SUB), op)()
                return carry

            def buf_tail(n, carry):
                getattr(zero_copy(used + n * MOE_SUB), op)()
                return carry
            lax.fori_loop(0, N_EXPERTS, seg_tail, 0)
            lax.fori_loop(0, (xs_ref.shape[0] - used) // MOE_SUB, buf_tail, 0)
        fill("start")
        fill("wait")

    rows_ref[...] = h_ref[...].astype(F32)

    def copy(r, k):
        p = pos_ref[k * T + i * td + r]
        return pltpu.make_async_copy(rows_ref.at[pl.ds(r, 1), :], xs_ref.at[pl.ds(p, 1), :], sem)
    _row_dma_loops(td, copy)


def dispatch(h2, pos_flat, off17, n_rows):
    T, D = h2.shape
    td = MOE_SUB
    return pl.pallas_call(
        functools.partial(_dispatch_kernel, td=td, T=T),
        out_shape=jax.ShapeDtypeStruct((n_rows, D), F32),
        grid_spec=pltpu.PrefetchScalarGridSpec(
            num_scalar_prefetch=2, grid=(T // td,),
            in_specs=[pl.BlockSpec((td, D), lambda i, *_: (i, 0))],
            out_specs=pl.BlockSpec(memory_space=pl.ANY),
            scratch_shapes=[pltpu.VMEM((td, D), F32), pltpu.SemaphoreType.DMA]),
        compiler_params=_cparams(("arbitrary",)),
        name="moe_dispatch",
    )(pos_flat, off17, h2)


def _moe_kernel(we_ref, ws_ref, wn_ref, xs_ref, wg_ref, wu_ref, wd_ref, ys_ref,
                xb, acc, wgb, wub, wdb, stage, ostage, sem_in, sem_out, *, nfc, n_work):
    w = pl.program_id(0)
    c = pl.program_id(1)
    n = wn_ref[w]
    s0 = ws_ref[w]

    def in_copy(start_row, i, slot):
        g0 = pl.multiple_of(start_row + i * MOE_SUB, MOE_SUB)
        return pltpu.make_async_copy(xs_ref.at[pl.ds(g0, MOE_SUB), :], stage.at[slot], sem_in.at[slot])

    def out_copy(i, slot):
        g0 = pl.multiple_of(s0 + i * MOE_SUB, MOE_SUB)
        return pltpu.make_async_copy(ostage.at[slot], ys_ref.at[pl.ds(g0, MOE_SUB), :], sem_out.at[slot])

    def request_first_two(start_row, count):
        @pl.when(count > 0)
        def _():
            in_copy(start_row, 0, 0).start()

        @pl.when(count > 1)
        def _():
            in_copy(start_row, 1, 1).start()

    @pl.when((w == 0) & (c == 0))
    def _():
        request_first_two(s0, n)

    def for_subs(fn):
        def body(i, carry):
            fn(i, pl.ds(pl.multiple_of(i * MOE_SUB, MOE_SUB), MOE_SUB), lax.rem(i, 2))
            return carry
        lax.fori_loop(0, n, body, 0)

    def ffn(rows):
        x = xb[rows, :]
        g = jnp.dot(x, wgb[...], preferred_element_type=F32)
        u = jnp.dot(x, wub[...], preferred_element_type=F32)
        a = (g * _sigmoid(g) * u).astype(BF16)
        return jnp.dot(a, wdb[...], preferred_element_type=F32)

    @pl.when(n > 0)
    def _():
        wgb[...] = wg_ref[...].astype(BF16)
        wub[...] = wu_ref[...].astype(BF16)
        wdb[...] = wd_ref[...].astype(BF16)

        @pl.when(c == 0)
        def _():
            def first(i, rows, slot):
                in_copy(s0, i, slot).wait()
                xb[rows, :] = stage[slot].astype(BF16)

                @pl.when(i + 2 < n)
                def _():
                    in_copy(s0, i + 2, slot).start()
                acc[rows, :] = ffn(rows)
            for_subs(first)

        @pl.when((c > 0) & (c < nfc - 1))
        def _():
            def middle(i, rows, slot):
                acc[rows, :] += ffn(rows)
            for_subs(middle)

        @pl.when(c == nfc - 1)
        def _():
            def last(i, rows, slot):
                @pl.when(i >= 2)
                def _():
                    out_copy(i - 2, slot).wait()
                ostage[slot] = acc[rows, :] + ffn(rows)
                out_copy(i, slot).start()
            for_subs(last)

            @pl.when(n >= 2)
            def _():
                out_copy(n - 2, lax.rem(n, 2)).wait()
            out_copy(n - 1, lax.rem(n - 1, 2)).wait()

            @pl.when(w + 1 < n_work)
            def _():
                nxt = jnp.minimum(w + 1, n_work - 1)
                request_first_two(ws_ref[nxt], wn_ref[nxt])


def moe_experts(xs, w_gate, w_up, w_down, l, e_w, start_w, nsub_w):
    n_rows, D = xs.shape
    nfc = D_FF // MOE_FC
    assert nfc >= 2
    n_work = e_w.shape[0]

    def chunk(c, wn, w):
        return jnp.where(wn[w] > 0, c, nfc - 1)

    return pl.pallas_call(
        functools.partial(_moe_kernel, nfc=nfc, n_work=n_work),
        out_shape=jax.ShapeDtypeStruct((n_rows, D), F32),
        grid_spec=pltpu.PrefetchScalarGridSpec(
            num_scalar_prefetch=3, grid=(n_work, nfc),
            in_specs=[pl.BlockSpec(memory_space=pl.ANY),
                      pl.BlockSpec((None, None, D, MOE_FC), lambda w, c, we, ws, wn: (l, we[w], 0, chunk(c, wn, w))),
                      pl.BlockSpec((None, None, D, MOE_FC), lambda w, c, we, ws, wn: (l, we[w], 0, chunk(c, wn, w))),
                      pl.BlockSpec((None, None, MOE_FC, D), lambda w, c, we, ws, wn: (l, we[w], chunk(c, wn, w), 0))],
            out_specs=pl.BlockSpec(memory_space=pl.ANY),
            scratch_shapes=[pltpu.VMEM((MOE_ROWS, D), BF16), pltpu.VMEM((MOE_ROWS, D), F32),
                            pltpu.VMEM((D, MOE_FC), BF16), pltpu.VMEM((D, MOE_FC), BF16),
                            pltpu.VMEM((MOE_FC, D), BF16),
                            pltpu.VMEM((2, MOE_SUB, D), F32), pltpu.VMEM((2, MOE_SUB, D), F32),
                            pltpu.SemaphoreType.DMA((2,)), pltpu.SemaphoreType.DMA((2,))]),
        input_output_aliases={3: 0},
        compiler_params=_cparams(("arbitrary", "arbitrary")),
        name="moe_experts",
    )(e_w, start_w, nsub_w, xs, w_gate, w_up, w_down)


def _combine_kernel(pos_ref, ys_ref, x_ref, rtw_ref, gate2_ref, g_ref, *rest, tc, T, final):
    if final:
        out_ref, ybuf, sem = rest
    else:
        sh_ref, sc_ref, xo_ref, h_ref, ybuf, sem = rest
    i = pl.program_id(0)

    def copy(r, k):
        p = pos_ref[k * T + i * tc + r]
        return pltpu.make_async_copy(ys_ref.at[pl.ds(p, 1), :], ybuf.at[k, pl.ds(r, 1), :], sem)
    _row_dma_loops(tc, copy)

    moe = rtw_ref[:, 0:1] * ybuf[0] + rtw_ref[:, 1:2] * ybuf[1]
    xo = x_ref[...] + gate2_ref[...] * moe
    if final:
        ms = jnp.mean(xo * xo, axis=-1, keepdims=True)
        out_ref[...] = xo * lax.rsqrt(ms + EPS) * g_ref[...]
    else:
        xo_ref[...] = xo
        h_ref[...] = _norm_mod(xo, g_ref[...], sh_ref[...], sc_ref[...]).astype(BF16)


def combine(ys, xn, rtw, pos_flat, modr, l, S, g_next, final):
    T, D = xn.shape
    tc = 256
    assert S % tc == 0
    row = lambda: pl.BlockSpec((tc, D), lambda i, *_: (i, 0))
    in_specs = [pl.BlockSpec(memory_space=pl.ANY), row(),
                pl.BlockSpec((tc, LANES), lambda i, *_: (i, 0)),
                _mod_spec(l, 5, S, tc)]
    args = [ys, xn, rtw, modr]
    if final:
        in_specs.append(pl.BlockSpec((1, D), lambda i, *_: (0, 0)))
        args.append(g_next.reshape(1, D))
        out_shape = jax.ShapeDtypeStruct((T, D), F32)
        out_specs = row()
    else:
        in_specs += [pl.BlockSpec((None, 1, D), lambda i, *_: (l + 1, 0, 0)),
                     _mod_spec(l + 1, 0, S, tc), _mod_spec(l + 1, 1, S, tc)]
        args += [g_next.reshape(-1, 1, D), modr, modr]
        out_shape = (jax.ShapeDtypeStruct((T, D), F32), jax.ShapeDtypeStruct((T, D), BF16))
        out_specs = (row(), row())
    return pl.pallas_call(
        functools.partial(_combine_kernel, tc=tc, T=T, final=final),
        out_shape=out_shape,
        grid_spec=pltpu.PrefetchScalarGridSpec(
            num_scalar_prefetch=1, grid=(T // tc,),
            in_specs=in_specs, out_specs=out_specs,
            scratch_shapes=[pltpu.VMEM((2, tc, D), F32), pltpu.SemaphoreType.DMA]),
        compiler_params=_cparams(("arbitrary",)),
        name="moe_combine_final" if final else "moe_combine",
    )(pos_flat, *args)


def kernel(x, c, positions, w_ada, b_ada, norm1_g, norm2_g, w_in, attn_sinks, w_attn_out, ret_norm_g,
           w_ret_out, w_o, w_router, b_router, w_gate, w_up, w_down, final_g):
    B, S, D = x.shape
    L = w_ada.shape[0]
    T = B * S
    assert D == D_MODEL and w_in.shape[-1] == IN_W and S % RET_CHUNK == 0

    tab = rope_tables(positions)
    mod = adaln_mod(c, w_ada, b_ada)
    modr = mod[:, :B].reshape(L, B, 6, D).transpose(0, 2, 1, 3).reshape(L, 6, B, 1, D)

    x2 = x.reshape(T, D)
    h = norm_modulate(x2, norm1_g.reshape(L, 1, D), modr, 0, S)
    n_work = N_EXPERTS + (2 * T + MOE_ROWS - 1) // MOE_ROWS
    n_rows = 2 * T + N_EXPERTS * MOE_SUB
    out = None
    for l in range(L):
        proj = in_projection(h, w_in, l, tab)
        att = swa_attention(proj, attn_sinks[l], B, S)
        ret = retention(proj, ret_norm_g, l, B, S)
        xn, h2, rti, rtw, cnt = mixer_out(att, ret, proj, x2, w_attn_out, w_ret_out, w_o, modr,
                                          norm2_g, w_router, b_router, l, S)
        off17, e_w, start_w, nsub_w = moe_plan(cnt[0, :N_EXPERTS], n_work)
        seg_off = jnp.sum(jnp.where(rti[:, 0:2, None] == jnp.arange(N_EXPERTS, dtype=I32),
                                    off17[:N_EXPERTS], 0), axis=-1)
        pos_flat = (seg_off + rti[:, 2:4]).T.reshape(2 * T)
        xs = dispatch(h2, pos_flat, off17, n_rows)
        ys = moe_experts(xs, w_gate, w_up, w_down, l, e_w, start_w, nsub_w)
        if l + 1 < L:
            x2, h = combine(ys, xn, rtw, pos_flat, modr, l, S, norm1_g, final=False)
        else:
            out = combine(ys, xn, rtw, pos_flat, modr, l, S, final_g, final=True)
    return out.reshape(B, S, D)
```

```python
import functools

import numpy as np
import jax
import jax.numpy as jnp
from jax import lax
from jax.experimental import pallas as pl
from jax.experimental.pallas import tpu as pltpu

F32 = jnp.float32
BF16 = jnp.bfloat16
I32 = jnp.int32

D_MODEL = 2048
ATT_HEAD_DIM = 64
ATT_Q_HEADS = 16
ATT_KV_HEADS = 4
WINDOW = 128
RET_HEADS = 8
RET_HEAD_DIM = 128
RET_CHUNK = 256
ROPE_THETA = 10000.0
N_GROUPS = 4
EXPERTS_PER_GROUP = 4
N_EXPERTS = 16
D_FF = 1024
EPS = 1e-6

ATT_Q_W = ATT_Q_HEADS * ATT_HEAD_DIM
ATT_KV_W = ATT_KV_HEADS * ATT_HEAD_DIM
RET_W = RET_HEADS * RET_HEAD_DIM
OFF_QA = 0
OFF_KA = OFF_QA + ATT_Q_W
OFF_VA = OFF_KA + ATT_KV_W
OFF_QR = OFF_VA + ATT_KV_W
OFF_KR = OFF_QR + RET_W
OFF_VR = OFF_KR + RET_W
OFF_GR = OFF_VR + RET_W
OFF_GA = OFF_GR + RET_W
OFF_GB = OFF_GA + D_MODEL
IN_W = OFF_GB + D_MODEL

LANES = 128
PROJ_TN = 512
ROW_GROUP = 8
ATTN_LOOKAHEAD = 2
COMBINE_GROUP = 64
RET_HEADS_PER_STEP = 2
MOE_SUB = 256
MOE_ROWS = 2048
MOE_FC = 256
VMEM_LIMIT = 56 * 1024 * 1024


def _pick(n, cands):
    for c in cands:
        if n % c == 0:
            return c
    raise ValueError(f"no tile in {cands} divides {n}")


def _sigmoid(x):
    return 1.0 / (1.0 + jnp.exp(-x))


def _cparams(sem, vmem=VMEM_LIMIT):
    return pltpu.CompilerParams(dimension_semantics=sem, vmem_limit_bytes=vmem)


def _rope_consts():
    def inv_freq(head_dim):
        half = head_dim // 2
        inv = ROPE_THETA ** (-2.0 * jnp.arange(half, dtype=F32) / head_dim)
        return jnp.broadcast_to(jnp.tile(inv, LANES // half), (8, LANES))
    lane = np.arange(LANES)
    sgn_att = np.where((lane % ATT_HEAD_DIM) < ATT_HEAD_DIM // 2, -1.0, 1.0)
    sgn_ret = np.where(lane < RET_HEAD_DIM // 2, -1.0, 1.0)
    sgn = np.stack([np.tile(sgn_att, (8, 1)), np.tile(sgn_ret, (8, 1))]).astype(np.float32)
    return jnp.stack([inv_freq(ATT_HEAD_DIM), inv_freq(RET_HEAD_DIM)]), jnp.asarray(sgn)


def _rope_kernel(pos_ref, inv_ref, sgn_ref, tab_ref):
    pos = pos_ref[...]
    for kind in range(2):
        ang = pos * inv_ref[kind, 0:1, :]
        tab_ref[kind, :, 0:LANES] = jnp.cos(ang)
        tab_ref[kind, :, LANES:2 * LANES] = jnp.sin(ang) * sgn_ref[kind, 0:1, :]


def rope_tables(positions):
    T = positions.size
    posb = jnp.broadcast_to(positions.reshape(T, 1).astype(F32), (T, LANES))
    inv, sgn = _rope_consts()
    tm = _pick(T, (1024, 512, 256, 128))
    return pl.pallas_call(
        _rope_kernel,
        out_shape=jax.ShapeDtypeStruct((2, T, 2 * LANES), F32),
        grid=(T // tm,),
        in_specs=[pl.BlockSpec((tm, LANES), lambda i: (i, 0)),
                  pl.BlockSpec((2, 8, LANES), lambda i: (0, 0, 0)),
                  pl.BlockSpec((2, 8, LANES), lambda i: (0, 0, 0))],
        out_specs=pl.BlockSpec((2, tm, 2 * LANES), lambda i: (0, i, 0)),
        compiler_params=_cparams(("arbitrary",)),
        name="rope_tables",
    )(posb, inv, sgn)


def _adaln_kernel(c_ref, w_ref, b_ref, o_ref):
    c = c_ref[...]
    ca = c * _sigmoid(c)
    hi = ca.astype(BF16).astype(F32)
    lhs = jnp.concatenate([hi, ca - hi], axis=0).astype(BF16)
    r = jnp.dot(lhs, w_ref[...].astype(BF16), preferred_element_type=F32)
    o_ref[...] = r[0:8] + r[8:16] + b_ref[...]


def adaln_mod(c, w_ada, b_ada):
    L, D, N = w_ada.shape
    B = c.shape[0]
    cp = jnp.pad(c, ((0, 8 - B), (0, 0)))
    tn = _pick(N, (1024, 512))
    return pl.pallas_call(
        _adaln_kernel,
        out_shape=jax.ShapeDtypeStruct((L, 8, N), F32),
        grid=(L, N // tn),
        in_specs=[pl.BlockSpec((8, D), lambda l, j: (0, 0)),
                  pl.BlockSpec((None, D, tn), lambda l, j: (l, 0, j)),
                  pl.BlockSpec((None, 1, tn), lambda l, j: (l, 0, j))],
        out_specs=pl.BlockSpec((None, 8, tn), lambda l, j: (l, 0, j)),
        compiler_params=_cparams(("arbitrary", "arbitrary")),
        name="adaln_mod",
    )(cp, w_ada, b_ada.reshape(L, 1, N))


def _norm_mod(x, g, shift, scale):
    ms = jnp.mean(x * x, axis=-1, keepdims=True)
    return (x * lax.rsqrt(ms + EPS) * g) * (1.0 + scale) + shift


def _norm_mod_kernel(x_ref, g_ref, sh_ref, sc_ref, h_ref):
    h_ref[...] = _norm_mod(x_ref[...], g_ref[...], sh_ref[...], sc_ref[...]).astype(BF16)


def _mod_spec(l, k, S, tm):
    return pl.BlockSpec((None, None, None, 1, D_MODEL), lambda i, *_: (l, k, (i * tm) // S, 0, 0))


def norm_modulate(x2, g, modr, l, S):
    T, D = x2.shape
    tm = _pick(S, (512, 256, 128))
    return pl.pallas_call(
        _norm_mod_kernel,
        out_shape=jax.ShapeDtypeStruct((T, D), BF16),
        grid=(T // tm,),
        in_specs=[pl.BlockSpec((tm, D), lambda i: (i, 0)),
                  pl.BlockSpec((None, 1, D), lambda i: (l, 0, 0)),
                  _mod_spec(l, 0, S, tm), _mod_spec(l, 1, S, tm)],
        out_specs=pl.BlockSpec((tm, D), lambda i: (i, 0)),
        compiler_params=_cparams(("arbitrary",)),
        name="norm1_modulate",
    )(x2, g, modr, modr)


def _rope_att(blk, cos, sin, scale):
    lane = lax.broadcasted_iota(I32, blk.shape, 1)
    rot = jnp.where((lane % 64) < 32, pltpu.roll(blk, 96, 1), pltpu.roll(blk, 32, 1))
    out = blk * cos + rot * sin
    return out * scale if scale != 1.0 else out


def _rope_ret(blk, cos, sin, scale):
    out = blk * cos + pltpu.roll(blk, 64, 1) * sin
    return out * scale if scale != 1.0 else out


def _inproj_kernel(h_ref, w_ref, tab_ref, o_ref, wbf_ref):
    j = pl.program_id(0)

    @pl.when(pl.program_id(1) == 0)
    def _():
        wbf_ref[...] = w_ref[...].astype(BF16)

    ngrp = PROJ_TN // LANES

    def run(epilogue):
        acc = jnp.dot(h_ref[...], wbf_ref[...], preferred_element_type=F32)
        cos = tab_ref[:, 0:LANES]
        sin = tab_ref[:, LANES:2 * LANES]
        for g in range(ngrp):
            blk = acc[:, g * LANES:(g + 1) * LANES]
            o_ref[:, g * LANES:(g + 1) * LANES] = epilogue(g, blk, cos, sin).astype(BF16)

    t = lambda off: off // PROJ_TN
    q_scale = float(ATT_HEAD_DIM) ** -0.5
    k_scale = float(RET_HEAD_DIM) ** -0.5

    @pl.when(j < t(OFF_KA))
    def _():
        run(lambda g, b, c, s: _rope_att(b, c, s, q_scale))

    @pl.when(j == t(OFF_KA))
    def _():
        run(lambda g, b, c, s: _rope_att(b, c, s, 1.0) if g < ATT_KV_W // LANES else b)

    @pl.when((j >= t(OFF_QR)) & (j < t(OFF_KR)))
    def _():
        run(lambda g, b, c, s: _rope_ret(b, c, s, 1.0))

    @pl.when((j >= t(OFF_KR)) & (j < t(OFF_VR)))
    def _():
        run(lambda g, b, c, s: _rope_ret(b, c, s, k_scale))

    @pl.when((j >= t(OFF_VR)) & (j < t(OFF_GR)))
    def _():
        run(lambda g, b, c, s: b)

    @pl.when((j >= t(OFF_GR)) & (j < t(OFF_GA)))
    def _():
        run(lambda g, b, c, s: b * _sigmoid(b))

    @pl.when(j >= t(OFF_GA))
    def _():
        run(lambda g, b, c, s: _sigmoid(b))


def in_projection(h, w_in, l, tab):
    T, D = h.shape
    tm = _pick(T, (1024, 768, 512, 256))
    nj = IN_W // PROJ_TN
    t_qr, t_vr = OFF_QR // PROJ_TN, OFF_VR // PROJ_TN

    def tab_map(j, i):
        kind = jnp.where(j < t_qr, 0, 1)
        return (kind, jnp.where(j < t_vr, i, 0), 0)

    return pl.pallas_call(
        _inproj_kernel,
        out_shape=jax.ShapeDtypeStruct((T, IN_W), BF16),
        grid=(nj, T // tm),
        in_specs=[pl.BlockSpec((tm, D), lambda j, i: (i, 0)),
                  pl.BlockSpec((None, D, PROJ_TN), lambda j, i: (l, 0, j)),
                  pl.BlockSpec((None, tm, 2 * LANES), tab_map)],
        out_specs=pl.BlockSpec((tm, PROJ_TN), lambda j, i: (i, j)),
        scratch_shapes=[pltpu.VMEM((D, PROJ_TN), BF16)],
        compiler_params=_cparams(("arbitrary", "arbitrary")),
        name="in_projection",
    )(h, w_in, tab)


def _attn_kernel(sinks_ref, q_ref, kvc_ref, kvp_ref, o_ref, *, tq):
    W = WINDOW
    nsub = tq // W
    is_first = pl.program_id(1) == 0
    qi = lax.broadcasted_iota(I32, (W, 2 * W), 0)
    kj = lax.broadcasted_iota(I32, (W, 2 * W), 1)
    rel = qi + W - kj
    band = (rel >= 0) & (rel < WINDOW)
    band0 = band & (jnp.logical_not(is_first) | (kj >= W))
    lane = lax.broadcasted_iota(I32, (tq + W, LANES), 1)
    lo = lane < ATT_HEAD_DIM

    def split_pair(raw, parity):
        x = raw.astype(F32)
        xr = pltpu.roll(x, ATT_HEAD_DIM, 1)
        if parity == 0:
            return jnp.where(lo, x, 0.0), jnp.where(lo, 0.0, xr)
        return jnp.where(lo, xr, 0.0), jnp.where(lo, 0.0, x)

    units = []
    for hk in range(ATT_KV_HEADS):
        grp, par = hk // 2, hk % 2
        kc = slice(grp * LANES, (grp + 1) * LANES)
        vc = slice(ATT_KV_W + grp * LANES, ATT_KV_W + (grp + 1) * LANES)
        k_pair = split_pair(jnp.concatenate([kvp_ref[:, kc], kvc_ref[:, kc]], axis=0), par)
        v_pair = split_pair(jnp.concatenate([kvp_ref[:, vc], kvc_ref[:, vc]], axis=0), par)
        kt_pair = [k.T.astype(BF16) for k in k_pair]
        v_pair = [v.astype(BF16) for v in v_pair]
        qcols = [slice(hk * 4 * ATT_HEAD_DIM + jq * LANES, hk * 4 * ATT_HEAD_DIM + (jq + 1) * LANES)
                 for jq in range(2)]
        for m in range(nsub):
            units.append((hk, m, kt_pair, v_pair, qcols))

    def scores(unit):
        hk, m, kt_pair, v_pair, qcols = unit
        rows = slice(m * W, (m + 1) * W)
        keys = slice(m * W, m * W + 2 * W)
        kcat = jnp.concatenate([kt_pair[0][:, keys], kt_pair[1][:, keys]], axis=1)
        q = jnp.concatenate([q_ref[rows, qcols[0]], q_ref[rows, qcols[1]]], axis=0)
        return jnp.dot(q, kcat, preferred_element_type=F32)

    def finish(unit, s_all):
        hk, m, kt_pair, v_pair, qcols = unit
        rows = slice(m * W, (m + 1) * W)
        keys = slice(m * W, m * W + 2 * W)
        mask = band0 if m == 0 else band
        vcat = jnp.concatenate([v_pair[0][keys], v_pair[1][keys]], axis=0)
        p_rows = []
        for jq in range(2):
            p_cols = []
            for e in range(2):
                s = s_all[jq * W:(jq + 1) * W, e * 2 * W:(e + 1) * 2 * W]
                s = jnp.where(mask, s, -1e30)
                sink = sinks_ref[hk * 4 + jq * 2 + e]
                mx = jnp.maximum(jnp.max(s, axis=1, keepdims=True), sink)
                p = jnp.exp(s - mx)
                den = jnp.sum(p, axis=1, keepdims=True) + jnp.exp(sink - mx)
                p_cols.append((p * (1.0 / den)).astype(BF16))
            p_rows.append(jnp.concatenate(p_cols, axis=1))
        o = jnp.dot(jnp.concatenate(p_rows, axis=0), vcat, preferred_element_type=F32)
        o_ref[rows, qcols[0]] = o[0:W].astype(BF16)
        o_ref[rows, qcols[1]] = o[W:2 * W].astype(BF16)

    pending = [scores(u) for u in units[:ATTN_LOOKAHEAD]]
    for idx, unit in enumerate(units):
        if idx + ATTN_LOOKAHEAD < len(units):
            pending.append(scores(units[idx + ATTN_LOOKAHEAD]))
        finish(unit, pending.pop(0))


def swa_attention(proj, sinks, B, S):
    T = proj.shape[0]
    tq = _pick(S, (512, 384, 256, 128))
    nq = S // tq
    kvw = 2 * ATT_KV_W
    kv_blk = OFF_KA // kvw

    def prev_map(b, i):
        return (jnp.maximum(b * (S // WINDOW) + i * (tq // WINDOW) - 1, 0), kv_blk)

    return pl.pallas_call(
        functools.partial(_attn_kernel, tq=tq),
        out_shape=jax.ShapeDtypeStruct((T, ATT_Q_W), BF16),
        grid=(B, nq),
        in_specs=[pl.BlockSpec(memory_space=pltpu.SMEM),
                  pl.BlockSpec((tq, ATT_Q_W), lambda b, i: (b * nq + i, 0)),
                  pl.BlockSpec((tq, kvw), lambda b, i: (b * nq + i, kv_blk)),
                  pl.BlockSpec((WINDOW, kvw), prev_map)],
        out_specs=pl.BlockSpec((tq, ATT_Q_W), lambda b, i: (b * nq + i, 0)),
        compiler_params=_cparams(("arbitrary", "arbitrary")),
        name="swa_attention",
    )(sinks, proj, proj, proj)


def _ret_consts():
    C = RET_CHUNK
    log_g = jnp.log1p(-jnp.exp2(-5.0 - jnp.arange(RET_HEADS, dtype=F32)))
    i = jnp.arange(C, dtype=F32)
    diff = i[:, None] - i[None, :]
    dm = jnp.where(diff[None] >= 0, jnp.exp(jnp.maximum(diff, 0.0)[None] * log_g[:, None, None]), 0.0)
    qd = jnp.exp((i + 1.0)[None, :] * log_g[:, None])
    kd = jnp.exp((C - 1.0 - i)[None, :] * log_g[:, None])
    cd = jnp.exp(C * log_g)
    bc = lambda v: jnp.broadcast_to(v[:, :, None], (RET_HEADS, C, LANES))
    return dm, bc(qd), bc(kd), jnp.broadcast_to(cd[:, None, None], (RET_HEADS, 8, LANES))


def _ret_kernel(q_ref, k_ref, v_ref, g_ref, gn_ref, dm_ref, qd_ref, kd_ref, cd_ref, o_ref, *, nchunk):
    C = RET_CHUNK
    heads = range(RET_HEADS_PER_STEP)
    lanes = [slice(hh * LANES, (hh + 1) * LANES) for hh in heads]
    rows = [slice(c * C, (c + 1) * C) for c in range(nchunk)]

    def independent(c, hh):
        q, k, v = q_ref[rows[c], lanes[hh]], k_ref[rows[c], lanes[hh]], v_ref[rows[c], lanes[hh]]
        s = lax.dot_general(q, k, (((1,), (1,)), ((), ())), preferred_element_type=F32) * dm_ref[hh]
        vk = (v.astype(F32) * kd_ref[hh]).astype(BF16)
        kv = lax.dot_general(k, vk, (((0,), (0,)), ((), ())), preferred_element_type=F32)
        return s.astype(BF16), kv

    def finish(c, hh, s, state):
        q, v = q_ref[rows[c], lanes[hh]], v_ref[rows[c], lanes[hh]]
        y = jnp.dot(s, v, preferred_element_type=F32)
        y = y + jnp.dot(q, state.astype(BF16), preferred_element_type=F32) * qd_ref[hh]
        ms = jnp.mean(y * y, axis=-1, keepdims=True)
        yn = y * lax.rsqrt(ms + EPS) * gn_ref[:, lanes[hh]]
        o_ref[rows[c], lanes[hh]] = (g_ref[rows[c], lanes[hh]].astype(F32) * yn).astype(BF16)

    state = [jnp.zeros((RET_HEAD_DIM, RET_HEAD_DIM), F32) for _ in heads]
    ahead = [independent(0, hh) for hh in heads]
    for c in range(nchunk):
        cur = ahead
        if c + 1 < nchunk:
            ahead = [independent(c + 1, hh) for hh in heads]
        for hh in heads:
            s, kv = cur[hh]
            finish(c, hh, s, state[hh])
            state[hh] = cd_ref[hh, 0:1, :] * state[hh] + kv


def retention(proj, ret_norm_g, l, B, S):
    T = proj.shape[0]
    dm, qd, kd, cd = _ret_consts()
    hp = RET_HEADS_PER_STEP
    wid = hp * LANES
    blk = lambda off: (lambda b, h: (b, off // wid + h))
    hconst = lambda rows: pl.BlockSpec((hp, rows, LANES), lambda b, h: (h, 0, 0))
    return pl.pallas_call(
        functools.partial(_ret_kernel, nchunk=S // RET_CHUNK),
        out_shape=jax.ShapeDtypeStruct((T, RET_W), BF16),
        grid=(B, RET_HEADS // hp),
        in_specs=[pl.BlockSpec((S, wid), blk(OFF_QR)),
                  pl.BlockSpec((S, wid), blk(OFF_KR)),
                  pl.BlockSpec((S, wid), blk(OFF_VR)),
                  pl.BlockSpec((S, wid), blk(OFF_GR)),
                  pl.BlockSpec((None, 1, wid), lambda b, h: (l, 0, h)),
                  pl.BlockSpec((hp, RET_CHUNK, RET_CHUNK), lambda b, h: (h, 0, 0)),
                  hconst(RET_CHUNK), hconst(RET_CHUNK), hconst(8)],
        out_specs=pl.BlockSpec((S, wid), lambda b, h: (b, h)),
        compiler_params=_cparams(("arbitrary", "arbitrary")),
        name="retention",
    )(proj, proj, proj, proj, ret_norm_g.reshape(-1, 1, RET_W), dm, qd, kd, cd)


def _load_cast(w_hbm, dst, stage, sem):
    rows = stage.shape[0]
    n = w_hbm.shape[0] // rows

    def body(i, carry):
        r0 = pl.multiple_of(i * rows, rows)
        cp = pltpu.make_async_copy(w_hbm.at[pl.ds(r0, rows), :], stage, sem)
        cp.start()
        cp.wait()
        dst[pl.ds(r0, rows), :] = stage[...].astype(BF16)
        return carry
    lax.fori_loop(0, n, body, 0)


def _mixer_out_kernel(att_ref, ret_ref, ga0, ga1, ga2, ga3, gb0, gb1, gb2, gb3, x_ref,
                      wa_hbm, wr_hbm, wo_hbm, gate1_ref, g2_ref, sh2_ref, sc2_ref,
                      wrt_ref, brt_ref, tril_ref,
                      xn_ref, h2_ref, rti_ref, rtw_ref, cnt_ref,
                      wa, wr, wo, wrt_hi, wrt_lo, mrg, carry, stage, sem, *, tm, l):
    i = pl.program_id(0)
    D = D_MODEL

    @pl.when(i == 0)
    def _():
        _load_cast(wa_hbm.at[l], wa, stage, sem)
        _load_cast(wr_hbm.at[l], wr, stage, sem)
        _load_cast(wo_hbm.at[l], wo, stage, sem)
        w = wrt_ref[...]
        hi = w.astype(BF16)
        wrt_hi[...] = hi
        wrt_lo[...] = (w - hi.astype(F32)).astype(BF16)
        carry[...] = jnp.zeros_like(carry)

    a = att_ref[...]
    r = ret_ref[...]
    ga = (ga0, ga1, ga2, ga3)
    gb = (gb0, gb1, gb2, gb3)
    for n in range(D // PROJ_TN):
        cols = slice(n * PROJ_TN, (n + 1) * PROJ_TN)
        ya = jnp.dot(a, wa[:, cols], preferred_element_type=F32)
        yr = jnp.dot(r, wr[:, cols], preferred_element_type=F32)
        mrg[:, cols] = (ga[n][...].astype(F32) * ya + gb[n][...].astype(F32) * yr).astype(BF16)
    o = jnp.dot(mrg[...], wo[...], preferred_element_type=F32)
    xn = x_ref[...] + gate1_ref[...] * o
    xn_ref[...] = xn

    h2 = _norm_mod(xn, g2_ref[...], sh2_ref[...], sc2_ref[...])
    h2b = h2.astype(BF16)
    h2_ref[...] = h2b

    h2l = (h2 - h2b.astype(F32)).astype(BF16)
    logits = (jnp.dot(h2b, wrt_hi[...], preferred_element_type=F32)
              + jnp.dot(h2l, wrt_hi[...], preferred_element_type=F32)
              + jnp.dot(h2b, wrt_lo[...], preferred_element_type=F32)) + brt_ref[...]
    lane = lax.broadcasted_iota(I32, (tm, LANES), 1)
    valid = lane < N_EXPERTS
    mx = jnp.max(logits, axis=-1, keepdims=True)
    p = jnp.where(valid, jnp.exp(logits - mx), 0.0)
    pos_in_grp = lane % EXPERTS_PER_GROUP
    grp_of = lane // EXPERTS_PER_GROUP

    def member(k):
        wrapped = pos_in_grp + k >= EXPERTS_PER_GROUP
        return jnp.where(wrapped, pltpu.roll(p, EXPERTS_PER_GROUP - k, 1), pltpu.roll(p, LANES - k, 1)), wrapped

    (b1, w1), (b2, w2), (b3, w3) = member(1), member(2), member(3)
    m_ab, n_ab = jnp.maximum(p, b1), jnp.minimum(p, b1)
    m_cd, n_cd = jnp.maximum(b2, b3), jnp.minimum(b2, b3)
    gscore = jnp.maximum(m_ab, m_cd) + jnp.maximum(jnp.minimum(m_ab, m_cd), jnp.maximum(n_ab, n_cd))
    gscore = jnp.where(valid, gscore, -1.0)
    gmax = jnp.max(gscore, axis=-1, keepdims=True)
    gsel = jnp.min(jnp.where(gscore == gmax, grp_of, N_GROUPS), axis=-1, keepdims=True)
    in_sel = grp_of == gsel
    beats = lambda b, w: ((b > p) | ((b == p) & w)).astype(I32)
    rank_in_grp = beats(b1, w1) + beats(b2, w2) + beats(b3, w3)
    sel0 = in_sel & (rank_in_grp == 0)
    sel1 = in_sel & (rank_in_grp == 1)
    lsum = lambda m, v: jnp.sum(jnp.where(m, v, 0.0), axis=-1, keepdims=True)
    v0, v1 = lsum(sel0, p), lsum(sel1, p)
    lanef = lane.astype(F32)
    e0, e1 = lsum(sel0, lanef), lsum(sel1, lanef)
    inv = 1.0 / (v0 + v1)
    onehot = jnp.where(sel0 | sel1, 1.0, 0.0)
    prefix = jnp.dot(tril_ref[...], onehot.astype(BF16), preferred_element_type=F32) + carry[0:1, :]
    r0, r1 = lsum(sel0, prefix), lsum(sel1, prefix)
    carry[...] = carry[...] + jnp.sum(onehot, axis=0, keepdims=True)
    cnt_ref[...] = carry[...].astype(I32)
    sel4 = lambda a0, a1, a2, a3: jnp.where(lane == 0, a0, jnp.where(lane == 1, a1, jnp.where(lane == 2, a2, a3)))
    rti_ref[...] = sel4(e0, e1, r0, r1).astype(I32)
    rtw_ref[...] = jnp.where(lane == 0, v0 * inv, jnp.where(lane == 1, v1 * inv, 0.0))


def mixer_out(att, ret, proj, x2, w_attn_out, w_ret_out, w_o, modr, norm2_g, w_router, b_router, l, S):
    T, D = x2.shape
    tm = 256
    assert S % tm == 0
    nga, ngb = OFF_GA // PROJ_TN, OFF_GB // PROJ_TN
    gate_spec = lambda blk: pl.BlockSpec((tm, PROJ_TN), lambda i: (i, blk))
    row = lambda: pl.BlockSpec((tm, D), lambda i: (i, 0))
    wrt = jnp.pad(w_router, ((0, 0), (0, LANES - N_EXPERTS)))
    brt = jnp.pad(b_router.astype(F32), (0, LANES - N_EXPERTS), constant_values=-1e30).reshape(1, LANES)
    tril = jnp.asarray(np.tril(np.ones((tm, tm), np.float32), -1), BF16)
    any_spec = pl.BlockSpec(memory_space=pl.ANY)
    outs = pl.pallas_call(
        functools.partial(_mixer_out_kernel, tm=tm, l=l),
        out_shape=(jax.ShapeDtypeStruct((T, D), F32),
                   jax.ShapeDtypeStruct((T, D), BF16),
                   jax.ShapeDtypeStruct((T, LANES), I32),
                   jax.ShapeDtypeStruct((T, LANES), F32),
                   jax.ShapeDtypeStruct((8, LANES), I32)),
        grid=(T // tm,),
        in_specs=[pl.BlockSpec((tm, ATT_Q_W), lambda i: (i, 0)),
                  pl.BlockSpec((tm, RET_W), lambda i: (i, 0)),
                  *[gate_spec(nga + n) for n in range(4)],
                  *[gate_spec(ngb + n) for n in range(4)],
                  row(), any_spec, any_spec, any_spec,
                  _mod_spec(l, 2, S, tm),
                  pl.BlockSpec((None, 1, D), lambda i: (l, 0, 0)),
                  _mod_spec(l, 3, S, tm), _mod_spec(l, 4, S, tm),
                  pl.BlockSpec((D, LANES), lambda i: (0, 0)),
                  pl.BlockSpec((1, LANES), lambda i: (0, 0)),
                  pl.BlockSpec((tm, tm), lambda i: (0, 0))],
        out_specs=(row(),
                   row(),
                   pl.BlockSpec((tm, LANES), lambda i: (i, 0)),
                   pl.BlockSpec((tm, LANES), lambda i: (i, 0)),
                   pl.BlockSpec((8, LANES), lambda i: (0, 0))),
        scratch_shapes=[pltpu.VMEM((ATT_Q_W, D), BF16), pltpu.VMEM((RET_W, D), BF16),
                        pltpu.VMEM((D, D), BF16),
                        pltpu.VMEM((D, LANES), BF16), pltpu.VMEM((D, LANES), BF16),
                        pltpu.VMEM((tm, D), BF16), pltpu.VMEM((8, LANES), F32),
                        pltpu.VMEM((256, D), F32), pltpu.SemaphoreType.DMA],
        compiler_params=_cparams(("arbitrary",)),
        name="mixer_out",
    )(att, ret, *([proj] * 8), x2, w_attn_out, w_ret_out, w_o,
      modr, norm2_g.reshape(-1, 1, D), modr, modr, wrt, brt, tril)
    return outs


def moe_plan(counts, n_work):
    sub_per = MOE_ROWS // MOE_SUB
    seg = ((counts + MOE_SUB - 1) // MOE_SUB) * MOE_SUB
    off = jnp.cumsum(seg) - seg
    off17 = jnp.concatenate([off, off[-1:] + seg[-1:]]).astype(I32)
    nb = (counts + MOE_ROWS - 1) // MOE_ROWS
    cum = jnp.cumsum(nb)
    total = cum[-1]
    w = jnp.arange(n_work, dtype=I32)
    wc = jnp.minimum(w, total - 1)
    e_w = jnp.sum((cum[None, :] <= wc[:, None]).astype(I32), axis=1)
    blk = wc - (cum[e_w] - nb[e_w])
    start = off[e_w] + blk * MOE_ROWS
    nsub = jnp.clip(seg[e_w] // MOE_SUB - blk * sub_per, 0, sub_per)
    nsub = jnp.where(w < total, nsub, 0)
    return off17, e_w, start.astype(I32), nsub.astype(I32)


def _row_dma_loops(n_rows, make_copy):
    def run(op):
        def body(g, carry):
            r0 = pl.multiple_of(g * ROW_GROUP, ROW_GROUP)
            for u in range(ROW_GROUP):
                for k in range(2):
                    getattr(make_copy(r0 + u, k), op)()
            return carry
        lax.fori_loop(0, n_rows // ROW_GROUP, body, 0)
    run("start")
    run("wait")


def _dispatch_kernel(pos_ref, off_ref, h_ref, xs_ref, rows_ref, sem, *, td, T):
    i = pl.program_id(0)
    zero_ref = rows_ref

    @pl.when(i == 0)
    def _():
        zero_ref[...] = jnp.zeros_like(zero_ref)
        used = off_ref[N_EXPERTS]

        def zero_copy(row):
            row = pl.multiple_of(row, MOE_SUB)
            return pltpu.make_async_copy(zero_ref, xs_ref.at[pl.ds(row, MOE_SUB), :], sem)

        def fill(op):
            def seg_tail(e, carry):
                @pl.when(off_ref[e + 1] > off_ref[e])
                def _():
                    getattr(zero_copy(off_ref[e + 1] - MOE_SUB), op)()
                return carry

            def buf_tail(n, carry):
                getattr(zero_copy(used + n * MOE_SUB), op)()
                return carry
            lax.fori_loop(0, N_EXPERTS, seg_tail, 0)
            lax.fori_loop(0, (xs_ref.shape[0] - used) // MOE_SUB, buf_tail, 0)
        fill("start")
        fill("wait")

    rows_ref[...] = h_ref[...].astype(F32)

    def copy(r, k):
        p = pos_ref[k * T + i * td + r]
        return pltpu.make_async_copy(rows_ref.at[pl.ds(r, 1), :], xs_ref.at[pl.ds(p, 1), :], sem)
    _row_dma_loops(td, copy)


def dispatch(h2, pos_flat, off17, n_rows):
    T, D = h2.shape
    td = MOE_SUB
    return pl.pallas_call(
        functools.partial(_dispatch_kernel, td=td, T=T),
        out_shape=jax.ShapeDtypeStruct((n_rows, D), F32),
        grid_spec=pltpu.PrefetchScalarGridSpec(
            num_scalar_prefetch=2, grid=(T // td,),
            in_specs=[pl.BlockSpec((td, D), lambda i, *_: (i, 0))],
            out_specs=pl.BlockSpec(memory_space=pl.ANY),
            scratch_shapes=[pltpu.VMEM((td, D), F32), pltpu.SemaphoreType.DMA]),
        compiler_params=_cparams(("arbitrary",)),
        name="moe_dispatch",
    )(pos_flat, off17, h2)


def _moe_kernel(we_ref, ws_ref, wn_ref, xs_ref, wg_ref, wu_ref, wd_ref, ys_ref,
                xb, acc, wgb, wub, wdb, stage, ostage, sem_in, sem_out, *, nfc, n_work):
    w = pl.program_id(0)
    c = pl.program_id(1)
    n = wn_ref[w]
    s0 = ws_ref[w]

    def in_copy(start_row, i, slot):
        g0 = pl.multiple_of(start_row + i * MOE_SUB, MOE_SUB)
        return pltpu.make_async_copy(xs_ref.at[pl.ds(g0, MOE_SUB), :], stage.at[slot], sem_in.at[slot])

    def out_copy(i, slot):
        g0 = pl.multiple_of(s0 + i * MOE_SUB, MOE_SUB)
        return pltpu.make_async_copy(ostage.at[slot], ys_ref.at[pl.ds(g0, MOE_SUB), :], sem_out.at[slot])

    def request_first_two(start_row, count):
        @pl.when(count > 0)
        def _():
            in_copy(start_row, 0, 0).start()

        @pl.when(count > 1)
        def _():
            in_copy(start_row, 1, 1).start()

    @pl.when((w == 0) & (c == 0))
    def _():
        request_first_two(s0, n)

    def for_subs(fn):
        def body(p, carry):
            fn(p * 2, 2)
            return carry
        lax.fori_loop(0, n // 2, body, 0)

        @pl.when(lax.rem(n, 2) == 1)
        def _():
            fn(n - 1, 1)

    def rows_of(i):
        return pl.ds(pl.multiple_of(i * MOE_SUB, MOE_SUB), MOE_SUB)

    def ffn(i0, count):
        xs_ = [xb[rows_of(i0 + a), :] for a in range(count)]
        gu = [(jnp.dot(x, wgb[...], preferred_element_type=F32), jnp.dot(x, wub[...], preferred_element_type=F32))
              for x in xs_]
        acts = [(g * _sigmoid(g) * u).astype(BF16) for g, u in gu]
        return [jnp.dot(a, wdb[...], preferred_element_type=F32) for a in acts]

    @pl.when(n > 0)
    def _():
        wgb[...] = wg_ref[...].astype(BF16)
        wub[...] = wu_ref[...].astype(BF16)
        wdb[...] = wd_ref[...].astype(BF16)

        @pl.when(c == 0)
        def _():
            def first(i0, count):
                for a in range(count):
                    in_copy(s0, i0 + a, a).wait()
                    xb[rows_of(i0 + a), :] = stage[a].astype(BF16)
                for a in range(count):
                    @pl.when(i0 + a + 2 < n)
                    def _():
                        in_copy(s0, i0 + a + 2, a).start()
                for a, y in enumerate(ffn(i0, count)):
                    acc[rows_of(i0 + a), :] = y
            for_subs(first)

        @pl.when((c > 0) & (c < nfc - 1))
        def _():
            def middle(i0, count):
                for a, y in enumerate(ffn(i0, count)):
                    acc[rows_of(i0 + a), :] += y
            for_subs(middle)

        @pl.when(c == nfc - 1)
        def _():
            def last(i0, count):
                for a in range(count):
                    @pl.when(i0 + a >= 2)
                    def _():
                        out_copy(i0 + a - 2, a).wait()
                for a, y in enumerate(ffn(i0, count)):
                    ostage[a] = acc[rows_of(i0 + a), :] + y
                    out_copy(i0 + a, a).start()
            for_subs(last)

            @pl.when(n >= 2)
            def _():
                out_copy(n - 2, lax.rem(n, 2)).wait()
            out_copy(n - 1, lax.rem(n - 1, 2)).wait()

            @pl.when(w + 1 < n_work)
            def _():
                nxt = jnp.minimum(w + 1, n_work - 1)
                request_first_two(ws_ref[nxt], wn_ref[nxt])


def moe_experts(xs, w_gate, w_up, w_down, l, e_w, start_w, nsub_w):
    n_rows, D = xs.shape
    nfc = D_FF // MOE_FC
    assert nfc >= 2
    n_work = e_w.shape[0]

    def chunk(c, wn, w):
        return jnp.where(wn[w] > 0, c, nfc - 1)

    return pl.pallas_call(
        functools.partial(_moe_kernel, nfc=nfc, n_work=n_work),
        out_shape=jax.ShapeDtypeStruct((n_rows, D), F32),
        grid_spec=pltpu.PrefetchScalarGridSpec(
            num_scalar_prefetch=3, grid=(n_work, nfc),
            in_specs=[pl.BlockSpec(memory_space=pl.ANY),
                      pl.BlockSpec((None, None, D, MOE_FC), lambda w, c, we, ws, wn: (l, we[w], 0, chunk(c, wn, w))),
                      pl.BlockSpec((None, None, D, MOE_FC), lambda w, c, we, ws, wn: (l, we[w], 0, chunk(c, wn, w))),
                      pl.BlockSpec((None, None, MOE_FC, D), lambda w, c, we, ws, wn: (l, we[w], chunk(c, wn, w), 0))],
            out_specs=pl.BlockSpec(memory_space=pl.ANY),
            scratch_shapes=[pltpu.VMEM((MOE_ROWS, D), BF16), pltpu.VMEM((MOE_ROWS, D), F32),
                            pltpu.VMEM((D, MOE_FC), BF16), pltpu.VMEM((D, MOE_FC), BF16),
                            pltpu.VMEM((MOE_FC, D), BF16),
                            pltpu.VMEM((2, MOE_SUB, D), F32), pltpu.VMEM((2, MOE_SUB, D), F32),
                            pltpu.SemaphoreType.DMA((2,)), pltpu.SemaphoreType.DMA((2,))]),
        input_output_aliases={3: 0},
        compiler_params=_cparams(("arbitrary", "arbitrary")),
        name="moe_experts",
    )(e_w, start_w, nsub_w, xs, w_gate, w_up, w_down)


def _combine_kernel(pos_ref, ys_ref, x_ref, rtw_ref, gate2_ref, g_ref, *rest, tc, T, final):
    if final:
        out_ref, ybuf, sem = rest
    else:
        sh_ref, sc_ref, xo_ref, h_ref, ybuf, sem = rest
    i = pl.program_id(0)
    slot = lax.rem(i, 2)
    ngroups = tc // COMBINE_GROUP

    def copy(tile, r, k, sl):
        p = pos_ref[k * T + tile * tc + r]
        return pltpu.make_async_copy(ys_ref.at[pl.ds(p, 1), :], ybuf.at[sl, k, pl.ds(r, 1), :], sem.at[sl])

    def for_rows(tile, sl, r0, op):
        for u in range(COMBINE_GROUP):
            for k in range(2):
                getattr(copy(tile, r0 + u, k, sl), op)()

    def groups(fn):
        def body(g, carry):
            fn(pl.multiple_of(g * COMBINE_GROUP, COMBINE_GROUP))
            return carry
        lax.fori_loop(0, ngroups, body, 0)

    def compute(r0):
        rows = pl.ds(r0, COMBINE_GROUP)
        moe = rtw_ref[rows, 0:1] * ybuf[slot, 0, rows, :] + rtw_ref[rows, 1:2] * ybuf[slot, 1, rows, :]
        xo = x_ref[rows, :] + gate2_ref[...] * moe
        if final:
            ms = jnp.mean(xo * xo, axis=-1, keepdims=True)
            out_ref[rows, :] = xo * lax.rsqrt(ms + EPS) * g_ref[...]
        else:
            xo_ref[rows, :] = xo
            h_ref[rows, :] = _norm_mod(xo, g_ref[...], sh_ref[...], sc_ref[...]).astype(BF16)

    @pl.when(i == 0)
    def _():
        groups(lambda r0: for_rows(0, 0, r0, "start"))
    groups(lambda r0: for_rows(i, slot, r0, "wait"))

    @pl.when(i + 1 < pl.num_programs(0))
    def _():
        def both(r0):
            compute(r0)
            for_rows(i + 1, 1 - slot, r0, "start")
        groups(both)

    @pl.when(i + 1 >= pl.num_programs(0))
    def _():
        groups(compute)


def combine(ys, xn, rtw, pos_flat, modr, l, S, g_next, final):
    T, D = xn.shape
    tc = 256
    assert S % tc == 0
    row = lambda: pl.BlockSpec((tc, D), lambda i, *_: (i, 0))
    in_specs = [pl.BlockSpec(memory_space=pl.ANY), row(),
                pl.BlockSpec((tc, LANES), lambda i, *_: (i, 0)),
                _mod_spec(l, 5, S, tc)]
    args = [ys, xn, rtw, modr]
    if final:
        in_specs.append(pl.BlockSpec((1, D), lambda i, *_: (0, 0)))
        args.append(g_next.reshape(1, D))
        out_shape = jax.ShapeDtypeStruct((T, D), F32)
        out_specs = row()
    else:
        in_specs += [pl.BlockSpec((None, 1, D), lambda i, *_: (l + 1, 0, 0)),
                     _mod_spec(l + 1, 0, S, tc), _mod_spec(l + 1, 1, S, tc)]
        args += [g_next.reshape(-1, 1, D), modr, modr]
        out_shape = (jax.ShapeDtypeStruct((T, D), F32), jax.ShapeDtypeStruct((T, D), BF16))
        out_specs = (row(), row())
    return pl.pallas_call(
        functools.partial(_combine_kernel, tc=tc, T=T, final=final),
        out_shape=out_shape,
        grid_spec=pltpu.PrefetchScalarGridSpec(
            num_scalar_prefetch=1, grid=(T // tc,),
            in_specs=in_specs, out_specs=out_specs,
            scratch_shapes=[pltpu.VMEM((2, 2, tc, D), F32), pltpu.SemaphoreType.DMA((2,))]),
        compiler_params=_cparams(("arbitrary",)),
        name="moe_combine_final" if final else "moe_combine",
    )(pos_flat, *args)


def kernel(x, c, positions, w_ada, b_ada, norm1_g, norm2_g, w_in, attn_sinks, w_attn_out, ret_norm_g,
           w_ret_out, w_o, w_router, b_router, w_gate, w_up, w_down, final_g):
    B, S, D = x.shape
    L = w_ada.shape[0]
    T = B * S
    assert D == D_MODEL and w_in.shape[-1] == IN_W and S % RET_CHUNK == 0

    tab = rope_tables(positions)
    mod = adaln_mod(c, w_ada, b_ada)
    modr = mod[:, :B].reshape(L, B, 6, D).transpose(0, 2, 1, 3).reshape(L, 6, B, 1, D)

    x2 = x.reshape(T, D)
    h = norm_modulate(x2, norm1_g.reshape(L, 1, D), modr, 0, S)
    n_work = N_EXPERTS + (2 * T + MOE_ROWS - 1) // MOE_ROWS
    n_rows = 2 * T + N_EXPERTS * MOE_SUB
    out = None
    for l in range(L):
        proj = in_projection(h, w_in, l, tab)
        att = swa_attention(proj, attn_sinks[l], B, S)
        ret = retention(proj, ret_norm_g, l, B, S)
        xn, h2, rti, rtw, cnt = mixer_out(att, ret, proj, x2, w_attn_out, w_ret_out, w_o, modr,
                                          norm2_g, w_router, b_router, l, S)
        off17, e_w, start_w, nsub_w = moe_plan(cnt[0, :N_EXPERTS], n_work)
        seg_off = jnp.sum(jnp.where(rti[:, 0:2, None] == jnp.arange(N_EXPERTS, dtype=I32),
                                    off17[:N_EXPERTS], 0), axis=-1)
        pos_flat = (seg_off + rti[:, 2:4]).T.reshape(2 * T)
        xs = dispatch(h2, pos_flat, off17, n_rows)
        ys = moe_experts(xs, w_gate, w_up, w_down, l, e_w, start_w, nsub_w)
        if l + 1 < L:
            x2, h = combine(ys, xn, rtw, pos_flat, modr, l, S, norm1_g, final=False)
        else:
            out = combine(ys, xn, rtw, pos_flat, modr, l, S, final_g, final=True)
    return out.reshape(B, S, D)
```

```python
import functools

import numpy as np
import jax
import jax.numpy as jnp
from jax import lax
from jax.experimental import pallas as pl
from jax.experimental.pallas import tpu as pltpu

F32 = jnp.float32
BF16 = jnp.bfloat16
I32 = jnp.int32

D_MODEL = 2048
ATT_HEAD_DIM = 64
ATT_Q_HEADS = 16
ATT_KV_HEADS = 4
WINDOW = 128
RET_HEADS = 8
RET_HEAD_DIM = 128
RET_CHUNK = 256
ROPE_THETA = 10000.0
N_GROUPS = 4
EXPERTS_PER_GROUP = 4
N_EXPERTS = 16
D_FF = 1024
EPS = 1e-6

ATT_Q_W = ATT_Q_HEADS * ATT_HEAD_DIM
ATT_KV_W = ATT_KV_HEADS * ATT_HEAD_DIM
RET_W = RET_HEADS * RET_HEAD_DIM
OFF_QA = 0
OFF_KA = OFF_QA + ATT_Q_W
OFF_VA = OFF_KA + ATT_KV_W
OFF_QR = OFF_VA + ATT_KV_W
OFF_KR = OFF_QR + RET_W
OFF_VR = OFF_KR + RET_W
OFF_GR = OFF_VR + RET_W
OFF_GA = OFF_GR + RET_W
OFF_GB = OFF_GA + D_MODEL
IN_W = OFF_GB + D_MODEL

LANES = 128
PROJ_TN = 512
ROW_GROUP = 8
ATTN_LOOKAHEAD = 2
COMBINE_GROUP = 64
RET_HEADS_PER_STEP = 2
MOE_SUB = 256
MOE_ROWS = 2048
MOE_FC = 256
VMEM_LIMIT = 56 * 1024 * 1024


def _pick(n, cands):
    for c in cands:
        if n % c == 0:
            return c
    raise ValueError(f"no tile in {cands} divides {n}")


def _sigmoid(x):
    return 1.0 / (1.0 + jnp.exp(-x))


def _cparams(sem, vmem=VMEM_LIMIT):
    return pltpu.CompilerParams(dimension_semantics=sem, vmem_limit_bytes=vmem)


def _rope_consts():
    def inv_freq(head_dim):
        half = head_dim // 2
        inv = ROPE_THETA ** (-2.0 * jnp.arange(half, dtype=F32) / head_dim)
        return jnp.broadcast_to(jnp.tile(inv, LANES // half), (8, LANES))
    lane = np.arange(LANES)
    sgn_att = np.where((lane % ATT_HEAD_DIM) < ATT_HEAD_DIM // 2, -1.0, 1.0)
    sgn_ret = np.where(lane < RET_HEAD_DIM // 2, -1.0, 1.0)
    sgn = np.stack([np.tile(sgn_att, (8, 1)), np.tile(sgn_ret, (8, 1))]).astype(np.float32)
    return jnp.stack([inv_freq(ATT_HEAD_DIM), inv_freq(RET_HEAD_DIM)]), jnp.asarray(sgn)


def _rope_kernel(pos_ref, inv_ref, sgn_ref, tab_ref):
    pos = pos_ref[...]
    for kind in range(2):
        ang = pos * inv_ref[kind, 0:1, :]
        tab_ref[kind, :, 0:LANES] = jnp.cos(ang)
        tab_ref[kind, :, LANES:2 * LANES] = jnp.sin(ang) * sgn_ref[kind, 0:1, :]


def rope_tables(positions):
    T = positions.size
    posb = jnp.broadcast_to(positions.reshape(T, 1).astype(F32), (T, LANES))
    inv, sgn = _rope_consts()
    tm = _pick(T, (1024, 512, 256, 128))
    return pl.pallas_call(
        _rope_kernel,
        out_shape=jax.ShapeDtypeStruct((2, T, 2 * LANES), F32),
        grid=(T // tm,),
        in_specs=[pl.BlockSpec((tm, LANES), lambda i: (i, 0)),
                  pl.BlockSpec((2, 8, LANES), lambda i: (0, 0, 0)),
                  pl.BlockSpec((2, 8, LANES), lambda i: (0, 0, 0))],
        out_specs=pl.BlockSpec((2, tm, 2 * LANES), lambda i: (0, i, 0)),
        compiler_params=_cparams(("arbitrary",)),
        name="rope_tables",
    )(posb, inv, sgn)


def _adaln_kernel(c_ref, w_ref, b_ref, o_ref):
    c = c_ref[...]
    ca = c * _sigmoid(c)
    hi = ca.astype(BF16).astype(F32)
    lhs = jnp.concatenate([hi, ca - hi], axis=0).astype(BF16)
    r = jnp.dot(lhs, w_ref[...].astype(BF16), preferred_element_type=F32)
    o_ref[...] = r[0:8] + r[8:16] + b_ref[...]


def adaln_mod(c, w_ada, b_ada):
    L, D, N = w_ada.shape
    B = c.shape[0]
    cp = jnp.pad(c, ((0, 8 - B), (0, 0)))
    tn = _pick(N, (1024, 512))
    return pl.pallas_call(
        _adaln_kernel,
        out_shape=jax.ShapeDtypeStruct((L, 8, N), F32),
        grid=(L, N // tn),
        in_specs=[pl.BlockSpec((8, D), lambda l, j: (0, 0)),
                  pl.BlockSpec((None, D, tn), lambda l, j: (l, 0, j)),
                  pl.BlockSpec((None, 1, tn), lambda l, j: (l, 0, j))],
        out_specs=pl.BlockSpec((None, 8, tn), lambda l, j: (l, 0, j)),
        compiler_params=_cparams(("arbitrary", "arbitrary")),
        name="adaln_mod",
    )(cp, w_ada, b_ada.reshape(L, 1, N))


def _norm_mod(x, g, shift, scale):
    ms = jnp.mean(x * x, axis=-1, keepdims=True)
    return (x * lax.rsqrt(ms + EPS) * g) * (1.0 + scale) + shift


def _norm_mod_kernel(x_ref, g_ref, sh_ref, sc_ref, h_ref):
    h_ref[...] = _norm_mod(x_ref[...], g_ref[...], sh_ref[...], sc_ref[...]).astype(BF16)


def _mod_spec(l, k, S, tm):
    return pl.BlockSpec((None, None, None, 1, D_MODEL), lambda i, *_: (l, k, (i * tm) // S, 0, 0))


def norm_modulate(x2, g, modr, l, S):
    T, D = x2.shape
    tm = _pick(S, (512, 256, 128))
    return pl.pallas_call(
        _norm_mod_kernel,
        out_shape=jax.ShapeDtypeStruct((T, D), BF16),
        grid=(T // tm,),
        in_specs=[pl.BlockSpec((tm, D), lambda i: (i, 0)),
                  pl.BlockSpec((None, 1, D), lambda i: (l, 0, 0)),
                  _mod_spec(l, 0, S, tm), _mod_spec(l, 1, S, tm)],
        out_specs=pl.BlockSpec((tm, D), lambda i: (i, 0)),
        compiler_params=_cparams(("arbitrary",)),
        name="norm1_modulate",
    )(x2, g, modr, modr)


def _rope_att(blk, cos, sin, scale):
    lane = lax.broadcasted_iota(I32, blk.shape, 1)
    rot = jnp.where((lane % 64) < 32, pltpu.roll(blk, 96, 1), pltpu.roll(blk, 32, 1))
    out = blk * cos + rot * sin
    return out * scale if scale != 1.0 else out


def _rope_ret(blk, cos, sin, scale):
    out = blk * cos + pltpu.roll(blk, 64, 1) * sin
    return out * scale if scale != 1.0 else out


def _inproj_kernel(h_ref, w_ref, tab_ref, o_ref, wbf_ref, acc_ref, *, ni, ntiles):
    s = pl.program_id(0)
    j = jnp.maximum(s - 1, 0) // ni

    @pl.when(s == 0)
    def _():
        acc_ref[1] = jnp.zeros(acc_ref.shape[1:], F32)

    @pl.when((lax.rem(s, ni) == 0) & (s < ntiles))
    def _():
        wbf_ref[...] = w_ref[...].astype(BF16)

    ngrp = PROJ_TN // LANES
    t = lambda off: off // PROJ_TN
    q_scale = float(ATT_HEAD_DIM) ** -0.5
    k_scale = float(RET_HEAD_DIM) ** -0.5

    def step(cur):
        def run(epilogue):
            cos = tab_ref[:, 0:LANES]
            sin = tab_ref[:, LANES:2 * LANES]
            for g in range(ngrp):
                blk = acc_ref[1 - cur, :, g * LANES:(g + 1) * LANES]
                o_ref[:, g * LANES:(g + 1) * LANES] = epilogue(g, blk, cos, sin).astype(BF16)
            acc_ref[cur] = jnp.dot(h_ref[...], wbf_ref[...], preferred_element_type=F32)

        @pl.when(j < t(OFF_KA))
        def _():
            run(lambda g, b, c, s: _rope_att(b, c, s, q_scale))

        @pl.when(j == t(OFF_KA))
        def _():
            run(lambda g, b, c, s: _rope_att(b, c, s, 1.0) if g < ATT_KV_W // LANES else b)

        @pl.when((j >= t(OFF_QR)) & (j < t(OFF_KR)))
        def _():
            run(lambda g, b, c, s: _rope_ret(b, c, s, 1.0))

        @pl.when((j >= t(OFF_KR)) & (j < t(OFF_VR)))
        def _():
            run(lambda g, b, c, s: _rope_ret(b, c, s, k_scale))

        @pl.when((j >= t(OFF_VR)) & (j < t(OFF_GR)))
        def _():
            run(lambda g, b, c, s: b)

        @pl.when((j >= t(OFF_GR)) & (j < t(OFF_GA)))
        def _():
            run(lambda g, b, c, s: b * _sigmoid(b))

        @pl.when(j >= t(OFF_GA))
        def _():
            run(lambda g, b, c, s: _sigmoid(b))

    for parity in range(2):
        pl.when(lax.rem(s, 2) == parity)(functools.partial(step, parity))


def in_projection(h, w_in, l, tab):
    T, D = h.shape
    tm = _pick(T, (1024, 768, 512, 256))
    nj, ni = IN_W // PROJ_TN, T // tm
    ntiles = nj * ni
    t_qr, t_vr = OFF_QR // PROJ_TN, OFF_VR // PROJ_TN
    mm = lambda s: jnp.minimum(s, ntiles - 1)
    ep = lambda s: jnp.maximum(s - 1, 0)

    def tab_map(s):
        j, i = ep(s) // ni, lax.rem(ep(s), ni)
        return (jnp.where(j < t_qr, 0, 1), jnp.where(j < t_vr, i, 0), 0)

    return pl.pallas_call(
        functools.partial(_inproj_kernel, ni=ni, ntiles=ntiles),
        out_shape=jax.ShapeDtypeStruct((T, IN_W), BF16),
        grid=(ntiles + 1,),
        in_specs=[pl.BlockSpec((tm, D), lambda s: (lax.rem(mm(s), ni), 0)),
                  pl.BlockSpec((None, D, PROJ_TN), lambda s: (l, 0, mm(s) // ni)),
                  pl.BlockSpec((None, tm, 2 * LANES), tab_map)],
        out_specs=pl.BlockSpec((tm, PROJ_TN), lambda s: (lax.rem(ep(s), ni), ep(s) // ni)),
        scratch_shapes=[pltpu.VMEM((D, PROJ_TN), BF16), pltpu.VMEM((2, tm, PROJ_TN), F32)],
        compiler_params=_cparams(("arbitrary",)),
        name="in_projection",
    )(h, w_in, tab)


def _attn_kernel(sinks_ref, q_ref, kvc_ref, kvp_ref, o_ref, *, tq):
    W = WINDOW
    nsub = tq // W
    is_first = pl.program_id(1) == 0
    qi = lax.broadcasted_iota(I32, (W, 2 * W), 0)
    kj = lax.broadcasted_iota(I32, (W, 2 * W), 1)
    rel = qi + W - kj
    band = (rel >= 0) & (rel < WINDOW)
    band0 = band & (jnp.logical_not(is_first) | (kj >= W))
    lane = lax.broadcasted_iota(I32, (tq + W, LANES), 1)
    lo = lane < ATT_HEAD_DIM

    def split_pair(raw, parity):
        x = raw.astype(F32)
        xr = pltpu.roll(x, ATT_HEAD_DIM, 1)
        if parity == 0:
            return jnp.where(lo, x, 0.0), jnp.where(lo, 0.0, xr)
        return jnp.where(lo, xr, 0.0), jnp.where(lo, 0.0, x)

    units = []
    for hk in range(ATT_KV_HEADS):
        grp, par = hk // 2, hk % 2
        kc = slice(grp * LANES, (grp + 1) * LANES)
        vc = slice(ATT_KV_W + grp * LANES, ATT_KV_W + (grp + 1) * LANES)
        k_pair = split_pair(jnp.concatenate([kvp_ref[:, kc], kvc_ref[:, kc]], axis=0), par)
        v_pair = split_pair(jnp.concatenate([kvp_ref[:, vc], kvc_ref[:, vc]], axis=0), par)
        kt_pair = [k.T.astype(BF16) for k in k_pair]
        v_pair = [v.astype(BF16) for v in v_pair]
        qcols = [slice(hk * 4 * ATT_HEAD_DIM + jq * LANES, hk * 4 * ATT_HEAD_DIM + (jq + 1) * LANES)
                 for jq in range(2)]
        for m in range(nsub):
            units.append((hk, m, kt_pair, v_pair, qcols))

    def scores(unit):
        hk, m, kt_pair, v_pair, qcols = unit
        rows = slice(m * W, (m + 1) * W)
        keys = slice(m * W, m * W + 2 * W)
        kcat = jnp.concatenate([kt_pair[0][:, keys], kt_pair[1][:, keys]], axis=1)
        q = jnp.concatenate([q_ref[rows, qcols[0]], q_ref[rows, qcols[1]]], axis=0)
        return jnp.dot(q, kcat, preferred_element_type=F32)

    def finish(unit, s_all):
        hk, m, kt_pair, v_pair, qcols = unit
        rows = slice(m * W, (m + 1) * W)
        keys = slice(m * W, m * W + 2 * W)
        mask = band0 if m == 0 else band
        vcat = jnp.concatenate([v_pair[0][keys], v_pair[1][keys]], axis=0)
        p_rows = []
        for jq in range(2):
            p_cols = []
            for e in range(2):
                s = s_all[jq * W:(jq + 1) * W, e * 2 * W:(e + 1) * 2 * W]
                s = jnp.where(mask, s, -1e30)
                sink = sinks_ref[hk * 4 + jq * 2 + e]
                mx = jnp.maximum(jnp.max(s, axis=1, keepdims=True), sink)
                p = jnp.exp(s - mx)
                den = jnp.sum(p, axis=1, keepdims=True) + jnp.exp(sink - mx)
                p_cols.append((p * (1.0 / den)).astype(BF16))
            p_rows.append(jnp.concatenate(p_cols, axis=1))
        o = jnp.dot(jnp.concatenate(p_rows, axis=0), vcat, preferred_element_type=F32)
        o_ref[rows, qcols[0]] = o[0:W].astype(BF16)
        o_ref[rows, qcols[1]] = o[W:2 * W].astype(BF16)

    pending = [scores(u) for u in units[:ATTN_LOOKAHEAD]]
    for idx, unit in enumerate(units):
        if idx + ATTN_LOOKAHEAD < len(units):
            pending.append(scores(units[idx + ATTN_LOOKAHEAD]))
        finish(unit, pending.pop(0))


def swa_attention(proj, sinks, B, S):
    T = proj.shape[0]
    tq = _pick(S, (512, 384, 256, 128))
    nq = S // tq
    kvw = 2 * ATT_KV_W
    kv_blk = OFF_KA // kvw

    def prev_map(b, i):
        return (jnp.maximum(b * (S // WINDOW) + i * (tq // WINDOW) - 1, 0), kv_blk)

    return pl.pallas_call(
        functools.partial(_attn_kernel, tq=tq),
        out_shape=jax.ShapeDtypeStruct((T, ATT_Q_W), BF16),
        grid=(B, nq),
        in_specs=[pl.BlockSpec(memory_space=pltpu.SMEM),
                  pl.BlockSpec((tq, ATT_Q_W), lambda b, i: (b * nq + i, 0)),
                  pl.BlockSpec((tq, kvw), lambda b, i: (b * nq + i, kv_blk)),
                  pl.BlockSpec((WINDOW, kvw), prev_map)],
        out_specs=pl.BlockSpec((tq, ATT_Q_W), lambda b, i: (b * nq + i, 0)),
        compiler_params=_cparams(("arbitrary", "arbitrary")),
        name="swa_attention",
    )(sinks, proj, proj, proj)


def _ret_consts():
    C = RET_CHUNK
    log_g = jnp.log1p(-jnp.exp2(-5.0 - jnp.arange(RET_HEADS, dtype=F32)))
    i = jnp.arange(C, dtype=F32)
    diff = i[:, None] - i[None, :]
    dm = jnp.where(diff[None] >= 0, jnp.exp(jnp.maximum(diff, 0.0)[None] * log_g[:, None, None]), 0.0)
    qd = jnp.exp((i + 1.0)[None, :] * log_g[:, None])
    kd = jnp.exp((C - 1.0 - i)[None, :] * log_g[:, None])
    cd = jnp.exp(C * log_g)
    bc = lambda v: jnp.broadcast_to(v[:, :, None], (RET_HEADS, C, LANES))
    return dm, bc(qd), bc(kd), jnp.broadcast_to(cd[:, None, None], (RET_HEADS, 8, LANES))


def _ret_kernel(q_ref, k_ref, v_ref, g_ref, gn_ref, dm_ref, qd_ref, kd_ref, cd_ref, o_ref, *, nchunk):
    C = RET_CHUNK
    heads = range(RET_HEADS_PER_STEP)
    lanes = [slice(hh * LANES, (hh + 1) * LANES) for hh in heads]
    rows = [slice(c * C, (c + 1) * C) for c in range(nchunk)]

    def independent(c, hh):
        q, k, v = q_ref[rows[c], lanes[hh]], k_ref[rows[c], lanes[hh]], v_ref[rows[c], lanes[hh]]
        s = lax.dot_general(q, k, (((1,), (1,)), ((), ())), preferred_element_type=F32) * dm_ref[hh]
        vk = (v.astype(F32) * kd_ref[hh]).astype(BF16)
        kv = lax.dot_general(k, vk, (((0,), (0,)), ((), ())), preferred_element_type=F32)
        return s.astype(BF16), kv

    def finish(c, hh, s, state):
        q, v = q_ref[rows[c], lanes[hh]], v_ref[rows[c], lanes[hh]]
        y = jnp.dot(s, v, preferred_element_type=F32)
        y = y + jnp.dot(q, state.astype(BF16), preferred_element_type=F32) * qd_ref[hh]
        ms = jnp.mean(y * y, axis=-1, keepdims=True)
        yn = y * lax.rsqrt(ms + EPS) * gn_ref[:, lanes[hh]]
        o_ref[rows[c], lanes[hh]] = (g_ref[rows[c], lanes[hh]].astype(F32) * yn).astype(BF16)

    state = [jnp.zeros((RET_HEAD_DIM, RET_HEAD_DIM), F32) for _ in heads]
    ahead = [independent(0, hh) for hh in heads]
    for c in range(nchunk):
        cur = ahead
        if c + 1 < nchunk:
            ahead = [independent(c + 1, hh) for hh in heads]
        for hh in heads:
            s, kv = cur[hh]
            finish(c, hh, s, state[hh])
            state[hh] = cd_ref[hh, 0:1, :] * state[hh] + kv


def retention(proj, ret_norm_g, l, B, S):
    T = proj.shape[0]
    dm, qd, kd, cd = _ret_consts()
    hp = RET_HEADS_PER_STEP
    wid = hp * LANES
    blk = lambda off: (lambda b, h: (b, off // wid + h))
    hconst = lambda rows: pl.BlockSpec((hp, rows, LANES), lambda b, h: (h, 0, 0))
    return pl.pallas_call(
        functools.partial(_ret_kernel, nchunk=S // RET_CHUNK),
        out_shape=jax.ShapeDtypeStruct((T, RET_W), BF16),
        grid=(B, RET_HEADS // hp),
        in_specs=[pl.BlockSpec((S, wid), blk(OFF_QR)),
                  pl.BlockSpec((S, wid), blk(OFF_KR)),
                  pl.BlockSpec((S, wid), blk(OFF_VR)),
                  pl.BlockSpec((S, wid), blk(OFF_GR)),
                  pl.BlockSpec((None, 1, wid), lambda b, h: (l, 0, h)),
                  pl.BlockSpec((hp, RET_CHUNK, RET_CHUNK), lambda b, h: (h, 0, 0)),
                  hconst(RET_CHUNK), hconst(RET_CHUNK), hconst(8)],
        out_specs=pl.BlockSpec((S, wid), lambda b, h: (b, h)),
        compiler_params=_cparams(("arbitrary", "arbitrary")),
        name="retention",
    )(proj, proj, proj, proj, ret_norm_g.reshape(-1, 1, RET_W), dm, qd, kd, cd)


def _load_cast(w_hbm, dst, stage, sem):
    rows = stage.shape[0]
    n = w_hbm.shape[0] // rows

    def body(i, carry):
        r0 = pl.multiple_of(i * rows, rows)
        cp = pltpu.make_async_copy(w_hbm.at[pl.ds(r0, rows), :], stage, sem)
        cp.start()
        cp.wait()
        dst[pl.ds(r0, rows), :] = stage[...].astype(BF16)
        return carry
    lax.fori_loop(0, n, body, 0)


def _mixer_out_kernel(att_ref, ret_ref, ga0, ga1, ga2, ga3, gb0, gb1, gb2, gb3, x_ref,
                      wa_hbm, wr_hbm, wo_hbm, gate1_ref, g2_ref, sh2_ref, sc2_ref,
                      wrt_ref, brt_ref, tril_ref,
                      xn_ref, h2_ref, rti_ref, rtw_ref, cnt_ref,
                      wa, wr, wo, wrt2, mrg, carry, hs, stage, sem, *, tm, l):
    i = pl.program_id(0)
    nt = pl.num_programs(0) - 1
    D = D_MODEL

    @pl.when(i == 0)
    def _():
        _load_cast(wa_hbm.at[l], wa, stage, sem)
        _load_cast(wr_hbm.at[l], wr, stage, sem)
        _load_cast(wo_hbm.at[l], wo, stage, sem)
        w = wrt_ref[...]
        hi = w.astype(BF16)
        wrt2[:, 0:LANES] = hi
        wrt2[:, LANES:2 * LANES] = (w - hi.astype(F32)).astype(BF16)
        carry[...] = jnp.zeros_like(carry)
        hs[...] = jnp.zeros_like(hs)

    def main_stage():
        a = att_ref[...]
        r = ret_ref[...]
        ga = (ga0, ga1, ga2, ga3)
        gb = (gb0, gb1, gb2, gb3)
        for n in range(D // PROJ_TN):
            cols = slice(n * PROJ_TN, (n + 1) * PROJ_TN)
            ya = jnp.dot(a, wa[:, cols], preferred_element_type=F32)
            yr = jnp.dot(r, wr[:, cols], preferred_element_type=F32)
            mrg[:, cols] = (ga[n][...].astype(F32) * ya + gb[n][...].astype(F32) * yr).astype(BF16)
        o = jnp.dot(mrg[...], wo[...], preferred_element_type=F32)
        xn = x_ref[...] + gate1_ref[...] * o
        xn_ref[...] = xn
        h2 = _norm_mod(xn, g2_ref[...], sh2_ref[...], sc2_ref[...])
        h2b = h2.astype(BF16)
        h2_ref[...] = h2b
        hs[0:tm, :] = h2b
        hs[tm:2 * tm, :] = (h2 - h2b.astype(F32)).astype(BF16)

    def router_select():
        r = jnp.dot(hs[...], wrt2[...], preferred_element_type=F32)
        logits = ((r[0:tm, 0:LANES] + r[tm:2 * tm, 0:LANES])
                  + (r[0:tm, LANES:2 * LANES] + r[tm:2 * tm, LANES:2 * LANES])) + brt_ref[...]
        lane = lax.broadcasted_iota(I32, (tm, LANES), 1)
        valid = lane < N_EXPERTS
        mx = jnp.max(logits, axis=-1, keepdims=True)
        p = jnp.where(valid, jnp.exp(logits - mx), 0.0)
        pos_in_grp = lane % EXPERTS_PER_GROUP
        grp_of = lane // EXPERTS_PER_GROUP

        def member(k):
            wrapped = pos_in_grp + k >= EXPERTS_PER_GROUP
            return jnp.where(wrapped, pltpu.roll(p, EXPERTS_PER_GROUP - k, 1), pltpu.roll(p, LANES - k, 1)), wrapped

        (b1, w1), (b2, w2), (b3, w3) = member(1), member(2), member(3)
        m_ab, n_ab = jnp.maximum(p, b1), jnp.minimum(p, b1)
        m_cd, n_cd = jnp.maximum(b2, b3), jnp.minimum(b2, b3)
        gscore = jnp.maximum(m_ab, m_cd) + jnp.maximum(jnp.minimum(m_ab, m_cd), jnp.maximum(n_ab, n_cd))
        gscore = jnp.where(valid, gscore, -1.0)
        gmax = jnp.max(gscore, axis=-1, keepdims=True)
        gsel = jnp.min(jnp.where(gscore == gmax, grp_of, N_GROUPS), axis=-1, keepdims=True)
        in_sel = grp_of == gsel
        beats = lambda b, w: ((b > p) | ((b == p) & w)).astype(I32)
        rank_in_grp = beats(b1, w1) + beats(b2, w2) + beats(b3, w3)
        sel0 = in_sel & (rank_in_grp == 0)
        sel1 = in_sel & (rank_in_grp == 1)
        lsum = lambda m, v: jnp.sum(jnp.where(m, v, 0.0), axis=-1, keepdims=True)
        v0, v1 = lsum(sel0, p), lsum(sel1, p)
        lanef = lane.astype(F32)
        e0, e1 = lsum(sel0, lanef), lsum(sel1, lanef)
        inv = 1.0 / (v0 + v1)
        return lane, sel0, sel1, e0, e1, v0 * inv, v1 * inv

    def router_finish(live, lane, sel0, sel1, e0, e1, w0, w1):
        lsum = lambda m, v: jnp.sum(jnp.where(m, v, 0.0), axis=-1, keepdims=True)
        onehot = jnp.where(sel0 | sel1, 1.0, 0.0)
        prefix = jnp.dot(tril_ref[...], onehot.astype(BF16), preferred_element_type=F32) + carry[0:1, :]
        r0, r1 = lsum(sel0, prefix), lsum(sel1, prefix)
        carry[...] = carry[...] + live * jnp.sum(onehot, axis=0, keepdims=True)
        cnt_ref[...] = carry[...].astype(I32)
        sel4 = lambda a0, a1, a2, a3: jnp.where(lane == 0, a0, jnp.where(lane == 1, a1, jnp.where(lane == 2, a2, a3)))
        rti_ref[...] = sel4(e0, e1, r0, r1).astype(I32)
        rtw_ref[...] = jnp.where(lane == 0, w0, jnp.where(lane == 1, w1, 0.0))

    @pl.when(i < nt)
    def _():
        picked = router_select()
        main_stage()
        router_finish(jnp.where(i > 0, 1.0, 0.0), *picked)

    @pl.when(i == nt)
    def _():
        router_finish(1.0, *router_select())


def mixer_out(att, ret, proj, x2, w_attn_out, w_ret_out, w_o, modr, norm2_g, w_router, b_router, l, S):
    T, D = x2.shape
    tm = 256
    assert S % tm == 0
    nga, ngb = OFF_GA // PROJ_TN, OFF_GB // PROJ_TN
    nt = T // tm
    cur = lambda i: jnp.minimum(i, nt - 1)
    prev = lambda i: jnp.maximum(i - 1, 0)
    gate_spec = lambda blk: pl.BlockSpec((tm, PROJ_TN), lambda i: (cur(i), blk))
    row = lambda: pl.BlockSpec((tm, D), lambda i: (cur(i), 0))
    mod_spec = lambda k: pl.BlockSpec((None, None, None, 1, D), lambda i: (l, k, (cur(i) * tm) // S, 0, 0))
    wrt = jnp.pad(w_router, ((0, 0), (0, LANES - N_EXPERTS)))
    brt = jnp.pad(b_router.astype(F32), (0, LANES - N_EXPERTS), constant_values=-1e30).reshape(1, LANES)
    tril = jnp.asarray(np.tril(np.ones((tm, tm), np.float32), -1), BF16)
    any_spec = pl.BlockSpec(memory_space=pl.ANY)
    outs = pl.pallas_call(
        functools.partial(_mixer_out_kernel, tm=tm, l=l),
        out_shape=(jax.ShapeDtypeStruct((T, D), F32),
                   jax.ShapeDtypeStruct((T, D), BF16),
                   jax.ShapeDtypeStruct((T, LANES), I32),
                   jax.ShapeDtypeStruct((T, LANES), F32),
                   jax.ShapeDtypeStruct((8, LANES), I32)),
        grid=(nt + 1,),
        in_specs=[pl.BlockSpec((tm, ATT_Q_W), lambda i: (cur(i), 0)),
                  pl.BlockSpec((tm, RET_W), lambda i: (cur(i), 0)),
                  *[gate_spec(nga + n) for n in range(4)],
                  *[gate_spec(ngb + n) for n in range(4)],
                  row(), any_spec, any_spec, any_spec,
                  mod_spec(2),
                  pl.BlockSpec((None, 1, D), lambda i: (l, 0, 0)),
                  mod_spec(3), mod_spec(4),
                  pl.BlockSpec((D, LANES), lambda i: (0, 0)),
                  pl.BlockSpec((1, LANES), lambda i: (0, 0)),
                  pl.BlockSpec((tm, tm), lambda i: (0, 0))],
        out_specs=(row(),
                   row(),
                   pl.BlockSpec((tm, LANES), lambda i: (prev(i), 0)),
                   pl.BlockSpec((tm, LANES), lambda i: (prev(i), 0)),
                   pl.BlockSpec((8, LANES), lambda i: (0, 0))),
        scratch_shapes=[pltpu.VMEM((ATT_Q_W, D), BF16), pltpu.VMEM((RET_W, D), BF16),
                        pltpu.VMEM((D, D), BF16),
                        pltpu.VMEM((D, 2 * LANES), BF16),
                        pltpu.VMEM((tm, D), BF16), pltpu.VMEM((8, LANES), F32),
                        pltpu.VMEM((2 * tm, D), BF16),
                        pltpu.VMEM((256, D), F32), pltpu.SemaphoreType.DMA],
        compiler_params=_cparams(("arbitrary",)),
        name="mixer_out",
    )(att, ret, *([proj] * 8), x2, w_attn_out, w_ret_out, w_o,
      modr, norm2_g.reshape(-1, 1, D), modr, modr, wrt, brt, tril)
    return outs


def moe_plan(counts, n_work):
    sub_per = MOE_ROWS // MOE_SUB
    seg = ((counts + MOE_SUB - 1) // MOE_SUB) * MOE_SUB
    off = jnp.cumsum(seg) - seg
    off17 = jnp.concatenate([off, off[-1:] + seg[-1:]]).astype(I32)
    nb = (counts + MOE_ROWS - 1) // MOE_ROWS
    cum = jnp.cumsum(nb)
    total = cum[-1]
    w = jnp.arange(n_work, dtype=I32)
    wc = jnp.minimum(w, total - 1)
    e_w = jnp.sum((cum[None, :] <= wc[:, None]).astype(I32), axis=1)
    blk = wc - (cum[e_w] - nb[e_w])
    start = off[e_w] + blk * MOE_ROWS
    nsub = jnp.clip(seg[e_w] // MOE_SUB - blk * sub_per, 0, sub_per)
    nsub = jnp.where(w < total, nsub, 0)
    return off17, e_w, start.astype(I32), nsub.astype(I32)


def _row_dma_loops(n_rows, make_copy):
    def run(op):
        def body(g, carry):
            r0 = pl.multiple_of(g * ROW_GROUP, ROW_GROUP)
            for u in range(ROW_GROUP):
                for k in range(2):
                    getattr(make_copy(r0 + u, k), op)()
            return carry
        lax.fori_loop(0, n_rows // ROW_GROUP, body, 0)
    run("start")
    run("wait")


def _dispatch_kernel(pos_ref, off_ref, h_ref, xs_ref, rows_ref, sem, *, td, T):
    i = pl.program_id(0)
    zero_ref = rows_ref

    @pl.when(i == 0)
    def _():
        zero_ref[...] = jnp.zeros_like(zero_ref)
        used = off_ref[N_EXPERTS]

        def zero_copy(row):
            row = pl.multiple_of(row, MOE_SUB)
            return pltpu.make_async_copy(zero_ref, xs_ref.at[pl.ds(row, MOE_SUB), :], sem)

        def fill(op):
            def seg_tail(e, carry):
                @pl.when(off_ref[e + 1] > off_ref[e])
                def _():
                    getattr(zero_copy(off_ref[e + 1] - MOE_SUB), op)()
                return carry

            def buf_tail(n, carry):
                getattr(zero_copy(used + n * MOE_SUB), op)()
                return carry
            lax.fori_loop(0, N_EXPERTS, seg_tail, 0)
            lax.fori_loop(0, (xs_ref.shape[0] - used) // MOE_SUB, buf_tail, 0)
        fill("start")
        fill("wait")

    rows_ref[...] = h_ref[...].astype(F32)

    def copy(r, k):
        p = pos_ref[k * T + i * td + r]
        return pltpu.make_async_copy(rows_ref.at[pl.ds(r, 1), :], xs_ref.at[pl.ds(p, 1), :], sem)
    _row_dma_loops(td, copy)


def dispatch(h2, pos_flat, off17, n_rows):
    T, D = h2.shape
    td = MOE_SUB
    return pl.pallas_call(
        functools.partial(_dispatch_kernel, td=td, T=T),
        out_shape=jax.ShapeDtypeStruct((n_rows, D), F32),
        grid_spec=pltpu.PrefetchScalarGridSpec(
            num_scalar_prefetch=2, grid=(T // td,),
            in_specs=[pl.BlockSpec((td, D), lambda i, *_: (i, 0))],
            out_specs=pl.BlockSpec(memory_space=pl.ANY),
            scratch_shapes=[pltpu.VMEM((td, D), F32), pltpu.SemaphoreType.DMA]),
        compiler_params=_cparams(("arbitrary",)),
        name="moe_dispatch",
    )(pos_flat, off17, h2)


def _moe_kernel(we_ref, ws_ref, wn_ref, xs_ref, wg_ref, wu_ref, wd_ref, ys_ref,
                xb, acc, wgb, wub, wdb, stage, ostage, sem_in, sem_out, *, nfc, n_work):
    w = pl.program_id(0)
    c = pl.program_id(1)
    n = wn_ref[w]
    s0 = ws_ref[w]

    def in_copy(start_row, i, slot):
        g0 = pl.multiple_of(start_row + i * MOE_SUB, MOE_SUB)
        return pltpu.make_async_copy(xs_ref.at[pl.ds(g0, MOE_SUB), :], stage.at[slot], sem_in.at[slot])

    def out_copy(i, slot):
        g0 = pl.multiple_of(s0 + i * MOE_SUB, MOE_SUB)
        return pltpu.make_async_copy(ostage.at[slot], ys_ref.at[pl.ds(g0, MOE_SUB), :], sem_out.at[slot])

    def request_first_two(start_row, count):
        @pl.when(count > 0)
        def _():
            in_copy(start_row, 0, 0).start()

        @pl.when(count > 1)
        def _():
            in_copy(start_row, 1, 1).start()

    @pl.when((w == 0) & (c == 0))
    def _():
        request_first_two(s0, n)

    def for_subs(fn):
        def body(p, carry):
            fn(p * 2, 2)
            return carry
        lax.fori_loop(0, n // 2, body, 0)

        @pl.when(lax.rem(n, 2) == 1)
        def _():
            fn(n - 1, 1)

    def rows_of(i):
        return pl.ds(pl.multiple_of(i * MOE_SUB, MOE_SUB), MOE_SUB)

    def ffn(i0, count):
        xs_ = [xb[rows_of(i0 + a), :] for a in range(count)]
        gu = [(jnp.dot(x, wgb[...], preferred_element_type=F32), jnp.dot(x, wub[...], preferred_element_type=F32))
              for x in xs_]
        acts = [(g * _sigmoid(g) * u).astype(BF16) for g, u in gu]
        return [jnp.dot(a, wdb[...], preferred_element_type=F32) for a in acts]

    @pl.when(n > 0)
    def _():
        wgb[...] = wg_ref[...].astype(BF16)
        wub[...] = wu_ref[...].astype(BF16)
        wdb[...] = wd_ref[...].astype(BF16)

        @pl.when(c == 0)
        def _():
            def first(i0, count):
                for a in range(count):
                    in_copy(s0, i0 + a, a).wait()
                    xb[rows_of(i0 + a), :] = stage[a].astype(BF16)
                for a in range(count):
                    @pl.when(i0 + a + 2 < n)
                    def _():
                        in_copy(s0, i0 + a + 2, a).start()
                for a, y in enumerate(ffn(i0, count)):
                    acc[rows_of(i0 + a), :] = y
            for_subs(first)

        @pl.when((c > 0) & (c < nfc - 1))
        def _():
            def middle(i0, count):
                for a, y in enumerate(ffn(i0, count)):
                    acc[rows_of(i0 + a), :] += y
            for_subs(middle)

        @pl.when(c == nfc - 1)
        def _():
            def last(i0, count):
                for a in range(count):
                    @pl.when(i0 + a >= 2)
                    def _():
                        out_copy(i0 + a - 2, a).wait()
                for a, y in enumerate(ffn(i0, count)):
                    ostage[a] = acc[rows_of(i0 + a), :] + y
                    out_copy(i0 + a, a).start()
            for_subs(last)

            @pl.when(n >= 2)
            def _():
                out_copy(n - 2, lax.rem(n, 2)).wait()
            out_copy(n - 1, lax.rem(n - 1, 2)).wait()

            @pl.when(w + 1 < n_work)
            def _():
                nxt = jnp.minimum(w + 1, n_work - 1)
                request_first_two(ws_ref[nxt], wn_ref[nxt])


def moe_experts(xs, w_gate, w_up, w_down, l, e_w, start_w, nsub_w):
    n_rows, D = xs.shape
    nfc = D_FF // MOE_FC
    assert nfc >= 2
    n_work = e_w.shape[0]

    def chunk(c, wn, w):
        return jnp.where(wn[w] > 0, c, nfc - 1)

    return pl.pallas_call(
        functools.partial(_moe_kernel, nfc=nfc, n_work=n_work),
        out_shape=jax.ShapeDtypeStruct((n_rows, D), F32),
        grid_spec=pltpu.PrefetchScalarGridSpec(
            num_scalar_prefetch=3, grid=(n_work, nfc),
            in_specs=[pl.BlockSpec(memory_space=pl.ANY),
                      pl.BlockSpec((None, None, D, MOE_FC), lambda w, c, we, ws, wn: (l, we[w], 0, chunk(c, wn, w))),
                      pl.BlockSpec((None, None, D, MOE_FC), lambda w, c, we, ws, wn: (l, we[w], 0, chunk(c, wn, w))),
                      pl.BlockSpec((None, None, MOE_FC, D), lambda w, c, we, ws, wn: (l, we[w], chunk(c, wn, w), 0))],
            out_specs=pl.BlockSpec(memory_space=pl.ANY),
            scratch_shapes=[pltpu.VMEM((MOE_ROWS, D), BF16), pltpu.VMEM((MOE_ROWS, D), F32),
                            pltpu.VMEM((D, MOE_FC), BF16), pltpu.VMEM((D, MOE_FC), BF16),
                            pltpu.VMEM((MOE_FC, D), BF16),
                            pltpu.VMEM((2, MOE_SUB, D), F32), pltpu.VMEM((2, MOE_SUB, D), F32),
                            pltpu.SemaphoreType.DMA((2,)), pltpu.SemaphoreType.DMA((2,))]),
        input_output_aliases={3: 0},
        compiler_params=_cparams(("arbitrary", "arbitrary")),
        name="moe_experts",
    )(e_w, start_w, nsub_w, xs, w_gate, w_up, w_down)


def _combine_kernel(pos_ref, ys_ref, x_ref, rtw_ref, gate2_ref, g_ref, *rest, tc, T, final):
    if final:
        out_ref, ybuf, sem = rest
    else:
        sh_ref, sc_ref, xo_ref, h_ref, ybuf, sem = rest
    i = pl.program_id(0)
    slot = lax.rem(i, 2)
    ngroups = tc // COMBINE_GROUP

    def copy(tile, r, k, sl):
        p = pos_ref[k * T + tile * tc + r]
        return pltpu.make_async_copy(ys_ref.at[pl.ds(p, 1), :], ybuf.at[sl, k, pl.ds(r, 1), :], sem.at[sl])

    def for_rows(tile, sl, r0, op):
        for u in range(COMBINE_GROUP):
            for k in range(2):
                getattr(copy(tile, r0 + u, k, sl), op)()

    def groups(fn):
        def body(g, carry):
            fn(pl.multiple_of(g * COMBINE_GROUP, COMBINE_GROUP))
            return carry
        lax.fori_loop(0, ngroups, body, 0)

    def compute(r0):
        rows = pl.ds(r0, COMBINE_GROUP)
        moe = rtw_ref[rows, 0:1] * ybuf[slot, 0, rows, :] + rtw_ref[rows, 1:2] * ybuf[slot, 1, rows, :]
        xo = x_ref[rows, :] + gate2_ref[...] * moe
        if final:
            ms = jnp.mean(xo * xo, axis=-1, keepdims=True)
            out_ref[rows, :] = xo * lax.rsqrt(ms + EPS) * g_ref[...]
        else:
            xo_ref[rows, :] = xo
            h_ref[rows, :] = _norm_mod(xo, g_ref[...], sh_ref[...], sc_ref[...]).astype(BF16)

    @pl.when(i == 0)
    def _():
        groups(lambda r0: for_rows(0, 0, r0, "start"))
    groups(lambda r0: for_rows(i, slot, r0, "wait"))

    @pl.when(i + 1 < pl.num_programs(0))
    def _():
        def both(r0):
            compute(r0)
            for_rows(i + 1, 1 - slot, r0, "start")
        groups(both)

    @pl.when(i + 1 >= pl.num_programs(0))
    def _():
        groups(compute)


def combine(ys, xn, rtw, pos_flat, modr, l, S, g_next, final):
    T, D = xn.shape
    tc = 256
    assert S % tc == 0
    row = lambda: pl.BlockSpec((tc, D), lambda i, *_: (i, 0))
    in_specs = [pl.BlockSpec(memory_space=pl.ANY), row(),
                pl.BlockSpec((tc, LANES), lambda i, *_: (i, 0)),
                _mod_spec(l, 5, S, tc)]
    args = [ys, xn, rtw, modr]
    if final:
        in_specs.append(pl.BlockSpec((1, D), lambda i, *_: (0, 0)))
        args.append(g_next.reshape(1, D))
        out_shape = jax.ShapeDtypeStruct((T, D), F32)
        out_specs = row()
    else:
        in_specs += [pl.BlockSpec((None, 1, D), lambda i, *_: (l + 1, 0, 0)),
                     _mod_spec(l + 1, 0, S, tc), _mod_spec(l + 1, 1, S, tc)]
        args += [g_next.reshape(-1, 1, D), modr, modr]
        out_shape = (jax.ShapeDtypeStruct((T, D), F32), jax.ShapeDtypeStruct((T, D), BF16))
        out_specs = (row(), row())
    return pl.pallas_call(
        functools.partial(_combine_kernel, tc=tc, T=T, final=final),
        out_shape=out_shape,
        grid_spec=pltpu.PrefetchScalarGridSpec(
            num_scalar_prefetch=1, grid=(T // tc,),
            in_specs=in_specs, out_specs=out_specs,
            scratch_shapes=[pltpu.VMEM((2, 2, tc, D), F32), pltpu.SemaphoreType.DMA((2,))]),
        compiler_params=_cparams(("arbitrary",)),
        name="moe_combine_final" if final else "moe_combine",
    )(pos_flat, *args)


def kernel(x, c, positions, w_ada, b_ada, norm1_g, norm2_g, w_in, attn_sinks, w_attn_out, ret_norm_g,
           w_ret_out, w_o, w_router, b_router, w_gate, w_up, w_down, final_g):
    B, S, D = x.shape
    L = w_ada.shape[0]
    T = B * S
    assert D == D_MODEL and w_in.shape[-1] == IN_W and S % RET_CHUNK == 0

    tab = rope_tables(positions)
    mod = adaln_mod(c, w_ada, b_ada)
    modr = mod[:, :B].reshape(L, B, 6, D).transpose(0, 2, 1, 3).reshape(L, 6, B, 1, D)

    x2 = x.reshape(T, D)
    h = norm_modulate(x2, norm1_g.reshape(L, 1, D), modr, 0, S)
    n_work = N_EXPERTS + (2 * T + MOE_ROWS - 1) // MOE_ROWS
    n_rows = 2 * T + N_EXPERTS * MOE_SUB
    out = None
    for l in range(L):
        proj = in_projection(h, w_in, l, tab)
        att = swa_attention(proj, attn_sinks[l], B, S)
        ret = retention(proj, ret_norm_g, l, B, S)
        xn, h2, rti, rtw, cnt = mixer_out(att, ret, proj, x2, w_attn_out, w_ret_out, w_o, modr,
                                          norm2_g, w_router, b_router, l, S)
        off17, e_w, start_w, nsub_w = moe_plan(cnt[0, :N_EXPERTS], n_work)
        seg_off = jnp.sum(jnp.where(rti[:, 0:2, None] == jnp.arange(N_EXPERTS, dtype=I32),
                                    off17[:N_EXPERTS], 0), axis=-1)
        pos_flat = (seg_off + rti[:, 2:4]).T.reshape(2 * T)
        xs = dispatch(h2, pos_flat, off17, n_rows)
        ys = moe_experts(xs, w_gate, w_up, w_down, l, e_w, start_w, nsub_w)
        if l + 1 < L:
            x2, h = combine(ys, xn, rtw, pos_flat, modr, l, S, norm1_g, final=False)
        else:
            out = combine(ys, xn, rtw, pos_flat, modr, l, S, final_g, final=True)
    return out.reshape(B, S, D)
```

```python
import functools

import numpy as np
import jax
import jax.numpy as jnp
from jax import lax
from jax.experimental import pallas as pl
from jax.experimental.pallas import tpu as pltpu

F32 = jnp.float32
BF16 = jnp.bfloat16
I32 = jnp.int32

D_MODEL = 2048
ATT_HEAD_DIM = 64
ATT_Q_HEADS = 16
ATT_KV_HEADS = 4
WINDOW = 128
RET_HEADS = 8
RET_HEAD_DIM = 128
RET_CHUNK = 256
ROPE_THETA = 10000.0
N_GROUPS = 4
EXPERTS_PER_GROUP = 4
N_EXPERTS = 16
D_FF = 1024
EPS = 1e-6

ATT_Q_W = ATT_Q_HEADS * ATT_HEAD_DIM
ATT_KV_W = ATT_KV_HEADS * ATT_HEAD_DIM
RET_W = RET_HEADS * RET_HEAD_DIM
OFF_QA = 0
OFF_KA = OFF_QA + ATT_Q_W
OFF_VA = OFF_KA + ATT_KV_W
OFF_QR = OFF_VA + ATT_KV_W
OFF_KR = OFF_QR + RET_W
OFF_VR = OFF_KR + RET_W
OFF_GR = OFF_VR + RET_W
OFF_GA = OFF_GR + RET_W
OFF_GB = OFF_GA + D_MODEL
IN_W = OFF_GB + D_MODEL

LANES = 128
PROJ_TN = 512
ROW_GROUP = 8
ATTN_LOOKAHEAD = 2
COMBINE_GROUP = 64
RET_HEADS_PER_STEP = 2
MOE_SUB = 256
MOE_ROWS = 2048
MOE_FC = 256
VMEM_LIMIT = 56 * 1024 * 1024


def _pick(n, cands):
    for c in cands:
        if n % c == 0:
            return c
    raise ValueError(f"no tile in {cands} divides {n}")


def _sigmoid(x):
    return 1.0 / (1.0 + jnp.exp(-x))


def _cparams(sem, vmem=VMEM_LIMIT):
    return pltpu.CompilerParams(dimension_semantics=sem, vmem_limit_bytes=vmem)


def _rope_consts():
    def inv_freq(head_dim):
        half = head_dim // 2
        inv = ROPE_THETA ** (-2.0 * jnp.arange(half, dtype=F32) / head_dim)
        return jnp.broadcast_to(jnp.tile(inv, LANES // half), (8, LANES))
    lane = np.arange(LANES)
    sgn_att = np.where((lane % ATT_HEAD_DIM) < ATT_HEAD_DIM // 2, -1.0, 1.0)
    sgn_ret = np.where(lane < RET_HEAD_DIM // 2, -1.0, 1.0)
    sgn = np.stack([np.tile(sgn_att, (8, 1)), np.tile(sgn_ret, (8, 1))]).astype(np.float32)
    return jnp.stack([inv_freq(ATT_HEAD_DIM), inv_freq(RET_HEAD_DIM)]), jnp.asarray(sgn)


def _rope_kernel(pos_ref, inv_ref, sgn_ref, tab_ref):
    pos = pos_ref[...]
    for kind in range(2):
        ang = pos * inv_ref[kind, 0:1, :]
        tab_ref[kind, :, 0:LANES] = jnp.cos(ang)
        tab_ref[kind, :, LANES:2 * LANES] = jnp.sin(ang) * sgn_ref[kind, 0:1, :]


def rope_tables(positions):
    T = positions.size
    posb = jnp.broadcast_to(positions.reshape(T, 1).astype(F32), (T, LANES))
    inv, sgn = _rope_consts()
    tm = _pick(T, (1024, 512, 256, 128))
    return pl.pallas_call(
        _rope_kernel,
        out_shape=jax.ShapeDtypeStruct((2, T, 2 * LANES), F32),
        grid=(T // tm,),
        in_specs=[pl.BlockSpec((tm, LANES), lambda i: (i, 0)),
                  pl.BlockSpec((2, 8, LANES), lambda i: (0, 0, 0)),
                  pl.BlockSpec((2, 8, LANES), lambda i: (0, 0, 0))],
        out_specs=pl.BlockSpec((2, tm, 2 * LANES), lambda i: (0, i, 0)),
        compiler_params=_cparams(("arbitrary",)),
        name="rope_tables",
    )(posb, inv, sgn)


def _adaln_kernel(c_ref, w_ref, b_ref, o_ref):
    c = c_ref[...]
    ca = c * _sigmoid(c)
    hi = ca.astype(BF16).astype(F32)
    lhs = jnp.concatenate([hi, ca - hi], axis=0).astype(BF16)
    r = jnp.dot(lhs, w_ref[...].astype(BF16), preferred_element_type=F32)
    o_ref[...] = r[0:8] + r[8:16] + b_ref[...]


def adaln_mod(c, w_ada, b_ada):
    L, D, N = w_ada.shape
    B = c.shape[0]
    cp = jnp.pad(c, ((0, 8 - B), (0, 0)))
    tn = _pick(N, (1024, 512))
    return pl.pallas_call(
        _adaln_kernel,
        out_shape=jax.ShapeDtypeStruct((L, 8, N), F32),
        grid=(L, N // tn),
        in_specs=[pl.BlockSpec((8, D), lambda l, j: (0, 0)),
                  pl.BlockSpec((None, D, tn), lambda l, j: (l, 0, j)),
                  pl.BlockSpec((None, 1, tn), lambda l, j: (l, 0, j))],
        out_specs=pl.BlockSpec((None, 8, tn), lambda l, j: (l, 0, j)),
        compiler_params=_cparams(("arbitrary", "arbitrary")),
        name="adaln_mod",
    )(cp, w_ada, b_ada.reshape(L, 1, N))


def _norm_mod(x, g, shift, scale):
    ms = jnp.mean(x * x, axis=-1, keepdims=True)
    return (x * lax.rsqrt(ms + EPS) * g) * (1.0 + scale) + shift


def _norm_mod_kernel(x_ref, g_ref, sh_ref, sc_ref, h_ref):
    h_ref[...] = _norm_mod(x_ref[...], g_ref[...], sh_ref[...], sc_ref[...]).astype(BF16)


def _mod_spec(l, k, S, tm):
    return pl.BlockSpec((None, None, None, 1, D_MODEL), lambda i, *_: (l, k, (i * tm) // S, 0, 0))


def norm_modulate(x2, g, modr, l, S):
    T, D = x2.shape
    tm = _pick(S, (512, 256, 128))
    return pl.pallas_call(
        _norm_mod_kernel,
        out_shape=jax.ShapeDtypeStruct((T, D), BF16),
        grid=(T // tm,),
        in_specs=[pl.BlockSpec((tm, D), lambda i: (i, 0)),
                  pl.BlockSpec((None, 1, D), lambda i: (l, 0, 0)),
                  _mod_spec(l, 0, S, tm), _mod_spec(l, 1, S, tm)],
        out_specs=pl.BlockSpec((tm, D), lambda i: (i, 0)),
        compiler_params=_cparams(("arbitrary",)),
        name="norm1_modulate",
    )(x2, g, modr, modr)


def _rope_att(blk, cos, sin, scale):
    lane = lax.broadcasted_iota(I32, blk.shape, 1)
    rot = jnp.where((lane % 64) < 32, pltpu.roll(blk, 96, 1), pltpu.roll(blk, 32, 1))
    out = blk * cos + rot * sin
    return out * scale if scale != 1.0 else out


def _rope_ret(blk, cos, sin, scale):
    out = blk * cos + pltpu.roll(blk, 64, 1) * sin
    return out * scale if scale != 1.0 else out


def _inproj_kernel(h_ref, w_ref, tab_ref, o_ref, wbf_ref, acc_ref, *, ni, ntiles):
    s = pl.program_id(0)
    j = jnp.maximum(s - 1, 0) // ni

    @pl.when(s == 0)
    def _():
        acc_ref[1] = jnp.zeros(acc_ref.shape[1:], F32)

    @pl.when((lax.rem(s, ni) == 0) & (s < ntiles))
    def _():
        wbf_ref[...] = w_ref[...].astype(BF16)

    ngrp = PROJ_TN // LANES
    t = lambda off: off // PROJ_TN
    q_scale = float(ATT_HEAD_DIM) ** -0.5
    k_scale = float(RET_HEAD_DIM) ** -0.5

    def step(cur):
        def run(epilogue):
            cos = tab_ref[:, 0:LANES]
            sin = tab_ref[:, LANES:2 * LANES]
            for g in range(ngrp):
                blk = acc_ref[1 - cur, :, g * LANES:(g + 1) * LANES]
                o_ref[:, g * LANES:(g + 1) * LANES] = epilogue(g, blk, cos, sin).astype(BF16)
            acc_ref[cur] = jnp.dot(h_ref[...], wbf_ref[...], preferred_element_type=F32)

        @pl.when(j < t(OFF_KA))
        def _():
            run(lambda g, b, c, s: _rope_att(b, c, s, q_scale))

        @pl.when(j == t(OFF_KA))
        def _():
            run(lambda g, b, c, s: _rope_att(b, c, s, 1.0) if g < ATT_KV_W // LANES else b)

        @pl.when((j >= t(OFF_QR)) & (j < t(OFF_KR)))
        def _():
            run(lambda g, b, c, s: _rope_ret(b, c, s, 1.0))

        @pl.when((j >= t(OFF_KR)) & (j < t(OFF_VR)))
        def _():
            run(lambda g, b, c, s: _rope_ret(b, c, s, k_scale))

        @pl.when((j >= t(OFF_VR)) & (j < t(OFF_GR)))
        def _():
            run(lambda g, b, c, s: b)

        @pl.when((j >= t(OFF_GR)) & (j < t(OFF_GA)))
        def _():
            run(lambda g, b, c, s: b * _sigmoid(b))

        @pl.when(j >= t(OFF_GA))
        def _():
            run(lambda g, b, c, s: _sigmoid(b))

    for parity in range(2):
        pl.when(lax.rem(s, 2) == parity)(functools.partial(step, parity))


def in_projection(h, w_in, l, tab):
    T, D = h.shape
    tm = _pick(T, (1024, 768, 512, 256))
    nj, ni = IN_W // PROJ_TN, T // tm
    ntiles = nj * ni
    t_qr, t_vr = OFF_QR // PROJ_TN, OFF_VR // PROJ_TN
    mm = lambda s: jnp.minimum(s, ntiles - 1)
    ep = lambda s: jnp.maximum(s - 1, 0)

    def tab_map(s):
        j, i = ep(s) // ni, lax.rem(ep(s), ni)
        return (jnp.where(j < t_qr, 0, 1), jnp.where(j < t_vr, i, 0), 0)

    return pl.pallas_call(
        functools.partial(_inproj_kernel, ni=ni, ntiles=ntiles),
        out_shape=jax.ShapeDtypeStruct((nj, T, PROJ_TN), BF16),
        grid=(ntiles + 1,),
        in_specs=[pl.BlockSpec((tm, D), lambda s: (lax.rem(mm(s), ni), 0)),
                  pl.BlockSpec((None, D, PROJ_TN), lambda s: (l, 0, mm(s) // ni)),
                  pl.BlockSpec((None, tm, 2 * LANES), tab_map)],
        out_specs=pl.BlockSpec((None, tm, PROJ_TN), lambda s: (ep(s) // ni, lax.rem(ep(s), ni), 0)),
        scratch_shapes=[pltpu.VMEM((D, PROJ_TN), BF16), pltpu.VMEM((2, tm, PROJ_TN), F32)],
        compiler_params=_cparams(("arbitrary",)),
        name="in_projection",
    )(h, w_in, tab)


def _attn_kernel(sinks_ref, q0_ref, q1_ref, kvc_ref, kvp_ref, o_ref, *, tq):
    W = WINDOW
    nsub = tq // W
    is_first = pl.program_id(1) == 0
    qi = lax.broadcasted_iota(I32, (W, 2 * W), 0)
    kj = lax.broadcasted_iota(I32, (W, 2 * W), 1)
    rel = qi + W - kj
    band = (rel >= 0) & (rel < WINDOW)
    band0 = band & (jnp.logical_not(is_first) | (kj >= W))
    lane = lax.broadcasted_iota(I32, (tq + W, LANES), 1)
    lo = lane < ATT_HEAD_DIM

    def split_pair(raw, parity):
        x = raw.astype(F32)
        xr = pltpu.roll(x, ATT_HEAD_DIM, 1)
        if parity == 0:
            return jnp.where(lo, x, 0.0), jnp.where(lo, 0.0, xr)
        return jnp.where(lo, xr, 0.0), jnp.where(lo, 0.0, x)

    units = []
    for hk in range(ATT_KV_HEADS):
        grp, par = hk // 2, hk % 2
        kc = slice(grp * LANES, (grp + 1) * LANES)
        vc = slice(ATT_KV_W + grp * LANES, ATT_KV_W + (grp + 1) * LANES)
        k_pair = split_pair(jnp.concatenate([kvp_ref[:, kc], kvc_ref[:, kc]], axis=0), par)
        v_pair = split_pair(jnp.concatenate([kvp_ref[:, vc], kvc_ref[:, vc]], axis=0), par)
        kt_pair = [k.T.astype(BF16) for k in k_pair]
        v_pair = [v.astype(BF16) for v in v_pair]
        qcols = [slice(hk * 4 * ATT_HEAD_DIM + jq * LANES, hk * 4 * ATT_HEAD_DIM + (jq + 1) * LANES)
                 for jq in range(2)]
        for m in range(nsub):
            units.append((hk, m, kt_pair, v_pair, qcols))

    def scores(unit):
        hk, m, kt_pair, v_pair, qcols = unit
        rows = slice(m * W, (m + 1) * W)
        keys = slice(m * W, m * W + 2 * W)
        kcat = jnp.concatenate([kt_pair[0][:, keys], kt_pair[1][:, keys]], axis=1)
        q_ref = (q0_ref, q1_ref)[hk // 2]
        c0 = (hk % 2) * 4 * ATT_HEAD_DIM
        q = jnp.concatenate([q_ref[rows, c0:c0 + LANES], q_ref[rows, c0 + LANES:c0 + 2 * LANES]], axis=0)
        return jnp.dot(q, kcat, preferred_element_type=F32)

    def finish(unit, s_all):
        hk, m, kt_pair, v_pair, qcols = unit
        rows = slice(m * W, (m + 1) * W)
        keys = slice(m * W, m * W + 2 * W)
        mask = band0 if m == 0 else band
        vcat = jnp.concatenate([v_pair[0][keys], v_pair[1][keys]], axis=0)
        p_rows = []
        for jq in range(2):
            p_cols = []
            for e in range(2):
                s = s_all[jq * W:(jq + 1) * W, e * 2 * W:(e + 1) * 2 * W]
                s = jnp.where(mask, s, -1e30)
                sink = sinks_ref[hk * 4 + jq * 2 + e]
                mx = jnp.maximum(jnp.max(s, axis=1, keepdims=True), sink)
                p = jnp.exp(s - mx)
                den = jnp.sum(p, axis=1, keepdims=True) + jnp.exp(sink - mx)
                p_cols.append((p * (1.0 / den)).astype(BF16))
            p_rows.append(jnp.concatenate(p_cols, axis=1))
        o = jnp.dot(jnp.concatenate(p_rows, axis=0), vcat, preferred_element_type=F32)
        o_ref[rows, qcols[0]] = o[0:W].astype(BF16)
        o_ref[rows, qcols[1]] = o[W:2 * W].astype(BF16)

    pending = [scores(u) for u in units[:ATTN_LOOKAHEAD]]
    for idx, unit in enumerate(units):
        if idx + ATTN_LOOKAHEAD < len(units):
            pending.append(scores(units[idx + ATTN_LOOKAHEAD]))
        finish(unit, pending.pop(0))


def swa_attention(proj, sinks, B, S):
    T = proj.shape[1]
    tq = _pick(S, (512, 384, 256, 128))
    nq = S // tq
    assert ATT_Q_W == 2 * PROJ_TN and 2 * ATT_KV_W == PROJ_TN
    kv_tile = OFF_KA // PROJ_TN
    tile = lambda j: pl.BlockSpec((None, tq, PROJ_TN), lambda b, i: (j, b * nq + i, 0))

    def prev_map(b, i):
        return (kv_tile, jnp.maximum(b * (S // WINDOW) + i * (tq // WINDOW) - 1, 0), 0)

    return pl.pallas_call(
        functools.partial(_attn_kernel, tq=tq),
        out_shape=jax.ShapeDtypeStruct((T, ATT_Q_W), BF16),
        grid=(B, nq),
        in_specs=[pl.BlockSpec(memory_space=pltpu.SMEM),
                  tile(0), tile(1), tile(kv_tile),
                  pl.BlockSpec((None, WINDOW, PROJ_TN), prev_map)],
        out_specs=pl.BlockSpec((tq, ATT_Q_W), lambda b, i: (b * nq + i, 0)),
        compiler_params=_cparams(("arbitrary", "arbitrary")),
        name="swa_attention",
    )(sinks, proj, proj, proj, proj)


def _ret_consts():
    C = RET_CHUNK
    log_g = jnp.log1p(-jnp.exp2(-5.0 - jnp.arange(RET_HEADS, dtype=F32)))
    i = jnp.arange(C, dtype=F32)
    diff = i[:, None] - i[None, :]
    dm = jnp.where(diff[None] >= 0, jnp.exp(jnp.maximum(diff, 0.0)[None] * log_g[:, None, None]), 0.0)
    qd = jnp.exp((i + 1.0)[None, :] * log_g[:, None])
    kd = jnp.exp((C - 1.0 - i)[None, :] * log_g[:, None])
    cd = jnp.exp(C * log_g)
    bc = lambda v: jnp.broadcast_to(v[:, :, None], (RET_HEADS, C, LANES))
    return dm, bc(qd), bc(kd), jnp.broadcast_to(cd[:, None, None], (RET_HEADS, 8, LANES))


def _ret_kernel(q_ref, k_ref, v_ref, g_ref, gn_ref, dm_ref, qd_ref, kd_ref, cd_ref, o_ref, *, nchunk):
    C = RET_CHUNK
    heads = range(RET_HEADS_PER_STEP)
    lanes = [slice(hh * LANES, (hh + 1) * LANES) for hh in heads]
    rows = [slice(c * C, (c + 1) * C) for c in range(nchunk)]

    def independent(c, hh):
        q, k, v = q_ref[rows[c], lanes[hh]], k_ref[rows[c], lanes[hh]], v_ref[rows[c], lanes[hh]]
        s = lax.dot_general(q, k, (((1,), (1,)), ((), ())), preferred_element_type=F32) * dm_ref[hh]
        vk = (v.astype(F32) * kd_ref[hh]).astype(BF16)
        kv = lax.dot_general(k, vk, (((0,), (0,)), ((), ())), preferred_element_type=F32)
        return s.astype(BF16), kv

    def finish(c, hh, s, state):
        q, v = q_ref[rows[c], lanes[hh]], v_ref[rows[c], lanes[hh]]
        y = jnp.dot(s, v, preferred_element_type=F32)
        y = y + jnp.dot(q, state.astype(BF16), preferred_element_type=F32) * qd_ref[hh]
        ms = jnp.mean(y * y, axis=-1, keepdims=True)
        yn = y * lax.rsqrt(ms + EPS) * gn_ref[:, lanes[hh]]
        o_ref[rows[c], lanes[hh]] = (g_ref[rows[c], lanes[hh]].astype(F32) * yn).astype(BF16)

    state = [jnp.zeros((RET_HEAD_DIM, RET_HEAD_DIM), F32) for _ in heads]
    ahead = [independent(0, hh) for hh in heads]
    for c in range(nchunk):
        cur = ahead
        if c + 1 < nchunk:
            ahead = [independent(c + 1, hh) for hh in heads]
        for hh in heads:
            s, kv = cur[hh]
            finish(c, hh, s, state[hh])
            state[hh] = cd_ref[hh, 0:1, :] * state[hh] + kv


def retention(proj, ret_norm_g, l, B, S):
    T = proj.shape[1]
    dm, qd, kd, cd = _ret_consts()
    hp = RET_HEADS_PER_STEP
    wid = hp * LANES
    per_tile = PROJ_TN // wid

    def part(off):
        return pl.BlockSpec((None, S, wid), lambda b, h: (off // PROJ_TN + h // per_tile, b, lax.rem(h, per_tile)))
    hconst = lambda rows: pl.BlockSpec((hp, rows, LANES), lambda b, h: (h, 0, 0))
    return pl.pallas_call(
        functools.partial(_ret_kernel, nchunk=S // RET_CHUNK),
        out_shape=jax.ShapeDtypeStruct((T, RET_W), BF16),
        grid=(B, RET_HEADS // hp),
        in_specs=[part(OFF_QR), part(OFF_KR), part(OFF_VR), part(OFF_GR),
                  pl.BlockSpec((None, 1, wid), lambda b, h: (l, 0, h)),
                  pl.BlockSpec((hp, RET_CHUNK, RET_CHUNK), lambda b, h: (h, 0, 0)),
                  hconst(RET_CHUNK), hconst(RET_CHUNK), hconst(8)],
        out_specs=pl.BlockSpec((S, wid), lambda b, h: (b, h)),
        compiler_params=_cparams(("arbitrary", "arbitrary")),
        name="retention",
    )(proj, proj, proj, proj, ret_norm_g.reshape(-1, 1, RET_W), dm, qd, kd, cd)


def _load_cast(w_hbm, dst, stage, sem):
    rows = stage.shape[0]
    n = w_hbm.shape[0] // rows

    def body(i, carry):
        r0 = pl.multiple_of(i * rows, rows)
        cp = pltpu.make_async_copy(w_hbm.at[pl.ds(r0, rows), :], stage, sem)
        cp.start()
        cp.wait()
        dst[pl.ds(r0, rows), :] = stage[...].astype(BF16)
        return carry
    lax.fori_loop(0, n, body, 0)


def _mixer_out_kernel(att_ref, ret_ref, ga0, ga1, ga2, ga3, gb0, gb1, gb2, gb3, x_ref,
                      wa_hbm, wr_hbm, wo_hbm, gate1_ref, g2_ref, sh2_ref, sc2_ref,
                      wrt_ref, brt_ref, tril_ref,
                      xn_ref, h2_ref, rti_ref, rtw_ref, cnt_ref,
                      wa, wr, wo, wrt2, mrg, carry, hs, stage, sem, *, tm, l):
    i = pl.program_id(0)
    nt = pl.num_programs(0) - 1
    D = D_MODEL

    @pl.when(i == 0)
    def _():
        _load_cast(wa_hbm.at[l], wa, stage, sem)
        _load_cast(wr_hbm.at[l], wr, stage, sem)
        _load_cast(wo_hbm.at[l], wo, stage, sem)
        w = wrt_ref[...]
        hi = w.astype(BF16)
        wrt2[:, 0:LANES] = hi
        wrt2[:, LANES:2 * LANES] = (w - hi.astype(F32)).astype(BF16)
        carry[...] = jnp.zeros_like(carry)
        hs[...] = jnp.zeros_like(hs)

    def main_stage():
        a = att_ref[...]
        r = ret_ref[...]
        ga = (ga0, ga1, ga2, ga3)
        gb = (gb0, gb1, gb2, gb3)
        for n in range(D // PROJ_TN):
            cols = slice(n * PROJ_TN, (n + 1) * PROJ_TN)
            ya = jnp.dot(a, wa[:, cols], preferred_element_type=F32)
            yr = jnp.dot(r, wr[:, cols], preferred_element_type=F32)
            mrg[:, cols] = (ga[n][...].astype(F32) * ya + gb[n][...].astype(F32) * yr).astype(BF16)
        o = jnp.dot(mrg[...], wo[...], preferred_element_type=F32)
        xn = x_ref[...] + gate1_ref[...] * o
        xn_ref[...] = xn
        h2 = _norm_mod(xn, g2_ref[...], sh2_ref[...], sc2_ref[...])
        h2b = h2.astype(BF16)
        h2_ref[...] = h2b
        hs[0:tm, :] = h2b
        hs[tm:2 * tm, :] = (h2 - h2b.astype(F32)).astype(BF16)

    def router_select():
        r = jnp.dot(hs[...], wrt2[...], preferred_element_type=F32)
        logits = ((r[0:tm, 0:LANES] + r[tm:2 * tm, 0:LANES])
                  + (r[0:tm, LANES:2 * LANES] + r[tm:2 * tm, LANES:2 * LANES])) + brt_ref[...]
        lane = lax.broadcasted_iota(I32, (tm, LANES), 1)
        valid = lane < N_EXPERTS
        mx = jnp.max(logits, axis=-1, keepdims=True)
        p = jnp.where(valid, jnp.exp(logits - mx), 0.0)
        pos_in_grp = lane % EXPERTS_PER_GROUP
        grp_of = lane // EXPERTS_PER_GROUP

        def member(k):
            wrapped = pos_in_grp + k >= EXPERTS_PER_GROUP
            return jnp.where(wrapped, pltpu.roll(p, EXPERTS_PER_GROUP - k, 1), pltpu.roll(p, LANES - k, 1)), wrapped

        (b1, w1), (b2, w2), (b3, w3) = member(1), member(2), member(3)
        m_ab, n_ab = jnp.maximum(p, b1), jnp.minimum(p, b1)
        m_cd, n_cd = jnp.maximum(b2, b3), jnp.minimum(b2, b3)
        gscore = jnp.maximum(m_ab, m_cd) + jnp.maximum(jnp.minimum(m_ab, m_cd), jnp.maximum(n_ab, n_cd))
        gscore = jnp.where(valid, gscore, -1.0)
        gmax = jnp.max(gscore, axis=-1, keepdims=True)
        gsel = jnp.min(jnp.where(gscore == gmax, grp_of, N_GROUPS), axis=-1, keepdims=True)
        in_sel = grp_of == gsel
        beats = lambda b, w: ((b > p) | ((b == p) & w)).astype(I32)
        rank_in_grp = beats(b1, w1) + beats(b2, w2) + beats(b3, w3)
        sel0 = in_sel & (rank_in_grp == 0)
        sel1 = in_sel & (rank_in_grp == 1)
        lsum = lambda m, v: jnp.sum(jnp.where(m, v, 0.0), axis=-1, keepdims=True)
        v0, v1 = lsum(sel0, p), lsum(sel1, p)
        lanef = lane.astype(F32)
        e0, e1 = lsum(sel0, lanef), lsum(sel1, lanef)
        inv = 1.0 / (v0 + v1)
        return lane, sel0, sel1, e0, e1, v0 * inv, v1 * inv

    def router_finish(live, lane, sel0, sel1, e0, e1, w0, w1):
        lsum = lambda m, v: jnp.sum(jnp.where(m, v, 0.0), axis=-1, keepdims=True)
        onehot = jnp.where(sel0 | sel1, 1.0, 0.0)
        prefix = jnp.dot(tril_ref[...], onehot.astype(BF16), preferred_element_type=F32) + carry[0:1, :]
        r0, r1 = lsum(sel0, prefix), lsum(sel1, prefix)
        carry[...] = carry[...] + live * jnp.sum(onehot, axis=0, keepdims=True)
        cnt_ref[...] = carry[...].astype(I32)
        sel4 = lambda a0, a1, a2, a3: jnp.where(lane == 0, a0, jnp.where(lane == 1, a1, jnp.where(lane == 2, a2, a3)))
        rti_ref[...] = sel4(e0, e1, r0, r1).astype(I32)
        rtw_ref[...] = jnp.where(lane == 0, w0, jnp.where(lane == 1, w1, 0.0))

    @pl.when(i < nt)
    def _():
        picked = router_select()
        main_stage()
        router_finish(jnp.where(i > 0, 1.0, 0.0), *picked)

    @pl.when(i == nt)
    def _():
        router_finish(1.0, *router_select())


def mixer_out(att, ret, proj, x2, w_attn_out, w_ret_out, w_o, modr, norm2_g, w_router, b_router, l, S):
    T, D = x2.shape
    tm = 256
    assert S % tm == 0
    nga, ngb = OFF_GA // PROJ_TN, OFF_GB // PROJ_TN
    nt = T // tm
    cur = lambda i: jnp.minimum(i, nt - 1)
    prev = lambda i: jnp.maximum(i - 1, 0)
    gate_spec = lambda blk: pl.BlockSpec((None, tm, PROJ_TN), lambda i: (blk, cur(i), 0))
    row = lambda: pl.BlockSpec((tm, D), lambda i: (cur(i), 0))
    mod_spec = lambda k: pl.BlockSpec((None, None, None, 1, D), lambda i: (l, k, (cur(i) * tm) // S, 0, 0))
    wrt = jnp.pad(w_router, ((0, 0), (0, LANES - N_EXPERTS)))
    brt = jnp.pad(b_router.astype(F32), (0, LANES - N_EXPERTS), constant_values=-1e30).reshape(1, LANES)
    tril = jnp.asarray(np.tril(np.ones((tm, tm), np.float32), -1), BF16)
    any_spec = pl.BlockSpec(memory_space=pl.ANY)
    outs = pl.pallas_call(
        functools.partial(_mixer_out_kernel, tm=tm, l=l),
        out_shape=(jax.ShapeDtypeStruct((T, D), F32),
                   jax.ShapeDtypeStruct((T, D), BF16),
                   jax.ShapeDtypeStruct((T, LANES), I32),
                   jax.ShapeDtypeStruct((T, LANES), F32),
                   jax.ShapeDtypeStruct((8, LANES), I32)),
        grid=(nt + 1,),
        in_specs=[pl.BlockSpec((tm, ATT_Q_W), lambda i: (cur(i), 0)),
                  pl.BlockSpec((tm, RET_W), lambda i: (cur(i), 0)),
                  *[gate_spec(nga + n) for n in range(4)],
                  *[gate_spec(ngb + n) for n in range(4)],
                  row(), any_spec, any_spec, any_spec,
                  mod_spec(2),
                  pl.BlockSpec((None, 1, D), lambda i: (l, 0, 0)),
                  mod_spec(3), mod_spec(4),
                  pl.BlockSpec((D, LANES), lambda i: (0, 0)),
                  pl.BlockSpec((1, LANES), lambda i: (0, 0)),
                  pl.BlockSpec((tm, tm), lambda i: (0, 0))],
        out_specs=(row(),
                   row(),
                   pl.BlockSpec((tm, LANES), lambda i: (prev(i), 0)),
                   pl.BlockSpec((tm, LANES), lambda i: (prev(i), 0)),
                   pl.BlockSpec((8, LANES), lambda i: (0, 0))),
        scratch_shapes=[pltpu.VMEM((ATT_Q_W, D), BF16), pltpu.VMEM((RET_W, D), BF16),
                        pltpu.VMEM((D, D), BF16),
                        pltpu.VMEM((D, 2 * LANES), BF16),
                        pltpu.VMEM((tm, D), BF16), pltpu.VMEM((8, LANES), F32),
                        pltpu.VMEM((2 * tm, D), BF16),
                        pltpu.VMEM((256, D), F32), pltpu.SemaphoreType.DMA],
        compiler_params=_cparams(("arbitrary",)),
        name="mixer_out",
    )(att, ret, *([proj] * 8), x2, w_attn_out, w_ret_out, w_o,
      modr, norm2_g.reshape(-1, 1, D), modr, modr, wrt, brt, tril)
    return outs


def moe_plan(counts, n_work):
    sub_per = MOE_ROWS // MOE_SUB
    seg = ((counts + MOE_SUB - 1) // MOE_SUB) * MOE_SUB
    off = jnp.cumsum(seg) - seg
    off17 = jnp.concatenate([off, off[-1:] + seg[-1:]]).astype(I32)
    nb = (counts + MOE_ROWS - 1) // MOE_ROWS
    cum = jnp.cumsum(nb)
    total = cum[-1]
    w = jnp.arange(n_work, dtype=I32)
    wc = jnp.minimum(w, total - 1)
    e_w = jnp.sum((cum[None, :] <= wc[:, None]).astype(I32), axis=1)
    blk = wc - (cum[e_w] - nb[e_w])
    start = off[e_w] + blk * MOE_ROWS
    nsub = jnp.clip(seg[e_w] // MOE_SUB - blk * sub_per, 0, sub_per)
    nsub = jnp.where(w < total, nsub, 0)
    return off17, e_w, start.astype(I32), nsub.astype(I32)


def _row_dma_loops(n_rows, make_copy):
    def run(op):
        def body(g, carry):
            r0 = pl.multiple_of(g * ROW_GROUP, ROW_GROUP)
            for u in range(ROW_GROUP):
                for k in range(2):
                    getattr(make_copy(r0 + u, k), op)()
            return carry
        lax.fori_loop(0, n_rows // ROW_GROUP, body, 0)
    run("start")
    run("wait")


def _dispatch_kernel(pos_ref, off_ref, h_ref, xs_ref, rows_ref, sem, *, td, T):
    i = pl.program_id(0)
    zero_ref = rows_ref

    @pl.when(i == 0)
    def _():
        zero_ref[...] = jnp.zeros_like(zero_ref)
        used = off_ref[N_EXPERTS]

        def zero_copy(row):
            row = pl.multiple_of(row, MOE_SUB)
            return pltpu.make_async_copy(zero_ref, xs_ref.at[pl.ds(row, MOE_SUB), :], sem)

        def fill(op):
            def seg_tail(e, carry):
                @pl.when(off_ref[e + 1] > off_ref[e])
                def _():
                    getattr(zero_copy(off_ref[e + 1] - MOE_SUB), op)()
                return carry

            def buf_tail(n, carry):
                getattr(zero_copy(used + n * MOE_SUB), op)()
                return carry
            lax.fori_loop(0, N_EXPERTS, seg_tail, 0)
            lax.fori_loop(0, (xs_ref.shape[0] - used) // MOE_SUB, buf_tail, 0)
        fill("start")
        fill("wait")

    rows_ref[...] = h_ref[...].astype(F32)

    def copy(r, k):
        p = pos_ref[k * T + i * td + r]
        return pltpu.make_async_copy(rows_ref.at[pl.ds(r, 1), :], xs_ref.at[pl.ds(p, 1), :], sem)
    _row_dma_loops(td, copy)


def dispatch(h2, pos_flat, off17, n_rows):
    T, D = h2.shape
    td = MOE_SUB
    return pl.pallas_call(
        functools.partial(_dispatch_kernel, td=td, T=T),
        out_shape=jax.ShapeDtypeStruct((n_rows, D), F32),
        grid_spec=pltpu.PrefetchScalarGridSpec(
            num_scalar_prefetch=2, grid=(T // td,),
            in_specs=[pl.BlockSpec((td, D), lambda i, *_: (i, 0))],
            out_specs=pl.BlockSpec(memory_space=pl.ANY),
            scratch_shapes=[pltpu.VMEM((td, D), F32), pltpu.SemaphoreType.DMA]),
        compiler_params=_cparams(("arbitrary",)),
        name="moe_dispatch",
    )(pos_flat, off17, h2)


def _moe_kernel(we_ref, ws_ref, wn_ref, xs_ref, wg_ref, wu_ref, wd_ref, ys_ref,
                xb, acc, wgb, wub, wdb, stage, ostage, sem_in, sem_out, *, nfc, n_work):
    w = pl.program_id(0)
    c = pl.program_id(1)
    n = wn_ref[w]
    s0 = ws_ref[w]

    def in_copy(start_row, i, slot):
        g0 = pl.multiple_of(start_row + i * MOE_SUB, MOE_SUB)
        return pltpu.make_async_copy(xs_ref.at[pl.ds(g0, MOE_SUB), :], stage.at[slot], sem_in.at[slot])

    def out_copy(i, slot):
        g0 = pl.multiple_of(s0 + i * MOE_SUB, MOE_SUB)
        return pltpu.make_async_copy(ostage.at[slot], ys_ref.at[pl.ds(g0, MOE_SUB), :], sem_out.at[slot])

    def request_first_two(start_row, count):
        @pl.when(count > 0)
        def _():
            in_copy(start_row, 0, 0).start()

        @pl.when(count > 1)
        def _():
            in_copy(start_row, 1, 1).start()

    @pl.when((w == 0) & (c == 0))
    def _():
        request_first_two(s0, n)

    def for_subs(fn):
        def body(p, carry):
            fn(p * 2, 2)
            return carry
        lax.fori_loop(0, n // 2, body, 0)

        @pl.when(lax.rem(n, 2) == 1)
        def _():
            fn(n - 1, 1)

    def rows_of(i):
        return pl.ds(pl.multiple_of(i * MOE_SUB, MOE_SUB), MOE_SUB)

    def ffn(i0, count):
        xs_ = [xb[rows_of(i0 + a), :] for a in range(count)]
        gu = [(jnp.dot(x, wgb[...], preferred_element_type=F32), jnp.dot(x, wub[...], preferred_element_type=F32))
              for x in xs_]
        acts = [(g * _sigmoid(g) * u).astype(BF16) for g, u in gu]
        return [jnp.dot(a, wdb[...], preferred_element_type=F32) for a in acts]

    @pl.when(n > 0)
    def _():
        wgb[...] = wg_ref[...].astype(BF16)
        wub[...] = wu_ref[...].astype(BF16)
        wdb[...] = wd_ref[...].astype(BF16)

        @pl.when(c == 0)
        def _():
            def first(i0, count):
                for a in range(count):
                    in_copy(s0, i0 + a, a).wait()
                    xb[rows_of(i0 + a), :] = stage[a].astype(BF16)
                for a in range(count):
                    @pl.when(i0 + a + 2 < n)
                    def _():
                        in_copy(s0, i0 + a + 2, a).start()
                for a, y in enumerate(ffn(i0, count)):
                    acc[rows_of(i0 + a), :] = y
            for_subs(first)

        @pl.when((c > 0) & (c < nfc - 1))
        def _():
            def middle(i0, count):
                for a, y in enumerate(ffn(i0, count)):
                    acc[rows_of(i0 + a), :] += y
            for_subs(middle)

        @pl.when(c == nfc - 1)
        def _():
            def last(i0, count):
                for a in range(count):
                    @pl.when(i0 + a >= 2)
                    def _():
                        out_copy(i0 + a - 2, a).wait()
                for a, y in enumerate(ffn(i0, count)):
                    ostage[a] = acc[rows_of(i0 + a), :] + y
                    out_copy(i0 + a, a).start()
            for_subs(last)

            @pl.when(n >= 2)
            def _():
                out_copy(n - 2, lax.rem(n, 2)).wait()
            out_copy(n - 1, lax.rem(n - 1, 2)).wait()

            @pl.when(w + 1 < n_work)
            def _():
                nxt = jnp.minimum(w + 1, n_work - 1)
                request_first_two(ws_ref[nxt], wn_ref[nxt])


def moe_experts(xs, w_gate, w_up, w_down, l, e_w, start_w, nsub_w):
    n_rows, D = xs.shape
    nfc = D_FF // MOE_FC
    assert nfc >= 2
    n_work = e_w.shape[0]

    def chunk(c, wn, w):
        return jnp.where(wn[w] > 0, c, nfc - 1)

    return pl.pallas_call(
        functools.partial(_moe_kernel, nfc=nfc, n_work=n_work),
        out_shape=jax.ShapeDtypeStruct((n_rows, D), F32),
        grid_spec=pltpu.PrefetchScalarGridSpec(
            num_scalar_prefetch=3, grid=(n_work, nfc),
            in_specs=[pl.BlockSpec(memory_space=pl.ANY),
                      pl.BlockSpec((None, None, D, MOE_FC), lambda w, c, we, ws, wn: (l, we[w], 0, chunk(c, wn, w))),
                      pl.BlockSpec((None, None, D, MOE_FC), lambda w, c, we, ws, wn: (l, we[w], 0, chunk(c, wn, w))),
                      pl.BlockSpec((None, None, MOE_FC, D), lambda w, c, we, ws, wn: (l, we[w], chunk(c, wn, w), 0))],
            out_specs=pl.BlockSpec(memory_space=pl.ANY),
            scratch_shapes=[pltpu.VMEM((MOE_ROWS, D), BF16), pltpu.VMEM((MOE_ROWS, D), F32),
                            pltpu.VMEM((D, MOE_FC), BF16), pltpu.VMEM((D, MOE_FC), BF16),
                            pltpu.VMEM((MOE_FC, D), BF16),
                            pltpu.VMEM((2, MOE_SUB, D), F32), pltpu.VMEM((2, MOE_SUB, D), F32),
                            pltpu.SemaphoreType.DMA((2,)), pltpu.SemaphoreType.DMA((2,))]),
        input_output_aliases={3: 0},
        compiler_params=_cparams(("arbitrary", "arbitrary")),
        name="moe_experts",
    )(e_w, start_w, nsub_w, xs, w_gate, w_up, w_down)


def _combine_kernel(pos_ref, ys_ref, x_ref, rtw_ref, gate2_ref, g_ref, *rest, tc, T, final):
    if final:
        out_ref, ybuf, sem = rest
    else:
        sh_ref, sc_ref, xo_ref, h_ref, ybuf, sem = rest
    i = pl.program_id(0)
    slot = lax.rem(i, 2)
    ngroups = tc // COMBINE_GROUP

    def copy(tile, r, k, sl):
        p = pos_ref[k * T + tile * tc + r]
        return pltpu.make_async_copy(ys_ref.at[pl.ds(p, 1), :], ybuf.at[sl, k, pl.ds(r, 1), :], sem.at[sl])

    def for_rows(tile, sl, r0, op):
        for u in range(COMBINE_GROUP):
            for k in range(2):
                getattr(copy(tile, r0 + u, k, sl), op)()

    def groups(fn):
        def body(g, carry):
            fn(pl.multiple_of(g * COMBINE_GROUP, COMBINE_GROUP))
            return carry
        lax.fori_loop(0, ngroups, body, 0)

    def compute(r0):
        rows = pl.ds(r0, COMBINE_GROUP)
        moe = rtw_ref[rows, 0:1] * ybuf[slot, 0, rows, :] + rtw_ref[rows, 1:2] * ybuf[slot, 1, rows, :]
        xo = x_ref[rows, :] + gate2_ref[...] * moe
        if final:
            ms = jnp.mean(xo * xo, axis=-1, keepdims=True)
            out_ref[rows, :] = xo * lax.rsqrt(ms + EPS) * g_ref[...]
        else:
            xo_ref[rows, :] = xo
            h_ref[rows, :] = _norm_mod(xo, g_ref[...], sh_ref[...], sc_ref[...]).astype(BF16)

    @pl.when(i == 0)
    def _():
        groups(lambda r0: for_rows(0, 0, r0, "start"))
    groups(lambda r0: for_rows(i, slot, r0, "wait"))

    @pl.when(i + 1 < pl.num_programs(0))
    def _():
        def both(r0):
            compute(r0)
            for_rows(i + 1, 1 - slot, r0, "start")
        groups(both)

    @pl.when(i + 1 >= pl.num_programs(0))
    def _():
        groups(compute)


def combine(ys, xn, rtw, pos_flat, modr, l, S, g_next, final):
    T, D = xn.shape
    tc = 256
    assert S % tc == 0
    row = lambda: pl.BlockSpec((tc, D), lambda i, *_: (i, 0))
    in_specs = [pl.BlockSpec(memory_space=pl.ANY), row(),
                pl.BlockSpec((tc, LANES), lambda i, *_: (i, 0)),
                _mod_spec(l, 5, S, tc)]
    args = [ys, xn, rtw, modr]
    if final:
        in_specs.append(pl.BlockSpec((1, D), lambda i, *_: (0, 0)))
        args.append(g_next.reshape(1, D))
        out_shape = jax.ShapeDtypeStruct((T, D), F32)
        out_specs = row()
    else:
        in_specs += [pl.BlockSpec((None, 1, D), lambda i, *_: (l + 1, 0, 0)),
                     _mod_spec(l + 1, 0, S, tc), _mod_spec(l + 1, 1, S, tc)]
        args += [g_next.reshape(-1, 1, D), modr, modr]
        out_shape = (jax.ShapeDtypeStruct((T, D), F32), jax.ShapeDtypeStruct((T, D), BF16))
        out_specs = (row(), row())
    return pl.pallas_call(
        functools.partial(_combine_kernel, tc=tc, T=T, final=final),
        out_shape=out_shape,
        grid_spec=pltpu.PrefetchScalarGridSpec(
            num_scalar_prefetch=1, grid=(T // tc,),
            in_specs=in_specs, out_specs=out_specs,
            scratch_shapes=[pltpu.VMEM((2, 2, tc, D), F32), pltpu.SemaphoreType.DMA((2,))]),
        compiler_params=_cparams(("arbitrary",)),
        name="moe_combine_final" if final else "moe_combine",
    )(pos_flat, *args)


def kernel(x, c, positions, w_ada, b_ada, norm1_g, norm2_g, w_in, attn_sinks, w_attn_out, ret_norm_g,
           w_ret_out, w_o, w_router, b_router, w_gate, w_up, w_down, final_g):
    B, S, D = x.shape
    L = w_ada.shape[0]
    T = B * S
    assert D == D_MODEL and w_in.shape[-1] == IN_W and S % RET_CHUNK == 0

    tab = rope_tables(positions)
    mod = adaln_mod(c, w_ada, b_ada)
    modr = mod[:, :B].reshape(L, B, 6, D).transpose(0, 2, 1, 3).reshape(L, 6, B, 1, D)

    x2 = x.reshape(T, D)
    h = norm_modulate(x2, norm1_g.reshape(L, 1, D), modr, 0, S)
    n_work = N_EXPERTS + (2 * T + MOE_ROWS - 1) // MOE_ROWS
    n_rows = 2 * T + N_EXPERTS * MOE_SUB
    out = None
    for l in range(L):
        proj = in_projection(h, w_in, l, tab)
        att = swa_attention(proj, attn_sinks[l], B, S)
        ret = retention(proj, ret_norm_g, l, B, S)
        xn, h2, rti, rtw, cnt = mixer_out(att, ret, proj, x2, w_attn_out, w_ret_out, w_o, modr,
                                          norm2_g, w_router, b_router, l, S)
        off17, e_w, start_w, nsub_w = moe_plan(cnt[0, :N_EXPERTS], n_work)
        seg_off = jnp.sum(jnp.where(rti[:, 0:2, None] == jnp.arange(N_EXPERTS, dtype=I32),
                                    off17[:N_EXPERTS], 0), axis=-1)
        pos_flat = (seg_off + rti[:, 2:4]).T.reshape(2 * T)
        xs = dispatch(h2, pos_flat, off17, n_rows)
        ys = moe_experts(xs, w_gate, w_up, w_down, l, e_w, start_w, nsub_w)
        if l + 1 < L:
            x2, h = combine(ys, xn, rtw, pos_flat, modr, l, S, norm1_g, final=False)
        else:
            out = combine(ys, xn, rtw, pos_flat, modr, l, S, final_g, final=True)
    return out.reshape(B, S, D)
```

```python
import functools

import numpy as np
import jax
import jax.numpy as jnp
from jax import lax
from jax.experimental import pallas as pl
from jax.experimental.pallas import tpu as pltpu

F32 = jnp.float32
BF16 = jnp.bfloat16
I32 = jnp.int32

D_MODEL = 2048
ATT_HEAD_DIM = 64
ATT_Q_HEADS = 16
ATT_KV_HEADS = 4
WINDOW = 128
RET_HEADS = 8
RET_HEAD_DIM = 128
RET_CHUNK = 256
ROPE_THETA = 10000.0
N_GROUPS = 4
EXPERTS_PER_GROUP = 4
N_EXPERTS = 16
D_FF = 1024
EPS = 1e-6

ATT_Q_W = ATT_Q_HEADS * ATT_HEAD_DIM
ATT_KV_W = ATT_KV_HEADS * ATT_HEAD_DIM
RET_W = RET_HEADS * RET_HEAD_DIM
OFF_QA = 0
OFF_KA = OFF_QA + ATT_Q_W
OFF_VA = OFF_KA + ATT_KV_W
OFF_QR = OFF_VA + ATT_KV_W
OFF_KR = OFF_QR + RET_W
OFF_VR = OFF_KR + RET_W
OFF_GR = OFF_VR + RET_W
OFF_GA = OFF_GR + RET_W
OFF_GB = OFF_GA + D_MODEL
IN_W = OFF_GB + D_MODEL

LANES = 128
PROJ_TN = 512
ROW_GROUP = 8
ATTN_LOOKAHEAD = 2
COMBINE_GROUP = 64
RET_HEADS_PER_STEP = 2
MOE_SUB = 256
MOE_ROWS = 2048
MOE_FC = 256
VMEM_LIMIT = 56 * 1024 * 1024


def _pick(n, cands):
    for c in cands:
        if n % c == 0:
            return c
    raise ValueError(f"no tile in {cands} divides {n}")


def _sigmoid(x):
    return 1.0 / (1.0 + jnp.exp(-x))


def _cparams(sem, vmem=VMEM_LIMIT):
    return pltpu.CompilerParams(dimension_semantics=sem, vmem_limit_bytes=vmem)


def _rope_consts():
    def inv_freq(head_dim):
        half = head_dim // 2
        inv = ROPE_THETA ** (-2.0 * jnp.arange(half, dtype=F32) / head_dim)
        return jnp.broadcast_to(jnp.tile(inv, LANES // half), (8, LANES))
    lane = np.arange(LANES)
    sgn_att = np.where((lane % ATT_HEAD_DIM) < ATT_HEAD_DIM // 2, -1.0, 1.0)
    sgn_ret = np.where(lane < RET_HEAD_DIM // 2, -1.0, 1.0)
    sgn = np.stack([np.tile(sgn_att, (8, 1)), np.tile(sgn_ret, (8, 1))]).astype(np.float32)
    return jnp.stack([inv_freq(ATT_HEAD_DIM), inv_freq(RET_HEAD_DIM)]), jnp.asarray(sgn)


def _rope_kernel(pos_ref, inv_ref, sgn_ref, tab_ref):
    pos = pos_ref[...]
    for kind in range(2):
        ang = pos * inv_ref[kind, 0:1, :]
        tab_ref[kind, :, 0:LANES] = jnp.cos(ang)
        tab_ref[kind, :, LANES:2 * LANES] = jnp.sin(ang) * sgn_ref[kind, 0:1, :]


def rope_tables(positions):
    T = positions.size
    posb = jnp.broadcast_to(positions.reshape(T, 1).astype(F32), (T, LANES))
    inv, sgn = _rope_consts()
    tm = _pick(T, (1024, 512, 256, 128))
    return pl.pallas_call(
        _rope_kernel,
        out_shape=jax.ShapeDtypeStruct((2, T, 2 * LANES), F32),
        grid=(T // tm,),
        in_specs=[pl.BlockSpec((tm, LANES), lambda i: (i, 0)),
                  pl.BlockSpec((2, 8, LANES), lambda i: (0, 0, 0)),
                  pl.BlockSpec((2, 8, LANES), lambda i: (0, 0, 0))],
        out_specs=pl.BlockSpec((2, tm, 2 * LANES), lambda i: (0, i, 0)),
        compiler_params=_cparams(("arbitrary",)),
        name="rope_tables",
    )(posb, inv, sgn)


def _adaln_kernel(c_ref, w_ref, b_ref, o_ref):
    c = c_ref[...]
    ca = c * _sigmoid(c)
    hi = ca.astype(BF16).astype(F32)
    lhs = jnp.concatenate([hi, ca - hi], axis=0).astype(BF16)
    r = jnp.dot(lhs, w_ref[...].astype(BF16), preferred_element_type=F32)
    o_ref[...] = r[0:8] + r[8:16] + b_ref[...]


def adaln_mod(c, w_ada, b_ada):
    L, D, N = w_ada.shape
    B = c.shape[0]
    cp = jnp.pad(c, ((0, 8 - B), (0, 0)))
    tn = _pick(N, (1024, 512))
    return pl.pallas_call(
        _adaln_kernel,
        out_shape=jax.ShapeDtypeStruct((L, 8, N), F32),
        grid=(L, N // tn),
        in_specs=[pl.BlockSpec((8, D), lambda l, j: (0, 0)),
                  pl.BlockSpec((None, D, tn), lambda l, j: (l, 0, j)),
                  pl.BlockSpec((None, 1, tn), lambda l, j: (l, 0, j))],
        out_specs=pl.BlockSpec((None, 8, tn), lambda l, j: (l, 0, j)),
        compiler_params=_cparams(("arbitrary", "arbitrary")),
        name="adaln_mod",
    )(cp, w_ada, b_ada.reshape(L, 1, N))


def _norm_mod(x, g, shift, scale):
    ms = jnp.mean(x * x, axis=-1, keepdims=True)
    return (x * lax.rsqrt(ms + EPS) * g) * (1.0 + scale) + shift


def _norm_mod_kernel(x_ref, g_ref, sh_ref, sc_ref, h_ref):
    h_ref[...] = _norm_mod(x_ref[...], g_ref[...], sh_ref[...], sc_ref[...]).astype(BF16)


def _mod_spec(l, k, S, tm):
    return pl.BlockSpec((None, None, None, 1, D_MODEL), lambda i, *_: (l, k, (i * tm) // S, 0, 0))


def norm_modulate(x2, g, modr, l, S):
    T, D = x2.shape
    tm = _pick(S, (512, 256, 128))
    return pl.pallas_call(
        _norm_mod_kernel,
        out_shape=jax.ShapeDtypeStruct((T, D), BF16),
        grid=(T // tm,),
        in_specs=[pl.BlockSpec((tm, D), lambda i: (i, 0)),
                  pl.BlockSpec((None, 1, D), lambda i: (l, 0, 0)),
                  _mod_spec(l, 0, S, tm), _mod_spec(l, 1, S, tm)],
        out_specs=pl.BlockSpec((tm, D), lambda i: (i, 0)),
        compiler_params=_cparams(("arbitrary",)),
        name="norm1_modulate",
    )(x2, g, modr, modr)


def _rope_att(blk, cos, sin, scale):
    lane = lax.broadcasted_iota(I32, blk.shape, 1)
    rot = jnp.where((lane % 64) < 32, pltpu.roll(blk, 96, 1), pltpu.roll(blk, 32, 1))
    out = blk * cos + rot * sin
    return out * scale if scale != 1.0 else out


def _rope_ret(blk, cos, sin, scale):
    out = blk * cos + pltpu.roll(blk, 64, 1) * sin
    return out * scale if scale != 1.0 else out


def _inproj_kernel(h_ref, w_ref, tab_ref, o_ref, wbf_ref, acc_ref, *, ni, ntiles):
    s = pl.program_id(0)
    j = jnp.maximum(s - 1, 0) // ni
    tm = o_ref.shape[0]
    h_rows = pl.ds(pl.multiple_of(lax.rem(jnp.minimum(s, ntiles - 1), ni) * tm, tm), tm)

    @pl.when(s == 0)
    def _():
        acc_ref[1] = jnp.zeros(acc_ref.shape[1:], F32)

    @pl.when((lax.rem(s, ni) == 0) & (s < ntiles))
    def _():
        wbf_ref[...] = w_ref[...].astype(BF16)

    ngrp = PROJ_TN // LANES
    t = lambda off: off // PROJ_TN
    q_scale = float(ATT_HEAD_DIM) ** -0.5
    k_scale = float(RET_HEAD_DIM) ** -0.5

    def step(cur):
        def run(epilogue):
            cos = tab_ref[:, 0:LANES]
            sin = tab_ref[:, LANES:2 * LANES]
            for g in range(ngrp):
                blk = acc_ref[1 - cur, :, g * LANES:(g + 1) * LANES]
                o_ref[:, g * LANES:(g + 1) * LANES] = epilogue(g, blk, cos, sin).astype(BF16)
            acc_ref[cur] = jnp.dot(h_ref[h_rows, :], wbf_ref[...], preferred_element_type=F32)

        @pl.when(j < t(OFF_KA))
        def _():
            run(lambda g, b, c, s: _rope_att(b, c, s, q_scale))

        @pl.when(j == t(OFF_KA))
        def _():
            run(lambda g, b, c, s: _rope_att(b, c, s, 1.0) if g < ATT_KV_W // LANES else b)

        @pl.when((j >= t(OFF_QR)) & (j < t(OFF_KR)))
        def _():
            run(lambda g, b, c, s: _rope_ret(b, c, s, 1.0))

        @pl.when((j >= t(OFF_KR)) & (j < t(OFF_VR)))
        def _():
            run(lambda g, b, c, s: _rope_ret(b, c, s, k_scale))

        @pl.when((j >= t(OFF_VR)) & (j < t(OFF_GR)))
        def _():
            run(lambda g, b, c, s: b)

        @pl.when((j >= t(OFF_GR)) & (j < t(OFF_GA)))
        def _():
            run(lambda g, b, c, s: b * _sigmoid(b))

        @pl.when(j >= t(OFF_GA))
        def _():
            run(lambda g, b, c, s: _sigmoid(b))

    for parity in range(2):
        pl.when(lax.rem(s, 2) == parity)(functools.partial(step, parity))


def in_projection(h, w_in, l, tab):
    T, D = h.shape
    tm = _pick(T, (1024, 768, 512, 256))
    nj, ni = IN_W // PROJ_TN, T // tm
    ntiles = nj * ni
    t_qr, t_vr = OFF_QR // PROJ_TN, OFF_VR // PROJ_TN
    mm = lambda s: jnp.minimum(s, ntiles - 1)
    ep = lambda s: jnp.maximum(s - 1, 0)

    def tab_map(s):
        j, i = ep(s) // ni, lax.rem(ep(s), ni)
        return (jnp.where(j < t_qr, 0, 1), jnp.where(j < t_vr, i, 0), 0)

    return pl.pallas_call(
        functools.partial(_inproj_kernel, ni=ni, ntiles=ntiles),
        out_shape=jax.ShapeDtypeStruct((nj, T, PROJ_TN), BF16),
        grid=(ntiles + 1,),
        in_specs=[pl.BlockSpec((T, D), lambda s: (0, 0), pipeline_mode=pl.Buffered(1)),
                  pl.BlockSpec((None, D, PROJ_TN), lambda s: (l, 0, mm(s) // ni)),
                  pl.BlockSpec((None, tm, 2 * LANES), tab_map)],
        out_specs=pl.BlockSpec((None, tm, PROJ_TN), lambda s: (ep(s) // ni, lax.rem(ep(s), ni), 0)),
        scratch_shapes=[pltpu.VMEM((D, PROJ_TN), BF16), pltpu.VMEM((2, tm, PROJ_TN), F32)],
        compiler_params=_cparams(("arbitrary",)),
        name="in_projection",
    )(h, w_in, tab)


def _attn_kernel(sinks_ref, q0_ref, q1_ref, kvc_ref, kvp_ref, o_ref, *, tq):
    W = WINDOW
    nsub = tq // W
    is_first = pl.program_id(1) == 0
    qi = lax.broadcasted_iota(I32, (W, 2 * W), 0)
    kj = lax.broadcasted_iota(I32, (W, 2 * W), 1)
    rel = qi + W - kj
    band = (rel >= 0) & (rel < WINDOW)
    band0 = band & (jnp.logical_not(is_first) | (kj >= W))
    lane = lax.broadcasted_iota(I32, (tq + W, LANES), 1)
    lo = lane < ATT_HEAD_DIM

    def split_pair(raw, parity):
        x = raw.astype(F32)
        xr = pltpu.roll(x, ATT_HEAD_DIM, 1)
        if parity == 0:
            return jnp.where(lo, x, 0.0), jnp.where(lo, 0.0, xr)
        return jnp.where(lo, xr, 0.0), jnp.where(lo, 0.0, x)

    units = []
    for hk in range(ATT_KV_HEADS):
        grp, par = hk // 2, hk % 2
        kc = slice(grp * LANES, (grp + 1) * LANES)
        vc = slice(ATT_KV_W + grp * LANES, ATT_KV_W + (grp + 1) * LANES)
        k_pair = split_pair(jnp.concatenate([kvp_ref[:, kc], kvc_ref[:, kc]], axis=0), par)
        v_pair = split_pair(jnp.concatenate([kvp_ref[:, vc], kvc_ref[:, vc]], axis=0), par)
        kt_pair = [k.T.astype(BF16) for k in k_pair]
        v_pair = [v.astype(BF16) for v in v_pair]
        qcols = [slice(hk * 4 * ATT_HEAD_DIM + jq * LANES, hk * 4 * ATT_HEAD_DIM + (jq + 1) * LANES)
                 for jq in range(2)]
        for m in range(nsub):
            units.append((hk, m, kt_pair, v_pair, qcols))

    def scores(unit):
        hk, m, kt_pair, v_pair, qcols = unit
        rows = slice(m * W, (m + 1) * W)
        keys = slice(m * W, m * W + 2 * W)
        kcat = jnp.concatenate([kt_pair[0][:, keys], kt_pair[1][:, keys]], axis=1)
        q_ref = (q0_ref, q1_ref)[hk // 2]
        c0 = (hk % 2) * 4 * ATT_HEAD_DIM
        q = jnp.concatenate([q_ref[rows, c0:c0 + LANES], q_ref[rows, c0 + LANES:c0 + 2 * LANES]], axis=0)
        return jnp.dot(q, kcat, preferred_element_type=F32)

    def finish(unit, s_all):
        hk, m, kt_pair, v_pair, qcols = unit
        rows = slice(m * W, (m + 1) * W)
        keys = slice(m * W, m * W + 2 * W)
        mask = band0 if m == 0 else band
        vcat = jnp.concatenate([v_pair[0][keys], v_pair[1][keys]], axis=0)
        p_rows = []
        for jq in range(2):
            p_cols = []
            for e in range(2):
                s = s_all[jq * W:(jq + 1) * W, e * 2 * W:(e + 1) * 2 * W]
                s = jnp.where(mask, s, -1e30)
                sink = sinks_ref[hk * 4 + jq * 2 + e]
                mx = jnp.maximum(jnp.max(s, axis=1, keepdims=True), sink)
                p = jnp.exp(s - mx)
                den = jnp.sum(p, axis=1, keepdims=True) + jnp.exp(sink - mx)
                p_cols.append((p * (1.0 / den)).astype(BF16))
            p_rows.append(jnp.concatenate(p_cols, axis=1))
        o = jnp.dot(jnp.concatenate(p_rows, axis=0), vcat, preferred_element_type=F32)
        o_ref[rows, qcols[0]] = o[0:W].astype(BF16)
        o_ref[rows, qcols[1]] = o[W:2 * W].astype(BF16)

    pending = [scores(u) for u in units[:ATTN_LOOKAHEAD]]
    for idx, unit in enumerate(units):
        if idx + ATTN_LOOKAHEAD < len(units):
            pending.append(scores(units[idx + ATTN_LOOKAHEAD]))
        finish(unit, pending.pop(0))


def swa_attention(proj, sinks, B, S):
    T = proj.shape[1]
    tq = _pick(S, (512, 384, 256, 128))
    nq = S // tq
    assert ATT_Q_W == 2 * PROJ_TN and 2 * ATT_KV_W == PROJ_TN
    kv_tile = OFF_KA // PROJ_TN
    tile = lambda j: pl.BlockSpec((None, tq, PROJ_TN), lambda b, i: (j, b * nq + i, 0))

    def prev_map(b, i):
        return (kv_tile, jnp.maximum(b * (S // WINDOW) + i * (tq // WINDOW) - 1, 0), 0)

    return pl.pallas_call(
        functools.partial(_attn_kernel, tq=tq),
        out_shape=jax.ShapeDtypeStruct((T, ATT_Q_W), BF16),
        grid=(B, nq),
        in_specs=[pl.BlockSpec(memory_space=pltpu.SMEM),
                  tile(0), tile(1), tile(kv_tile),
                  pl.BlockSpec((None, WINDOW, PROJ_TN), prev_map)],
        out_specs=pl.BlockSpec((tq, ATT_Q_W), lambda b, i: (b * nq + i, 0)),
        compiler_params=_cparams(("arbitrary", "arbitrary")),
        name="swa_attention",
    )(sinks, proj, proj, proj, proj)


def _ret_consts():
    C = RET_CHUNK
    log_g = jnp.log1p(-jnp.exp2(-5.0 - jnp.arange(RET_HEADS, dtype=F32)))
    i = jnp.arange(C, dtype=F32)
    diff = i[:, None] - i[None, :]
    dm = jnp.where(diff[None] >= 0, jnp.exp(jnp.maximum(diff, 0.0)[None] * log_g[:, None, None]), 0.0)
    qd = jnp.exp((i + 1.0)[None, :] * log_g[:, None])
    kd = jnp.exp((C - 1.0 - i)[None, :] * log_g[:, None])
    cd = jnp.exp(C * log_g)
    bc = lambda v: jnp.broadcast_to(v[:, :, None], (RET_HEADS, C, LANES))
    return dm, bc(qd), bc(kd), jnp.broadcast_to(cd[:, None, None], (RET_HEADS, 8, LANES))


def _ret_kernel(q_ref, k_ref, v_ref, g_ref, gn_ref, dm_ref, qd_ref, kd_ref, cd_ref, o_ref, *, nchunk):
    C = RET_CHUNK
    heads = range(RET_HEADS_PER_STEP)
    lanes = [slice(hh * LANES, (hh + 1) * LANES) for hh in heads]
    rows = [slice(c * C, (c + 1) * C) for c in range(nchunk)]

    def independent(c, hh):
        q, k, v = q_ref[rows[c], lanes[hh]], k_ref[rows[c], lanes[hh]], v_ref[rows[c], lanes[hh]]
        s = lax.dot_general(q, k, (((1,), (1,)), ((), ())), preferred_element_type=F32) * dm_ref[hh]
        vk = (v.astype(F32) * kd_ref[hh]).astype(BF16)
        kv = lax.dot_general(k, vk, (((0,), (0,)), ((), ())), preferred_element_type=F32)
        return s.astype(BF16), kv

    def finish(c, hh, s, state):
        q, v = q_ref[rows[c], lanes[hh]], v_ref[rows[c], lanes[hh]]
        y = jnp.dot(s, v, preferred_element_type=F32)
        y = y + jnp.dot(q, state.astype(BF16), preferred_element_type=F32) * qd_ref[hh]
        ms = jnp.mean(y * y, axis=-1, keepdims=True)
        yn = y * lax.rsqrt(ms + EPS) * gn_ref[:, lanes[hh]]
        o_ref[rows[c], lanes[hh]] = (g_ref[rows[c], lanes[hh]].astype(F32) * yn).astype(BF16)

    state = [jnp.zeros((RET_HEAD_DIM, RET_HEAD_DIM), F32) for _ in heads]
    ahead = [independent(0, hh) for hh in heads]
    for c in range(nchunk):
        cur = ahead
        if c + 1 < nchunk:
            ahead = [independent(c + 1, hh) for hh in heads]
        for hh in heads:
            s, kv = cur[hh]
            finish(c, hh, s, state[hh])
            state[hh] = cd_ref[hh, 0:1, :] * state[hh] + kv


def retention(proj, ret_norm_g, l, B, S):
    T = proj.shape[1]
    dm, qd, kd, cd = _ret_consts()
    hp = RET_HEADS_PER_STEP
    wid = hp * LANES
    per_tile = PROJ_TN // wid

    def part(off):
        return pl.BlockSpec((None, S, wid), lambda b, h: (off // PROJ_TN + h // per_tile, b, lax.rem(h, per_tile)))
    hconst = lambda rows: pl.BlockSpec((hp, rows, LANES), lambda b, h: (h, 0, 0))
    return pl.pallas_call(
        functools.partial(_ret_kernel, nchunk=S // RET_CHUNK),
        out_shape=jax.ShapeDtypeStruct((T, RET_W), BF16),
        grid=(B, RET_HEADS // hp),
        in_specs=[part(OFF_QR), part(OFF_KR), part(OFF_VR), part(OFF_GR),
                  pl.BlockSpec((None, 1, wid), lambda b, h: (l, 0, h)),
                  pl.BlockSpec((hp, RET_CHUNK, RET_CHUNK), lambda b, h: (h, 0, 0)),
                  hconst(RET_CHUNK), hconst(RET_CHUNK), hconst(8)],
        out_specs=pl.BlockSpec((S, wid), lambda b, h: (b, h)),
        compiler_params=_cparams(("arbitrary", "arbitrary")),
        name="retention",
    )(proj, proj, proj, proj, ret_norm_g.reshape(-1, 1, RET_W), dm, qd, kd, cd)


def _load_cast(w_hbm, dst, stage, sem):
    rows = stage.shape[0]
    n = w_hbm.shape[0] // rows

    def body(i, carry):
        r0 = pl.multiple_of(i * rows, rows)
        cp = pltpu.make_async_copy(w_hbm.at[pl.ds(r0, rows), :], stage, sem)
        cp.start()
        cp.wait()
        dst[pl.ds(r0, rows), :] = stage[...].astype(BF16)
        return carry
    lax.fori_loop(0, n, body, 0)


def _mixer_out_kernel(att_ref, ret_ref, ga0, ga1, ga2, ga3, gb0, gb1, gb2, gb3, x_ref,
                      wa_hbm, wr_hbm, wo_hbm, gate1_ref, g2_ref, sh2_ref, sc2_ref,
                      wrt_ref, brt_ref, tril_ref,
                      xn_ref, h2_ref, rti_ref, rtw_ref, cnt_ref,
                      wa, wr, wo, wrt2, mrg, carry, hs, stage, sem, *, tm, l):
    i = pl.program_id(0)
    nt = pl.num_programs(0) - 1
    D = D_MODEL

    @pl.when(i == 0)
    def _():
        _load_cast(wa_hbm.at[l], wa, stage, sem)
        _load_cast(wr_hbm.at[l], wr, stage, sem)
        _load_cast(wo_hbm.at[l], wo, stage, sem)
        w = wrt_ref[...]
        hi = w.astype(BF16)
        wrt2[:, 0:LANES] = hi
        wrt2[:, LANES:2 * LANES] = (w - hi.astype(F32)).astype(BF16)
        carry[...] = jnp.zeros_like(carry)
        hs[...] = jnp.zeros_like(hs)

    def main_stage():
        a = att_ref[...]
        r = ret_ref[...]
        ga = (ga0, ga1, ga2, ga3)
        gb = (gb0, gb1, gb2, gb3)
        for n in range(D // PROJ_TN):
            cols = slice(n * PROJ_TN, (n + 1) * PROJ_TN)
            ya = jnp.dot(a, wa[:, cols], preferred_element_type=F32)
            yr = jnp.dot(r, wr[:, cols], preferred_element_type=F32)
            mrg[:, cols] = (ga[n][...].astype(F32) * ya + gb[n][...].astype(F32) * yr).astype(BF16)
        o = jnp.dot(mrg[...], wo[...], preferred_element_type=F32)
        xn = x_ref[...] + gate1_ref[...] * o
        xn_ref[...] = xn
        h2 = _norm_mod(xn, g2_ref[...], sh2_ref[...], sc2_ref[...])
        h2b = h2.astype(BF16)
        h2_ref[...] = h2b
        hs[0:tm, :] = h2b
        hs[tm:2 * tm, :] = (h2 - h2b.astype(F32)).astype(BF16)

    def router_select():
        r = jnp.dot(hs[...], wrt2[...], preferred_element_type=F32)
        logits = ((r[0:tm, 0:LANES] + r[tm:2 * tm, 0:LANES])
                  + (r[0:tm, LANES:2 * LANES] + r[tm:2 * tm, LANES:2 * LANES])) + brt_ref[...]
        lane = lax.broadcasted_iota(I32, (tm, LANES), 1)
        valid = lane < N_EXPERTS
        mx = jnp.max(logits, axis=-1, keepdims=True)
        p = jnp.where(valid, jnp.exp(logits - mx), 0.0)
        pos_in_grp = lane % EXPERTS_PER_GROUP
        grp_of = lane // EXPERTS_PER_GROUP

        def member(k):
            wrapped = pos_in_grp + k >= EXPERTS_PER_GROUP
            return jnp.where(wrapped, pltpu.roll(p, EXPERTS_PER_GROUP - k, 1), pltpu.roll(p, LANES - k, 1)), wrapped

        (b1, w1), (b2, w2), (b3, w3) = member(1), member(2), member(3)
        m_ab, n_ab = jnp.maximum(p, b1), jnp.minimum(p, b1)
        m_cd, n_cd = jnp.maximum(b2, b3), jnp.minimum(b2, b3)
        gscore = jnp.maximum(m_ab, m_cd) + jnp.maximum(jnp.minimum(m_ab, m_cd), jnp.maximum(n_ab, n_cd))
        gscore = jnp.where(valid, gscore, -1.0)
        gmax = jnp.max(gscore, axis=-1, keepdims=True)
        gsel = jnp.min(jnp.where(gscore == gmax, grp_of, N_GROUPS), axis=-1, keepdims=True)
        in_sel = grp_of == gsel
        beats = lambda b, w: ((b > p) | ((b == p) & w)).astype(I32)
        rank_in_grp = beats(b1, w1) + beats(b2, w2) + beats(b3, w3)
        sel0 = in_sel & (rank_in_grp == 0)
        sel1 = in_sel & (rank_in_grp == 1)
        lsum = lambda m, v: jnp.sum(jnp.where(m, v, 0.0), axis=-1, keepdims=True)
        v0, v1 = lsum(sel0, p), lsum(sel1, p)
        lanef = lane.astype(F32)
        e0, e1 = lsum(sel0, lanef), lsum(sel1, lanef)
        inv = 1.0 / (v0 + v1)
        return lane, sel0, sel1, e0, e1, v0 * inv, v1 * inv

    def router_finish(live, lane, sel0, sel1, e0, e1, w0, w1):
        lsum = lambda m, v: jnp.sum(jnp.where(m, v, 0.0), axis=-1, keepdims=True)
        onehot = jnp.where(sel0 | sel1, 1.0, 0.0)
        prefix = jnp.dot(tril_ref[...], onehot.astype(BF16), preferred_element_type=F32) + carry[0:1, :]
        r0, r1 = lsum(sel0, prefix), lsum(sel1, prefix)
        carry[...] = carry[...] + live * jnp.sum(onehot, axis=0, keepdims=True)
        cnt_ref[...] = carry[...].astype(I32)
        sel4 = lambda a0, a1, a2, a3: jnp.where(lane == 0, a0, jnp.where(lane == 1, a1, jnp.where(lane == 2, a2, a3)))
        rti_ref[...] = sel4(e0, e1, r0, r1).astype(I32)
        rtw_ref[...] = jnp.where(lane == 0, w0, jnp.where(lane == 1, w1, 0.0))

    @pl.when(i < nt)
    def _():
        picked = router_select()
        main_stage()
        router_finish(jnp.where(i > 0, 1.0, 0.0), *picked)

    @pl.when(i == nt)
    def _():
        router_finish(1.0, *router_select())


def mixer_out(att, ret, proj, x2, w_attn_out, w_ret_out, w_o, modr, norm2_g, w_router, b_router, l, S):
    T, D = x2.shape
    tm = 256
    assert S % tm == 0
    nga, ngb = OFF_GA // PROJ_TN, OFF_GB // PROJ_TN
    nt = T // tm
    cur = lambda i: jnp.minimum(i, nt - 1)
    prev = lambda i: jnp.maximum(i - 1, 0)
    gate_spec = lambda blk: pl.BlockSpec((None, tm, PROJ_TN), lambda i: (blk, cur(i), 0))
    row = lambda: pl.BlockSpec((tm, D), lambda i: (cur(i), 0))
    mod_spec = lambda k: pl.BlockSpec((None, None, None, 1, D), lambda i: (l, k, (cur(i) * tm) // S, 0, 0))
    wrt = jnp.pad(w_router, ((0, 0), (0, LANES - N_EXPERTS)))
    brt = jnp.pad(b_router.astype(F32), (0, LANES - N_EXPERTS), constant_values=-1e30).reshape(1, LANES)
    tril = jnp.asarray(np.tril(np.ones((tm, tm), np.float32), -1), BF16)
    any_spec = pl.BlockSpec(memory_space=pl.ANY)
    outs = pl.pallas_call(
        functools.partial(_mixer_out_kernel, tm=tm, l=l),
        out_shape=(jax.ShapeDtypeStruct((T, D), F32),
                   jax.ShapeDtypeStruct((T, D), BF16),
                   jax.ShapeDtypeStruct((T, LANES), I32),
                   jax.ShapeDtypeStruct((T, LANES), F32),
                   jax.ShapeDtypeStruct((8, LANES), I32)),
        grid=(nt + 1,),
        in_specs=[pl.BlockSpec((tm, ATT_Q_W), lambda i: (cur(i), 0)),
                  pl.BlockSpec((tm, RET_W), lambda i: (cur(i), 0)),
                  *[gate_spec(nga + n) for n in range(4)],
                  *[gate_spec(ngb + n) for n in range(4)],
                  row(), any_spec, any_spec, any_spec,
                  mod_spec(2),
                  pl.BlockSpec((None, 1, D), lambda i: (l, 0, 0)),
                  mod_spec(3), mod_spec(4),
                  pl.BlockSpec((D, LANES), lambda i: (0, 0)),
                  pl.BlockSpec((1, LANES), lambda i: (0, 0)),
                  pl.BlockSpec((tm, tm), lambda i: (0, 0))],
        out_specs=(row(),
                   row(),
                   pl.BlockSpec((tm, LANES), lambda i: (prev(i), 0)),
                   pl.BlockSpec((tm, LANES), lambda i: (prev(i), 0)),
                   pl.BlockSpec((8, LANES), lambda i: (0, 0))),
        scratch_shapes=[pltpu.VMEM((ATT_Q_W, D), BF16), pltpu.VMEM((RET_W, D), BF16),
                        pltpu.VMEM((D, D), BF16),
                        pltpu.VMEM((D, 2 * LANES), BF16),
                        pltpu.VMEM((tm, D), BF16), pltpu.VMEM((8, LANES), F32),
                        pltpu.VMEM((2 * tm, D), BF16),
                        pltpu.VMEM((256, D), F32), pltpu.SemaphoreType.DMA],
        compiler_params=_cparams(("arbitrary",)),
        name="mixer_out",
    )(att, ret, *([proj] * 8), x2, w_attn_out, w_ret_out, w_o,
      modr, norm2_g.reshape(-1, 1, D), modr, modr, wrt, brt, tril)
    return outs


def moe_plan(counts, n_work):
    sub_per = MOE_ROWS // MOE_SUB
    seg = ((counts + MOE_SUB - 1) // MOE_SUB) * MOE_SUB
    off = jnp.cumsum(seg) - seg
    off17 = jnp.concatenate([off, off[-1:] + seg[-1:]]).astype(I32)
    nb = (counts + MOE_ROWS - 1) // MOE_ROWS
    cum = jnp.cumsum(nb)
    total = cum[-1]
    w = jnp.arange(n_work, dtype=I32)
    wc = jnp.minimum(w, total - 1)
    e_w = jnp.sum((cum[None, :] <= wc[:, None]).astype(I32), axis=1)
    blk = wc - (cum[e_w] - nb[e_w])
    start = off[e_w] + blk * MOE_ROWS
    nsub = jnp.clip(seg[e_w] // MOE_SUB - blk * sub_per, 0, sub_per)
    nsub = jnp.where(w < total, nsub, 0)
    return off17, e_w, start.astype(I32), nsub.astype(I32)


def _row_dma_loops(n_rows, make_copy):
    def run(op):
        def body(g, carry):
            r0 = pl.multiple_of(g * ROW_GROUP, ROW_GROUP)
            for u in range(ROW_GROUP):
                for k in range(2):
                    getattr(make_copy(r0 + u, k), op)()
            return carry
        lax.fori_loop(0, n_rows // ROW_GROUP, body, 0)
    run("start")
    run("wait")


def _dispatch_kernel(pos_ref, off_ref, h_ref, xs_ref, rows_ref, sem, *, td, T):
    i = pl.program_id(0)
    zero_ref = rows_ref

    @pl.when(i == 0)
    def _():
        zero_ref[...] = jnp.zeros_like(zero_ref)
        used = off_ref[N_EXPERTS]

        def zero_copy(row):
            row = pl.multiple_of(row, MOE_SUB)
            return pltpu.make_async_copy(zero_ref, xs_ref.at[pl.ds(row, MOE_SUB), :], sem)

        def fill(op):
            def seg_tail(e, carry):
                @pl.when(off_ref[e + 1] > off_ref[e])
                def _():
                    getattr(zero_copy(off_ref[e + 1] - MOE_SUB), op)()
                return carry

            def buf_tail(n, carry):
                getattr(zero_copy(used + n * MOE_SUB), op)()
                return carry
            lax.fori_loop(0, N_EXPERTS, seg_tail, 0)
            lax.fori_loop(0, (xs_ref.shape[0] - used) // MOE_SUB, buf_tail, 0)
        fill("start")
        fill("wait")

    rows_ref[...] = h_ref[...].astype(F32)

    def copy(r, k):
        p = pos_ref[k * T + i * td + r]
        return pltpu.make_async_copy(rows_ref.at[pl.ds(r, 1), :], xs_ref.at[pl.ds(p, 1), :], sem)
    _row_dma_loops(td, copy)


def dispatch(h2, pos_flat, off17, n_rows):
    T, D = h2.shape
    td = MOE_SUB
    return pl.pallas_call(
        functools.partial(_dispatch_kernel, td=td, T=T),
        out_shape=jax.ShapeDtypeStruct((n_rows, D), F32),
        grid_spec=pltpu.PrefetchScalarGridSpec(
            num_scalar_prefetch=2, grid=(T // td,),
            in_specs=[pl.BlockSpec((td, D), lambda i, *_: (i, 0))],
            out_specs=pl.BlockSpec(memory_space=pl.ANY),
            scratch_shapes=[pltpu.VMEM((td, D), F32), pltpu.SemaphoreType.DMA]),
        compiler_params=_cparams(("arbitrary",)),
        name="moe_dispatch",
    )(pos_flat, off17, h2)


def _moe_kernel(we_ref, ws_ref, wn_ref, xs_ref, wg_ref, wu_ref, wd_ref, ys_ref,
                xb, acc, wgb, wub, wdb, stage, ostage, sem_in, sem_out, *, nfc, n_work):
    w = pl.program_id(0)
    c = pl.program_id(1)
    n = wn_ref[w]
    s0 = ws_ref[w]

    def in_copy(start_row, i, slot):
        g0 = pl.multiple_of(start_row + i * MOE_SUB, MOE_SUB)
        return pltpu.make_async_copy(xs_ref.at[pl.ds(g0, MOE_SUB), :], stage.at[slot], sem_in.at[slot])

    def out_copy(i, slot):
        g0 = pl.multiple_of(s0 + i * MOE_SUB, MOE_SUB)
        return pltpu.make_async_copy(ostage.at[slot], ys_ref.at[pl.ds(g0, MOE_SUB), :], sem_out.at[slot])

    def request_first_two(start_row, count):
        @pl.when(count > 0)
        def _():
            in_copy(start_row, 0, 0).start()

        @pl.when(count > 1)
        def _():
            in_copy(start_row, 1, 1).start()

    @pl.when((w == 0) & (c == 0))
    def _():
        request_first_two(s0, n)

    def for_subs(fn):
        def body(p, carry):
            fn(p * 2, 2)
            return carry
        lax.fori_loop(0, n // 2, body, 0)

        @pl.when(lax.rem(n, 2) == 1)
        def _():
            fn(n - 1, 1)

    def rows_of(i):
        return pl.ds(pl.multiple_of(i * MOE_SUB, MOE_SUB), MOE_SUB)

    def ffn(i0, count):
        xs_ = [xb[rows_of(i0 + a), :] for a in range(count)]
        gu = [(jnp.dot(x, wgb[...], preferred_element_type=F32), jnp.dot(x, wub[...], preferred_element_type=F32))
              for x in xs_]
        acts = [(g * _sigmoid(g) * u).astype(BF16) for g, u in gu]
        return [jnp.dot(a, wdb[...], preferred_element_type=F32) for a in acts]

    @pl.when(n > 0)
    def _():
        wgb[...] = wg_ref[...].astype(BF16)
        wub[...] = wu_ref[...].astype(BF16)
        wdb[...] = wd_ref[...].astype(BF16)

        @pl.when(c == 0)
        def _():
            def first(i0, count):
                for a in range(count):
                    in_copy(s0, i0 + a, a).wait()
                    xb[rows_of(i0 + a), :] = stage[a].astype(BF16)
                for a in range(count):
                    @pl.when(i0 + a + 2 < n)
                    def _():
                        in_copy(s0, i0 + a + 2, a).start()
                for a, y in enumerate(ffn(i0, count)):
                    acc[rows_of(i0 + a), :] = y
            for_subs(first)

        @pl.when((c > 0) & (c < nfc - 1))
        def _():
            def middle(i0, count):
                for a, y in enumerate(ffn(i0, count)):
                    acc[rows_of(i0 + a), :] += y
            for_subs(middle)

        @pl.when(c == nfc - 1)
        def _():
            def last(i0, count):
                for a in range(count):
                    @pl.when(i0 + a >= 2)
                    def _():
                        out_copy(i0 + a - 2, a).wait()
                for a, y in enumerate(ffn(i0, count)):
                    ostage[a] = acc[rows_of(i0 + a), :] + y
                    out_copy(i0 + a, a).start()
            for_subs(last)

            @pl.when(n >= 2)
            def _():
                out_copy(n - 2, lax.rem(n, 2)).wait()
            out_copy(n - 1, lax.rem(n - 1, 2)).wait()

            @pl.when(w + 1 < n_work)
            def _():
                nxt = jnp.minimum(w + 1, n_work - 1)
                request_first_two(ws_ref[nxt], wn_ref[nxt])


def moe_experts(xs, w_gate, w_up, w_down, l, e_w, start_w, nsub_w):
    n_rows, D = xs.shape
    nfc = D_FF // MOE_FC
    assert nfc >= 2
    n_work = e_w.shape[0]

    def chunk(c, wn, w):
        return jnp.where(wn[w] > 0, c, nfc - 1)

    return pl.pallas_call(
        functools.partial(_moe_kernel, nfc=nfc, n_work=n_work),
        out_shape=jax.ShapeDtypeStruct((n_rows, D), F32),
        grid_spec=pltpu.PrefetchScalarGridSpec(
            num_scalar_prefetch=3, grid=(n_work, nfc),
            in_specs=[pl.BlockSpec(memory_space=pl.ANY),
                      pl.BlockSpec((None, None, D, MOE_FC), lambda w, c, we, ws, wn: (l, we[w], 0, chunk(c, wn, w))),
                      pl.BlockSpec((None, None, D, MOE_FC), lambda w, c, we, ws, wn: (l, we[w], 0, chunk(c, wn, w))),
                      pl.BlockSpec((None, None, MOE_FC, D), lambda w, c, we, ws, wn: (l, we[w], chunk(c, wn, w), 0))],
            out_specs=pl.BlockSpec(memory_space=pl.ANY),
            scratch_shapes=[pltpu.VMEM((MOE_ROWS, D), BF16), pltpu.VMEM((MOE_ROWS, D), F32),
                            pltpu.VMEM((D, MOE_FC), BF16), pltpu.VMEM((D, MOE_FC), BF16),
                            pltpu.VMEM((MOE_FC, D), BF16),
                            pltpu.VMEM((2, MOE_SUB, D), F32), pltpu.VMEM((2, MOE_SUB, D), F32),
                            pltpu.SemaphoreType.DMA((2,)), pltpu.SemaphoreType.DMA((2,))]),
        input_output_aliases={3: 0},
        compiler_params=_cparams(("arbitrary", "arbitrary")),
        name="moe_experts",
    )(e_w, start_w, nsub_w, xs, w_gate, w_up, w_down)


def _combine_kernel(pos_ref, ys_ref, x_ref, rtw_ref, gate2_ref, g_ref, *rest, tc, T, final):
    if final:
        out_ref, ybuf, sem = rest
    else:
        sh_ref, sc_ref, xo_ref, h_ref, ybuf, sem = rest
    i = pl.program_id(0)
    slot = lax.rem(i, 2)
    ngroups = tc // COMBINE_GROUP

    def copy(tile, r, k, sl):
        p = pos_ref[k * T + tile * tc + r]
        return pltpu.make_async_copy(ys_ref.at[pl.ds(p, 1), :], ybuf.at[sl, k, pl.ds(r, 1), :], sem.at[sl])

    def for_rows(tile, sl, r0, op):
        for u in range(COMBINE_GROUP):
            for k in range(2):
                getattr(copy(tile, r0 + u, k, sl), op)()

    def groups(fn):
        def body(g, carry):
            fn(pl.multiple_of(g * COMBINE_GROUP, COMBINE_GROUP))
            return carry
        lax.fori_loop(0, ngroups, body, 0)

    def compute(r0):
        rows = pl.ds(r0, COMBINE_GROUP)
        moe = rtw_ref[rows, 0:1] * ybuf[slot, 0, rows, :] + rtw_ref[rows, 1:2] * ybuf[slot, 1, rows, :]
        xo = x_ref[rows, :] + gate2_ref[...] * moe
        if final:
            ms = jnp.mean(xo * xo, axis=-1, keepdims=True)
            out_ref[rows, :] = xo * lax.rsqrt(ms + EPS) * g_ref[...]
        else:
            xo_ref[rows, :] = xo
            h_ref[rows, :] = _norm_mod(xo, g_ref[...], sh_ref[...], sc_ref[...]).astype(BF16)

    @pl.when(i == 0)
    def _():
        groups(lambda r0: for_rows(0, 0, r0, "start"))
    groups(lambda r0: for_rows(i, slot, r0, "wait"))

    @pl.when(i + 1 < pl.num_programs(0))
    def _():
        def both(r0):
            compute(r0)
            for_rows(i + 1, 1 - slot, r0, "start")
        groups(both)

    @pl.when(i + 1 >= pl.num_programs(0))
    def _():
        groups(compute)


def combine(ys, xn, rtw, pos_flat, modr, l, S, g_next, final):
    T, D = xn.shape
    tc = 256
    assert S % tc == 0
    row = lambda: pl.BlockSpec((tc, D), lambda i, *_: (i, 0))
    in_specs = [pl.BlockSpec(memory_space=pl.ANY), row(),
                pl.BlockSpec((tc, LANES), lambda i, *_: (i, 0)),
                _mod_spec(l, 5, S, tc)]
    args = [ys, xn, rtw, modr]
    if final:
        in_specs.append(pl.BlockSpec((1, D), lambda i, *_: (0, 0)))
        args.append(g_next.reshape(1, D))
        out_shape = jax.ShapeDtypeStruct((T, D), F32)
        out_specs = row()
    else:
        in_specs += [pl.BlockSpec((None, 1, D), lambda i, *_: (l + 1, 0, 0)),
                     _mod_spec(l + 1, 0, S, tc), _mod_spec(l + 1, 1, S, tc)]
        args += [g_next.reshape(-1, 1, D), modr, modr]
        out_shape = (jax.ShapeDtypeStruct((T, D), F32), jax.ShapeDtypeStruct((T, D), BF16))
        out_specs = (row(), row())
    return pl.pallas_call(
        functools.partial(_combine_kernel, tc=tc, T=T, final=final),
        out_shape=out_shape,
        grid_spec=pltpu.PrefetchScalarGridSpec(
            num_scalar_prefetch=1, grid=(T // tc,),
            in_specs=in_specs, out_specs=out_specs,
            scratch_shapes=[pltpu.VMEM((2, 2, tc, D), F32), pltpu.SemaphoreType.DMA((2,))]),
        compiler_params=_cparams(("arbitrary",)),
        name="moe_combine_final" if final else "moe_combine",
    )(pos_flat, *args)


def kernel(x, c, positions, w_ada, b_ada, norm1_g, norm2_g, w_in, attn_sinks, w_attn_out, ret_norm_g,
           w_ret_out, w_o, w_router, b_router, w_gate, w_up, w_down, final_g):
    B, S, D = x.shape
    L = w_ada.shape[0]
    T = B * S
    assert D == D_MODEL and w_in.shape[-1] == IN_W and S % RET_CHUNK == 0

    tab = rope_tables(positions)
    mod = adaln_mod(c, w_ada, b_ada)
    modr = mod[:, :B].reshape(L, B, 6, D).transpose(0, 2, 1, 3).reshape(L, 6, B, 1, D)

    x2 = x.reshape(T, D)
    h = norm_modulate(x2, norm1_g.reshape(L, 1, D), modr, 0, S)
    n_work = N_EXPERTS + (2 * T + MOE_ROWS - 1) // MOE_ROWS
    n_rows = 2 * T + N_EXPERTS * MOE_SUB
    out = None
    for l in range(L):
        proj = in_projection(h, w_in, l, tab)
        att = swa_attention(proj, attn_sinks[l], B, S)
        ret = retention(proj, ret_norm_g, l, B, S)
        xn, h2, rti, rtw, cnt = mixer_out(att, ret, proj, x2, w_attn_out, w_ret_out, w_o, modr,
                                          norm2_g, w_router, b_router, l, S)
        off17, e_w, start_w, nsub_w = moe_plan(cnt[0, :N_EXPERTS], n_work)
        seg_off = jnp.sum(jnp.where(rti[:, 0:2, None] == jnp.arange(N_EXPERTS, dtype=I32),
                                    off17[:N_EXPERTS], 0), axis=-1)
        pos_flat = (seg_off + rti[:, 2:4]).T.reshape(2 * T)
        xs = dispatch(h2, pos_flat, off17, n_rows)
        ys = moe_experts(xs, w_gate, w_up, w_down, l, e_w, start_w, nsub_w)
        if l + 1 < L:
            x2, h = combine(ys, xn, rtw, pos_flat, modr, l, S, norm1_g, final=False)
        else:
            out = combine(ys, xn, rtw, pos_flat, modr, l, S, final_g, final=True)
    return out.reshape(B, S, D)
```

```python
import functools

import numpy as np
import jax
import jax.numpy as jnp
from jax import lax
from jax.experimental import pallas as pl
from jax.experimental.pallas import tpu as pltpu

F32 = jnp.float32
BF16 = jnp.bfloat16
I32 = jnp.int32

D_MODEL = 2048
ATT_HEAD_DIM = 64
ATT_Q_HEADS = 16
ATT_KV_HEADS = 4
WINDOW = 128
RET_HEADS = 8
RET_HEAD_DIM = 128
RET_CHUNK = 256
ROPE_THETA = 10000.0
N_GROUPS = 4
EXPERTS_PER_GROUP = 4
N_EXPERTS = 16
D_FF = 1024
EPS = 1e-6

ATT_Q_W = ATT_Q_HEADS * ATT_HEAD_DIM
ATT_KV_W = ATT_KV_HEADS * ATT_HEAD_DIM
RET_W = RET_HEADS * RET_HEAD_DIM
OFF_QA = 0
OFF_KA = OFF_QA + ATT_Q_W
OFF_VA = OFF_KA + ATT_KV_W
OFF_QR = OFF_VA + ATT_KV_W
OFF_KR = OFF_QR + RET_W
OFF_VR = OFF_KR + RET_W
OFF_GR = OFF_VR + RET_W
OFF_GA = OFF_GR + RET_W
OFF_GB = OFF_GA + D_MODEL
IN_W = OFF_GB + D_MODEL

LANES = 128
PROJ_TN = 512
ROW_GROUP = 8
ATTN_LOOKAHEAD = 2
COMBINE_GROUP = 64
RET_HEADS_PER_STEP = 2
MOE_SUB = 256
MOE_ROWS = 2048
MOE_FC = 256
VMEM_LIMIT = 56 * 1024 * 1024


def _pick(n, cands):
    for c in cands:
        if n % c == 0:
            return c
    raise ValueError(f"no tile in {cands} divides {n}")


def _sigmoid(x):
    return 1.0 / (1.0 + jnp.exp(-x))


def _cparams(sem, vmem=VMEM_LIMIT):
    return pltpu.CompilerParams(dimension_semantics=sem, vmem_limit_bytes=vmem)


def _rope_consts():
    def inv_freq(head_dim):
        half = head_dim // 2
        inv = ROPE_THETA ** (-2.0 * jnp.arange(half, dtype=F32) / head_dim)
        return jnp.broadcast_to(jnp.tile(inv, LANES // half), (8, LANES))
    lane = np.arange(LANES)
    sgn_att = np.where((lane % ATT_HEAD_DIM) < ATT_HEAD_DIM // 2, -1.0, 1.0)
    sgn_ret = np.where(lane < RET_HEAD_DIM // 2, -1.0, 1.0)
    sgn = np.stack([np.tile(sgn_att, (8, 1)), np.tile(sgn_ret, (8, 1))]).astype(np.float32)
    return jnp.stack([inv_freq(ATT_HEAD_DIM), inv_freq(RET_HEAD_DIM)]), jnp.asarray(sgn)


def _rope_kernel(pos_ref, inv_ref, sgn_ref, tab_ref):
    pos = pos_ref[...]
    for kind in range(2):
        ang = pos * inv_ref[kind, 0:1, :]
        tab_ref[kind, :, 0:LANES] = jnp.cos(ang)
        tab_ref[kind, :, LANES:2 * LANES] = jnp.sin(ang) * sgn_ref[kind, 0:1, :]


def rope_tables(positions):
    T = positions.size
    posb = jnp.broadcast_to(positions.reshape(T, 1).astype(F32), (T, LANES))
    inv, sgn = _rope_consts()
    tm = _pick(T, (1024, 512, 256, 128))
    return pl.pallas_call(
        _rope_kernel,
        out_shape=jax.ShapeDtypeStruct((2, T, 2 * LANES), F32),
        grid=(T // tm,),
        in_specs=[pl.BlockSpec((tm, LANES), lambda i: (i, 0)),
                  pl.BlockSpec((2, 8, LANES), lambda i: (0, 0, 0)),
                  pl.BlockSpec((2, 8, LANES), lambda i: (0, 0, 0))],
        out_specs=pl.BlockSpec((2, tm, 2 * LANES), lambda i: (0, i, 0)),
        compiler_params=_cparams(("arbitrary",)),
        name="rope_tables",
    )(posb, inv, sgn)


def _adaln_kernel(c_ref, w_ref, b_ref, o_ref):
    c = c_ref[...]
    ca = c * _sigmoid(c)
    hi = ca.astype(BF16).astype(F32)
    lhs = jnp.concatenate([hi, ca - hi], axis=0).astype(BF16)
    r = jnp.dot(lhs, w_ref[...].astype(BF16), preferred_element_type=F32)
    o_ref[...] = r[0:8] + r[8:16] + b_ref[...]


def adaln_mod(c, w_ada, b_ada):
    L, D, N = w_ada.shape
    B = c.shape[0]
    cp = jnp.pad(c, ((0, 8 - B), (0, 0)))
    tn = _pick(N, (1024, 512))
    return pl.pallas_call(
        _adaln_kernel,
        out_shape=jax.ShapeDtypeStruct((L, 8, N), F32),
        grid=(L, N // tn),
        in_specs=[pl.BlockSpec((8, D), lambda l, j: (0, 0)),
                  pl.BlockSpec((None, D, tn), lambda l, j: (l, 0, j)),
                  pl.BlockSpec((None, 1, tn), lambda l, j: (l, 0, j))],
        out_specs=pl.BlockSpec((None, 8, tn), lambda l, j: (l, 0, j)),
        compiler_params=_cparams(("arbitrary", "arbitrary")),
        name="adaln_mod",
    )(cp, w_ada, b_ada.reshape(L, 1, N))


def _norm_mod(x, g, shift, scale):
    ms = jnp.mean(x * x, axis=-1, keepdims=True)
    return (x * lax.rsqrt(ms + EPS) * g) * (1.0 + scale) + shift


def _norm_mod_kernel(x_ref, g_ref, sh_ref, sc_ref, h_ref):
    h_ref[...] = _norm_mod(x_ref[...], g_ref[...], sh_ref[...], sc_ref[...]).astype(BF16)


def _mod_spec(l, k, S, tm):
    return pl.BlockSpec((None, None, None, 1, D_MODEL), lambda i, *_: (l, k, (i * tm) // S, 0, 0))


def norm_modulate(x2, g, modr, l, S):
    T, D = x2.shape
    tm = _pick(S, (512, 256, 128))
    return pl.pallas_call(
        _norm_mod_kernel,
        out_shape=jax.ShapeDtypeStruct((T, D), BF16),
        grid=(T // tm,),
        in_specs=[pl.BlockSpec((tm, D), lambda i: (i, 0)),
                  pl.BlockSpec((None, 1, D), lambda i: (l, 0, 0)),
                  _mod_spec(l, 0, S, tm), _mod_spec(l, 1, S, tm)],
        out_specs=pl.BlockSpec((tm, D), lambda i: (i, 0)),
        compiler_params=_cparams(("arbitrary",)),
        name="norm1_modulate",
    )(x2, g, modr, modr)


def _rope_att(blk, cos, sin, scale):
    lane = lax.broadcasted_iota(I32, blk.shape, 1)
    rot = jnp.where((lane % 64) < 32, pltpu.roll(blk, 96, 1), pltpu.roll(blk, 32, 1))
    out = blk * cos + rot * sin
    return out * scale if scale != 1.0 else out


def _rope_ret(blk, cos, sin, scale):
    out = blk * cos + pltpu.roll(blk, 64, 1) * sin
    return out * scale if scale != 1.0 else out


def _inproj_kernel(h_ref, w_ref, tab_ref, o_ref, wbf_ref, acc_ref, *, ni, ntiles):
    s = pl.program_id(0)
    j = jnp.maximum(s - 1, 0) // ni
    tm = o_ref.shape[0]
    h_rows = pl.ds(pl.multiple_of(lax.rem(jnp.minimum(s, ntiles - 1), ni) * tm, tm), tm)

    @pl.when(s == 0)
    def _():
        acc_ref[1] = jnp.zeros(acc_ref.shape[1:], F32)

    @pl.when((lax.rem(s, ni) == 0) & (s < ntiles))
    def _():
        wbf_ref[...] = w_ref[...].astype(BF16)

    ngrp = PROJ_TN // LANES
    t = lambda off: off // PROJ_TN
    q_scale = float(ATT_HEAD_DIM) ** -0.5
    k_scale = float(RET_HEAD_DIM) ** -0.5

    def step(cur):
        def run(epilogue):
            cos = tab_ref[:, 0:LANES]
            sin = tab_ref[:, LANES:2 * LANES]
            for g in range(ngrp):
                blk = acc_ref[1 - cur, :, g * LANES:(g + 1) * LANES]
                o_ref[:, g * LANES:(g + 1) * LANES] = epilogue(g, blk, cos, sin).astype(BF16)
            acc_ref[cur] = jnp.dot(h_ref[h_rows, :], wbf_ref[...], preferred_element_type=F32)

        @pl.when(j < t(OFF_KA))
        def _():
            run(lambda g, b, c, s: _rope_att(b, c, s, q_scale))

        @pl.when(j == t(OFF_KA))
        def _():
            run(lambda g, b, c, s: _rope_att(b, c, s, 1.0) if g < ATT_KV_W // LANES else b)

        @pl.when((j >= t(OFF_QR)) & (j < t(OFF_KR)))
        def _():
            run(lambda g, b, c, s: _rope_ret(b, c, s, 1.0))

        @pl.when((j >= t(OFF_KR)) & (j < t(OFF_VR)))
        def _():
            run(lambda g, b, c, s: _rope_ret(b, c, s, k_scale))

        @pl.when((j >= t(OFF_VR)) & (j < t(OFF_GR)))
        def _():
            run(lambda g, b, c, s: b)

        @pl.when((j >= t(OFF_GR)) & (j < t(OFF_GA)))
        def _():
            run(lambda g, b, c, s: b * _sigmoid(b))

        @pl.when(j >= t(OFF_GA))
        def _():
            run(lambda g, b, c, s: _sigmoid(b))

    for parity in range(2):
        pl.when(lax.rem(s, 2) == parity)(functools.partial(step, parity))


def in_projection(h, w_in, l, tab):
    T, D = h.shape
    tm = _pick(T, (1024, 768, 512, 256))
    nj, ni = IN_W // PROJ_TN, T // tm
    ntiles = nj * ni
    t_qr, t_vr = OFF_QR // PROJ_TN, OFF_VR // PROJ_TN
    mm = lambda s: jnp.minimum(s, ntiles - 1)
    ep = lambda s: jnp.maximum(s - 1, 0)

    def tab_map(s):
        j, i = ep(s) // ni, lax.rem(ep(s), ni)
        return (jnp.where(j < t_qr, 0, 1), jnp.where(j < t_vr, i, 0), 0)

    return pl.pallas_call(
        functools.partial(_inproj_kernel, ni=ni, ntiles=ntiles),
        out_shape=jax.ShapeDtypeStruct((nj, T, PROJ_TN), BF16),
        grid=(ntiles + 1,),
        in_specs=[pl.BlockSpec((T, D), lambda s: (0, 0), pipeline_mode=pl.Buffered(1)),
                  pl.BlockSpec((None, D, PROJ_TN), lambda s: (l, 0, mm(s) // ni)),
                  pl.BlockSpec((None, tm, 2 * LANES), tab_map)],
        out_specs=pl.BlockSpec((None, tm, PROJ_TN), lambda s: (ep(s) // ni, lax.rem(ep(s), ni), 0)),
        scratch_shapes=[pltpu.VMEM((D, PROJ_TN), BF16), pltpu.VMEM((2, tm, PROJ_TN), F32)],
        compiler_params=_cparams(("arbitrary",)),
        name="in_projection",
    )(h, w_in, tab)


def _attn_kernel(sinks_ref, q0_ref, q1_ref, kvc_ref, kvp_ref, o_ref, *, tq):
    W = WINDOW
    nsub = tq // W
    is_first = pl.program_id(1) == 0
    qi = lax.broadcasted_iota(I32, (W, 2 * W), 0)
    kj = lax.broadcasted_iota(I32, (W, 2 * W), 1)
    rel = qi + W - kj
    band = (rel >= 0) & (rel < WINDOW)
    band0 = band & (jnp.logical_not(is_first) | (kj >= W))
    lane = lax.broadcasted_iota(I32, (tq + W, LANES), 1)
    lo = lane < ATT_HEAD_DIM

    def split_pair(raw, parity):
        x = raw.astype(F32)
        xr = pltpu.roll(x, ATT_HEAD_DIM, 1)
        if parity == 0:
            return jnp.where(lo, x, 0.0), jnp.where(lo, 0.0, xr)
        return jnp.where(lo, xr, 0.0), jnp.where(lo, 0.0, x)

    units = []
    for hk in range(ATT_KV_HEADS):
        grp, par = hk // 2, hk % 2
        kc = slice(grp * LANES, (grp + 1) * LANES)
        vc = slice(ATT_KV_W + grp * LANES, ATT_KV_W + (grp + 1) * LANES)
        k_pair = split_pair(jnp.concatenate([kvp_ref[:, kc], kvc_ref[:, kc]], axis=0), par)
        v_pair = split_pair(jnp.concatenate([kvp_ref[:, vc], kvc_ref[:, vc]], axis=0), par)
        kt_pair = [k.T.astype(BF16) for k in k_pair]
        v_pair = [v.astype(BF16) for v in v_pair]
        qcols = [slice(hk * 4 * ATT_HEAD_DIM + jq * LANES, hk * 4 * ATT_HEAD_DIM + (jq + 1) * LANES)
                 for jq in range(2)]
        for m in range(nsub):
            units.append((hk, m, kt_pair, v_pair, qcols))

    def scores(unit):
        hk, m, kt_pair, v_pair, qcols = unit
        rows = slice(m * W, (m + 1) * W)
        keys = slice(m * W, m * W + 2 * W)
        kcat = jnp.concatenate([kt_pair[0][:, keys], kt_pair[1][:, keys]], axis=1)
        q_ref = (q0_ref, q1_ref)[hk // 2]
        c0 = (hk % 2) * 4 * ATT_HEAD_DIM
        q = jnp.concatenate([q_ref[rows, c0:c0 + LANES], q_ref[rows, c0 + LANES:c0 + 2 * LANES]], axis=0)
        return jnp.dot(q, kcat, preferred_element_type=F32)

    def finish(unit, s_all):
        hk, m, kt_pair, v_pair, qcols = unit
        rows = slice(m * W, (m + 1) * W)
        keys = slice(m * W, m * W + 2 * W)
        mask = band0 if m == 0 else band
        vcat = jnp.concatenate([v_pair[0][keys], v_pair[1][keys]], axis=0)
        p_rows = []
        for jq in range(2):
            p_cols = []
            for e in range(2):
                s = s_all[jq * W:(jq + 1) * W, e * 2 * W:(e + 1) * 2 * W]
                s = jnp.where(mask, s, -1e30)
                sink = sinks_ref[hk * 4 + jq * 2 + e]
                mx = jnp.maximum(jnp.max(s, axis=1, keepdims=True), sink)
                p = jnp.exp(s - mx)
                den = jnp.sum(p, axis=1, keepdims=True) + jnp.exp(sink - mx)
                p_cols.append((p * (1.0 / den)).astype(BF16))
            p_rows.append(jnp.concatenate(p_cols, axis=1))
        o = jnp.dot(jnp.concatenate(p_rows, axis=0), vcat, preferred_element_type=F32)
        o_ref[rows, qcols[0]] = o[0:W].astype(BF16)
        o_ref[rows, qcols[1]] = o[W:2 * W].astype(BF16)

    pending = [scores(u) for u in units[:ATTN_LOOKAHEAD]]
    for idx, unit in enumerate(units):
        if idx + ATTN_LOOKAHEAD < len(units):
            pending.append(scores(units[idx + ATTN_LOOKAHEAD]))
        finish(unit, pending.pop(0))


def swa_attention(proj, sinks, B, S):
    T = proj.shape[1]
    tq = _pick(S, (512, 384, 256, 128))
    nq = S // tq
    assert ATT_Q_W == 2 * PROJ_TN and 2 * ATT_KV_W == PROJ_TN
    kv_tile = OFF_KA // PROJ_TN
    tile = lambda j: pl.BlockSpec((None, tq, PROJ_TN), lambda b, i: (j, b * nq + i, 0))

    def prev_map(b, i):
        return (kv_tile, jnp.maximum(b * (S // WINDOW) + i * (tq // WINDOW) - 1, 0), 0)

    return pl.pallas_call(
        functools.partial(_attn_kernel, tq=tq),
        out_shape=jax.ShapeDtypeStruct((T, ATT_Q_W), BF16),
        grid=(B, nq),
        in_specs=[pl.BlockSpec(memory_space=pltpu.SMEM),
                  tile(0), tile(1), tile(kv_tile),
                  pl.BlockSpec((None, WINDOW, PROJ_TN), prev_map)],
        out_specs=pl.BlockSpec((tq, ATT_Q_W), lambda b, i: (b * nq + i, 0)),
        compiler_params=_cparams(("arbitrary", "arbitrary")),
        name="swa_attention",
    )(sinks, proj, proj, proj, proj)


def _ret_consts():
    C = RET_CHUNK
    log_g = jnp.log1p(-jnp.exp2(-5.0 - jnp.arange(RET_HEADS, dtype=F32)))
    i = jnp.arange(C, dtype=F32)
    diff = i[:, None] - i[None, :]
    dm = jnp.where(diff[None] >= 0, jnp.exp(jnp.maximum(diff, 0.0)[None] * log_g[:, None, None]), 0.0)
    qd = jnp.exp((i + 1.0)[None, :] * log_g[:, None])
    kd = jnp.exp((C - 1.0 - i)[None, :] * log_g[:, None])
    cd = jnp.exp(C * log_g)
    bc = lambda v: jnp.broadcast_to(v[:, :, None], (RET_HEADS, C, LANES))
    return dm, bc(qd), bc(kd), jnp.broadcast_to(cd[:, None, None], (RET_HEADS, 8, LANES))


def _ret_kernel(q_ref, k_ref, v_ref, g_ref, gn_ref, dm_ref, qd_ref, kd_ref, cd_ref, o_ref, *, nchunk):
    C = RET_CHUNK
    heads = range(RET_HEADS_PER_STEP)
    lanes = [slice(hh * LANES, (hh + 1) * LANES) for hh in heads]
    rows = [slice(c * C, (c + 1) * C) for c in range(nchunk)]

    def independent(c, hh):
        q, k, v = q_ref[rows[c], lanes[hh]], k_ref[rows[c], lanes[hh]], v_ref[rows[c], lanes[hh]]
        s = lax.dot_general(q, k, (((1,), (1,)), ((), ())), preferred_element_type=F32) * dm_ref[hh]
        vk = (v.astype(F32) * kd_ref[hh]).astype(BF16)
        kv = lax.dot_general(k, vk, (((0,), (0,)), ((), ())), preferred_element_type=F32)
        return s.astype(BF16), kv

    def finish(c, hh, s, state):
        q, v = q_ref[rows[c], lanes[hh]], v_ref[rows[c], lanes[hh]]
        y = jnp.dot(s, v, preferred_element_type=F32)
        y = y + jnp.dot(q, state.astype(BF16), preferred_element_type=F32) * qd_ref[hh]
        ms = jnp.mean(y * y, axis=-1, keepdims=True)
        yn = y * lax.rsqrt(ms + EPS) * gn_ref[:, lanes[hh]]
        o_ref[rows[c], lanes[hh]] = (g_ref[rows[c], lanes[hh]].astype(F32) * yn).astype(BF16)

    state = [jnp.zeros((RET_HEAD_DIM, RET_HEAD_DIM), F32) for _ in heads]
    ahead = [independent(0, hh) for hh in heads]
    for c in range(nchunk):
        cur = ahead
        if c + 1 < nchunk:
            ahead = [independent(c + 1, hh) for hh in heads]
        for hh in heads:
            s, kv = cur[hh]
            finish(c, hh, s, state[hh])
            state[hh] = cd_ref[hh, 0:1, :] * state[hh] + kv


def retention(proj, ret_norm_g, l, B, S):
    T = proj.shape[1]
    dm, qd, kd, cd = _ret_consts()
    hp = RET_HEADS_PER_STEP
    wid = hp * LANES
    per_tile = PROJ_TN // wid

    def part(off):
        return pl.BlockSpec((None, S, wid), lambda b, h: (off // PROJ_TN + h // per_tile, b, lax.rem(h, per_tile)))
    hconst = lambda rows: pl.BlockSpec((hp, rows, LANES), lambda b, h: (h, 0, 0))
    return pl.pallas_call(
        functools.partial(_ret_kernel, nchunk=S // RET_CHUNK),
        out_shape=jax.ShapeDtypeStruct((T, RET_W), BF16),
        grid=(B, RET_HEADS // hp),
        in_specs=[part(OFF_QR), part(OFF_KR), part(OFF_VR), part(OFF_GR),
                  pl.BlockSpec((None, 1, wid), lambda b, h: (l, 0, h)),
                  pl.BlockSpec((hp, RET_CHUNK, RET_CHUNK), lambda b, h: (h, 0, 0)),
                  hconst(RET_CHUNK), hconst(RET_CHUNK), hconst(8)],
        out_specs=pl.BlockSpec((S, wid), lambda b, h: (b, h)),
        compiler_params=_cparams(("arbitrary", "arbitrary")),
        name="retention",
    )(proj, proj, proj, proj, ret_norm_g.reshape(-1, 1, RET_W), dm, qd, kd, cd)


def _load_cast(w_hbm, dst, stage, sem):
    rows = stage.shape[0]
    n = w_hbm.shape[0] // rows

    def body(i, carry):
        r0 = pl.multiple_of(i * rows, rows)
        cp = pltpu.make_async_copy(w_hbm.at[pl.ds(r0, rows), :], stage, sem)
        cp.start()
        cp.wait()
        dst[pl.ds(r0, rows), :] = stage[...].astype(BF16)
        return carry
    lax.fori_loop(0, n, body, 0)


def _mixer_out_kernel(att_ref, ret_ref, ga0, ga1, ga2, ga3, gb0, gb1, gb2, gb3, x_ref,
                      wa_hbm, wr_hbm, wo_hbm, gate1_ref, g2_ref, sh2_ref, sc2_ref,
                      wrt_ref, brt_ref, tril_ref,
                      xn_ref, h2_ref, rti_ref, rtw_ref, cnt_ref,
                      wa, wr, wo, wrt2, mrg, carry, hs, stage, sem, *, tm, l):
    i = pl.program_id(0)
    nt = pl.num_programs(0) - 1
    D = D_MODEL

    @pl.when(i == 0)
    def _():
        _load_cast(wa_hbm.at[l], wa, stage, sem)
        _load_cast(wr_hbm.at[l], wr, stage, sem)
        _load_cast(wo_hbm.at[l], wo, stage, sem)
        w = wrt_ref[...]
        hi = w.astype(BF16)
        wrt2[:, 0:LANES] = hi
        wrt2[:, LANES:2 * LANES] = (w - hi.astype(F32)).astype(BF16)
        carry[...] = jnp.zeros_like(carry)
        hs[...] = jnp.zeros_like(hs)

    def main_stage():
        a = att_ref[...]
        r = ret_ref[...]
        ga = (ga0, ga1, ga2, ga3)
        gb = (gb0, gb1, gb2, gb3)
        for n in range(D // PROJ_TN):
            cols = slice(n * PROJ_TN, (n + 1) * PROJ_TN)
            ya = jnp.dot(a, wa[:, cols], preferred_element_type=F32)
            yr = jnp.dot(r, wr[:, cols], preferred_element_type=F32)
            mrg[:, cols] = (ga[n][...].astype(F32) * ya + gb[n][...].astype(F32) * yr).astype(BF16)
        o = jnp.dot(mrg[...], wo[...], preferred_element_type=F32)
        xn = x_ref[...] + gate1_ref[...] * o
        xn_ref[...] = xn
        h2 = _norm_mod(xn, g2_ref[...], sh2_ref[...], sc2_ref[...])
        h2b = h2.astype(BF16)
        h2_ref[...] = h2b
        hs[0:tm, :] = h2b
        hs[tm:2 * tm, :] = (h2 - h2b.astype(F32)).astype(BF16)

    def router_select():
        r = jnp.dot(hs[...], wrt2[...], preferred_element_type=F32)
        logits = ((r[0:tm, 0:LANES] + r[tm:2 * tm, 0:LANES])
                  + (r[0:tm, LANES:2 * LANES] + r[tm:2 * tm, LANES:2 * LANES])) + brt_ref[...]
        lane = lax.broadcasted_iota(I32, (tm, LANES), 1)
        valid = lane < N_EXPERTS
        mx = jnp.max(logits, axis=-1, keepdims=True)
        p = jnp.where(valid, jnp.exp(logits - mx), 0.0)
        pos_in_grp = lane % EXPERTS_PER_GROUP
        grp_of = lane // EXPERTS_PER_GROUP

        def member(k):
            wrapped = pos_in_grp + k >= EXPERTS_PER_GROUP
            return jnp.where(wrapped, pltpu.roll(p, EXPERTS_PER_GROUP - k, 1), pltpu.roll(p, LANES - k, 1)), wrapped

        (b1, w1), (b2, w2), (b3, w3) = member(1), member(2), member(3)
        m_ab, n_ab = jnp.maximum(p, b1), jnp.minimum(p, b1)
        m_cd, n_cd = jnp.maximum(b2, b3), jnp.minimum(b2, b3)
        gscore = jnp.maximum(m_ab, m_cd) + jnp.maximum(jnp.minimum(m_ab, m_cd), jnp.maximum(n_ab, n_cd))
        gscore = jnp.where(valid, gscore, -1.0)
        gmax = jnp.max(gscore, axis=-1, keepdims=True)
        gsel = jnp.min(jnp.where(gscore == gmax, grp_of, N_GROUPS), axis=-1, keepdims=True)
        in_sel = grp_of == gsel
        beats = lambda b, w: ((b > p) | ((b == p) & w)).astype(I32)
        rank_in_grp = beats(b1, w1) + beats(b2, w2) + beats(b3, w3)
        sel0 = in_sel & (rank_in_grp == 0)
        sel1 = in_sel & (rank_in_grp == 1)
        lsum = lambda m, v: jnp.sum(jnp.where(m, v, 0.0), axis=-1, keepdims=True)
        v0, v1 = lsum(sel0, p), lsum(sel1, p)
        lanef = lane.astype(F32)
        e0, e1 = lsum(sel0, lanef), lsum(sel1, lanef)
        inv = 1.0 / (v0 + v1)
        return lane, sel0, sel1, e0, e1, v0 * inv, v1 * inv

    def router_finish(live, lane, sel0, sel1, e0, e1, w0, w1):
        lsum = lambda m, v: jnp.sum(jnp.where(m, v, 0.0), axis=-1, keepdims=True)
        onehot = jnp.where(sel0 | sel1, 1.0, 0.0)
        prefix = jnp.dot(tril_ref[...], onehot.astype(BF16), preferred_element_type=F32) + carry[0:1, :]
        r0, r1 = lsum(sel0, prefix), lsum(sel1, prefix)
        carry[...] = carry[...] + live * jnp.sum(onehot, axis=0, keepdims=True)
        cnt_ref[...] = carry[...].astype(I32)
        sel4 = lambda a0, a1, a2, a3: jnp.where(lane == 0, a0, jnp.where(lane == 1, a1, jnp.where(lane == 2, a2, a3)))
        rti_ref[...] = sel4(e0, e1, r0, r1).astype(I32)
        rtw_ref[...] = jnp.where(lane == 0, w0, jnp.where(lane == 1, w1, 0.0))

    @pl.when(i < nt)
    def _():
        picked = router_select()
        main_stage()
        router_finish(jnp.where(i > 0, 1.0, 0.0), *picked)

    @pl.when(i == nt)
    def _():
        router_finish(1.0, *router_select())


def mixer_out(att, ret, proj, x2, w_attn_out, w_ret_out, w_o, modr, norm2_g, w_router, b_router, l, S):
    T, D = x2.shape
    tm = 256
    assert S % tm == 0
    nga, ngb = OFF_GA // PROJ_TN, OFF_GB // PROJ_TN
    nt = T // tm
    cur = lambda i: jnp.minimum(i, nt - 1)
    prev = lambda i: jnp.maximum(i - 1, 0)
    gate_spec = lambda blk: pl.BlockSpec((None, tm, PROJ_TN), lambda i: (blk, cur(i), 0))
    row = lambda: pl.BlockSpec((tm, D), lambda i: (cur(i), 0))
    mod_spec = lambda k: pl.BlockSpec((None, None, None, 1, D), lambda i: (l, k, (cur(i) * tm) // S, 0, 0))
    wrt = jnp.pad(w_router, ((0, 0), (0, LANES - N_EXPERTS)))
    brt = jnp.pad(b_router.astype(F32), (0, LANES - N_EXPERTS), constant_values=-1e30).reshape(1, LANES)
    tril = jnp.asarray(np.tril(np.ones((tm, tm), np.float32), -1), BF16)
    any_spec = pl.BlockSpec(memory_space=pl.ANY)
    outs = pl.pallas_call(
        functools.partial(_mixer_out_kernel, tm=tm, l=l),
        out_shape=(jax.ShapeDtypeStruct((T, D), F32),
                   jax.ShapeDtypeStruct((T, D), BF16),
                   jax.ShapeDtypeStruct((T, LANES), I32),
                   jax.ShapeDtypeStruct((T, LANES), F32),
                   jax.ShapeDtypeStruct((8, LANES), I32)),
        grid=(nt + 1,),
        in_specs=[pl.BlockSpec((tm, ATT_Q_W), lambda i: (cur(i), 0)),
                  pl.BlockSpec((tm, RET_W), lambda i: (cur(i), 0)),
                  *[gate_spec(nga + n) for n in range(4)],
                  *[gate_spec(ngb + n) for n in range(4)],
                  row(), any_spec, any_spec, any_spec,
                  mod_spec(2),
                  pl.BlockSpec((None, 1, D), lambda i: (l, 0, 0)),
                  mod_spec(3), mod_spec(4),
                  pl.BlockSpec((D, LANES), lambda i: (0, 0)),
                  pl.BlockSpec((1, LANES), lambda i: (0, 0)),
                  pl.BlockSpec((tm, tm), lambda i: (0, 0))],
        out_specs=(row(),
                   row(),
                   pl.BlockSpec((tm, LANES), lambda i: (prev(i), 0)),
                   pl.BlockSpec((tm, LANES), lambda i: (prev(i), 0)),
                   pl.BlockSpec((8, LANES), lambda i: (0, 0))),
        scratch_shapes=[pltpu.VMEM((ATT_Q_W, D), BF16), pltpu.VMEM((RET_W, D), BF16),
                        pltpu.VMEM((D, D), BF16),
                        pltpu.VMEM((D, 2 * LANES), BF16),
                        pltpu.VMEM((tm, D), BF16), pltpu.VMEM((8, LANES), F32),
                        pltpu.VMEM((2 * tm, D), BF16),
                        pltpu.VMEM((256, D), F32), pltpu.SemaphoreType.DMA],
        compiler_params=_cparams(("arbitrary",)),
        name="mixer_out",
    )(att, ret, *([proj] * 8), x2, w_attn_out, w_ret_out, w_o,
      modr, norm2_g.reshape(-1, 1, D), modr, modr, wrt, brt, tril)
    return outs


def moe_plan(counts, n_work):
    sub_per = MOE_ROWS // MOE_SUB
    seg = ((counts + MOE_SUB - 1) // MOE_SUB) * MOE_SUB
    off = jnp.cumsum(seg) - seg
    off17 = jnp.concatenate([off, off[-1:] + seg[-1:]]).astype(I32)
    nb = (counts + MOE_ROWS - 1) // MOE_ROWS
    cum = jnp.cumsum(nb)
    total = cum[-1]
    w = jnp.arange(n_work, dtype=I32)
    wc = jnp.minimum(w, total - 1)
    e_w = jnp.sum((cum[None, :] <= wc[:, None]).astype(I32), axis=1)
    blk = wc - (cum[e_w] - nb[e_w])
    start = off[e_w] + blk * MOE_ROWS
    nsub = jnp.clip(seg[e_w] // MOE_SUB - blk * sub_per, 0, sub_per)
    nsub = jnp.where(w < total, nsub, 0)
    return off17, e_w, start.astype(I32), nsub.astype(I32)


def _row_dma_loops(n_rows, make_copy):
    def run(op):
        def body(g, carry):
            for u in range(ROW_GROUP):
                for k in range(2):
                    getattr(make_copy(g, u, k), op)()
            return carry
        lax.fori_loop(0, n_rows // ROW_GROUP, body, 0)
    run("start")
    run("wait")


def _dispatch_kernel(pos_ref, off_ref, h_ref, xs_ref, rows_ref, zero_ref, sem, *, td, T):
    i = pl.program_id(0)

    @pl.when(i == 0)
    def _():
        zero_ref[...] = jnp.zeros_like(zero_ref)
        used = off_ref[N_EXPERTS]

        def zero_copy(row):
            row = pl.multiple_of(row, MOE_SUB)
            return pltpu.make_async_copy(zero_ref, xs_ref.at[pl.ds(row, MOE_SUB), :], sem)

        def fill(op):
            def seg_tail(e, carry):
                @pl.when(off_ref[e + 1] > off_ref[e])
                def _():
                    getattr(zero_copy(off_ref[e + 1] - MOE_SUB), op)()
                return carry

            def buf_tail(n, carry):
                getattr(zero_copy(used + n * MOE_SUB), op)()
                return carry
            lax.fori_loop(0, N_EXPERTS, seg_tail, 0)
            lax.fori_loop(0, (xs_ref.shape[0] - used) // MOE_SUB, buf_tail, 0)
        fill("start")
        fill("wait")

    rows_ref[...] = h_ref[...].astype(F32).reshape(rows_ref.shape)

    def copy(g, u, k):
        p = pos_ref[k * T + i * td + g * ROW_GROUP + u]
        return pltpu.make_async_copy(rows_ref.at[g, pl.ds(u, 1), :], xs_ref.at[pl.ds(p, 1), :], sem)
    _row_dma_loops(td, copy)


def dispatch(h2, pos_flat, off17, n_rows):
    T, D = h2.shape
    td = MOE_SUB
    return pl.pallas_call(
        functools.partial(_dispatch_kernel, td=td, T=T),
        out_shape=jax.ShapeDtypeStruct((n_rows, D), F32),
        grid_spec=pltpu.PrefetchScalarGridSpec(
            num_scalar_prefetch=2, grid=(T // td,),
            in_specs=[pl.BlockSpec((td, D), lambda i, *_: (i, 0))],
            out_specs=pl.BlockSpec(memory_space=pl.ANY),
            scratch_shapes=[pltpu.VMEM((td // ROW_GROUP, ROW_GROUP, D), F32), pltpu.VMEM((MOE_SUB, D), F32),
                            pltpu.SemaphoreType.DMA]),
        compiler_params=_cparams(("arbitrary",)),
        name="moe_dispatch",
    )(pos_flat, off17, h2)


def _moe_kernel(we_ref, ws_ref, wn_ref, xs_ref, wg_ref, wu_ref, wd_ref, ys_ref,
                xb, acc, wgb, wub, wdb, stage, ostage, sem_in, sem_out, *, nfc, n_work):
    w = pl.program_id(0)
    c = pl.program_id(1)
    n = wn_ref[w]
    s0 = ws_ref[w]

    def in_copy(start_row, i, slot):
        g0 = pl.multiple_of(start_row + i * MOE_SUB, MOE_SUB)
        return pltpu.make_async_copy(xs_ref.at[pl.ds(g0, MOE_SUB), :], stage.at[slot], sem_in.at[slot])

    def out_copy(i, slot):
        g0 = pl.multiple_of(s0 + i * MOE_SUB, MOE_SUB)
        return pltpu.make_async_copy(ostage.at[slot], ys_ref.at[pl.ds(g0, MOE_SUB), :], sem_out.at[slot])

    def request_first_two(start_row, count):
        @pl.when(count > 0)
        def _():
            in_copy(start_row, 0, 0).start()

        @pl.when(count > 1)
        def _():
            in_copy(start_row, 1, 1).start()

    @pl.when((w == 0) & (c == 0))
    def _():
        request_first_two(s0, n)

    def for_subs(fn):
        def body(p, carry):
            fn(p * 2, 2)
            return carry
        lax.fori_loop(0, n // 2, body, 0)

        @pl.when(lax.rem(n, 2) == 1)
        def _():
            fn(n - 1, 1)

    def rows_of(i):
        return pl.ds(pl.multiple_of(i * MOE_SUB, MOE_SUB), MOE_SUB)

    def ffn(i0, count):
        xs_ = [xb[rows_of(i0 + a), :] for a in range(count)]
        gu = [(jnp.dot(x, wgb[...], preferred_element_type=F32), jnp.dot(x, wub[...], preferred_element_type=F32))
              for x in xs_]
        acts = [(g * _sigmoid(g) * u).astype(BF16) for g, u in gu]
        return [jnp.dot(a, wdb[...], preferred_element_type=F32) for a in acts]

    @pl.when(n > 0)
    def _():
        wgb[...] = wg_ref[...].astype(BF16)
        wub[...] = wu_ref[...].astype(BF16)
        wdb[...] = wd_ref[...].astype(BF16)

        @pl.when(c == 0)
        def _():
            def first(i0, count):
                for a in range(count):
                    in_copy(s0, i0 + a, a).wait()
                    xb[rows_of(i0 + a), :] = stage[a].astype(BF16)
                for a in range(count):
                    @pl.when(i0 + a + 2 < n)
                    def _():
                        in_copy(s0, i0 + a + 2, a).start()
                for a, y in enumerate(ffn(i0, count)):
                    acc[rows_of(i0 + a), :] = y
            for_subs(first)

        @pl.when((c > 0) & (c < nfc - 1))
        def _():
            def middle(i0, count):
                for a, y in enumerate(ffn(i0, count)):
                    acc[rows_of(i0 + a), :] += y
            for_subs(middle)

        @pl.when(c == nfc - 1)
        def _():
            def last(i0, count):
                for a in range(count):
                    @pl.when(i0 + a >= 2)
                    def _():
                        out_copy(i0 + a - 2, a).wait()
                for a, y in enumerate(ffn(i0, count)):
                    ostage[a] = acc[rows_of(i0 + a), :] + y
                    out_copy(i0 + a, a).start()
            for_subs(last)

            @pl.when(n >= 2)
            def _():
                out_copy(n - 2, lax.rem(n, 2)).wait()
            out_copy(n - 1, lax.rem(n - 1, 2)).wait()

            @pl.when(w + 1 < n_work)
            def _():
                nxt = jnp.minimum(w + 1, n_work - 1)
                request_first_two(ws_ref[nxt], wn_ref[nxt])


def moe_experts(xs, w_gate, w_up, w_down, l, e_w, start_w, nsub_w):
    n_rows, D = xs.shape
    nfc = D_FF // MOE_FC
    assert nfc >= 2
    n_work = e_w.shape[0]

    def chunk(c, wn, w):
        return jnp.where(wn[w] > 0, c, nfc - 1)

    return pl.pallas_call(
        functools.partial(_moe_kernel, nfc=nfc, n_work=n_work),
        out_shape=jax.ShapeDtypeStruct((n_rows, D), F32),
        grid_spec=pltpu.PrefetchScalarGridSpec(
            num_scalar_prefetch=3, grid=(n_work, nfc),
            in_specs=[pl.BlockSpec(memory_space=pl.ANY),
                      pl.BlockSpec((None, None, D, MOE_FC), lambda w, c, we, ws, wn: (l, we[w], 0, chunk(c, wn, w))),
                      pl.BlockSpec((None, None, D, MOE_FC), lambda w, c, we, ws, wn: (l, we[w], 0, chunk(c, wn, w))),
                      pl.BlockSpec((None, None, MOE_FC, D), lambda w, c, we, ws, wn: (l, we[w], chunk(c, wn, w), 0))],
            out_specs=pl.BlockSpec(memory_space=pl.ANY),
            scratch_shapes=[pltpu.VMEM((MOE_ROWS, D), BF16), pltpu.VMEM((MOE_ROWS, D), F32),
                            pltpu.VMEM((D, MOE_FC), BF16), pltpu.VMEM((D, MOE_FC), BF16),
                            pltpu.VMEM((MOE_FC, D), BF16),
                            pltpu.VMEM((2, MOE_SUB, D), F32), pltpu.VMEM((2, MOE_SUB, D), F32),
                            pltpu.SemaphoreType.DMA((2,)), pltpu.SemaphoreType.DMA((2,))]),
        input_output_aliases={3: 0},
        compiler_params=_cparams(("arbitrary", "arbitrary")),
        name="moe_experts",
    )(e_w, start_w, nsub_w, xs, w_gate, w_up, w_down)


def _combine_kernel(pos_ref, ys_ref, x_ref, rtw_ref, gate2_ref, g_ref, *rest, tc, T, final):
    if final:
        out_ref, ybuf, sem = rest
    else:
        sh_ref, sc_ref, xo_ref, h_ref, ybuf, sem = rest
    i = pl.program_id(0)
    slot = lax.rem(i, 2)
    ngroups = tc // COMBINE_GROUP

    def copy(tile, g, u, k, sl):
        p = pos_ref[k * T + tile * tc + g * ROW_GROUP + u]
        return pltpu.make_async_copy(ys_ref.at[pl.ds(p, 1), :], ybuf.at[sl, k, g, pl.ds(u, 1), :], sem.at[sl])

    def for_rows(tile, sl, r0, op):
        g0 = r0 // ROW_GROUP
        for gg in range(COMBINE_GROUP // ROW_GROUP):
            for u in range(ROW_GROUP):
                for k in range(2):
                    getattr(copy(tile, g0 + gg, u, k, sl), op)()

    def groups(fn):
        def body(g, carry):
            fn(pl.multiple_of(g * COMBINE_GROUP, COMBINE_GROUP))
            return carry
        lax.fori_loop(0, ngroups, body, 0)

    def compute(r0):
        rows = pl.ds(r0, COMBINE_GROUP)
        grp = pl.ds(r0 // ROW_GROUP, COMBINE_GROUP // ROW_GROUP)
        y = [ybuf[slot, k, grp, :, :].reshape(COMBINE_GROUP, D_MODEL) for k in range(2)]
        moe = rtw_ref[rows, 0:1] * y[0] + rtw_ref[rows, 1:2] * y[1]
        xo = x_ref[rows, :] + gate2_ref[...] * moe
        if final:
            ms = jnp.mean(xo * xo, axis=-1, keepdims=True)
            out_ref[rows, :] = xo * lax.rsqrt(ms + EPS) * g_ref[...]
        else:
            xo_ref[rows, :] = xo
            h_ref[rows, :] = _norm_mod(xo, g_ref[...], sh_ref[...], sc_ref[...]).astype(BF16)

    @pl.when(i == 0)
    def _():
        groups(lambda r0: for_rows(0, 0, r0, "start"))
    groups(lambda r0: for_rows(i, slot, r0, "wait"))

    @pl.when(i + 1 < pl.num_programs(0))
    def _():
        def both(r0):
            compute(r0)
            for_rows(i + 1, 1 - slot, r0, "start")
        groups(both)

    @pl.when(i + 1 >= pl.num_programs(0))
    def _():
        groups(compute)


def combine(ys, xn, rtw, pos_flat, modr, l, S, g_next, final):
    T, D = xn.shape
    tc = 256
    assert S % tc == 0
    row = lambda: pl.BlockSpec((tc, D), lambda i, *_: (i, 0))
    in_specs = [pl.BlockSpec(memory_space=pl.ANY), row(),
                pl.BlockSpec((tc, LANES), lambda i, *_: (i, 0)),
                _mod_spec(l, 5, S, tc)]
    args = [ys, xn, rtw, modr]
    if final:
        in_specs.append(pl.BlockSpec((1, D), lambda i, *_: (0, 0)))
        args.append(g_next.reshape(1, D))
        out_shape = jax.ShapeDtypeStruct((T, D), F32)
        out_specs = row()
    else:
        in_specs += [pl.BlockSpec((None, 1, D), lambda i, *_: (l + 1, 0, 0)),
                     _mod_spec(l + 1, 0, S, tc), _mod_spec(l + 1, 1, S, tc)]
        args += [g_next.reshape(-1, 1, D), modr, modr]
        out_shape = (jax.ShapeDtypeStruct((T, D), F32), jax.ShapeDtypeStruct((T, D), BF16))
        out_specs = (row(), row())
    return pl.pallas_call(
        functools.partial(_combine_kernel, tc=tc, T=T, final=final),
        out_shape=out_shape,
        grid_spec=pltpu.PrefetchScalarGridSpec(
            num_scalar_prefetch=1, grid=(T // tc,),
            in_specs=in_specs, out_specs=out_specs,
            scratch_shapes=[pltpu.VMEM((2, 2, tc // ROW_GROUP, ROW_GROUP, D), F32), pltpu.SemaphoreType.DMA((2,))]),
        compiler_params=_cparams(("arbitrary",)),
        name="moe_combine_final" if final else "moe_combine",
    )(pos_flat, *args)


def kernel(x, c, positions, w_ada, b_ada, norm1_g, norm2_g, w_in, attn_sinks, w_attn_out, ret_norm_g,
           w_ret_out, w_o, w_router, b_router, w_gate, w_up, w_down, final_g):
    B, S, D = x.shape
    L = w_ada.shape[0]
    T = B * S
    assert D == D_MODEL and w_in.shape[-1] == IN_W and S % RET_CHUNK == 0

    tab = rope_tables(positions)
    mod = adaln_mod(c, w_ada, b_ada)
    modr = mod[:, :B].reshape(L, B, 6, D).transpose(0, 2, 1, 3).reshape(L, 6, B, 1, D)

    x2 = x.reshape(T, D)
    h = norm_modulate(x2, norm1_g.reshape(L, 1, D), modr, 0, S)
    n_work = N_EXPERTS + (2 * T + MOE_ROWS - 1) // MOE_ROWS
    n_rows = 2 * T + N_EXPERTS * MOE_SUB
    out = None
    for l in range(L):
        proj = in_projection(h, w_in, l, tab)
        att = swa_attention(proj, attn_sinks[l], B, S)
        ret = retention(proj, ret_norm_g, l, B, S)
        xn, h2, rti, rtw, cnt = mixer_out(att, ret, proj, x2, w_attn_out, w_ret_out, w_o, modr,
                                          norm2_g, w_router, b_router, l, S)
        off17, e_w, start_w, nsub_w = moe_plan(cnt[0, :N_EXPERTS], n_work)
        seg_off = jnp.sum(jnp.where(rti[:, 0:2, None] == jnp.arange(N_EXPERTS, dtype=I32),
                                    off17[:N_EXPERTS], 0), axis=-1)
        pos_flat = (seg_off + rti[:, 2:4]).T.reshape(2 * T)
        xs = dispatch(h2, pos_flat, off17, n_rows)
        ys = moe_experts(xs, w_gate, w_up, w_down, l, e_w, start_w, nsub_w)
        if l + 1 < L:
            x2, h = combine(ys, xn, rtw, pos_flat, modr, l, S, norm1_g, final=False)
        else:
            out = combine(ys, xn, rtw, pos_flat, modr, l, S, final_g, final=True)
    return out.reshape(B, S, D)
```

```python
import functools

import numpy as np
import jax
import jax.numpy as jnp
from jax import lax
from jax.experimental import pallas as pl
from jax.experimental.pallas import tpu as pltpu

F32 = jnp.float32
BF16 = jnp.bfloat16
I32 = jnp.int32

D_MODEL = 2048
ATT_HEAD_DIM = 64
ATT_Q_HEADS = 16
ATT_KV_HEADS = 4
WINDOW = 128
RET_HEADS = 8
RET_HEAD_DIM = 128
RET_CHUNK = 256
ROPE_THETA = 10000.0
N_GROUPS = 4
EXPERTS_PER_GROUP = 4
N_EXPERTS = 16
D_FF = 1024
EPS = 1e-6

ATT_Q_W = ATT_Q_HEADS * ATT_HEAD_DIM
ATT_KV_W = ATT_KV_HEADS * ATT_HEAD_DIM
RET_W = RET_HEADS * RET_HEAD_DIM
OFF_QA = 0
OFF_KA = OFF_QA + ATT_Q_W
OFF_VA = OFF_KA + ATT_KV_W
OFF_QR = OFF_VA + ATT_KV_W
OFF_KR = OFF_QR + RET_W
OFF_VR = OFF_KR + RET_W
OFF_GR = OFF_VR + RET_W
OFF_GA = OFF_GR + RET_W
OFF_GB = OFF_GA + D_MODEL
IN_W = OFF_GB + D_MODEL

LANES = 128
PROJ_TN = 512
ROW_GROUP = 8
ATTN_LOOKAHEAD = 2
COMBINE_GROUP = 64
RET_HEADS_PER_STEP = 2
MOE_SUB = 256
MOE_ROWS = 2048
MOE_FC = 256
VMEM_LIMIT = 56 * 1024 * 1024


def _pick(n, cands):
    for c in cands:
        if n % c == 0:
            return c
    raise ValueError(f"no tile in {cands} divides {n}")


def _sigmoid(x):
    return 1.0 / (1.0 + jnp.exp(-x))


def _cparams(sem, vmem=VMEM_LIMIT):
    return pltpu.CompilerParams(dimension_semantics=sem, vmem_limit_bytes=vmem)


def _rope_consts():
    def inv_freq(head_dim):
        half = head_dim // 2
        inv = ROPE_THETA ** (-2.0 * jnp.arange(half, dtype=F32) / head_dim)
        return jnp.broadcast_to(jnp.tile(inv, LANES // half), (8, LANES))
    lane = np.arange(LANES)
    sgn_att = np.where((lane % ATT_HEAD_DIM) < ATT_HEAD_DIM // 2, -1.0, 1.0)
    sgn_ret = np.where(lane < RET_HEAD_DIM // 2, -1.0, 1.0)
    sgn = np.stack([np.tile(sgn_att, (8, 1)), np.tile(sgn_ret, (8, 1))]).astype(np.float32)
    return jnp.stack([inv_freq(ATT_HEAD_DIM), inv_freq(RET_HEAD_DIM)]), jnp.asarray(sgn)


def _rope_kernel(pos_ref, inv_ref, sgn_ref, tab_ref):
    pos = pos_ref[...]
    for kind in range(2):
        ang = pos * inv_ref[kind, 0:1, :]
        tab_ref[kind, :, 0:LANES] = jnp.cos(ang)
        tab_ref[kind, :, LANES:2 * LANES] = jnp.sin(ang) * sgn_ref[kind, 0:1, :]


def rope_tables(positions):
    T = positions.size
    posb = jnp.broadcast_to(positions.reshape(T, 1).astype(F32), (T, LANES))
    inv, sgn = _rope_consts()
    tm = _pick(T, (1024, 512, 256, 128))
    return pl.pallas_call(
        _rope_kernel,
        out_shape=jax.ShapeDtypeStruct((2, T, 2 * LANES), F32),
        grid=(T // tm,),
        in_specs=[pl.BlockSpec((tm, LANES), lambda i: (i, 0)),
                  pl.BlockSpec((2, 8, LANES), lambda i: (0, 0, 0)),
                  pl.BlockSpec((2, 8, LANES), lambda i: (0, 0, 0))],
        out_specs=pl.BlockSpec((2, tm, 2 * LANES), lambda i: (0, i, 0)),
        compiler_params=_cparams(("arbitrary",)),
        name="rope_tables",
    )(posb, inv, sgn)


def _adaln_kernel(c_ref, w_ref, b_ref, o_ref):
    c = c_ref[...]
    ca = c * _sigmoid(c)
    hi = ca.astype(BF16).astype(F32)
    lhs = jnp.concatenate([hi, ca - hi], axis=0).astype(BF16)
    r = jnp.dot(lhs, w_ref[...].astype(BF16), preferred_element_type=F32)
    o_ref[...] = r[0:8] + r[8:16] + b_ref[...]


def adaln_mod(c, w_ada, b_ada):
    L, D, N = w_ada.shape
    B = c.shape[0]
    cp = jnp.pad(c, ((0, 8 - B), (0, 0)))
    tn = _pick(N, (1024, 512))
    return pl.pallas_call(
        _adaln_kernel,
        out_shape=jax.ShapeDtypeStruct((L, 8, N), F32),
        grid=(L, N // tn),
        in_specs=[pl.BlockSpec((8, D), lambda l, j: (0, 0)),
                  pl.BlockSpec((None, D, tn), lambda l, j: (l, 0, j)),
                  pl.BlockSpec((None, 1, tn), lambda l, j: (l, 0, j))],
        out_specs=pl.BlockSpec((None, 8, tn), lambda l, j: (l, 0, j)),
        compiler_params=_cparams(("arbitrary", "arbitrary")),
        name="adaln_mod",
    )(cp, w_ada, b_ada.reshape(L, 1, N))


def _norm_mod(x, g, shift, scale):
    ms = jnp.mean(x * x, axis=-1, keepdims=True)
    return (x * lax.rsqrt(ms + EPS) * g) * (1.0 + scale) + shift


def _norm_mod_kernel(x_ref, g_ref, sh_ref, sc_ref, h_ref):
    h_ref[...] = _norm_mod(x_ref[...], g_ref[...], sh_ref[...], sc_ref[...]).astype(BF16)


def _mod_spec(l, k, S, tm):
    return pl.BlockSpec((None, None, None, 1, D_MODEL), lambda i, *_: (l, k, (i * tm) // S, 0, 0))


def norm_modulate(x2, g, modr, l, S):
    T, D = x2.shape
    tm = _pick(S, (512, 256, 128))
    return pl.pallas_call(
        _norm_mod_kernel,
        out_shape=jax.ShapeDtypeStruct((T, D), BF16),
        grid=(T // tm,),
        in_specs=[pl.BlockSpec((tm, D), lambda i: (i, 0)),
                  pl.BlockSpec((None, 1, D), lambda i: (l, 0, 0)),
                  _mod_spec(l, 0, S, tm), _mod_spec(l, 1, S, tm)],
        out_specs=pl.BlockSpec((tm, D), lambda i: (i, 0)),
        compiler_params=_cparams(("arbitrary",)),
        name="norm1_modulate",
    )(x2, g, modr, modr)


def _rope_att(blk, cos, sin, scale):
    lane = lax.broadcasted_iota(I32, blk.shape, 1)
    rot = jnp.where((lane % 64) < 32, pltpu.roll(blk, 96, 1), pltpu.roll(blk, 32, 1))
    out = blk * cos + rot * sin
    return out * scale if scale != 1.0 else out


def _rope_ret(blk, cos, sin, scale):
    out = blk * cos + pltpu.roll(blk, 64, 1) * sin
    return out * scale if scale != 1.0 else out


def _inproj_kernel(h_ref, w_ref, tab_ref, o_ref, wbf_ref, acc_ref, *, ni, ntiles):
    s = pl.program_id(0)
    j = jnp.maximum(s - 1, 0) // ni
    tm = o_ref.shape[0]
    h_rows = pl.ds(pl.multiple_of(lax.rem(jnp.minimum(s, ntiles - 1), ni) * tm, tm), tm)

    @pl.when(s == 0)
    def _():
        acc_ref[1] = jnp.zeros(acc_ref.shape[1:], F32)

    @pl.when((lax.rem(s, ni) == 0) & (s < ntiles))
    def _():
        wbf_ref[...] = w_ref[...].astype(BF16)

    ngrp = PROJ_TN // LANES
    t = lambda off: off // PROJ_TN
    q_scale = float(ATT_HEAD_DIM) ** -0.5
    k_scale = float(RET_HEAD_DIM) ** -0.5

    def step(cur):
        def run(epilogue):
            cos = tab_ref[:, 0:LANES]
            sin = tab_ref[:, LANES:2 * LANES]
            for g in range(ngrp):
                blk = acc_ref[1 - cur, :, g * LANES:(g + 1) * LANES]
                o_ref[:, g * LANES:(g + 1) * LANES] = epilogue(g, blk, cos, sin).astype(BF16)
            acc_ref[cur] = jnp.dot(h_ref[h_rows, :], wbf_ref[...], preferred_element_type=F32)

        @pl.when(j < t(OFF_KA))
        def _():
            run(lambda g, b, c, s: _rope_att(b, c, s, q_scale))

        @pl.when(j == t(OFF_KA))
        def _():
            run(lambda g, b, c, s: _rope_att(b, c, s, 1.0) if g < ATT_KV_W // LANES else b)

        @pl.when((j >= t(OFF_QR)) & (j < t(OFF_KR)))
        def _():
            run(lambda g, b, c, s: _rope_ret(b, c, s, 1.0))

        @pl.when((j >= t(OFF_KR)) & (j < t(OFF_VR)))
        def _():
            run(lambda g, b, c, s: _rope_ret(b, c, s, k_scale))

        @pl.when((j >= t(OFF_VR)) & (j < t(OFF_GR)))
        def _():
            run(lambda g, b, c, s: b)

        @pl.when((j >= t(OFF_GR)) & (j < t(OFF_GA)))
        def _():
            run(lambda g, b, c, s: b * _sigmoid(b))

        @pl.when(j >= t(OFF_GA))
        def _():
            run(lambda g, b, c, s: _sigmoid(b))

    for parity in range(2):
        pl.when(lax.rem(s, 2) == parity)(functools.partial(step, parity))


def in_projection(h, w_in, l, tab):
    T, D = h.shape
    tm = _pick(T, (1024, 768, 512, 256))
    nj, ni = IN_W // PROJ_TN, T // tm
    ntiles = nj * ni
    t_qr, t_vr = OFF_QR // PROJ_TN, OFF_VR // PROJ_TN
    mm = lambda s: jnp.minimum(s, ntiles - 1)
    ep = lambda s: jnp.maximum(s - 1, 0)

    def tab_map(s):
        j, i = ep(s) // ni, lax.rem(ep(s), ni)
        return (jnp.where(j < t_qr, 0, 1), jnp.where(j < t_vr, i, 0), 0)

    return pl.pallas_call(
        functools.partial(_inproj_kernel, ni=ni, ntiles=ntiles),
        out_shape=jax.ShapeDtypeStruct((nj, T, PROJ_TN), BF16),
        grid=(ntiles + 1,),
        in_specs=[pl.BlockSpec((T, D), lambda s: (0, 0), pipeline_mode=pl.Buffered(1)),
                  pl.BlockSpec((None, D, PROJ_TN), lambda s: (l, 0, mm(s) // ni)),
                  pl.BlockSpec((None, tm, 2 * LANES), tab_map)],
        out_specs=pl.BlockSpec((None, tm, PROJ_TN), lambda s: (ep(s) // ni, lax.rem(ep(s), ni), 0)),
        scratch_shapes=[pltpu.VMEM((D, PROJ_TN), BF16), pltpu.VMEM((2, tm, PROJ_TN), F32)],
        compiler_params=_cparams(("arbitrary",)),
        name="in_projection",
    )(h, w_in, tab)


def _attn_kernel(sinks_ref, q0_ref, q1_ref, kvc_ref, kvp_ref, o_ref, *, tq):
    W = WINDOW
    nsub = tq // W
    is_first = pl.program_id(1) == 0
    qi = lax.broadcasted_iota(I32, (W, 2 * W), 0)
    kj = lax.broadcasted_iota(I32, (W, 2 * W), 1)
    rel = qi + W - kj
    band = (rel >= 0) & (rel < WINDOW)
    band0 = band & (jnp.logical_not(is_first) | (kj >= W))
    lane = lax.broadcasted_iota(I32, (tq + W, LANES), 1)
    lo = lane < ATT_HEAD_DIM

    def split_pair(raw, parity):
        x = raw.astype(F32)
        xr = pltpu.roll(x, ATT_HEAD_DIM, 1)
        if parity == 0:
            return jnp.where(lo, x, 0.0), jnp.where(lo, 0.0, xr)
        return jnp.where(lo, xr, 0.0), jnp.where(lo, 0.0, x)

    units = []
    for hk in range(ATT_KV_HEADS):
        grp, par = hk // 2, hk % 2
        kc = slice(grp * LANES, (grp + 1) * LANES)
        vc = slice(ATT_KV_W + grp * LANES, ATT_KV_W + (grp + 1) * LANES)
        k_pair = split_pair(jnp.concatenate([kvp_ref[:, kc], kvc_ref[:, kc]], axis=0), par)
        v_pair = split_pair(jnp.concatenate([kvp_ref[:, vc], kvc_ref[:, vc]], axis=0), par)
        kt_pair = [k.T.astype(BF16) for k in k_pair]
        v_pair = [v.astype(BF16) for v in v_pair]
        qcols = [slice(hk * 4 * ATT_HEAD_DIM + jq * LANES, hk * 4 * ATT_HEAD_DIM + (jq + 1) * LANES)
                 for jq in range(2)]
        for m in range(nsub):
            units.append((hk, m, kt_pair, v_pair, qcols))

    def scores(unit):
        hk, m, kt_pair, v_pair, qcols = unit
        rows = slice(m * W, (m + 1) * W)
        keys = slice(m * W, m * W + 2 * W)
        kcat = jnp.concatenate([kt_pair[0][:, keys], kt_pair[1][:, keys]], axis=1)
        q_ref = (q0_ref, q1_ref)[hk // 2]
        c0 = (hk % 2) * 4 * ATT_HEAD_DIM
        q = jnp.concatenate([q_ref[rows, c0:c0 + LANES], q_ref[rows, c0 + LANES:c0 + 2 * LANES]], axis=0)
        return jnp.dot(q, kcat, preferred_element_type=F32)

    def finish(unit, s_all):
        hk, m, kt_pair, v_pair, qcols = unit
        rows = slice(m * W, (m + 1) * W)
        keys = slice(m * W, m * W + 2 * W)
        mask = band0 if m == 0 else band
        vcat = jnp.concatenate([v_pair[0][keys], v_pair[1][keys]], axis=0)
        p_rows = []
        for jq in range(2):
            p_cols = []
            for e in range(2):
                s = s_all[jq * W:(jq + 1) * W, e * 2 * W:(e + 1) * 2 * W]
                s = jnp.where(mask, s, -1e30)
                sink = sinks_ref[hk * 4 + jq * 2 + e]
                mx = jnp.maximum(jnp.max(s, axis=1, keepdims=True), sink)
                p = jnp.exp(s - mx)
                den = jnp.sum(p, axis=1, keepdims=True) + jnp.exp(sink - mx)
                p_cols.append((p * (1.0 / den)).astype(BF16))
            p_rows.append(jnp.concatenate(p_cols, axis=1))
        o = jnp.dot(jnp.concatenate(p_rows, axis=0), vcat, preferred_element_type=F32)
        o_ref[rows, qcols[0]] = o[0:W].astype(BF16)
        o_ref[rows, qcols[1]] = o[W:2 * W].astype(BF16)

    pending = [scores(u) for u in units[:ATTN_LOOKAHEAD]]
    for idx, unit in enumerate(units):
        if idx + ATTN_LOOKAHEAD < len(units):
            pending.append(scores(units[idx + ATTN_LOOKAHEAD]))
        finish(unit, pending.pop(0))


def swa_attention(proj, sinks, B, S):
    T = proj.shape[1]
    tq = _pick(S, (512, 384, 256, 128))
    nq = S // tq
    assert ATT_Q_W == 2 * PROJ_TN and 2 * ATT_KV_W == PROJ_TN
    kv_tile = OFF_KA // PROJ_TN
    tile = lambda j: pl.BlockSpec((None, tq, PROJ_TN), lambda b, i: (j, b * nq + i, 0))

    def prev_map(b, i):
        return (kv_tile, jnp.maximum(b * (S // WINDOW) + i * (tq // WINDOW) - 1, 0), 0)

    return pl.pallas_call(
        functools.partial(_attn_kernel, tq=tq),
        out_shape=jax.ShapeDtypeStruct((T, ATT_Q_W), BF16),
        grid=(B, nq),
        in_specs=[pl.BlockSpec(memory_space=pltpu.SMEM),
                  tile(0), tile(1), tile(kv_tile),
                  pl.BlockSpec((None, WINDOW, PROJ_TN), prev_map)],
        out_specs=pl.BlockSpec((tq, ATT_Q_W), lambda b, i: (b * nq + i, 0)),
        compiler_params=_cparams(("arbitrary", "arbitrary")),
        name="swa_attention",
    )(sinks, proj, proj, proj, proj)


def _ret_consts():
    C = RET_CHUNK
    log_g = jnp.log1p(-jnp.exp2(-5.0 - jnp.arange(RET_HEADS, dtype=F32)))
    i = jnp.arange(C, dtype=F32)
    diff = i[:, None] - i[None, :]
    dm = jnp.where(diff[None] >= 0, jnp.exp(jnp.maximum(diff, 0.0)[None] * log_g[:, None, None]), 0.0)
    qd = jnp.exp((i + 1.0)[None, :] * log_g[:, None])
    kd = jnp.exp((C - 1.0 - i)[None, :] * log_g[:, None])
    cd = jnp.exp(C * log_g)
    bc = lambda v: jnp.broadcast_to(v[:, :, None], (RET_HEADS, C, LANES))
    return dm, bc(qd), bc(kd), jnp.broadcast_to(cd[:, None, None], (RET_HEADS, 8, LANES))


def _ret_kernel(q_ref, k_ref, v_ref, g_ref, gn_ref, dm_ref, qd_ref, kd_ref, cd_ref, o_ref, *, nchunk):
    C = RET_CHUNK
    heads = range(RET_HEADS_PER_STEP)
    lanes = [slice(hh * LANES, (hh + 1) * LANES) for hh in heads]
    rows = [slice(c * C, (c + 1) * C) for c in range(nchunk)]

    def independent(c, hh):
        q, k, v = q_ref[rows[c], lanes[hh]], k_ref[rows[c], lanes[hh]], v_ref[rows[c], lanes[hh]]
        s = lax.dot_general(q, k, (((1,), (1,)), ((), ())), preferred_element_type=F32) * dm_ref[hh]
        vk = (v.astype(F32) * kd_ref[hh]).astype(BF16)
        kv = lax.dot_general(k, vk, (((0,), (0,)), ((), ())), preferred_element_type=F32)
        return s.astype(BF16), kv

    def finish(c, hh, s, state):
        q, v = q_ref[rows[c], lanes[hh]], v_ref[rows[c], lanes[hh]]
        y = jnp.dot(s, v, preferred_element_type=F32)
        y = y + jnp.dot(q, state.astype(BF16), preferred_element_type=F32) * qd_ref[hh]
        ms = jnp.mean(y * y, axis=-1, keepdims=True)
        yn = y * lax.rsqrt(ms + EPS) * gn_ref[:, lanes[hh]]
        o_ref[rows[c], lanes[hh]] = (g_ref[rows[c], lanes[hh]].astype(F32) * yn).astype(BF16)

    state = [jnp.zeros((RET_HEAD_DIM, RET_HEAD_DIM), F32) for _ in heads]
    ahead = [independent(0, hh) for hh in heads]
    for c in range(nchunk):
        cur = ahead
        if c + 1 < nchunk:
            ahead = [independent(c + 1, hh) for hh in heads]
        for hh in heads:
            s, kv = cur[hh]
            finish(c, hh, s, state[hh])
            state[hh] = cd_ref[hh, 0:1, :] * state[hh] + kv


def retention(proj, ret_norm_g, l, B, S):
    T = proj.shape[1]
    dm, qd, kd, cd = _ret_consts()
    hp = RET_HEADS_PER_STEP
    wid = hp * LANES
    per_tile = PROJ_TN // wid

    def part(off):
        return pl.BlockSpec((None, S, wid), lambda b, h: (off // PROJ_TN + h // per_tile, b, lax.rem(h, per_tile)))
    hconst = lambda rows: pl.BlockSpec((hp, rows, LANES), lambda b, h: (h, 0, 0))
    return pl.pallas_call(
        functools.partial(_ret_kernel, nchunk=S // RET_CHUNK),
        out_shape=jax.ShapeDtypeStruct((T, RET_W), BF16),
        grid=(B, RET_HEADS // hp),
        in_specs=[part(OFF_QR), part(OFF_KR), part(OFF_VR), part(OFF_GR),
                  pl.BlockSpec((None, 1, wid), lambda b, h: (l, 0, h)),
                  pl.BlockSpec((hp, RET_CHUNK, RET_CHUNK), lambda b, h: (h, 0, 0)),
                  hconst(RET_CHUNK), hconst(RET_CHUNK), hconst(8)],
        out_specs=pl.BlockSpec((S, wid), lambda b, h: (b, h)),
        compiler_params=_cparams(("arbitrary", "arbitrary")),
        name="retention",
    )(proj, proj, proj, proj, ret_norm_g.reshape(-1, 1, RET_W), dm, qd, kd, cd)


def _load_cast(w_hbm, dst, stage, sem):
    rows = stage.shape[0]
    n = w_hbm.shape[0] // rows

    def body(i, carry):
        r0 = pl.multiple_of(i * rows, rows)
        cp = pltpu.make_async_copy(w_hbm.at[pl.ds(r0, rows), :], stage, sem)
        cp.start()
        cp.wait()
        dst[pl.ds(r0, rows), :] = stage[...].astype(BF16)
        return carry
    lax.fori_loop(0, n, body, 0)


def _mixer_out_kernel(att_ref, ret_ref, ga0, ga1, ga2, ga3, gb0, gb1, gb2, gb3, x_ref,
                      wa_hbm, wr_hbm, wo_hbm, gate1_ref, g2_ref, sh2_ref, sc2_ref,
                      wrt_ref, brt_ref, tril_ref,
                      xn_ref, h2_ref, rti_ref, rtw_ref, cnt_ref,
                      wa, wr, wo, wrt2, mrg, carry, hs, stage, sem, *, tm, l):
    i = pl.program_id(0)
    nt = pl.num_programs(0) - 1
    D = D_MODEL

    @pl.when(i == 0)
    def _():
        _load_cast(wa_hbm.at[l], wa, stage, sem)
        _load_cast(wr_hbm.at[l], wr, stage, sem)
        _load_cast(wo_hbm.at[l], wo, stage, sem)
        w = wrt_ref[...]
        hi = w.astype(BF16)
        wrt2[:, 0:LANES] = hi
        wrt2[:, LANES:2 * LANES] = (w - hi.astype(F32)).astype(BF16)
        carry[...] = jnp.zeros_like(carry)
        hs[...] = jnp.zeros_like(hs)

    def main_stage():
        a = att_ref[...]
        r = ret_ref[...]
        ga = (ga0, ga1, ga2, ga3)
        gb = (gb0, gb1, gb2, gb3)
        for n in range(D // PROJ_TN):
            cols = slice(n * PROJ_TN, (n + 1) * PROJ_TN)
            ya = jnp.dot(a, wa[:, cols], preferred_element_type=F32)
            yr = jnp.dot(r, wr[:, cols], preferred_element_type=F32)
            mrg[:, cols] = (ga[n][...].astype(F32) * ya + gb[n][...].astype(F32) * yr).astype(BF16)
        o = jnp.dot(mrg[...], wo[...], preferred_element_type=F32)
        xn = x_ref[...] + gate1_ref[...] * o
        xn_ref[...] = xn
        h2 = _norm_mod(xn, g2_ref[...], sh2_ref[...], sc2_ref[...])
        h2b = h2.astype(BF16)
        h2_ref[...] = h2b
        hs[0:tm, :] = h2b
        hs[tm:2 * tm, :] = (h2 - h2b.astype(F32)).astype(BF16)

    def router_select():
        r = jnp.dot(hs[...], wrt2[...], preferred_element_type=F32)
        logits = ((r[0:tm, 0:LANES] + r[tm:2 * tm, 0:LANES])
                  + (r[0:tm, LANES:2 * LANES] + r[tm:2 * tm, LANES:2 * LANES])) + brt_ref[...]
        lane = lax.broadcasted_iota(I32, (tm, LANES), 1)
        valid = lane < N_EXPERTS
        mx = jnp.max(logits, axis=-1, keepdims=True)
        p = jnp.where(valid, jnp.exp(logits - mx), 0.0)
        pos_in_grp = lane % EXPERTS_PER_GROUP
        grp_of = lane // EXPERTS_PER_GROUP

        def member(k):
            wrapped = pos_in_grp + k >= EXPERTS_PER_GROUP
            return jnp.where(wrapped, pltpu.roll(p, EXPERTS_PER_GROUP - k, 1), pltpu.roll(p, LANES - k, 1)), wrapped

        (b1, w1), (b2, w2), (b3, w3) = member(1), member(2), member(3)
        m_ab, n_ab = jnp.maximum(p, b1), jnp.minimum(p, b1)
        m_cd, n_cd = jnp.maximum(b2, b3), jnp.minimum(b2, b3)
        gscore = jnp.maximum(m_ab, m_cd) + jnp.maximum(jnp.minimum(m_ab, m_cd), jnp.maximum(n_ab, n_cd))
        gscore = jnp.where(valid, gscore, -1.0)
        gmax = jnp.max(gscore, axis=-1, keepdims=True)
        gsel = jnp.min(jnp.where(gscore == gmax, grp_of, N_GROUPS), axis=-1, keepdims=True)
        in_sel = grp_of == gsel
        beats = lambda b, w: ((b > p) | ((b == p) & w)).astype(I32)
        rank_in_grp = beats(b1, w1) + beats(b2, w2) + beats(b3, w3)
        sel0 = in_sel & (rank_in_grp == 0)
        sel1 = in_sel & (rank_in_grp == 1)
        lsum = lambda m, v: jnp.sum(jnp.where(m, v, 0.0), axis=-1, keepdims=True)
        v0, v1 = lsum(sel0, p), lsum(sel1, p)
        lanef = lane.astype(F32)
        e0, e1 = lsum(sel0, lanef), lsum(sel1, lanef)
        inv = 1.0 / (v0 + v1)
        return lane, sel0, sel1, e0, e1, v0 * inv, v1 * inv

    def router_finish(live, lane, sel0, sel1, e0, e1, w0, w1):
        lsum = lambda m, v: jnp.sum(jnp.where(m, v, 0.0), axis=-1, keepdims=True)
        onehot = jnp.where(sel0 | sel1, 1.0, 0.0)
        prefix = jnp.dot(tril_ref[...], onehot.astype(BF16), preferred_element_type=F32) + carry[0:1, :]
        r0, r1 = lsum(sel0, prefix), lsum(sel1, prefix)
        carry[...] = carry[...] + live * jnp.sum(onehot, axis=0, keepdims=True)
        cnt_ref[...] = carry[...].astype(I32)
        sel4 = lambda a0, a1, a2, a3: jnp.where(lane == 0, a0, jnp.where(lane == 1, a1, jnp.where(lane == 2, a2, a3)))
        rti_ref[...] = sel4(e0, e1, r0, r1).astype(I32)
        rtw_ref[...] = jnp.where(lane == 0, w0, jnp.where(lane == 1, w1, 0.0))

    @pl.when(i < nt)
    def _():
        picked = router_select()
        main_stage()
        router_finish(jnp.where(i > 0, 1.0, 0.0), *picked)

    @pl.when(i == nt)
    def _():
        router_finish(1.0, *router_select())


def mixer_out(att, ret, proj, x2, w_attn_out, w_ret_out, w_o, modr, norm2_g, w_router, b_router, l, S):
    T, D = x2.shape
    tm = 256
    assert S % tm == 0
    nga, ngb = OFF_GA // PROJ_TN, OFF_GB // PROJ_TN
    nt = T // tm
    cur = lambda i: jnp.minimum(i, nt - 1)
    prev = lambda i: jnp.maximum(i - 1, 0)
    gate_spec = lambda blk: pl.BlockSpec((None, tm, PROJ_TN), lambda i: (blk, cur(i), 0))
    row = lambda: pl.BlockSpec((tm, D), lambda i: (cur(i), 0))
    mod_spec = lambda k: pl.BlockSpec((None, None, None, 1, D), lambda i: (l, k, (cur(i) * tm) // S, 0, 0))
    wrt = jnp.pad(w_router, ((0, 0), (0, LANES - N_EXPERTS)))
    brt = jnp.pad(b_router.astype(F32), (0, LANES - N_EXPERTS), constant_values=-1e30).reshape(1, LANES)
    tril = jnp.asarray(np.tril(np.ones((tm, tm), np.float32), -1), BF16)
    any_spec = pl.BlockSpec(memory_space=pl.ANY)
    outs = pl.pallas_call(
        functools.partial(_mixer_out_kernel, tm=tm, l=l),
        out_shape=(jax.ShapeDtypeStruct((T, D), F32),
                   jax.ShapeDtypeStruct((T, D), BF16),
                   jax.ShapeDtypeStruct((T, LANES), I32),
                   jax.ShapeDtypeStruct((T, LANES), F32),
                   jax.ShapeDtypeStruct((8, LANES), I32)),
        grid=(nt + 1,),
        in_specs=[pl.BlockSpec((tm, ATT_Q_W), lambda i: (cur(i), 0)),
                  pl.BlockSpec((tm, RET_W), lambda i: (cur(i), 0)),
                  *[gate_spec(nga + n) for n in range(4)],
                  *[gate_spec(ngb + n) for n in range(4)],
                  row(), any_spec, any_spec, any_spec,
                  mod_spec(2),
                  pl.BlockSpec((None, 1, D), lambda i: (l, 0, 0)),
                  mod_spec(3), mod_spec(4),
                  pl.BlockSpec((D, LANES), lambda i: (0, 0)),
                  pl.BlockSpec((1, LANES), lambda i: (0, 0)),
                  pl.BlockSpec((tm, tm), lambda i: (0, 0))],
        out_specs=(row(),
                   row(),
                   pl.BlockSpec((tm, LANES), lambda i: (prev(i), 0)),
                   pl.BlockSpec((tm, LANES), lambda i: (prev(i), 0)),
                   pl.BlockSpec((8, LANES), lambda i: (0, 0))),
        scratch_shapes=[pltpu.VMEM((ATT_Q_W, D), BF16), pltpu.VMEM((RET_W, D), BF16),
                        pltpu.VMEM((D, D), BF16),
                        pltpu.VMEM((D, 2 * LANES), BF16),
                        pltpu.VMEM((tm, D), BF16), pltpu.VMEM((8, LANES), F32),
                        pltpu.VMEM((2 * tm, D), BF16),
                        pltpu.VMEM((256, D), F32), pltpu.SemaphoreType.DMA],
        compiler_params=_cparams(("arbitrary",)),
        name="mixer_out",
    )(att, ret, *([proj] * 8), x2, w_attn_out, w_ret_out, w_o,
      modr, norm2_g.reshape(-1, 1, D), modr, modr, wrt, brt, tril)
    return outs


def moe_plan(counts, n_work):
    sub_per = MOE_ROWS // MOE_SUB
    seg = ((counts + MOE_SUB - 1) // MOE_SUB) * MOE_SUB
    off = jnp.cumsum(seg) - seg
    off17 = jnp.concatenate([off, off[-1:] + seg[-1:]]).astype(I32)
    nb = (counts + MOE_ROWS - 1) // MOE_ROWS
    cum = jnp.cumsum(nb)
    total = cum[-1]
    w = jnp.arange(n_work, dtype=I32)
    wc = jnp.minimum(w, total - 1)
    e_w = jnp.sum((cum[None, :] <= wc[:, None]).astype(I32), axis=1)
    blk = wc - (cum[e_w] - nb[e_w])
    start = off[e_w] + blk * MOE_ROWS
    nsub = jnp.clip(seg[e_w] // MOE_SUB - blk * sub_per, 0, sub_per)
    nsub = jnp.where(w < total, nsub, 0)
    return off17, e_w, start.astype(I32), nsub.astype(I32)


def _row_dma_loops(n_rows, make_copy):
    def run(op):
        def body(g, carry):
            for u in range(ROW_GROUP):
                for k in range(2):
                    getattr(make_copy(g, u, k), op)()
            return carry
        lax.fori_loop(0, n_rows // ROW_GROUP, body, 0)
    run("start")
    run("wait")


def _dispatch_kernel(pos_ref, off_ref, h_ref, xs_ref, rows_ref, zero_ref, sem, *, td, T):
    i = pl.program_id(0)

    @pl.when(i == 0)
    def _():
        zero_ref[...] = jnp.zeros_like(zero_ref)
        used = off_ref[N_EXPERTS]

        def zero_copy(row):
            row = pl.multiple_of(row, MOE_SUB)
            return pltpu.make_async_copy(zero_ref, xs_ref.at[pl.ds(row, MOE_SUB), :], sem)

        def fill(op):
            def seg_tail(e, carry):
                @pl.when(off_ref[e + 1] > off_ref[e])
                def _():
                    getattr(zero_copy(off_ref[e + 1] - MOE_SUB), op)()
                return carry

            def buf_tail(n, carry):
                getattr(zero_copy(used + n * MOE_SUB), op)()
                return carry
            lax.fori_loop(0, N_EXPERTS, seg_tail, 0)
            lax.fori_loop(0, (xs_ref.shape[0] - used) // MOE_SUB, buf_tail, 0)
        fill("start")
        fill("wait")

    rows_ref[...] = h_ref[...].astype(F32).reshape(rows_ref.shape)

    def copy(g, u, k):
        p = pos_ref[k * T + i * td + g * ROW_GROUP + u]
        return pltpu.make_async_copy(rows_ref.at[g, pl.ds(u, 1), :], xs_ref.at[pl.ds(p, 1), :], sem)
    _row_dma_loops(td, copy)


def dispatch(h2, pos_flat, off17, n_rows):
    T, D = h2.shape
    td = MOE_SUB
    return pl.pallas_call(
        functools.partial(_dispatch_kernel, td=td, T=T),
        out_shape=jax.ShapeDtypeStruct((n_rows, D), F32),
        grid_spec=pltpu.PrefetchScalarGridSpec(
            num_scalar_prefetch=2, grid=(T // td,),
            in_specs=[pl.BlockSpec((td, D), lambda i, *_: (i, 0))],
            out_specs=pl.BlockSpec(memory_space=pl.ANY),
            scratch_shapes=[pltpu.VMEM((td // ROW_GROUP, ROW_GROUP, D), F32), pltpu.VMEM((MOE_SUB, D), F32),
                            pltpu.SemaphoreType.DMA]),
        compiler_params=_cparams(("arbitrary",)),
        name="moe_dispatch",
    )(pos_flat, off17, h2)


def _moe_kernel(we_ref, ws_ref, wn_ref, xs_ref, wg_ref, wu_ref, wd_ref, ys_ref,
                xb, acc, wgb, wub, wdb, stage, ostage, sem_in, sem_out, *, nfc, n_work):
    w = pl.program_id(0)
    c = pl.program_id(1)
    n = wn_ref[w]
    s0 = ws_ref[w]

    def in_copy(start_row, i, slot):
        g0 = pl.multiple_of(start_row + i * MOE_SUB, MOE_SUB)
        return pltpu.make_async_copy(xs_ref.at[pl.ds(g0, MOE_SUB), :], stage.at[slot], sem_in.at[slot])

    def out_copy(i, slot):
        g0 = pl.multiple_of(s0 + i * MOE_SUB, MOE_SUB)
        return pltpu.make_async_copy(ostage.at[slot], ys_ref.at[pl.ds(g0, MOE_SUB), :], sem_out.at[slot])

    def request_first_two(start_row, count):
        @pl.when(count > 0)
        def _():
            in_copy(start_row, 0, 0).start()

        @pl.when(count > 1)
        def _():
            in_copy(start_row, 1, 1).start()

    @pl.when((w == 0) & (c == 0))
    def _():
        request_first_two(s0, n)

    def for_subs(fn):
        @pl.when(n >= 2)
        def _():
            fn(0, 2, True)

        @pl.when(n == 1)
        def _():
            fn(0, 1, True)

        def body(p, carry):
            fn(p * 2, 2, False)
            return carry
        lax.fori_loop(1, n // 2, body, 0)

        @pl.when((lax.rem(n, 2) == 1) & (n >= 3))
        def _():
            fn(n - 1, 1, False)

    def rows_of(i):
        return pl.ds(pl.multiple_of(i * MOE_SUB, MOE_SUB), MOE_SUB)

    def ffn(i0, count, cast_weights):
        xs_ = [xb[rows_of(i0 + a), :] for a in range(count)]
        gs, us = [], []
        for a, x in enumerate(xs_):
            if cast_weights and a == 0:
                wgb[...] = wg_ref[...].astype(BF16)
            gs.append(jnp.dot(x, wgb[...], preferred_element_type=F32))
            if cast_weights and a == 0:
                wub[...] = wu_ref[...].astype(BF16)
            us.append(jnp.dot(x, wub[...], preferred_element_type=F32))
        acts = [(g * _sigmoid(g) * u).astype(BF16) for g, u in zip(gs, us)]
        if cast_weights:
            wdb[...] = wd_ref[...].astype(BF16)
        return [jnp.dot(a, wdb[...], preferred_element_type=F32) for a in acts]

    @pl.when(n > 0)
    def _():
        @pl.when(c == 0)
        def _():
            def first(i0, count, cast_weights):
                for a in range(count):
                    in_copy(s0, i0 + a, a).wait()
                    xb[rows_of(i0 + a), :] = stage[a].astype(BF16)
                for a in range(count):
                    @pl.when(i0 + a + 2 < n)
                    def _():
                        in_copy(s0, i0 + a + 2, a).start()
                for a, y in enumerate(ffn(i0, count, cast_weights)):
                    acc[rows_of(i0 + a), :] = y
            for_subs(first)

        @pl.when((c > 0) & (c < nfc - 1))
        def _():
            def middle(i0, count, cast_weights):
                for a, y in enumerate(ffn(i0, count, cast_weights)):
                    acc[rows_of(i0 + a), :] += y
            for_subs(middle)

        @pl.when(c == nfc - 1)
        def _():
            def last(i0, count, cast_weights):
                for a in range(count):
                    @pl.when(i0 + a >= 2)
                    def _():
                        out_copy(i0 + a - 2, a).wait()
                for a, y in enumerate(ffn(i0, count, cast_weights)):
                    ostage[a] = acc[rows_of(i0 + a), :] + y
                    out_copy(i0 + a, a).start()
            for_subs(last)

            @pl.when(n >= 2)
            def _():
                out_copy(n - 2, lax.rem(n, 2)).wait()
            out_copy(n - 1, lax.rem(n - 1, 2)).wait()

            @pl.when(w + 1 < n_work)
            def _():
                nxt = jnp.minimum(w + 1, n_work - 1)
                request_first_two(ws_ref[nxt], wn_ref[nxt])


def moe_experts(xs, w_gate, w_up, w_down, l, e_w, start_w, nsub_w):
    n_rows, D = xs.shape
    nfc = D_FF // MOE_FC
    assert nfc >= 2
    n_work = e_w.shape[0]

    def chunk(c, wn, w):
        return jnp.where(wn[w] > 0, c, nfc - 1)

    return pl.pallas_call(
        functools.partial(_moe_kernel, nfc=nfc, n_work=n_work),
        out_shape=jax.ShapeDtypeStruct((n_rows, D), F32),
        grid_spec=pltpu.PrefetchScalarGridSpec(
            num_scalar_prefetch=3, grid=(n_work, nfc),
            in_specs=[pl.BlockSpec(memory_space=pl.ANY),
                      pl.BlockSpec((None, None, D, MOE_FC), lambda w, c, we, ws, wn: (l, we[w], 0, chunk(c, wn, w))),
                      pl.BlockSpec((None, None, D, MOE_FC), lambda w, c, we, ws, wn: (l, we[w], 0, chunk(c, wn, w))),
                      pl.BlockSpec((None, None, MOE_FC, D), lambda w, c, we, ws, wn: (l, we[w], chunk(c, wn, w), 0))],
            out_specs=pl.BlockSpec(memory_space=pl.ANY),
            scratch_shapes=[pltpu.VMEM((MOE_ROWS, D), BF16), pltpu.VMEM((MOE_ROWS, D), F32),
                            pltpu.VMEM((D, MOE_FC), BF16), pltpu.VMEM((D, MOE_FC), BF16),
                            pltpu.VMEM((MOE_FC, D), BF16),
                            pltpu.VMEM((2, MOE_SUB, D), F32), pltpu.VMEM((2, MOE_SUB, D), F32),
                            pltpu.SemaphoreType.DMA((2,)), pltpu.SemaphoreType.DMA((2,))]),
        input_output_aliases={3: 0},
        compiler_params=_cparams(("arbitrary", "arbitrary")),
        name="moe_experts",
    )(e_w, start_w, nsub_w, xs, w_gate, w_up, w_down)


def _combine_kernel(pos_ref, ys_ref, x_ref, rtw_ref, gate2_ref, g_ref, *rest, tc, T, final):
    if final:
        out_ref, ybuf, sem = rest
    else:
        sh_ref, sc_ref, xo_ref, h_ref, ybuf, sem = rest
    i = pl.program_id(0)
    slot = lax.rem(i, 2)
    ngroups = tc // COMBINE_GROUP

    def copy(tile, g, u, k, sl):
        p = pos_ref[k * T + tile * tc + g * ROW_GROUP + u]
        return pltpu.make_async_copy(ys_ref.at[pl.ds(p, 1), :], ybuf.at[sl, k, g, pl.ds(u, 1), :], sem.at[sl])

    def for_rows(tile, sl, r0, op):
        g0 = r0 // ROW_GROUP
        for gg in range(COMBINE_GROUP // ROW_GROUP):
            for u in range(ROW_GROUP):
                for k in range(2):
                    getattr(copy(tile, g0 + gg, u, k, sl), op)()

    def groups(fn):
        def body(g, carry):
            fn(pl.multiple_of(g * COMBINE_GROUP, COMBINE_GROUP))
            return carry
        lax.fori_loop(0, ngroups, body, 0)

    def compute(r0):
        rows = pl.ds(r0, COMBINE_GROUP)
        grp = pl.ds(r0 // ROW_GROUP, COMBINE_GROUP // ROW_GROUP)
        y = [ybuf[slot, k, grp, :, :].reshape(COMBINE_GROUP, D_MODEL) for k in range(2)]
        moe = rtw_ref[rows, 0:1] * y[0] + rtw_ref[rows, 1:2] * y[1]
        xo = x_ref[rows, :] + gate2_ref[...] * moe
        if final:
            ms = jnp.mean(xo * xo, axis=-1, keepdims=True)
            out_ref[rows, :] = xo * lax.rsqrt(ms + EPS) * g_ref[...]
        else:
            xo_ref[rows, :] = xo
            h_ref[rows, :] = _norm_mod(xo, g_ref[...], sh_ref[...], sc_ref[...]).astype(BF16)

    @pl.when(i == 0)
    def _():
        groups(lambda r0: for_rows(0, 0, r0, "start"))
    groups(lambda r0: for_rows(i, slot, r0, "wait"))

    @pl.when(i + 1 < pl.num_programs(0))
    def _():
        def both(r0):
            compute(r0)
            for_rows(i + 1, 1 - slot, r0, "start")
        groups(both)

    @pl.when(i + 1 >= pl.num_programs(0))
    def _():
        groups(compute)


def combine(ys, xn, rtw, pos_flat, modr, l, S, g_next, final):
    T, D = xn.shape
    tc = 256
    assert S % tc == 0
    row = lambda: pl.BlockSpec((tc, D), lambda i, *_: (i, 0))
    in_specs = [pl.BlockSpec(memory_space=pl.ANY), row(),
                pl.BlockSpec((tc, LANES), lambda i, *_: (i, 0)),
                _mod_spec(l, 5, S, tc)]
    args = [ys, xn, rtw, modr]
    if final:
        in_specs.append(pl.BlockSpec((1, D), lambda i, *_: (0, 0)))
        args.append(g_next.reshape(1, D))
        out_shape = jax.ShapeDtypeStruct((T, D), F32)
        out_specs = row()
    else:
        in_specs += [pl.BlockSpec((None, 1, D), lambda i, *_: (l + 1, 0, 0)),
                     _mod_spec(l + 1, 0, S, tc), _mod_spec(l + 1, 1, S, tc)]
        args += [g_next.reshape(-1, 1, D), modr, modr]
        out_shape = (jax.ShapeDtypeStruct((T, D), F32), jax.ShapeDtypeStruct((T, D), BF16))
        out_specs = (row(), row())
    return pl.pallas_call(
        functools.partial(_combine_kernel, tc=tc, T=T, final=final),
        out_shape=out_shape,
        grid_spec=pltpu.PrefetchScalarGridSpec(
            num_scalar_prefetch=1, grid=(T // tc,),
            in_specs=in_specs, out_specs=out_specs,
            scratch_shapes=[pltpu.VMEM((2, 2, tc // ROW_GROUP, ROW_GROUP, D), F32), pltpu.SemaphoreType.DMA((2,))]),
        compiler_params=_cparams(("arbitrary",)),
        name="moe_combine_final" if final else "moe_combine",
    )(pos_flat, *args)


def kernel(x, c, positions, w_ada, b_ada, norm1_g, norm2_g, w_in, attn_sinks, w_attn_out, ret_norm_g,
           w_ret_out, w_o, w_router, b_router, w_gate, w_up, w_down, final_g):
    B, S, D = x.shape
    L = w_ada.shape[0]
    T = B * S
    assert D == D_MODEL and w_in.shape[-1] == IN_W and S % RET_CHUNK == 0

    tab = rope_tables(positions)
    mod = adaln_mod(c, w_ada, b_ada)
    modr = mod[:, :B].reshape(L, B, 6, D).transpose(0, 2, 1, 3).reshape(L, 6, B, 1, D)

    x2 = x.reshape(T, D)
    h = norm_modulate(x2, norm1_g.reshape(L, 1, D), modr, 0, S)
    n_work = N_EXPERTS + (2 * T + MOE_ROWS - 1) // MOE_ROWS
    n_rows = 2 * T + N_EXPERTS * MOE_SUB
    out = None
    for l in range(L):
        proj = in_projection(h, w_in, l, tab)
        att = swa_attention(proj, attn_sinks[l], B, S)
        ret = retention(proj, ret_norm_g, l, B, S)
        xn, h2, rti, rtw, cnt = mixer_out(att, ret, proj, x2, w_attn_out, w_ret_out, w_o, modr,
                                          norm2_g, w_router, b_router, l, S)
        off17, e_w, start_w, nsub_w = moe_plan(cnt[0, :N_EXPERTS], n_work)
        seg_off = jnp.sum(jnp.where(rti[:, 0:2, None] == jnp.arange(N_EXPERTS, dtype=I32),
                                    off17[:N_EXPERTS], 0), axis=-1)
        pos_flat = (seg_off + rti[:, 2:4]).T.reshape(2 * T)
        xs = dispatch(h2, pos_flat, off17, n_rows)
        ys = moe_experts(xs, w_gate, w_up, w_down, l, e_w, start_w, nsub_w)
        if l + 1 < L:
            x2, h = combine(ys, xn, rtw, pos_flat, modr, l, S, norm1_g, final=False)
        else:
            out = combine(ys, xn, rtw, pos_flat, modr, l, S, final_g, final=True)
    return out.reshape(B, S, D)
```

```python
import functools

import numpy as np
import jax
import jax.numpy as jnp
from jax import lax
from jax.experimental import pallas as pl
from jax.experimental.pallas import tpu as pltpu

F32 = jnp.float32
BF16 = jnp.bfloat16
I32 = jnp.int32

D_MODEL = 2048
ATT_HEAD_DIM = 64
ATT_Q_HEADS = 16
ATT_KV_HEADS = 4
WINDOW = 128
RET_HEADS = 8
RET_HEAD_DIM = 128
RET_CHUNK = 256
ROPE_THETA = 10000.0
N_GROUPS = 4
EXPERTS_PER_GROUP = 4
N_EXPERTS = 16
D_FF = 1024
EPS = 1e-6

ATT_Q_W = ATT_Q_HEADS * ATT_HEAD_DIM
ATT_KV_W = ATT_KV_HEADS * ATT_HEAD_DIM
RET_W = RET_HEADS * RET_HEAD_DIM
OFF_QA = 0
OFF_KA = OFF_QA + ATT_Q_W
OFF_VA = OFF_KA + ATT_KV_W
OFF_QR = OFF_VA + ATT_KV_W
OFF_KR = OFF_QR + RET_W
OFF_VR = OFF_KR + RET_W
OFF_GR = OFF_VR + RET_W
OFF_GA = OFF_GR + RET_W
OFF_GB = OFF_GA + D_MODEL
IN_W = OFF_GB + D_MODEL

LANES = 128
PROJ_TN = 512
ROW_GROUP = 8
ATTN_LOOKAHEAD = 2
COMBINE_GROUP = 64
RET_HEADS_PER_STEP = 2
MOE_SUB = 256
MOE_ROWS = 2048
MOE_FC = 256
MOE_WSPLIT = 4
VMEM_LIMIT = 56 * 1024 * 1024


def _pick(n, cands):
    for c in cands:
        if n % c == 0:
            return c
    raise ValueError(f"no tile in {cands} divides {n}")


def _sigmoid(x):
    return 1.0 / (1.0 + jnp.exp(-x))


def _cparams(sem, vmem=VMEM_LIMIT):
    return pltpu.CompilerParams(dimension_semantics=sem, vmem_limit_bytes=vmem)


def _rope_consts():
    def inv_freq(head_dim):
        half = head_dim // 2
        inv = ROPE_THETA ** (-2.0 * jnp.arange(half, dtype=F32) / head_dim)
        return jnp.broadcast_to(jnp.tile(inv, LANES // half), (8, LANES))
    lane = np.arange(LANES)
    sgn_att = np.where((lane % ATT_HEAD_DIM) < ATT_HEAD_DIM // 2, -1.0, 1.0)
    sgn_ret = np.where(lane < RET_HEAD_DIM // 2, -1.0, 1.0)
    sgn = np.stack([np.tile(sgn_att, (8, 1)), np.tile(sgn_ret, (8, 1))]).astype(np.float32)
    return jnp.stack([inv_freq(ATT_HEAD_DIM), inv_freq(RET_HEAD_DIM)]), jnp.asarray(sgn)


def _rope_kernel(pos_ref, inv_ref, sgn_ref, tab_ref):
    pos = pos_ref[...]
    for kind in range(2):
        ang = pos * inv_ref[kind, 0:1, :]
        tab_ref[kind, :, 0:LANES] = jnp.cos(ang)
        tab_ref[kind, :, LANES:2 * LANES] = jnp.sin(ang) * sgn_ref[kind, 0:1, :]


def rope_tables(positions):
    T = positions.size
    posb = jnp.broadcast_to(positions.reshape(T, 1).astype(F32), (T, LANES))
    inv, sgn = _rope_consts()
    tm = _pick(T, (1024, 512, 256, 128))
    return pl.pallas_call(
        _rope_kernel,
        out_shape=jax.ShapeDtypeStruct((2, T, 2 * LANES), F32),
        grid=(T // tm,),
        in_specs=[pl.BlockSpec((tm, LANES), lambda i: (i, 0)),
                  pl.BlockSpec((2, 8, LANES), lambda i: (0, 0, 0)),
                  pl.BlockSpec((2, 8, LANES), lambda i: (0, 0, 0))],
        out_specs=pl.BlockSpec((2, tm, 2 * LANES), lambda i: (0, i, 0)),
        compiler_params=_cparams(("arbitrary",)),
        name="rope_tables",
    )(posb, inv, sgn)


def _adaln_kernel(c_ref, w_ref, b_ref, o_ref):
    c = c_ref[...]
    ca = c * _sigmoid(c)
    hi = ca.astype(BF16).astype(F32)
    lhs = jnp.concatenate([hi, ca - hi], axis=0).astype(BF16)
    r = jnp.dot(lhs, w_ref[...].astype(BF16), preferred_element_type=F32)
    o_ref[...] = r[0:8] + r[8:16] + b_ref[...]


def adaln_mod(c, w_ada, b_ada):
    L, D, N = w_ada.shape
    B = c.shape[0]
    cp = jnp.pad(c, ((0, 8 - B), (0, 0)))
    tn = _pick(N, (1024, 512))
    return pl.pallas_call(
        _adaln_kernel,
        out_shape=jax.ShapeDtypeStruct((L, 8, N), F32),
        grid=(L, N // tn),
        in_specs=[pl.BlockSpec((8, D), lambda l, j: (0, 0)),
                  pl.BlockSpec((None, D, tn), lambda l, j: (l, 0, j)),
                  pl.BlockSpec((None, 1, tn), lambda l, j: (l, 0, j))],
        out_specs=pl.BlockSpec((None, 8, tn), lambda l, j: (l, 0, j)),
        compiler_params=_cparams(("arbitrary", "arbitrary")),
        name="adaln_mod",
    )(cp, w_ada, b_ada.reshape(L, 1, N))


def _norm_mod(x, g, shift, scale):
    ms = jnp.mean(x * x, axis=-1, keepdims=True)
    return (x * lax.rsqrt(ms + EPS) * g) * (1.0 + scale) + shift


def _norm_mod_kernel(x_ref, g_ref, sh_ref, sc_ref, h_ref):
    h_ref[...] = _norm_mod(x_ref[...], g_ref[...], sh_ref[...], sc_ref[...]).astype(BF16)


def _mod_spec(l, k, S, tm):
    return pl.BlockSpec((None, None, None, 1, D_MODEL), lambda i, *_: (l, k, (i * tm) // S, 0, 0))


def norm_modulate(x2, g, modr, l, S):
    T, D = x2.shape
    tm = _pick(S, (512, 256, 128))
    return pl.pallas_call(
        _norm_mod_kernel,
        out_shape=jax.ShapeDtypeStruct((T, D), BF16),
        grid=(T // tm,),
        in_specs=[pl.BlockSpec((tm, D), lambda i: (i, 0)),
                  pl.BlockSpec((None, 1, D), lambda i: (l, 0, 0)),
                  _mod_spec(l, 0, S, tm), _mod_spec(l, 1, S, tm)],
        out_specs=pl.BlockSpec((tm, D), lambda i: (i, 0)),
        compiler_params=_cparams(("arbitrary",)),
        name="norm1_modulate",
    )(x2, g, modr, modr)


def _rope_att(blk, cos, sin, scale):
    lane = lax.broadcasted_iota(I32, blk.shape, 1)
    rot = jnp.where((lane % 64) < 32, pltpu.roll(blk, 96, 1), pltpu.roll(blk, 32, 1))
    out = blk * cos + rot * sin
    return out * scale if scale != 1.0 else out


def _rope_ret(blk, cos, sin, scale):
    out = blk * cos + pltpu.roll(blk, 64, 1) * sin
    return out * scale if scale != 1.0 else out


def _inproj_kernel(h_ref, w_ref, tab_ref, o_ref, wbf_ref, acc_ref, *, ni, ntiles):
    s = pl.program_id(0)
    j = jnp.maximum(s - 1, 0) // ni
    tm = o_ref.shape[0]
    h_rows = pl.ds(pl.multiple_of(lax.rem(jnp.minimum(s, ntiles - 1), ni) * tm, tm), tm)

    @pl.when(s == 0)
    def _():
        acc_ref[1] = jnp.zeros(acc_ref.shape[1:], F32)

    @pl.when((lax.rem(s, ni) == 0) & (s < ntiles))
    def _():
        wbf_ref[...] = w_ref[...].astype(BF16)

    ngrp = PROJ_TN // LANES
    t = lambda off: off // PROJ_TN
    q_scale = float(ATT_HEAD_DIM) ** -0.5
    k_scale = float(RET_HEAD_DIM) ** -0.5

    def step(cur):
        def run(epilogue):
            cos = tab_ref[:, 0:LANES]
            sin = tab_ref[:, LANES:2 * LANES]
            for g in range(ngrp):
                blk = acc_ref[1 - cur, :, g * LANES:(g + 1) * LANES]
                o_ref[:, g * LANES:(g + 1) * LANES] = epilogue(g, blk, cos, sin).astype(BF16)
            acc_ref[cur] = jnp.dot(h_ref[h_rows, :], wbf_ref[...], preferred_element_type=F32)

        @pl.when(j < t(OFF_KA))
        def _():
            run(lambda g, b, c, s: _rope_att(b, c, s, q_scale))

        @pl.when(j == t(OFF_KA))
        def _():
            run(lambda g, b, c, s: _rope_att(b, c, s, 1.0) if g < ATT_KV_W // LANES else b)

        @pl.when((j >= t(OFF_QR)) & (j < t(OFF_KR)))
        def _():
            run(lambda g, b, c, s: _rope_ret(b, c, s, 1.0))

        @pl.when((j >= t(OFF_KR)) & (j < t(OFF_VR)))
        def _():
            run(lambda g, b, c, s: _rope_ret(b, c, s, k_scale))

        @pl.when((j >= t(OFF_VR)) & (j < t(OFF_GR)))
        def _():
            run(lambda g, b, c, s: b)

        @pl.when((j >= t(OFF_GR)) & (j < t(OFF_GA)))
        def _():
            run(lambda g, b, c, s: b * _sigmoid(b))

        @pl.when(j >= t(OFF_GA))
        def _():
            run(lambda g, b, c, s: _sigmoid(b))

    for parity in range(2):
        pl.when(lax.rem(s, 2) == parity)(functools.partial(step, parity))


def in_projection(h, w_in, l, tab):
    T, D = h.shape
    tm = _pick(T, (1024, 768, 512, 256))
    nj, ni = IN_W // PROJ_TN, T // tm
    ntiles = nj * ni
    t_qr, t_vr = OFF_QR // PROJ_TN, OFF_VR // PROJ_TN
    mm = lambda s: jnp.minimum(s, ntiles - 1)
    ep = lambda s: jnp.maximum(s - 1, 0)

    def tab_map(s):
        j, i = ep(s) // ni, lax.rem(ep(s), ni)
        return (jnp.where(j < t_qr, 0, 1), jnp.where(j < t_vr, i, 0), 0)

    return pl.pallas_call(
        functools.partial(_inproj_kernel, ni=ni, ntiles=ntiles),
        out_shape=jax.ShapeDtypeStruct((nj, T, PROJ_TN), BF16),
        grid=(ntiles + 1,),
        in_specs=[pl.BlockSpec((T, D), lambda s: (0, 0), pipeline_mode=pl.Buffered(1)),
                  pl.BlockSpec((None, D, PROJ_TN), lambda s: (l, 0, mm(s) // ni)),
                  pl.BlockSpec((None, tm, 2 * LANES), tab_map)],
        out_specs=pl.BlockSpec((None, tm, PROJ_TN), lambda s: (ep(s) // ni, lax.rem(ep(s), ni), 0)),
        scratch_shapes=[pltpu.VMEM((D, PROJ_TN), BF16), pltpu.VMEM((2, tm, PROJ_TN), F32)],
        compiler_params=_cparams(("arbitrary",)),
        name="in_projection",
    )(h, w_in, tab)


def _attn_kernel(sinks_ref, q0_ref, q1_ref, kvc_ref, kvp_ref, o_ref, *, tq):
    W = WINDOW
    nsub = tq // W
    is_first = pl.program_id(1) == 0
    qi = lax.broadcasted_iota(I32, (W, 2 * W), 0)
    kj = lax.broadcasted_iota(I32, (W, 2 * W), 1)
    rel = qi + W - kj
    band = (rel >= 0) & (rel < WINDOW)
    band0 = band & (jnp.logical_not(is_first) | (kj >= W))
    lane = lax.broadcasted_iota(I32, (tq + W, LANES), 1)
    lo = lane < ATT_HEAD_DIM

    def split_pair(raw, parity):
        x = raw.astype(F32)
        xr = pltpu.roll(x, ATT_HEAD_DIM, 1)
        if parity == 0:
            return jnp.where(lo, x, 0.0), jnp.where(lo, 0.0, xr)
        return jnp.where(lo, xr, 0.0), jnp.where(lo, 0.0, x)

    units = []
    for hk in range(ATT_KV_HEADS):
        grp, par = hk // 2, hk % 2
        kc = slice(grp * LANES, (grp + 1) * LANES)
        vc = slice(ATT_KV_W + grp * LANES, ATT_KV_W + (grp + 1) * LANES)
        k_pair = split_pair(jnp.concatenate([kvp_ref[:, kc], kvc_ref[:, kc]], axis=0), par)
        v_pair = split_pair(jnp.concatenate([kvp_ref[:, vc], kvc_ref[:, vc]], axis=0), par)
        kt_pair = [k.T.astype(BF16) for k in k_pair]
        v_pair = [v.astype(BF16) for v in v_pair]
        qcols = [slice(hk * 4 * ATT_HEAD_DIM + jq * LANES, hk * 4 * ATT_HEAD_DIM + (jq + 1) * LANES)
                 for jq in range(2)]
        for m in range(nsub):
            units.append((hk, m, kt_pair, v_pair, qcols))

    def scores(unit):
        hk, m, kt_pair, v_pair, qcols = unit
        rows = slice(m * W, (m + 1) * W)
        keys = slice(m * W, m * W + 2 * W)
        kcat = jnp.concatenate([kt_pair[0][:, keys], kt_pair[1][:, keys]], axis=1)
        q_ref = (q0_ref, q1_ref)[hk // 2]
        c0 = (hk % 2) * 4 * ATT_HEAD_DIM
        q = jnp.concatenate([q_ref[rows, c0:c0 + LANES], q_ref[rows, c0 + LANES:c0 + 2 * LANES]], axis=0)
        return jnp.dot(q, kcat, preferred_element_type=F32)

    def finish(unit, s_all):
        hk, m, kt_pair, v_pair, qcols = unit
        rows = slice(m * W, (m + 1) * W)
        keys = slice(m * W, m * W + 2 * W)
        mask = band0 if m == 0 else band
        vcat = jnp.concatenate([v_pair[0][keys], v_pair[1][keys]], axis=0)
        p_rows = []
        for jq in range(2):
            p_cols = []
            for e in range(2):
                s = s_all[jq * W:(jq + 1) * W, e * 2 * W:(e + 1) * 2 * W]
                s = jnp.where(mask, s, -1e30)
                sink = sinks_ref[hk * 4 + jq * 2 + e]
                mx = jnp.maximum(jnp.max(s, axis=1, keepdims=True), sink)
                p = jnp.exp(s - mx)
                den = jnp.sum(p, axis=1, keepdims=True) + jnp.exp(sink - mx)
                p_cols.append((p * (1.0 / den)).astype(BF16))
            p_rows.append(jnp.concatenate(p_cols, axis=1))
        o = jnp.dot(jnp.concatenate(p_rows, axis=0), vcat, preferred_element_type=F32)
        o_ref[rows, qcols[0]] = o[0:W].astype(BF16)
        o_ref[rows, qcols[1]] = o[W:2 * W].astype(BF16)

    pending = [scores(u) for u in units[:ATTN_LOOKAHEAD]]
    for idx, unit in enumerate(units):
        if idx + ATTN_LOOKAHEAD < len(units):
            pending.append(scores(units[idx + ATTN_LOOKAHEAD]))
        finish(unit, pending.pop(0))


def swa_attention(proj, sinks, B, S):
    T = proj.shape[1]
    tq = _pick(S, (512, 384, 256, 128))
    nq = S // tq
    assert ATT_Q_W == 2 * PROJ_TN and 2 * ATT_KV_W == PROJ_TN
    kv_tile = OFF_KA // PROJ_TN
    tile = lambda j: pl.BlockSpec((None, tq, PROJ_TN), lambda b, i: (j, b * nq + i, 0))

    def prev_map(b, i):
        return (kv_tile, jnp.maximum(b * (S // WINDOW) + i * (tq // WINDOW) - 1, 0), 0)

    return pl.pallas_call(
        functools.partial(_attn_kernel, tq=tq),
        out_shape=jax.ShapeDtypeStruct((T, ATT_Q_W), BF16),
        grid=(B, nq),
        in_specs=[pl.BlockSpec(memory_space=pltpu.SMEM),
                  tile(0), tile(1), tile(kv_tile),
                  pl.BlockSpec((None, WINDOW, PROJ_TN), prev_map)],
        out_specs=pl.BlockSpec((tq, ATT_Q_W), lambda b, i: (b * nq + i, 0)),
        compiler_params=_cparams(("arbitrary", "arbitrary")),
        name="swa_attention",
    )(sinks, proj, proj, proj, proj)


def _ret_consts():
    C = RET_CHUNK
    log_g = jnp.log1p(-jnp.exp2(-5.0 - jnp.arange(RET_HEADS, dtype=F32)))
    i = jnp.arange(C, dtype=F32)
    diff = i[:, None] - i[None, :]
    dm = jnp.where(diff[None] >= 0, jnp.exp(jnp.maximum(diff, 0.0)[None] * log_g[:, None, None]), 0.0)
    qd = jnp.exp((i + 1.0)[None, :] * log_g[:, None])
    kd = jnp.exp((C - 1.0 - i)[None, :] * log_g[:, None])
    cd = jnp.exp(C * log_g)
    bc = lambda v: jnp.broadcast_to(v[:, :, None], (RET_HEADS, C, LANES))
    return dm, bc(qd), bc(kd), jnp.broadcast_to(cd[:, None, None], (RET_HEADS, 8, LANES))


def _ret_kernel(q_ref, k_ref, v_ref, g_ref, gn_ref, dm_ref, qd_ref, kd_ref, cd_ref, o_ref, *, nchunk):
    C = RET_CHUNK
    heads = range(RET_HEADS_PER_STEP)
    lanes = [slice(hh * LANES, (hh + 1) * LANES) for hh in heads]
    rows = [slice(c * C, (c + 1) * C) for c in range(nchunk)]

    def independent(c, hh):
        q, k, v = q_ref[rows[c], lanes[hh]], k_ref[rows[c], lanes[hh]], v_ref[rows[c], lanes[hh]]
        s = lax.dot_general(q, k, (((1,), (1,)), ((), ())), preferred_element_type=F32) * dm_ref[hh]
        vk = (v.astype(F32) * kd_ref[hh]).astype(BF16)
        kv = lax.dot_general(k, vk, (((0,), (0,)), ((), ())), preferred_element_type=F32)
        return s.astype(BF16), kv

    def finish(c, hh, s, state):
        q, v = q_ref[rows[c], lanes[hh]], v_ref[rows[c], lanes[hh]]
        y = jnp.dot(s, v, preferred_element_type=F32)
        y = y + jnp.dot(q, state.astype(BF16), preferred_element_type=F32) * qd_ref[hh]
        ms = jnp.mean(y * y, axis=-1, keepdims=True)
        yn = y * lax.rsqrt(ms + EPS) * gn_ref[:, lanes[hh]]
        o_ref[rows[c], lanes[hh]] = (g_ref[rows[c], lanes[hh]].astype(F32) * yn).astype(BF16)

    state = [jnp.zeros((RET_HEAD_DIM, RET_HEAD_DIM), F32) for _ in heads]
    ahead = [independent(0, hh) for hh in heads]
    for c in range(nchunk):
        cur = ahead
        if c + 1 < nchunk:
            ahead = [independent(c + 1, hh) for hh in heads]
        for hh in heads:
            s, kv = cur[hh]
            finish(c, hh, s, state[hh])
            state[hh] = cd_ref[hh, 0:1, :] * state[hh] + kv


def retention(proj, ret_norm_g, l, B, S):
    T = proj.shape[1]
    dm, qd, kd, cd = _ret_consts()
    hp = RET_HEADS_PER_STEP
    wid = hp * LANES
    per_tile = PROJ_TN // wid

    def part(off):
        return pl.BlockSpec((None, S, wid), lambda b, h: (off // PROJ_TN + h // per_tile, b, lax.rem(h, per_tile)))
    hconst = lambda rows: pl.BlockSpec((hp, rows, LANES), lambda b, h: (h, 0, 0))
    return pl.pallas_call(
        functools.partial(_ret_kernel, nchunk=S // RET_CHUNK),
        out_shape=jax.ShapeDtypeStruct((T, RET_W), BF16),
        grid=(B, RET_HEADS // hp),
        in_specs=[part(OFF_QR), part(OFF_KR), part(OFF_VR), part(OFF_GR),
                  pl.BlockSpec((None, 1, wid), lambda b, h: (l, 0, h)),
                  pl.BlockSpec((hp, RET_CHUNK, RET_CHUNK), lambda b, h: (h, 0, 0)),
                  hconst(RET_CHUNK), hconst(RET_CHUNK), hconst(8)],
        out_specs=pl.BlockSpec((S, wid), lambda b, h: (b, h)),
        compiler_params=_cparams(("arbitrary", "arbitrary")),
        name="retention",
    )(proj, proj, proj, proj, ret_norm_g.reshape(-1, 1, RET_W), dm, qd, kd, cd)


def _load_cast(w_hbm, dst, stage, sem):
    rows = stage.shape[0]
    n = w_hbm.shape[0] // rows

    def body(i, carry):
        r0 = pl.multiple_of(i * rows, rows)
        cp = pltpu.make_async_copy(w_hbm.at[pl.ds(r0, rows), :], stage, sem)
        cp.start()
        cp.wait()
        dst[pl.ds(r0, rows), :] = stage[...].astype(BF16)
        return carry
    lax.fori_loop(0, n, body, 0)


def _mixer_out_kernel(att_ref, ret_ref, ga0, ga1, ga2, ga3, gb0, gb1, gb2, gb3, x_ref,
                      wa_hbm, wr_hbm, wo_hbm, gate1_ref, g2_ref, sh2_ref, sc2_ref,
                      wrt_ref, brt_ref, tril_ref,
                      xn_ref, h2_ref, rti_ref, rtw_ref, cnt_ref,
                      wa, wr, wo, wrt2, mrg, carry, hs, stage, sem, *, tm, l):
    i = pl.program_id(0)
    nt = pl.num_programs(0) - 1
    D = D_MODEL

    @pl.when(i == 0)
    def _():
        _load_cast(wa_hbm.at[l], wa, stage, sem)
        _load_cast(wr_hbm.at[l], wr, stage, sem)
        _load_cast(wo_hbm.at[l], wo, stage, sem)
        w = wrt_ref[...]
        hi = w.astype(BF16)
        wrt2[:, 0:LANES] = hi
        wrt2[:, LANES:2 * LANES] = (w - hi.astype(F32)).astype(BF16)
        carry[...] = jnp.zeros_like(carry)
        hs[...] = jnp.zeros_like(hs)

    def main_stage():
        a = att_ref[...]
        r = ret_ref[...]
        ga = (ga0, ga1, ga2, ga3)
        gb = (gb0, gb1, gb2, gb3)
        for n in range(D // PROJ_TN):
            cols = slice(n * PROJ_TN, (n + 1) * PROJ_TN)
            ya = jnp.dot(a, wa[:, cols], preferred_element_type=F32)
            yr = jnp.dot(r, wr[:, cols], preferred_element_type=F32)
            mrg[:, cols] = (ga[n][...].astype(F32) * ya + gb[n][...].astype(F32) * yr).astype(BF16)
        o = jnp.dot(mrg[...], wo[...], preferred_element_type=F32)
        xn = x_ref[...] + gate1_ref[...] * o
        xn_ref[...] = xn
        h2 = _norm_mod(xn, g2_ref[...], sh2_ref[...], sc2_ref[...])
        h2b = h2.astype(BF16)
        h2_ref[...] = h2b
        hs[0:tm, :] = h2b
        hs[tm:2 * tm, :] = (h2 - h2b.astype(F32)).astype(BF16)

    def router_select():
        r = jnp.dot(hs[...], wrt2[...], preferred_element_type=F32)
        logits = ((r[0:tm, 0:LANES] + r[tm:2 * tm, 0:LANES])
                  + (r[0:tm, LANES:2 * LANES] + r[tm:2 * tm, LANES:2 * LANES])) + brt_ref[...]
        lane = lax.broadcasted_iota(I32, (tm, LANES), 1)
        valid = lane < N_EXPERTS
        mx = jnp.max(logits, axis=-1, keepdims=True)
        p = jnp.where(valid, jnp.exp(logits - mx), 0.0)
        pos_in_grp = lane % EXPERTS_PER_GROUP
        grp_of = lane // EXPERTS_PER_GROUP

        def member(k):
            wrapped = pos_in_grp + k >= EXPERTS_PER_GROUP
            return jnp.where(wrapped, pltpu.roll(p, EXPERTS_PER_GROUP - k, 1), pltpu.roll(p, LANES - k, 1)), wrapped

        (b1, w1), (b2, w2), (b3, w3) = member(1), member(2), member(3)
        m_ab, n_ab = jnp.maximum(p, b1), jnp.minimum(p, b1)
        m_cd, n_cd = jnp.maximum(b2, b3), jnp.minimum(b2, b3)
        gscore = jnp.maximum(m_ab, m_cd) + jnp.maximum(jnp.minimum(m_ab, m_cd), jnp.maximum(n_ab, n_cd))
        gscore = jnp.where(valid, gscore, -1.0)
        gmax = jnp.max(gscore, axis=-1, keepdims=True)
        gsel = jnp.min(jnp.where(gscore == gmax, grp_of, N_GROUPS), axis=-1, keepdims=True)
        in_sel = grp_of == gsel
        beats = lambda b, w: ((b > p) | ((b == p) & w)).astype(I32)
        rank_in_grp = beats(b1, w1) + beats(b2, w2) + beats(b3, w3)
        sel0 = in_sel & (rank_in_grp == 0)
        sel1 = in_sel & (rank_in_grp == 1)
        lsum = lambda m, v: jnp.sum(jnp.where(m, v, 0.0), axis=-1, keepdims=True)
        v0, v1 = lsum(sel0, p), lsum(sel1, p)
        lanef = lane.astype(F32)
        e0, e1 = lsum(sel0, lanef), lsum(sel1, lanef)
        inv = 1.0 / (v0 + v1)
        return lane, sel0, sel1, e0, e1, v0 * inv, v1 * inv

    def router_finish(live, lane, sel0, sel1, e0, e1, w0, w1):
        lsum = lambda m, v: jnp.sum(jnp.where(m, v, 0.0), axis=-1, keepdims=True)
        onehot = jnp.where(sel0 | sel1, 1.0, 0.0)
        prefix = jnp.dot(tril_ref[...], onehot.astype(BF16), preferred_element_type=F32) + carry[0:1, :]
        r0, r1 = lsum(sel0, prefix), lsum(sel1, prefix)
        carry[...] = carry[...] + live * jnp.sum(onehot, axis=0, keepdims=True)
        cnt_ref[...] = carry[...].astype(I32)
        sel4 = lambda a0, a1, a2, a3: jnp.where(lane == 0, a0, jnp.where(lane == 1, a1, jnp.where(lane == 2, a2, a3)))
        rti_ref[...] = sel4(e0, e1, r0, r1).astype(I32)
        rtw_ref[...] = jnp.where(lane == 0, w0, jnp.where(lane == 1, w1, 0.0))

    @pl.when(i < nt)
    def _():
        picked = router_select()
        main_stage()
        router_finish(jnp.where(i > 0, 1.0, 0.0), *picked)

    @pl.when(i == nt)
    def _():
        router_finish(1.0, *router_select())


def mixer_out(att, ret, proj, x2, w_attn_out, w_ret_out, w_o, modr, norm2_g, w_router, b_router, l, S):
    T, D = x2.shape
    tm = 256
    assert S % tm == 0
    nga, ngb = OFF_GA // PROJ_TN, OFF_GB // PROJ_TN
    nt = T // tm
    cur = lambda i: jnp.minimum(i, nt - 1)
    prev = lambda i: jnp.maximum(i - 1, 0)
    gate_spec = lambda blk: pl.BlockSpec((None, tm, PROJ_TN), lambda i: (blk, cur(i), 0))
    row = lambda: pl.BlockSpec((tm, D), lambda i: (cur(i), 0))
    mod_spec = lambda k: pl.BlockSpec((None, None, None, 1, D), lambda i: (l, k, (cur(i) * tm) // S, 0, 0))
    wrt = jnp.pad(w_router, ((0, 0), (0, LANES - N_EXPERTS)))
    brt = jnp.pad(b_router.astype(F32), (0, LANES - N_EXPERTS), constant_values=-1e30).reshape(1, LANES)
    tril = jnp.asarray(np.tril(np.ones((tm, tm), np.float32), -1), BF16)
    any_spec = pl.BlockSpec(memory_space=pl.ANY)
    outs = pl.pallas_call(
        functools.partial(_mixer_out_kernel, tm=tm, l=l),
        out_shape=(jax.ShapeDtypeStruct((T, D), F32),
                   jax.ShapeDtypeStruct((T, D), BF16),
                   jax.ShapeDtypeStruct((T, LANES), I32),
                   jax.ShapeDtypeStruct((T, LANES), F32),
                   jax.ShapeDtypeStruct((8, LANES), I32)),
        grid=(nt + 1,),
        in_specs=[pl.BlockSpec((tm, ATT_Q_W), lambda i: (cur(i), 0)),
                  pl.BlockSpec((tm, RET_W), lambda i: (cur(i), 0)),
                  *[gate_spec(nga + n) for n in range(4)],
                  *[gate_spec(ngb + n) for n in range(4)],
                  row(), any_spec, any_spec, any_spec,
                  mod_spec(2),
                  pl.BlockSpec((None, 1, D), lambda i: (l, 0, 0)),
                  mod_spec(3), mod_spec(4),
                  pl.BlockSpec((D, LANES), lambda i: (0, 0)),
                  pl.BlockSpec((1, LANES), lambda i: (0, 0)),
                  pl.BlockSpec((tm, tm), lambda i: (0, 0))],
        out_specs=(row(),
                   row(),
                   pl.BlockSpec((tm, LANES), lambda i: (prev(i), 0)),
                   pl.BlockSpec((tm, LANES), lambda i: (prev(i), 0)),
                   pl.BlockSpec((8, LANES), lambda i: (0, 0))),
        scratch_shapes=[pltpu.VMEM((ATT_Q_W, D), BF16), pltpu.VMEM((RET_W, D), BF16),
                        pltpu.VMEM((D, D), BF16),
                        pltpu.VMEM((D, 2 * LANES), BF16),
                        pltpu.VMEM((tm, D), BF16), pltpu.VMEM((8, LANES), F32),
                        pltpu.VMEM((2 * tm, D), BF16),
                        pltpu.VMEM((256, D), F32), pltpu.SemaphoreType.DMA],
        compiler_params=_cparams(("arbitrary",)),
        name="mixer_out",
    )(att, ret, *([proj] * 8), x2, w_attn_out, w_ret_out, w_o,
      modr, norm2_g.reshape(-1, 1, D), modr, modr, wrt, brt, tril)
    return outs


def moe_plan(counts, n_work):
    sub_per = MOE_ROWS // MOE_SUB
    seg = ((counts + MOE_SUB - 1) // MOE_SUB) * MOE_SUB
    off = jnp.cumsum(seg) - seg
    off17 = jnp.concatenate([off, off[-1:] + seg[-1:]]).astype(I32)
    nb = (counts + MOE_ROWS - 1) // MOE_ROWS
    cum = jnp.cumsum(nb)
    total = cum[-1]
    w = jnp.arange(n_work, dtype=I32)
    wc = jnp.minimum(w, total - 1)
    e_w = jnp.sum((cum[None, :] <= wc[:, None]).astype(I32), axis=1)
    blk = wc - (cum[e_w] - nb[e_w])
    start = off[e_w] + blk * MOE_ROWS
    nsub = jnp.clip(seg[e_w] // MOE_SUB - blk * sub_per, 0, sub_per)
    nsub = jnp.where(w < total, nsub, 0)
    return off17, e_w, start.astype(I32), nsub.astype(I32)


def _row_dma_loops(n_rows, make_copy):
    def run(op):
        def body(g, carry):
            for u in range(ROW_GROUP):
                for k in range(2):
                    getattr(make_copy(g, u, k), op)()
            return carry
        lax.fori_loop(0, n_rows // ROW_GROUP, body, 0)
    run("start")
    run("wait")


def _dispatch_kernel(pos_ref, off_ref, h_ref, xs_ref, rows_ref, zero_ref, sem, *, td, T):
    i = pl.program_id(0)

    @pl.when(i == 0)
    def _():
        zero_ref[...] = jnp.zeros_like(zero_ref)
        used = off_ref[N_EXPERTS]

        def zero_copy(row):
            row = pl.multiple_of(row, MOE_SUB)
            return pltpu.make_async_copy(zero_ref, xs_ref.at[pl.ds(row, MOE_SUB), :], sem)

        def fill(op):
            def seg_tail(e, carry):
                @pl.when(off_ref[e + 1] > off_ref[e])
                def _():
                    getattr(zero_copy(off_ref[e + 1] - MOE_SUB), op)()
                return carry

            def buf_tail(n, carry):
                getattr(zero_copy(used + n * MOE_SUB), op)()
                return carry
            lax.fori_loop(0, N_EXPERTS, seg_tail, 0)
            lax.fori_loop(0, (xs_ref.shape[0] - used) // MOE_SUB, buf_tail, 0)
        fill("start")
        fill("wait")

    rows_ref[...] = h_ref[...].astype(F32).reshape(rows_ref.shape)

    def copy(g, u, k):
        p = pos_ref[k * T + i * td + g * ROW_GROUP + u]
        return pltpu.make_async_copy(rows_ref.at[g, pl.ds(u, 1), :], xs_ref.at[pl.ds(p, 1), :], sem)
    _row_dma_loops(td, copy)


def dispatch(h2, pos_flat, off17, n_rows):
    T, D = h2.shape
    td = MOE_SUB
    return pl.pallas_call(
        functools.partial(_dispatch_kernel, td=td, T=T),
        out_shape=jax.ShapeDtypeStruct((n_rows, D), F32),
        grid_spec=pltpu.PrefetchScalarGridSpec(
            num_scalar_prefetch=2, grid=(T // td,),
            in_specs=[pl.BlockSpec((td, D), lambda i, *_: (i, 0))],
            out_specs=pl.BlockSpec(memory_space=pl.ANY),
            scratch_shapes=[pltpu.VMEM((td // ROW_GROUP, ROW_GROUP, D), F32), pltpu.VMEM((MOE_SUB, D), F32),
                            pltpu.SemaphoreType.DMA]),
        compiler_params=_cparams(("arbitrary",)),
        name="moe_dispatch",
    )(pos_flat, off17, h2)


def _moe_kernel(we_ref, ws_ref, wn_ref, xs_ref, *rest, nfc, n_work):
    wg_parts, wu_parts = rest[0:MOE_WSPLIT], rest[MOE_WSPLIT:2 * MOE_WSPLIT]
    wd_ref, ys_ref, xb, acc, wgb, wub, wdb, stage, ostage, sem_in, sem_out = rest[2 * MOE_WSPLIT:]
    w = pl.program_id(0)
    c = pl.program_id(1)
    n = wn_ref[w]
    s0 = ws_ref[w]

    def in_copy(start_row, i, slot):
        g0 = pl.multiple_of(start_row + i * MOE_SUB, MOE_SUB)
        return pltpu.make_async_copy(xs_ref.at[pl.ds(g0, MOE_SUB), :], stage.at[slot], sem_in.at[slot])

    def out_copy(i, slot):
        g0 = pl.multiple_of(s0 + i * MOE_SUB, MOE_SUB)
        return pltpu.make_async_copy(ostage.at[slot], ys_ref.at[pl.ds(g0, MOE_SUB), :], sem_out.at[slot])

    def request_first_two(start_row, count):
        @pl.when(count > 0)
        def _():
            in_copy(start_row, 0, 0).start()

        @pl.when(count > 1)
        def _():
            in_copy(start_row, 1, 1).start()

    @pl.when((w == 0) & (c == 0))
    def _():
        request_first_two(s0, n)

    def for_subs(fn):
        @pl.when(n >= 2)
        def _():
            fn(0, 2, True)

        @pl.when(n == 1)
        def _():
            fn(0, 1, True)

        def body(p, carry):
            fn(p * 2, 2, False)
            return carry
        lax.fori_loop(1, n // 2, body, 0)

        @pl.when((lax.rem(n, 2) == 1) & (n >= 3))
        def _():
            fn(n - 1, 1, False)

    def rows_of(i):
        return pl.ds(pl.multiple_of(i * MOE_SUB, MOE_SUB), MOE_SUB)

    def cast_bands(parts, dst):
        band = D_MODEL // MOE_WSPLIT
        for q, part in enumerate(parts):
            dst[q * band:(q + 1) * band, :] = part[...].astype(BF16)

    def ffn(i0, count, cast_weights):
        xs_ = [xb[rows_of(i0 + a), :] for a in range(count)]
        gs, us = [], []
        for a, x in enumerate(xs_):
            if cast_weights and a == 0:
                cast_bands(wg_parts, wgb)
            gs.append(jnp.dot(x, wgb[...], preferred_element_type=F32))
            if cast_weights and a == 0:
                cast_bands(wu_parts, wub)
            us.append(jnp.dot(x, wub[...], preferred_element_type=F32))
        acts = [(g * _sigmoid(g) * u).astype(BF16) for g, u in zip(gs, us)]
        if cast_weights:
            wdb[...] = wd_ref[...].astype(BF16)
        return [jnp.dot(a, wdb[...], preferred_element_type=F32) for a in acts]

    @pl.when(n > 0)
    def _():
        @pl.when(c == 0)
        def _():
            def first(i0, count, cast_weights):
                for a in range(count):
                    in_copy(s0, i0 + a, a).wait()
                    xb[rows_of(i0 + a), :] = stage[a].astype(BF16)
                for a in range(count):
                    @pl.when(i0 + a + 2 < n)
                    def _():
                        in_copy(s0, i0 + a + 2, a).start()
                for a, y in enumerate(ffn(i0, count, cast_weights)):
                    acc[rows_of(i0 + a), :] = y
            for_subs(first)

        @pl.when((c > 0) & (c < nfc - 1))
        def _():
            def middle(i0, count, cast_weights):
                for a, y in enumerate(ffn(i0, count, cast_weights)):
                    acc[rows_of(i0 + a), :] += y
            for_subs(middle)

        @pl.when(c == nfc - 1)
        def _():
            def last(i0, count, cast_weights):
                for a in range(count):
                    @pl.when(i0 + a >= 2)
                    def _():
                        out_copy(i0 + a - 2, a).wait()
                for a, y in enumerate(ffn(i0, count, cast_weights)):
                    ostage[a] = acc[rows_of(i0 + a), :] + y
                    out_copy(i0 + a, a).start()
            for_subs(last)

            @pl.when(n >= 2)
            def _():
                out_copy(n - 2, lax.rem(n, 2)).wait()
            out_copy(n - 1, lax.rem(n - 1, 2)).wait()

            @pl.when(w + 1 < n_work)
            def _():
                nxt = jnp.minimum(w + 1, n_work - 1)
                request_first_two(ws_ref[nxt], wn_ref[nxt])


def moe_experts(xs, w_gate, w_up, w_down, l, e_w, start_w, nsub_w):
    n_rows, D = xs.shape
    nfc = D_FF // MOE_FC
    assert nfc >= 2
    n_work = e_w.shape[0]

    def chunk(c, wn, w):
        return jnp.where(wn[w] > 0, c, nfc - 1)

    def up_band(q):
        return pl.BlockSpec((None, None, D // MOE_WSPLIT, MOE_FC),
                            lambda w, c, we, ws, wn: (l, we[w], q, chunk(c, wn, w)))

    return pl.pallas_call(
        functools.partial(_moe_kernel, nfc=nfc, n_work=n_work),
        out_shape=jax.ShapeDtypeStruct((n_rows, D), F32),
        grid_spec=pltpu.PrefetchScalarGridSpec(
            num_scalar_prefetch=3, grid=(n_work, nfc),
            in_specs=[pl.BlockSpec(memory_space=pl.ANY),
                      *[up_band(q) for q in range(MOE_WSPLIT)],
                      *[up_band(q) for q in range(MOE_WSPLIT)],
                      pl.BlockSpec((None, None, MOE_FC, D), lambda w, c, we, ws, wn: (l, we[w], chunk(c, wn, w), 0))],
            out_specs=pl.BlockSpec(memory_space=pl.ANY),
            scratch_shapes=[pltpu.VMEM((MOE_ROWS, D), BF16), pltpu.VMEM((MOE_ROWS, D), F32),
                            pltpu.VMEM((D, MOE_FC), BF16), pltpu.VMEM((D, MOE_FC), BF16),
                            pltpu.VMEM((MOE_FC, D), BF16),
                            pltpu.VMEM((2, MOE_SUB, D), F32), pltpu.VMEM((2, MOE_SUB, D), F32),
                            pltpu.SemaphoreType.DMA((2,)), pltpu.SemaphoreType.DMA((2,))]),
        input_output_aliases={3: 0},
        compiler_params=_cparams(("arbitrary", "arbitrary")),
        name="moe_experts",
    )(e_w, start_w, nsub_w, xs, *([w_gate] * MOE_WSPLIT), *([w_up] * MOE_WSPLIT), w_down)


def _combine_kernel(pos_ref, ys_ref, x_ref, rtw_ref, gate2_ref, g_ref, *rest, tc, T, final):
    if final:
        out_ref, ybuf, sem = rest
    else:
        sh_ref, sc_ref, xo_ref, h_ref, ybuf, sem = rest
    i = pl.program_id(0)
    slot = lax.rem(i, 2)
    ngroups = tc // COMBINE_GROUP

    def copy(tile, g, u, k, sl):
        p = pos_ref[k * T + tile * tc + g * ROW_GROUP + u]
        return pltpu.make_async_copy(ys_ref.at[pl.ds(p, 1), :], ybuf.at[sl, k, g, pl.ds(u, 1), :], sem.at[sl])

    def for_rows(tile, sl, r0, op):
        g0 = r0 // ROW_GROUP
        for gg in range(COMBINE_GROUP // ROW_GROUP):
            for u in range(ROW_GROUP):
                for k in range(2):
                    getattr(copy(tile, g0 + gg, u, k, sl), op)()

    def groups(fn):
        def body(g, carry):
            fn(pl.multiple_of(g * COMBINE_GROUP, COMBINE_GROUP))
            return carry
        lax.fori_loop(0, ngroups, body, 0)

    def compute(r0):
        rows = pl.ds(r0, COMBINE_GROUP)
        grp = pl.ds(r0 // ROW_GROUP, COMBINE_GROUP // ROW_GROUP)
        y = [ybuf[slot, k, grp, :, :].reshape(COMBINE_GROUP, D_MODEL) for k in range(2)]
        moe = rtw_ref[rows, 0:1] * y[0] + rtw_ref[rows, 1:2] * y[1]
        xo = x_ref[rows, :] + gate2_ref[...] * moe
        if final:
            ms = jnp.mean(xo * xo, axis=-1, keepdims=True)
            out_ref[rows, :] = xo * lax.rsqrt(ms + EPS) * g_ref[...]
        else:
            xo_ref[rows, :] = xo
            h_ref[rows, :] = _norm_mod(xo, g_ref[...], sh_ref[...], sc_ref[...]).astype(BF16)

    @pl.when(i == 0)
    def _():
        groups(lambda r0: for_rows(0, 0, r0, "start"))
    groups(lambda r0: for_rows(i, slot, r0, "wait"))

    @pl.when(i + 1 < pl.num_programs(0))
    def _():
        def both(r0):
            compute(r0)
            for_rows(i + 1, 1 - slot, r0, "start")
        groups(both)

    @pl.when(i + 1 >= pl.num_programs(0))
    def _():
        groups(compute)


def combine(ys, xn, rtw, pos_flat, modr, l, S, g_next, final):
    T, D = xn.shape
    tc = 256
    assert S % tc == 0
    row = lambda: pl.BlockSpec((tc, D), lambda i, *_: (i, 0))
    in_specs = [pl.BlockSpec(memory_space=pl.ANY), row(),
                pl.BlockSpec((tc, LANES), lambda i, *_: (i, 0)),
                _mod_spec(l, 5, S, tc)]
    args = [ys, xn, rtw, modr]
    if final:
        in_specs.append(pl.BlockSpec((1, D), lambda i, *_: (0, 0)))
        args.append(g_next.reshape(1, D))
        out_shape = jax.ShapeDtypeStruct((T, D), F32)
        out_specs = row()
    else:
        in_specs += [pl.BlockSpec((None, 1, D), lambda i, *_: (l + 1, 0, 0)),
                     _mod_spec(l + 1, 0, S, tc), _mod_spec(l + 1, 1, S, tc)]
        args += [g_next.reshape(-1, 1, D), modr, modr]
        out_shape = (jax.ShapeDtypeStruct((T, D), F32), jax.ShapeDtypeStruct((T, D), BF16))
        out_specs = (row(), row())
    return pl.pallas_call(
        functools.partial(_combine_kernel, tc=tc, T=T, final=final),
        out_shape=out_shape,
        grid_spec=pltpu.PrefetchScalarGridSpec(
            num_scalar_prefetch=1, grid=(T // tc,),
            in_specs=in_specs, out_specs=out_specs,
            scratch_shapes=[pltpu.VMEM((2, 2, tc // ROW_GROUP, ROW_GROUP, D), F32), pltpu.SemaphoreType.DMA((2,))]),
        compiler_params=_cparams(("arbitrary",)),
        name="moe_combine_final" if final else "moe_combine",
    )(pos_flat, *args)


def kernel(x, c, positions, w_ada, b_ada, norm1_g, norm2_g, w_in, attn_sinks, w_attn_out, ret_norm_g,
           w_ret_out, w_o, w_router, b_router, w_gate, w_up, w_down, final_g):
    B, S, D = x.shape
    L = w_ada.shape[0]
    T = B * S
    assert D == D_MODEL and w_in.shape[-1] == IN_W and S % RET_CHUNK == 0

    tab = rope_tables(positions)
    mod = adaln_mod(c, w_ada, b_ada)
    modr = mod[:, :B].reshape(L, B, 6, D).transpose(0, 2, 1, 3).reshape(L, 6, B, 1, D)

    x2 = x.reshape(T, D)
    h = norm_modulate(x2, norm1_g.reshape(L, 1, D), modr, 0, S)
    n_work = N_EXPERTS + (2 * T + MOE_ROWS - 1) // MOE_ROWS
    n_rows = 2 * T + N_EXPERTS * MOE_SUB
    out = None
    for l in range(L):
        proj = in_projection(h, w_in, l, tab)
        att = swa_attention(proj, attn_sinks[l], B, S)
        ret = retention(proj, ret_norm_g, l, B, S)
        xn, h2, rti, rtw, cnt = mixer_out(att, ret, proj, x2, w_attn_out, w_ret_out, w_o, modr,
                                          norm2_g, w_router, b_router, l, S)
        off17, e_w, start_w, nsub_w = moe_plan(cnt[0, :N_EXPERTS], n_work)
        seg_off = jnp.sum(jnp.where(rti[:, 0:2, None] == jnp.arange(N_EXPERTS, dtype=I32),
                                    off17[:N_EXPERTS], 0), axis=-1)
        pos_flat = (seg_off + rti[:, 2:4]).T.reshape(2 * T)
        xs = dispatch(h2, pos_flat, off17, n_rows)
        ys = moe_experts(xs, w_gate, w_up, w_down, l, e_w, start_w, nsub_w)
        if l + 1 < L:
            x2, h = combine(ys, xn, rtw, pos_flat, modr, l, S, norm1_g, final=False)
        else:
            out = combine(ys, xn, rtw, pos_flat, modr, l, S, final_g, final=True)
    return out.reshape(B, S, D)
```

```python
import functools

import numpy as np
import jax
import jax.numpy as jnp
from jax import lax
from jax.experimental import pallas as pl
from jax.experimental.pallas import tpu as pltpu

F32 = jnp.float32
BF16 = jnp.bfloat16
I32 = jnp.int32

D_MODEL = 2048
ATT_HEAD_DIM = 64
ATT_Q_HEADS = 16
ATT_KV_HEADS = 4
WINDOW = 128
RET_HEADS = 8
RET_HEAD_DIM = 128
RET_CHUNK = 256
ROPE_THETA = 10000.0
N_GROUPS = 4
EXPERTS_PER_GROUP = 4
N_EXPERTS = 16
D_FF = 1024
EPS = 1e-6

ATT_Q_W = ATT_Q_HEADS * ATT_HEAD_DIM
ATT_KV_W = ATT_KV_HEADS * ATT_HEAD_DIM
RET_W = RET_HEADS * RET_HEAD_DIM
OFF_QA = 0
OFF_KA = OFF_QA + ATT_Q_W
OFF_VA = OFF_KA + ATT_KV_W
OFF_QR = OFF_VA + ATT_KV_W
OFF_KR = OFF_QR + RET_W
OFF_VR = OFF_KR + RET_W
OFF_GR = OFF_VR + RET_W
OFF_GA = OFF_GR + RET_W
OFF_GB = OFF_GA + D_MODEL
IN_W = OFF_GB + D_MODEL

LANES = 128
PROJ_TN = 512
ROW_GROUP = 8
ATTN_LOOKAHEAD = 2
COMBINE_GROUP = 64
RET_HEADS_PER_STEP = 2
MOE_SUB = 256
MOE_ROWS = 2048
MOE_FC = 256
VMEM_LIMIT = 56 * 1024 * 1024


def _pick(n, cands):
    for c in cands:
        if n % c == 0:
            return c
    raise ValueError(f"no tile in {cands} divides {n}")


def _sigmoid(x):
    return 1.0 / (1.0 + jnp.exp(-x))


def _cparams(sem, vmem=VMEM_LIMIT):
    return pltpu.CompilerParams(dimension_semantics=sem, vmem_limit_bytes=vmem)


def _rope_consts():
    def inv_freq(head_dim):
        return ROPE_THETA ** (-2.0 * jnp.arange(head_dim // 2, dtype=F32) / head_dim)
    n_att, n_ret = ATT_HEAD_DIM // 2, RET_HEAD_DIM // 2
    inv = jnp.concatenate([inv_freq(ATT_HEAD_DIM), inv_freq(RET_HEAD_DIM), jnp.zeros((LANES - n_att - n_ret,), F32)])
    lane = np.arange(LANES)
    sgn_att = np.where((lane % ATT_HEAD_DIM) < n_att, -1.0, 1.0)
    sgn_ret = np.where(lane < n_ret, -1.0, 1.0)
    sgn = np.stack([np.tile(sgn_att, (8, 1)), np.tile(sgn_ret, (8, 1))]).astype(np.float32)
    return jnp.broadcast_to(inv, (8, LANES)), jnp.asarray(sgn)


def _rope_kernel(pos_ref, inv_ref, sgn_ref, tab_ref):
    ang = pos_ref[...] * inv_ref[0:1, :]
    lane = lax.broadcasted_iota(I32, ang.shape, 1)
    q = ATT_HEAD_DIM // 2
    for col, base in ((0, jnp.cos(ang)), (LANES, jnp.sin(ang))):
        r1, r2, r3 = pltpu.roll(base, q, 1), pltpu.roll(base, 2 * q, 1), pltpu.roll(base, 3 * q, 1)
        att = jnp.where(lane < q, base, jnp.where(lane < 2 * q, r1, jnp.where(lane < 3 * q, r2, r3)))
        ret = jnp.where(lane < RET_HEAD_DIM // 2, r3, r1)
        if col:
            att, ret = att * sgn_ref[0, 0:1, :], ret * sgn_ref[1, 0:1, :]
        tab_ref[0, :, col:col + LANES] = att
        tab_ref[1, :, col:col + LANES] = ret


def rope_tables(positions):
    T = positions.size
    posb = jnp.broadcast_to(positions.reshape(T, 1).astype(F32), (T, LANES))
    inv, sgn = _rope_consts()
    tm = _pick(T, (1024, 512, 256, 128))
    return pl.pallas_call(
        _rope_kernel,
        out_shape=jax.ShapeDtypeStruct((2, T, 2 * LANES), F32),
        grid=(T // tm,),
        in_specs=[pl.BlockSpec((tm, LANES), lambda i: (i, 0)),
                  pl.BlockSpec((8, LANES), lambda i: (0, 0)),
                  pl.BlockSpec((2, 8, LANES), lambda i: (0, 0, 0))],
        out_specs=pl.BlockSpec((2, tm, 2 * LANES), lambda i: (0, i, 0)),
        compiler_params=_cparams(("arbitrary",)),
        name="rope_tables",
    )(posb, inv, sgn)


def _adaln_kernel(c_ref, w_ref, b_ref, o_ref):
    c = c_ref[...]
    ca = c * _sigmoid(c)
    hi = ca.astype(BF16).astype(F32)
    lhs = jnp.concatenate([hi, ca - hi], axis=0).astype(BF16)
    r = jnp.dot(lhs, w_ref[...].astype(BF16), preferred_element_type=F32)
    o_ref[...] = r[0:8] + r[8:16] + b_ref[...]


def adaln_mod(c, w_ada, b_ada):
    L, D, N = w_ada.shape
    B = c.shape[0]
    cp = jnp.pad(c, ((0, 8 - B), (0, 0)))
    tn = _pick(N, (1024, 512))
    return pl.pallas_call(
        _adaln_kernel,
        out_shape=jax.ShapeDtypeStruct((L, 8, N), F32),
        grid=(L, N // tn),
        in_specs=[pl.BlockSpec((8, D), lambda l, j: (0, 0)),
                  pl.BlockSpec((None, D, tn), lambda l, j: (l, 0, j)),
                  pl.BlockSpec((None, 1, tn), lambda l, j: (l, 0, j))],
        out_specs=pl.BlockSpec((None, 8, tn), lambda l, j: (l, 0, j)),
        compiler_params=_cparams(("arbitrary", "arbitrary")),
        name="adaln_mod",
    )(cp, w_ada, b_ada.reshape(L, 1, N))


def _norm_mod(x, g, shift, scale):
    ms = jnp.mean(x * x, axis=-1, keepdims=True)
    return (x * lax.rsqrt(ms + EPS) * g) * (1.0 + scale) + shift


def _norm_mod_kernel(x_ref, g_ref, sh_ref, sc_ref, h_ref):
    h_ref[...] = _norm_mod(x_ref[...], g_ref[...], sh_ref[...], sc_ref[...]).astype(BF16)


def _mod_spec(l, k, S, tm):
    return pl.BlockSpec((None, None, None, 1, D_MODEL), lambda i, *_: (l, k, (i * tm) // S, 0, 0))


def norm_modulate(x2, g, modr, l, S):
    T, D = x2.shape
    tm = _pick(S, (512, 256, 128))
    return pl.pallas_call(
        _norm_mod_kernel,
        out_shape=jax.ShapeDtypeStruct((T, D), BF16),
        grid=(T // tm,),
        in_specs=[pl.BlockSpec((tm, D), lambda i: (i, 0)),
                  pl.BlockSpec((None, 1, D), lambda i: (l, 0, 0)),
                  _mod_spec(l, 0, S, tm), _mod_spec(l, 1, S, tm)],
        out_specs=pl.BlockSpec((tm, D), lambda i: (i, 0)),
        compiler_params=_cparams(("arbitrary",)),
        name="norm1_modulate",
    )(x2, g, modr, modr)


def _rope_att(blk, cos, sin, scale):
    lane = lax.broadcasted_iota(I32, blk.shape, 1)
    rot = jnp.where((lane % 64) < 32, pltpu.roll(blk, 96, 1), pltpu.roll(blk, 32, 1))
    out = blk * cos + rot * sin
    return out * scale if scale != 1.0 else out


def _rope_ret(blk, cos, sin, scale):
    out = blk * cos + pltpu.roll(blk, 64, 1) * sin
    return out * scale if scale != 1.0 else out


def _inproj_kernel(h_hbm, w_ref, tab_ref, o_ref, wbf_ref, acc_ref, h_ref, h_sem, *, ni, ntiles):
    s = pl.program_id(0)
    j = jnp.maximum(s - 1, 0) // ni
    tm = o_ref.shape[0]
    h_rows = pl.ds(pl.multiple_of(lax.rem(jnp.minimum(s, ntiles - 1), ni) * tm, tm), tm)

    def h_copy(i):
        rows = pl.ds(pl.multiple_of(i * tm, tm), tm)
        return pltpu.make_async_copy(h_hbm.at[rows, :], h_ref.at[rows, :], h_sem.at[i])

    @pl.when(s == 0)
    def _():
        acc_ref[1] = jnp.zeros(acc_ref.shape[1:], F32)
        for i in range(ni):
            h_copy(i).start()

    @pl.when(s < ni)
    def _():
        h_copy(s).wait()

    @pl.when((lax.rem(s, ni) == 0) & (s < ntiles))
    def _():
        wbf_ref[...] = w_ref[...].astype(BF16)

    ngrp = PROJ_TN // LANES
    t = lambda off: off // PROJ_TN
    q_scale = float(ATT_HEAD_DIM) ** -0.5
    k_scale = float(RET_HEAD_DIM) ** -0.5

    def step(cur):
        def run(epilogue):
            cos = tab_ref[:, 0:LANES]
            sin = tab_ref[:, LANES:2 * LANES]
            for g in range(ngrp):
                blk = acc_ref[1 - cur, :, g * LANES:(g + 1) * LANES]
                o_ref[:, g * LANES:(g + 1) * LANES] = epilogue(g, blk, cos, sin).astype(BF16)
            acc_ref[cur] = jnp.dot(h_ref[h_rows, :], wbf_ref[...], preferred_element_type=F32)

        @pl.when(j < t(OFF_KA))
        def _():
            run(lambda g, b, c, s: _rope_att(b, c, s, q_scale))

        @pl.when(j == t(OFF_KA))
        def _():
            run(lambda g, b, c, s: _rope_att(b, c, s, 1.0) if g < ATT_KV_W // LANES else b)

        @pl.when((j >= t(OFF_QR)) & (j < t(OFF_KR)))
        def _():
            run(lambda g, b, c, s: _rope_ret(b, c, s, 1.0))

        @pl.when((j >= t(OFF_KR)) & (j < t(OFF_VR)))
        def _():
            run(lambda g, b, c, s: _rope_ret(b, c, s, k_scale))

        @pl.when((j >= t(OFF_VR)) & (j < t(OFF_GR)))
        def _():
            run(lambda g, b, c, s: b)

        @pl.when((j >= t(OFF_GR)) & (j < t(OFF_GA)))
        def _():
            run(lambda g, b, c, s: b * _sigmoid(b))

        @pl.when(j >= t(OFF_GA))
        def _():
            run(lambda g, b, c, s: _sigmoid(b))

    for parity in range(2):
        pl.when(lax.rem(s, 2) == parity)(functools.partial(step, parity))


def in_projection(h, w_in, l, tab):
    T, D = h.shape
    tm = _pick(T, (1024, 768, 512, 256))
    nj, ni = IN_W // PROJ_TN, T // tm
    ntiles = nj * ni
    t_qr, t_vr = OFF_QR // PROJ_TN, OFF_VR // PROJ_TN
    mm = lambda s: jnp.minimum(s, ntiles - 1)
    ep = lambda s: jnp.maximum(s - 1, 0)

    def tab_map(s):
        j, i = ep(s) // ni, lax.rem(ep(s), ni)
        return (jnp.where(j < t_qr, 0, 1), jnp.where(j < t_vr, i, 0), 0)

    return pl.pallas_call(
        functools.partial(_inproj_kernel, ni=ni, ntiles=ntiles),
        out_shape=jax.ShapeDtypeStruct((nj, T, PROJ_TN), BF16),
        grid=(ntiles + 1,),
        in_specs=[pl.BlockSpec(memory_space=pl.ANY),
                  pl.BlockSpec((None, D, PROJ_TN), lambda s: (l, 0, mm(s) // ni)),
                  pl.BlockSpec((None, tm, 2 * LANES), tab_map)],
        out_specs=pl.BlockSpec((None, tm, PROJ_TN), lambda s: (ep(s) // ni, lax.rem(ep(s), ni), 0)),
        scratch_shapes=[pltpu.VMEM((D, PROJ_TN), BF16), pltpu.VMEM((2, tm, PROJ_TN), F32),
                        pltpu.VMEM((T, D), BF16), pltpu.SemaphoreType.DMA((ni,))],
        compiler_params=_cparams(("arbitrary",)),
        name="in_projection",
    )(h, w_in, tab)


def _attn_kernel(sinks_ref, q0_ref, q1_ref, kvc_ref, kvp_ref, o_ref, *, tq):
    W = WINDOW
    nsub = tq // W
    is_first = pl.program_id(1) == 0
    qi = lax.broadcasted_iota(I32, (W, 2 * W), 0)
    kj = lax.broadcasted_iota(I32, (W, 2 * W), 1)
    rel = qi + W - kj
    band = (rel >= 0) & (rel < WINDOW)
    band0 = band & (jnp.logical_not(is_first) | (kj >= W))
    lane = lax.broadcasted_iota(I32, (tq + W, LANES), 1)
    lo = lane < ATT_HEAD_DIM

    def split_pair(raw, parity):
        x = raw.astype(F32)
        xr = pltpu.roll(x, ATT_HEAD_DIM, 1)
        if parity == 0:
            return jnp.where(lo, x, 0.0), jnp.where(lo, 0.0, xr)
        return jnp.where(lo, xr, 0.0), jnp.where(lo, 0.0, x)

    units = []
    for hk in range(ATT_KV_HEADS):
        grp, par = hk // 2, hk % 2
        kc = slice(grp * LANES, (grp + 1) * LANES)
        vc = slice(ATT_KV_W + grp * LANES, ATT_KV_W + (grp + 1) * LANES)
        k_pair = split_pair(jnp.concatenate([kvp_ref[:, kc], kvc_ref[:, kc]], axis=0), par)
        v_pair = split_pair(jnp.concatenate([kvp_ref[:, vc], kvc_ref[:, vc]], axis=0), par)
        kt_pair = [k.T.astype(BF16) for k in k_pair]
        v_pair = [v.astype(BF16) for v in v_pair]
        qcols = [slice(hk * 4 * ATT_HEAD_DIM + jq * LANES, hk * 4 * ATT_HEAD_DIM + (jq + 1) * LANES)
                 for jq in range(2)]
        for m in range(nsub):
            units.append((hk, m, kt_pair, v_pair, qcols))

    def scores(unit):
        hk, m, kt_pair, v_pair, qcols = unit
        rows = slice(m * W, (m + 1) * W)
        keys = slice(m * W, m * W + 2 * W)
        kcat = jnp.concatenate([kt_pair[0][:, keys], kt_pair[1][:, keys]], axis=1)
        q_ref = (q0_ref, q1_ref)[hk // 2]
        c0 = (hk % 2) * 4 * ATT_HEAD_DIM
        q = jnp.concatenate([q_ref[rows, c0:c0 + LANES], q_ref[rows, c0 + LANES:c0 + 2 * LANES]], axis=0)
        return jnp.dot(q, kcat, preferred_element_type=F32)

    def finish(unit, s_all):
        hk, m, kt_pair, v_pair, qcols = unit
        rows = slice(m * W, (m + 1) * W)
        keys = slice(m * W, m * W + 2 * W)
        mask = band0 if m == 0 else band
        vcat = jnp.concatenate([v_pair[0][keys], v_pair[1][keys]], axis=0)
        p_rows = []
        for jq in range(2):
            p_cols = []
            for e in range(2):
                s = s_all[jq * W:(jq + 1) * W, e * 2 * W:(e + 1) * 2 * W]
                s = jnp.where(mask, s, -1e30)
                sink = sinks_ref[hk * 4 + jq * 2 + e]
                mx = jnp.maximum(jnp.max(s, axis=1, keepdims=True), sink)
                p = jnp.exp(s - mx)
                den = jnp.sum(p, axis=1, keepdims=True) + jnp.exp(sink - mx)
                p_cols.append((p * (1.0 / den)).astype(BF16))
            p_rows.append(jnp.concatenate(p_cols, axis=1))
        o = jnp.dot(jnp.concatenate(p_rows, axis=0), vcat, preferred_element_type=F32)
        o_ref[rows, qcols[0]] = o[0:W].astype(BF16)
        o_ref[rows, qcols[1]] = o[W:2 * W].astype(BF16)

    pending = [scores(u) for u in units[:ATTN_LOOKAHEAD]]
    for idx, unit in enumerate(units):
        if idx + ATTN_LOOKAHEAD < len(units):
            pending.append(scores(units[idx + ATTN_LOOKAHEAD]))
        finish(unit, pending.pop(0))


def swa_attention(proj, sinks, B, S):
    T = proj.shape[1]
    tq = _pick(S, (512, 384, 256, 128))
    nq = S // tq
    assert ATT_Q_W == 2 * PROJ_TN and 2 * ATT_KV_W == PROJ_TN
    kv_tile = OFF_KA // PROJ_TN
    tile = lambda j: pl.BlockSpec((None, tq, PROJ_TN), lambda b, i: (j, b * nq + i, 0))

    def prev_map(b, i):
        return (kv_tile, jnp.maximum(b * (S // WINDOW) + i * (tq // WINDOW) - 1, 0), 0)

    return pl.pallas_call(
        functools.partial(_attn_kernel, tq=tq),
        out_shape=jax.ShapeDtypeStruct((T, ATT_Q_W), BF16),
        grid=(B, nq),
        in_specs=[pl.BlockSpec(memory_space=pltpu.SMEM),
                  tile(0), tile(1), tile(kv_tile),
                  pl.BlockSpec((None, WINDOW, PROJ_TN), prev_map)],
        out_specs=pl.BlockSpec((tq, ATT_Q_W), lambda b, i: (b * nq + i, 0)),
        compiler_params=_cparams(("arbitrary", "arbitrary")),
        name="swa_attention",
    )(sinks, proj, proj, proj, proj)


def _ret_consts():
    C = RET_CHUNK
    log_g = jnp.log1p(-jnp.exp2(-5.0 - jnp.arange(RET_HEADS, dtype=F32)))
    i = jnp.arange(C, dtype=F32)
    diff = i[:, None] - i[None, :]
    dm = jnp.where(diff[None] >= 0, jnp.exp(jnp.maximum(diff, 0.0)[None] * log_g[:, None, None]), 0.0)
    qd = jnp.exp((i + 1.0)[None, :] * log_g[:, None])
    kd = jnp.exp((C - 1.0 - i)[None, :] * log_g[:, None])
    cd = jnp.exp(C * log_g)
    bc = lambda v: jnp.broadcast_to(v[:, :, None], (RET_HEADS, C, LANES))
    return dm, bc(qd), bc(kd), jnp.broadcast_to(cd[:, None, None], (RET_HEADS, 8, LANES))


def _ret_kernel(q_ref, k_ref, v_ref, g_ref, gn_ref, dm_ref, qd_ref, kd_ref, cd_ref, o_ref, *, nchunk):
    C = RET_CHUNK
    heads = range(RET_HEADS_PER_STEP)
    lanes = [slice(hh * LANES, (hh + 1) * LANES) for hh in heads]
    rows = [slice(c * C, (c + 1) * C) for c in range(nchunk)]

    def independent(c, hh):
        q, k, v = q_ref[rows[c], lanes[hh]], k_ref[rows[c], lanes[hh]], v_ref[rows[c], lanes[hh]]
        s = lax.dot_general(q, k, (((1,), (1,)), ((), ())), preferred_element_type=F32) * dm_ref[hh]
        vk = (v.astype(F32) * kd_ref[hh]).astype(BF16)
        kv = lax.dot_general(k, vk, (((0,), (0,)), ((), ())), preferred_element_type=F32)
        return s.astype(BF16), kv

    def finish(c, hh, s, state):
        q, v = q_ref[rows[c], lanes[hh]], v_ref[rows[c], lanes[hh]]
        y = jnp.dot(s, v, preferred_element_type=F32)
        y = y + jnp.dot(q, state.astype(BF16), preferred_element_type=F32) * qd_ref[hh]
        ms = jnp.mean(y * y, axis=-1, keepdims=True)
        yn = y * lax.rsqrt(ms + EPS) * gn_ref[:, lanes[hh]]
        o_ref[rows[c], lanes[hh]] = (g_ref[rows[c], lanes[hh]].astype(F32) * yn).astype(BF16)

    state = [jnp.zeros((RET_HEAD_DIM, RET_HEAD_DIM), F32) for _ in heads]
    ahead = [independent(0, hh) for hh in heads]
    for c in range(nchunk):
        cur = ahead
        if c + 1 < nchunk:
            ahead = [independent(c + 1, hh) for hh in heads]
        for hh in heads:
            s, kv = cur[hh]
            finish(c, hh, s, state[hh])
            state[hh] = cd_ref[hh, 0:1, :] * state[hh] + kv


def retention(proj, ret_norm_g, l, B, S):
    T = proj.shape[1]
    dm, qd, kd, cd = _ret_consts()
    hp = RET_HEADS_PER_STEP
    wid = hp * LANES
    per_tile = PROJ_TN // wid

    def part(off):
        return pl.BlockSpec((None, S, wid), lambda b, h: (off // PROJ_TN + h // per_tile, b, lax.rem(h, per_tile)))
    hconst = lambda rows: pl.BlockSpec((hp, rows, LANES), lambda b, h: (h, 0, 0))
    return pl.pallas_call(
        functools.partial(_ret_kernel, nchunk=S // RET_CHUNK),
        out_shape=jax.ShapeDtypeStruct((T, RET_W), BF16),
        grid=(B, RET_HEADS // hp),
        in_specs=[part(OFF_QR), part(OFF_KR), part(OFF_VR), part(OFF_GR),
                  pl.BlockSpec((None, 1, wid), lambda b, h: (l, 0, h)),
                  pl.BlockSpec((hp, RET_CHUNK, RET_CHUNK), lambda b, h: (h, 0, 0)),
                  hconst(RET_CHUNK), hconst(RET_CHUNK), hconst(8)],
        out_specs=pl.BlockSpec((S, wid), lambda b, h: (b, h)),
        compiler_params=_cparams(("arbitrary", "arbitrary")),
        name="retention",
    )(proj, proj, proj, proj, ret_norm_g.reshape(-1, 1, RET_W), dm, qd, kd, cd)


def _load_cast(w_hbm, dst, stage, sem):
    rows = stage.shape[0]
    n = w_hbm.shape[0] // rows

    def body(i, carry):
        r0 = pl.multiple_of(i * rows, rows)
        cp = pltpu.make_async_copy(w_hbm.at[pl.ds(r0, rows), :], stage, sem)
        cp.start()
        cp.wait()
        dst[pl.ds(r0, rows), :] = stage[...].astype(BF16)
        return carry
    lax.fori_loop(0, n, body, 0)


def _mixer_out_kernel(att_ref, ret_ref, ga0, ga1, ga2, ga3, gb0, gb1, gb2, gb3, x_ref,
                      wa_hbm, wr_hbm, wo_hbm, gate1_ref, g2_ref, sh2_ref, sc2_ref,
                      wrt_ref, brt_ref, tril_ref,
                      xn_ref, h2_ref, rti_ref, rtw_ref, cnt_ref,
                      wa, wr, wo, wrt2, mrg, carry, hs, stage, sem, *, tm, l):
    i = pl.program_id(0)
    nt = pl.num_programs(0) - 1
    D = D_MODEL

    @pl.when(i == 0)
    def _():
        _load_cast(wa_hbm.at[l], wa, stage, sem)
        _load_cast(wr_hbm.at[l], wr, stage, sem)
        _load_cast(wo_hbm.at[l], wo, stage, sem)
        w = wrt_ref[...]
        hi = w.astype(BF16)
        wrt2[:, 0:LANES] = hi
        wrt2[:, LANES:2 * LANES] = (w - hi.astype(F32)).astype(BF16)
        carry[...] = jnp.zeros_like(carry)
        hs[...] = jnp.zeros_like(hs)

    def main_stage():
        a = att_ref[...]
        r = ret_ref[...]
        ga = (ga0, ga1, ga2, ga3)
        gb = (gb0, gb1, gb2, gb3)
        for n in range(D // PROJ_TN):
            cols = slice(n * PROJ_TN, (n + 1) * PROJ_TN)
            ya = jnp.dot(a, wa[:, cols], preferred_element_type=F32)
            yr = jnp.dot(r, wr[:, cols], preferred_element_type=F32)
            mrg[:, cols] = (ga[n][...].astype(F32) * ya + gb[n][...].astype(F32) * yr).astype(BF16)
        o = jnp.dot(mrg[...], wo[...], preferred_element_type=F32)
        xn = x_ref[...] + gate1_ref[...] * o
        xn_ref[...] = xn
        h2 = _norm_mod(xn, g2_ref[...], sh2_ref[...], sc2_ref[...])
        h2b = h2.astype(BF16)
        h2_ref[...] = h2b
        hs[0:tm, :] = h2b
        hs[tm:2 * tm, :] = (h2 - h2b.astype(F32)).astype(BF16)

    def router_select():
        r = jnp.dot(hs[...], wrt2[...], preferred_element_type=F32)
        logits = ((r[0:tm, 0:LANES] + r[tm:2 * tm, 0:LANES])
                  + (r[0:tm, LANES:2 * LANES] + r[tm:2 * tm, LANES:2 * LANES])) + brt_ref[...]
        lane = lax.broadcasted_iota(I32, (tm, LANES), 1)
        valid = lane < N_EXPERTS
        mx = jnp.max(logits, axis=-1, keepdims=True)
        p = jnp.where(valid, jnp.exp(logits - mx), 0.0)
        pos_in_grp = lane % EXPERTS_PER_GROUP
        grp_of = lane // EXPERTS_PER_GROUP

        def member(k):
            wrapped = pos_in_grp + k >= EXPERTS_PER_GROUP
            return jnp.where(wrapped, pltpu.roll(p, EXPERTS_PER_GROUP - k, 1), pltpu.roll(p, LANES - k, 1)), wrapped

        (b1, w1), (b2, w2), (b3, w3) = member(1), member(2), member(3)
        m_ab, n_ab = jnp.maximum(p, b1), jnp.minimum(p, b1)
        m_cd, n_cd = jnp.maximum(b2, b3), jnp.minimum(b2, b3)
        gscore = jnp.maximum(m_ab, m_cd) + jnp.maximum(jnp.minimum(m_ab, m_cd), jnp.maximum(n_ab, n_cd))
        gscore = jnp.where(valid, gscore, -1.0)
        gmax = jnp.max(gscore, axis=-1, keepdims=True)
        gsel = jnp.min(jnp.where(gscore == gmax, grp_of, N_GROUPS), axis=-1, keepdims=True)
        in_sel = grp_of == gsel
        beats = lambda b, w: ((b > p) | ((b == p) & w)).astype(I32)
        rank_in_grp = beats(b1, w1) + beats(b2, w2) + beats(b3, w3)
        sel0 = in_sel & (rank_in_grp == 0)
        sel1 = in_sel & (rank_in_grp == 1)
        lsum = lambda m, v: jnp.sum(jnp.where(m, v, 0.0), axis=-1, keepdims=True)
        v0, v1 = lsum(sel0, p), lsum(sel1, p)
        lanef = lane.astype(F32)
        e0, e1 = lsum(sel0, lanef), lsum(sel1, lanef)
        inv = 1.0 / (v0 + v1)
        return lane, sel0, sel1, e0, e1, v0 * inv, v1 * inv

    def router_finish(live, lane, sel0, sel1, e0, e1, w0, w1):
        lsum = lambda m, v: jnp.sum(jnp.where(m, v, 0.0), axis=-1, keepdims=True)
        onehot = jnp.where(sel0 | sel1, 1.0, 0.0)
        prefix = jnp.dot(tril_ref[...], onehot.astype(BF16), preferred_element_type=F32) + carry[0:1, :]
        r0, r1 = lsum(sel0, prefix), lsum(sel1, prefix)
        carry[...] = carry[...] + live * jnp.sum(onehot, axis=0, keepdims=True)
        cnt_ref[...] = carry[...].astype(I32)
        sel4 = lambda a0, a1, a2, a3: jnp.where(lane == 0, a0, jnp.where(lane == 1, a1, jnp.where(lane == 2, a2, a3)))
        rti_ref[...] = sel4(e0, e1, r0, r1).astype(I32)
        rtw_ref[...] = jnp.where(lane == 0, w0, jnp.where(lane == 1, w1, 0.0))

    @pl.when(i < nt)
    def _():
        picked = router_select()
        main_stage()
        router_finish(jnp.where(i > 0, 1.0, 0.0), *picked)

    @pl.when(i == nt)
    def _():
        router_finish(1.0, *router_select())


def mixer_out(att, ret, proj, x2, w_attn_out, w_ret_out, w_o, modr, norm2_g, w_router, b_router, l, S):
    T, D = x2.shape
    tm = 256
    assert S % tm == 0
    nga, ngb = OFF_GA // PROJ_TN, OFF_GB // PROJ_TN
    nt = T // tm
    cur = lambda i: jnp.minimum(i, nt - 1)
    prev = lambda i: jnp.maximum(i - 1, 0)
    gate_spec = lambda blk: pl.BlockSpec((None, tm, PROJ_TN), lambda i: (blk, cur(i), 0))
    row = lambda: pl.BlockSpec((tm, D), lambda i: (cur(i), 0))
    mod_spec = lambda k: pl.BlockSpec((None, None, None, 1, D), lambda i: (l, k, (cur(i) * tm) // S, 0, 0))
    wrt = jnp.pad(w_router, ((0, 0), (0, LANES - N_EXPERTS)))
    brt = jnp.pad(b_router.astype(F32), (0, LANES - N_EXPERTS), constant_values=-1e30).reshape(1, LANES)
    tril = jnp.asarray(np.tril(np.ones((tm, tm), np.float32), -1), BF16)
    any_spec = pl.BlockSpec(memory_space=pl.ANY)
    outs = pl.pallas_call(
        functools.partial(_mixer_out_kernel, tm=tm, l=l),
        out_shape=(jax.ShapeDtypeStruct((T, D), F32),
                   jax.ShapeDtypeStruct((T, D), BF16),
                   jax.ShapeDtypeStruct((T, LANES), I32),
                   jax.ShapeDtypeStruct((T, LANES), F32),
                   jax.ShapeDtypeStruct((8, LANES), I32)),
        grid=(nt + 1,),
        in_specs=[pl.BlockSpec((tm, ATT_Q_W), lambda i: (cur(i), 0)),
                  pl.BlockSpec((tm, RET_W), lambda i: (cur(i), 0)),
                  *[gate_spec(nga + n) for n in range(4)],
                  *[gate_spec(ngb + n) for n in range(4)],
                  row(), any_spec, any_spec, any_spec,
                  mod_spec(2),
                  pl.BlockSpec((None, 1, D), lambda i: (l, 0, 0)),
                  mod_spec(3), mod_spec(4),
                  pl.BlockSpec((D, LANES), lambda i: (0, 0)),
                  pl.BlockSpec((1, LANES), lambda i: (0, 0)),
                  pl.BlockSpec((tm, tm), lambda i: (0, 0))],
        out_specs=(row(),
                   row(),
                   pl.BlockSpec((tm, LANES), lambda i: (prev(i), 0)),
                   pl.BlockSpec((tm, LANES), lambda i: (prev(i), 0)),
                   pl.BlockSpec((8, LANES), lambda i: (0, 0))),
        scratch_shapes=[pltpu.VMEM((ATT_Q_W, D), BF16), pltpu.VMEM((RET_W, D), BF16),
                        pltpu.VMEM((D, D), BF16),
                        pltpu.VMEM((D, 2 * LANES), BF16),
                        pltpu.VMEM((tm, D), BF16), pltpu.VMEM((8, LANES), F32),
                        pltpu.VMEM((2 * tm, D), BF16),
                        pltpu.VMEM((256, D), F32), pltpu.SemaphoreType.DMA],
        compiler_params=_cparams(("arbitrary",)),
        name="mixer_out",
    )(att, ret, *([proj] * 8), x2, w_attn_out, w_ret_out, w_o,
      modr, norm2_g.reshape(-1, 1, D), modr, modr, wrt, brt, tril)
    return outs


def moe_plan(counts, n_work):
    sub_per = MOE_ROWS // MOE_SUB
    seg = ((counts + MOE_SUB - 1) // MOE_SUB) * MOE_SUB
    off = jnp.cumsum(seg) - seg
    off17 = jnp.concatenate([off, off[-1:] + seg[-1:]]).astype(I32)
    nb = (counts + MOE_ROWS - 1) // MOE_ROWS
    cum = jnp.cumsum(nb)
    total = cum[-1]
    w = jnp.arange(n_work, dtype=I32)
    wc = jnp.minimum(w, total - 1)
    e_w = jnp.sum((cum[None, :] <= wc[:, None]).astype(I32), axis=1)
    blk = wc - (cum[e_w] - nb[e_w])
    start = off[e_w] + blk * MOE_ROWS
    nsub = jnp.clip(seg[e_w] // MOE_SUB - blk * sub_per, 0, sub_per)
    nsub = jnp.where(w < total, nsub, 0)
    return off17, e_w, start.astype(I32), nsub.astype(I32)


def _row_dma_loops(n_rows, make_copy):
    def run(op):
        def body(g, carry):
            for u in range(ROW_GROUP):
                for k in range(2):
                    getattr(make_copy(g, u, k), op)()
            return carry
        lax.fori_loop(0, n_rows // ROW_GROUP, body, 0)
    run("start")
    run("wait")


def _dispatch_kernel(pos_ref, off_ref, h_ref, xs_ref, rows_ref, zero_ref, sem, *, td, T):
    i = pl.program_id(0)

    @pl.when(i == 0)
    def _():
        zero_ref[...] = jnp.zeros_like(zero_ref)
        used = off_ref[N_EXPERTS]

        def zero_copy(row):
            row = pl.multiple_of(row, MOE_SUB)
            return pltpu.make_async_copy(zero_ref, xs_ref.at[pl.ds(row, MOE_SUB), :], sem)

        def fill(op):
            def seg_tail(e, carry):
                @pl.when(off_ref[e + 1] > off_ref[e])
                def _():
                    getattr(zero_copy(off_ref[e + 1] - MOE_SUB), op)()
                return carry

            def buf_tail(n, carry):
                getattr(zero_copy(used + n * MOE_SUB), op)()
                return carry
            lax.fori_loop(0, N_EXPERTS, seg_tail, 0)
            lax.fori_loop(0, (xs_ref.shape[0] - used) // MOE_SUB, buf_tail, 0)
        fill("start")
        fill("wait")

    rows_ref[...] = h_ref[...].astype(F32).reshape(rows_ref.shape)

    def copy(g, u, k):
        p = pos_ref[k * T + i * td + g * ROW_GROUP + u]
        return pltpu.make_async_copy(rows_ref.at[g, pl.ds(u, 1), :], xs_ref.at[pl.ds(p, 1), :], sem)
    _row_dma_loops(td, copy)


def dispatch(h2, pos_flat, off17, n_rows):
    T, D = h2.shape
    td = MOE_SUB
    return pl.pallas_call(
        functools.partial(_dispatch_kernel, td=td, T=T),
        out_shape=jax.ShapeDtypeStruct((n_rows, D), F32),
        grid_spec=pltpu.PrefetchScalarGridSpec(
            num_scalar_prefetch=2, grid=(T // td,),
            in_specs=[pl.BlockSpec((td, D), lambda i, *_: (i, 0))],
            out_specs=pl.BlockSpec(memory_space=pl.ANY),
            scratch_shapes=[pltpu.VMEM((td // ROW_GROUP, ROW_GROUP, D), F32), pltpu.VMEM((MOE_SUB, D), F32),
                            pltpu.SemaphoreType.DMA]),
        compiler_params=_cparams(("arbitrary",)),
        name="moe_dispatch",
    )(pos_flat, off17, h2)


def _moe_kernel(we_ref, ws_ref, wn_ref, xs_ref, wg_ref, wu_ref, wd_ref, ys_ref,
                xb, acc, wgb, wub, wdb, stage, ostage, sem_in, sem_out, *, nfc, n_work):
    w = pl.program_id(0)
    c = pl.program_id(1)
    n = wn_ref[w]
    s0 = ws_ref[w]

    def in_copy(start_row, i, slot):
        g0 = pl.multiple_of(start_row + i * MOE_SUB, MOE_SUB)
        return pltpu.make_async_copy(xs_ref.at[pl.ds(g0, MOE_SUB), :], stage.at[slot], sem_in.at[slot])

    def out_copy(i, slot):
        g0 = pl.multiple_of(s0 + i * MOE_SUB, MOE_SUB)
        return pltpu.make_async_copy(ostage.at[slot], ys_ref.at[pl.ds(g0, MOE_SUB), :], sem_out.at[slot])

    def request_first_two(start_row, count):
        @pl.when(count > 0)
        def _():
            in_copy(start_row, 0, 0).start()

        @pl.when(count > 1)
        def _():
            in_copy(start_row, 1, 1).start()

    @pl.when((w == 0) & (c == 0))
    def _():
        request_first_two(s0, n)

    def for_subs(fn):
        def body(p, carry):
            fn(p * 2, 2)
            return carry
        lax.fori_loop(0, n // 2, body, 0)

        @pl.when(lax.rem(n, 2) == 1)
        def _():
            fn(n - 1, 1)

    def rows_of(i):
        return pl.ds(pl.multiple_of(i * MOE_SUB, MOE_SUB), MOE_SUB)

    def ffn(i0, count):
        xs_ = [xb[rows_of(i0 + a), :] for a in range(count)]
        gu = [(jnp.dot(x, wgb[...], preferred_element_type=F32), jnp.dot(x, wub[...], preferred_element_type=F32))
              for x in xs_]
        acts = [(g * _sigmoid(g) * u).astype(BF16) for g, u in gu]
        return [jnp.dot(a, wdb[...], preferred_element_type=F32) for a in acts]

    @pl.when(n > 0)
    def _():
        wgb[...] = wg_ref[...].astype(BF16)
        wub[...] = wu_ref[...].astype(BF16)
        wdb[...] = wd_ref[...].astype(BF16)

        @pl.when(c == 0)
        def _():
            def first(i0, count):
                for a in range(count):
                    in_copy(s0, i0 + a, a).wait()
                    xb[rows_of(i0 + a), :] = stage[a].astype(BF16)
                for a in range(count):
                    @pl.when(i0 + a + 2 < n)
                    def _():
                        in_copy(s0, i0 + a + 2, a).start()
                for a, y in enumerate(ffn(i0, count)):
                    acc[rows_of(i0 + a), :] = y
            for_subs(first)

        @pl.when((c > 0) & (c < nfc - 1))
        def _():
            def middle(i0, count):
                for a, y in enumerate(ffn(i0, count)):
                    acc[rows_of(i0 + a), :] += y
            for_subs(middle)

        @pl.when(c == nfc - 1)
        def _():
            def last(i0, count):
                for a in range(count):
                    @pl.when(i0 + a >= 2)
                    def _():
                        out_copy(i0 + a - 2, a).wait()
                for a, y in enumerate(ffn(i0, count)):
                    ostage[a] = acc[rows_of(i0 + a), :] + y
                    out_copy(i0 + a, a).start()
            for_subs(last)

            @pl.when(n >= 2)
            def _():
                out_copy(n - 2, lax.rem(n, 2)).wait()
            out_copy(n - 1, lax.rem(n - 1, 2)).wait()

            @pl.when(w + 1 < n_work)
            def _():
                nxt = jnp.minimum(w + 1, n_work - 1)
                request_first_two(ws_ref[nxt], wn_ref[nxt])


def moe_experts(xs, w_gate, w_up, w_down, l, e_w, start_w, nsub_w):
    n_rows, D = xs.shape
    nfc = D_FF // MOE_FC
    assert nfc >= 2
    n_work = e_w.shape[0]

    def chunk(c, wn, w):
        return jnp.where(wn[w] > 0, c, nfc - 1)

    return pl.pallas_call(
        functools.partial(_moe_kernel, nfc=nfc, n_work=n_work),
        out_shape=jax.ShapeDtypeStruct((n_rows, D), F32),
        grid_spec=pltpu.PrefetchScalarGridSpec(
            num_scalar_prefetch=3, grid=(n_work, nfc),
            in_specs=[pl.BlockSpec(memory_space=pl.ANY),
                      pl.BlockSpec((None, None, D, MOE_FC), lambda w, c, we, ws, wn: (l, we[w], 0, chunk(c, wn, w))),
                      pl.BlockSpec((None, None, D, MOE_FC), lambda w, c, we, ws, wn: (l, we[w], 0, chunk(c, wn, w))),
                      pl.BlockSpec((None, None, MOE_FC, D), lambda w, c, we, ws, wn: (l, we[w], chunk(c, wn, w), 0))],
            out_specs=pl.BlockSpec(memory_space=pl.ANY),
            scratch_shapes=[pltpu.VMEM((MOE_ROWS, D), BF16), pltpu.VMEM((MOE_ROWS, D), F32),
                            pltpu.VMEM((D, MOE_FC), BF16), pltpu.VMEM((D, MOE_FC), BF16),
                            pltpu.VMEM((MOE_FC, D), BF16),
                            pltpu.VMEM((2, MOE_SUB, D), F32), pltpu.VMEM((2, MOE_SUB, D), F32),
                            pltpu.SemaphoreType.DMA((2,)), pltpu.SemaphoreType.DMA((2,))]),
        input_output_aliases={3: 0},
        compiler_params=_cparams(("arbitrary", "arbitrary")),
        name="moe_experts",
    )(e_w, start_w, nsub_w, xs, w_gate, w_up, w_down)


def _combine_kernel(pos_ref, ys_ref, x_ref, rtw_ref, gate2_ref, g_ref, *rest, tc, T, final):
    if final:
        out_ref, ybuf, sem = rest
    else:
        sh_ref, sc_ref, xo_ref, h_ref, ybuf, sem = rest
    i = pl.program_id(0)
    slot = lax.rem(i, 2)
    ngroups = tc // COMBINE_GROUP

    def copy(tile, g, u, k, sl):
        p = pos_ref[k * T + tile * tc + g * ROW_GROUP + u]
        return pltpu.make_async_copy(ys_ref.at[pl.ds(p, 1), :], ybuf.at[sl, k, g, pl.ds(u, 1), :], sem.at[sl])

    def for_rows(tile, sl, r0, op):
        g0 = r0 // ROW_GROUP
        for gg in range(COMBINE_GROUP // ROW_GROUP):
            for u in range(ROW_GROUP):
                for k in range(2):
                    getattr(copy(tile, g0 + gg, u, k, sl), op)()

    def groups(fn):
        def body(g, carry):
            fn(pl.multiple_of(g * COMBINE_GROUP, COMBINE_GROUP))
            return carry
        lax.fori_loop(0, ngroups, body, 0)

    def compute(r0):
        rows = pl.ds(r0, COMBINE_GROUP)
        grp = pl.ds(r0 // ROW_GROUP, COMBINE_GROUP // ROW_GROUP)
        y = [ybuf[slot, k, grp, :, :].reshape(COMBINE_GROUP, D_MODEL) for k in range(2)]
        moe = rtw_ref[rows, 0:1] * y[0] + rtw_ref[rows, 1:2] * y[1]
        xo = x_ref[rows, :] + gate2_ref[...] * moe
        if final:
            ms = jnp.mean(xo * xo, axis=-1, keepdims=True)
            out_ref[rows, :] = xo * lax.rsqrt(ms + EPS) * g_ref[...]
        else:
            xo_ref[rows, :] = xo
            h_ref[rows, :] = _norm_mod(xo, g_ref[...], sh_ref[...], sc_ref[...]).astype(BF16)

    @pl.when(i == 0)
    def _():
        groups(lambda r0: for_rows(0, 0, r0, "start"))
    groups(lambda r0: for_rows(i, slot, r0, "wait"))

    @pl.when(i + 1 < pl.num_programs(0))
    def _():
        def both(r0):
            compute(r0)
            for_rows(i + 1, 1 - slot, r0, "start")
        groups(both)

    @pl.when(i + 1 >= pl.num_programs(0))
    def _():
        groups(compute)


def combine(ys, xn, rtw, pos_flat, modr, l, S, g_next, final):
    T, D = xn.shape
    tc = 256
    assert S % tc == 0
    row = lambda: pl.BlockSpec((tc, D), lambda i, *_: (i, 0))
    in_specs = [pl.BlockSpec(memory_space=pl.ANY), row(),
                pl.BlockSpec((tc, LANES), lambda i, *_: (i, 0)),
                _mod_spec(l, 5, S, tc)]
    args = [ys, xn, rtw, modr]
    if final:
        in_specs.append(pl.BlockSpec((1, D), lambda i, *_: (0, 0)))
        args.append(g_next.reshape(1, D))
        out_shape = jax.ShapeDtypeStruct((T, D), F32)
        out_specs = row()
    else:
        in_specs += [pl.BlockSpec((None, 1, D), lambda i, *_: (l + 1, 0, 0)),
                     _mod_spec(l + 1, 0, S, tc), _mod_spec(l + 1, 1, S, tc)]
        args += [g_next.reshape(-1, 1, D), modr, modr]
        out_shape = (jax.ShapeDtypeStruct((T, D), F32), jax.ShapeDtypeStruct((T, D), BF16))
        out_specs = (row(), row())
    return pl.pallas_call(
        functools.partial(_combine_kernel, tc=tc, T=T, final=final),
        out_shape=out_shape,
        grid_spec=pltpu.PrefetchScalarGridSpec(
            num_scalar_prefetch=1, grid=(T // tc,),
            in_specs=in_specs, out_specs=out_specs,
            scratch_shapes=[pltpu.VMEM((2, 2, tc // ROW_GROUP, ROW_GROUP, D), F32), pltpu.SemaphoreType.DMA((2,))]),
        compiler_params=_cparams(("arbitrary",)),
        name="moe_combine_final" if final else "moe_combine",
    )(pos_flat, *args)


def kernel(x, c, positions, w_ada, b_ada, norm1_g, norm2_g, w_in, attn_sinks, w_attn_out, ret_norm_g,
           w_ret_out, w_o, w_router, b_router, w_gate, w_up, w_down, final_g):
    B, S, D = x.shape
    L = w_ada.shape[0]
    T = B * S
    assert D == D_MODEL and w_in.shape[-1] == IN_W and S % RET_CHUNK == 0

    tab = rope_tables(positions)
    mod = adaln_mod(c, w_ada, b_ada)
    modr = mod[:, :B].reshape(L, B, 6, D).transpose(0, 2, 1, 3).reshape(L, 6, B, 1, D)

    x2 = x.reshape(T, D)
    h = norm_modulate(x2, norm1_g.reshape(L, 1, D), modr, 0, S)
    n_work = N_EXPERTS + (2 * T + MOE_ROWS - 1) // MOE_ROWS
    n_rows = 2 * T + N_EXPERTS * MOE_SUB
    out = None
    for l in range(L):
        proj = in_projection(h, w_in, l, tab)
        att = swa_attention(proj, attn_sinks[l], B, S)
        ret = retention(proj, ret_norm_g, l, B, S)
        xn, h2, rti, rtw, cnt = mixer_out(att, ret, proj, x2, w_attn_out, w_ret_out, w_o, modr,
                                          norm2_g, w_router, b_router, l, S)
        off17, e_w, start_w, nsub_w = moe_plan(cnt[0, :N_EXPERTS], n_work)
        seg_off = jnp.sum(jnp.where(rti[:, 0:2, None] == jnp.arange(N_EXPERTS, dtype=I32),
                                    off17[:N_EXPERTS], 0), axis=-1)
        pos_flat = (seg_off + rti[:, 2:4]).T.reshape(2 * T)
        xs = dispatch(h2, pos_flat, off17, n_rows)
        ys = moe_experts(xs, w_gate, w_up, w_down, l, e_w, start_w, nsub_w)
        if l + 1 < L:
            x2, h = combine(ys, xn, rtw, pos_flat, modr, l, S, norm1_g, final=False)
        else:
            out = combine(ys, xn, rtw, pos_flat, modr, l, S, final_g, final=True)
    return out.reshape(B, S, D)
```

```python
import functools

import numpy as np
import jax
import jax.numpy as jnp
from jax import lax
from jax.experimental import pallas as pl
from jax.experimental.pallas import tpu as pltpu

F32 = jnp.float32
BF16 = jnp.bfloat16
I32 = jnp.int32

D_MODEL = 2048
ATT_HEAD_DIM = 64
ATT_Q_HEADS = 16
ATT_KV_HEADS = 4
WINDOW = 128
RET_HEADS = 8
RET_HEAD_DIM = 128
RET_CHUNK = 256
ROPE_THETA = 10000.0
N_GROUPS = 4
EXPERTS_PER_GROUP = 4
N_EXPERTS = 16
D_FF = 1024
EPS = 1e-6

ATT_Q_W = ATT_Q_HEADS * ATT_HEAD_DIM
ATT_KV_W = ATT_KV_HEADS * ATT_HEAD_DIM
RET_W = RET_HEADS * RET_HEAD_DIM
OFF_QA = 0
OFF_KA = OFF_QA + ATT_Q_W
OFF_VA = OFF_KA + ATT_KV_W
OFF_QR = OFF_VA + ATT_KV_W
OFF_KR = OFF_QR + RET_W
OFF_VR = OFF_KR + RET_W
OFF_GR = OFF_VR + RET_W
OFF_GA = OFF_GR + RET_W
OFF_GB = OFF_GA + D_MODEL
IN_W = OFF_GB + D_MODEL

LANES = 128
PROJ_TN = 512
ROW_GROUP = 8
ATTN_LOOKAHEAD = 2
COMBINE_GROUP = 64
RET_HEADS_PER_STEP = 2
MOE_SUB = 256
MOE_ROWS = 2048
MOE_FC = 256
VMEM_LIMIT = 56 * 1024 * 1024


def _pick(n, cands):
    for c in cands:
        if n % c == 0:
            return c
    raise ValueError(f"no tile in {cands} divides {n}")


def _sigmoid(x):
    return 1.0 / (1.0 + jnp.exp(-x))


def _cparams(sem, vmem=VMEM_LIMIT):
    return pltpu.CompilerParams(dimension_semantics=sem, vmem_limit_bytes=vmem)


def _rope_consts():
    def inv_freq(head_dim):
        return ROPE_THETA ** (-2.0 * jnp.arange(head_dim // 2, dtype=F32) / head_dim)
    n_att, n_ret = ATT_HEAD_DIM // 2, RET_HEAD_DIM // 2
    inv = jnp.concatenate([inv_freq(ATT_HEAD_DIM), inv_freq(RET_HEAD_DIM), jnp.zeros((LANES - n_att - n_ret,), F32)])
    lane = np.arange(LANES)
    sgn_att = np.where((lane % ATT_HEAD_DIM) < n_att, -1.0, 1.0)
    sgn_ret = np.where(lane < n_ret, -1.0, 1.0)
    sgn = np.stack([np.tile(sgn_att, (8, 1)), np.tile(sgn_ret, (8, 1))]).astype(np.float32)
    return jnp.broadcast_to(inv, (8, LANES)), jnp.asarray(sgn)


def _rope_kernel(pos_ref, inv_ref, sgn_ref, tab_ref):
    ang = pos_ref[...] * inv_ref[0:1, :]
    lane = lax.broadcasted_iota(I32, ang.shape, 1)
    q = ATT_HEAD_DIM // 2
    for col, base in ((0, jnp.cos(ang)), (LANES, jnp.sin(ang))):
        r1, r2, r3 = pltpu.roll(base, q, 1), pltpu.roll(base, 2 * q, 1), pltpu.roll(base, 3 * q, 1)
        att = jnp.where(lane < q, base, jnp.where(lane < 2 * q, r1, jnp.where(lane < 3 * q, r2, r3)))
        ret = jnp.where(lane < RET_HEAD_DIM // 2, r3, r1)
        if col:
            att, ret = att * sgn_ref[0, 0:1, :], ret * sgn_ref[1, 0:1, :]
        tab_ref[0, :, col:col + LANES] = att
        tab_ref[1, :, col:col + LANES] = ret


def rope_tables(positions):
    T = positions.size
    posb = jnp.broadcast_to(positions.reshape(T, 1).astype(F32), (T, LANES))
    inv, sgn = _rope_consts()
    tm = _pick(T, (1024, 512, 256, 128))
    return pl.pallas_call(
        _rope_kernel,
        out_shape=jax.ShapeDtypeStruct((2, T, 2 * LANES), F32),
        grid=(T // tm,),
        in_specs=[pl.BlockSpec((tm, LANES), lambda i: (i, 0)),
                  pl.BlockSpec((8, LANES), lambda i: (0, 0)),
                  pl.BlockSpec((2, 8, LANES), lambda i: (0, 0, 0))],
        out_specs=pl.BlockSpec((2, tm, 2 * LANES), lambda i: (0, i, 0)),
        compiler_params=_cparams(("arbitrary",)),
        name="rope_tables",
    )(posb, inv, sgn)


def _adaln_kernel(c_ref, w_ref, b_ref, o_ref):
    c = c_ref[...]
    ca = c * _sigmoid(c)
    hi = ca.astype(BF16).astype(F32)
    lhs = jnp.concatenate([hi, ca - hi], axis=0).astype(BF16)
    r = jnp.dot(lhs, w_ref[...].astype(BF16), preferred_element_type=F32)
    o_ref[...] = r[0:8] + r[8:16] + b_ref[...]


def adaln_mod(c, w_ada, b_ada):
    L, D, N = w_ada.shape
    B = c.shape[0]
    cp = jnp.pad(c, ((0, 8 - B), (0, 0)))
    tn = _pick(N, (2048, 1024, 512))
    return pl.pallas_call(
        _adaln_kernel,
        out_shape=jax.ShapeDtypeStruct((L, 8, N), F32),
        grid=(L, N // tn),
        in_specs=[pl.BlockSpec((8, D), lambda l, j: (0, 0)),
                  pl.BlockSpec((None, D, tn), lambda l, j: (l, 0, j)),
                  pl.BlockSpec((None, 1, tn), lambda l, j: (l, 0, j))],
        out_specs=pl.BlockSpec((None, 8, tn), lambda l, j: (l, 0, j)),
        compiler_params=_cparams(("arbitrary", "arbitrary")),
        name="adaln_mod",
    )(cp, w_ada, b_ada.reshape(L, 1, N))


def _norm_mod(x, g, shift, scale):
    ms = jnp.mean(x * x, axis=-1, keepdims=True)
    return (x * lax.rsqrt(ms + EPS) * g) * (1.0 + scale) + shift


def _norm_mod_kernel(x_ref, g_ref, sh_ref, sc_ref, h_ref):
    h_ref[...] = _norm_mod(x_ref[...], g_ref[...], sh_ref[...], sc_ref[...]).astype(BF16)


def _mod_spec(l, k, S, tm):
    return pl.BlockSpec((None, None, None, 1, D_MODEL), lambda i, *_: (l, k, (i * tm) // S, 0, 0))


def norm_modulate(x2, g, modr, l, S):
    T, D = x2.shape
    tm = _pick(S, (512, 256, 128))
    return pl.pallas_call(
        _norm_mod_kernel,
        out_shape=jax.ShapeDtypeStruct((T, D), BF16),
        grid=(T // tm,),
        in_specs=[pl.BlockSpec((tm, D), lambda i: (i, 0)),
                  pl.BlockSpec((None, 1, D), lambda i: (l, 0, 0)),
                  _mod_spec(l, 0, S, tm), _mod_spec(l, 1, S, tm)],
        out_specs=pl.BlockSpec((tm, D), lambda i: (i, 0)),
        compiler_params=_cparams(("arbitrary",)),
        name="norm1_modulate",
    )(x2, g, modr, modr)


def _rope_att(blk, cos, sin, scale):
    lane = lax.broadcasted_iota(I32, blk.shape, 1)
    rot = jnp.where((lane % 64) < 32, pltpu.roll(blk, 96, 1), pltpu.roll(blk, 32, 1))
    out = blk * cos + rot * sin
    return out * scale if scale != 1.0 else out


def _rope_ret(blk, cos, sin, scale):
    out = blk * cos + pltpu.roll(blk, 64, 1) * sin
    return out * scale if scale != 1.0 else out


def _inproj_kernel(h_hbm, w_ref, tab_ref, o_ref, wbf_ref, acc_ref, h_ref, h_sem, *, ni, ntiles):
    s = pl.program_id(0)
    j = jnp.maximum(s - 1, 0) // ni
    tm = o_ref.shape[0]
    h_rows = pl.ds(pl.multiple_of(lax.rem(jnp.minimum(s, ntiles - 1), ni) * tm, tm), tm)

    def h_copy(i):
        rows = pl.ds(pl.multiple_of(i * tm, tm), tm)
        return pltpu.make_async_copy(h_hbm.at[rows, :], h_ref.at[rows, :], h_sem.at[i])

    @pl.when(s == 0)
    def _():
        acc_ref[1] = jnp.zeros(acc_ref.shape[1:], F32)
        for i in range(ni):
            h_copy(i).start()

    @pl.when(s < ni)
    def _():
        h_copy(s).wait()

    @pl.when((lax.rem(s, ni) == 0) & (s < ntiles))
    def _():
        wbf_ref[...] = w_ref[...].astype(BF16)

    ngrp = PROJ_TN // LANES
    t = lambda off: off // PROJ_TN
    q_scale = float(ATT_HEAD_DIM) ** -0.5
    k_scale = float(RET_HEAD_DIM) ** -0.5

    def step(cur):
        def run(epilogue):
            cos = tab_ref[:, 0:LANES]
            sin = tab_ref[:, LANES:2 * LANES]
            for g in range(ngrp):
                blk = acc_ref[1 - cur, :, g * LANES:(g + 1) * LANES]
                o_ref[:, g * LANES:(g + 1) * LANES] = epilogue(g, blk, cos, sin).astype(BF16)
            acc_ref[cur] = jnp.dot(h_ref[h_rows, :], wbf_ref[...], preferred_element_type=F32)

        @pl.when(j < t(OFF_KA))
        def _():
            run(lambda g, b, c, s: _rope_att(b, c, s, q_scale))

        @pl.when(j == t(OFF_KA))
        def _():
            run(lambda g, b, c, s: _rope_att(b, c, s, 1.0) if g < ATT_KV_W // LANES else b)

        @pl.when((j >= t(OFF_QR)) & (j < t(OFF_KR)))
        def _():
            run(lambda g, b, c, s: _rope_ret(b, c, s, 1.0))

        @pl.when((j >= t(OFF_KR)) & (j < t(OFF_VR)))
        def _():
            run(lambda g, b, c, s: _rope_ret(b, c, s, k_scale))

        @pl.when((j >= t(OFF_VR)) & (j < t(OFF_GR)))
        def _():
            run(lambda g, b, c, s: b)

        @pl.when((j >= t(OFF_GR)) & (j < t(OFF_GA)))
        def _():
            run(lambda g, b, c, s: b * _sigmoid(b))

        @pl.when(j >= t(OFF_GA))
        def _():
            run(lambda g, b, c, s: _sigmoid(b))

    for parity in range(2):
        pl.when(lax.rem(s, 2) == parity)(functools.partial(step, parity))


def in_projection(h, w_in, l, tab):
    T, D = h.shape
    tm = _pick(T, (1024, 768, 512, 256))
    nj, ni = IN_W // PROJ_TN, T // tm
    ntiles = nj * ni
    t_qr, t_vr = OFF_QR // PROJ_TN, OFF_VR // PROJ_TN
    mm = lambda s: jnp.minimum(s, ntiles - 1)
    ep = lambda s: jnp.maximum(s - 1, 0)

    def tab_map(s):
        j, i = ep(s) // ni, lax.rem(ep(s), ni)
        return (jnp.where(j < t_qr, 0, 1), jnp.where(j < t_vr, i, 0), 0)

    return pl.pallas_call(
        functools.partial(_inproj_kernel, ni=ni, ntiles=ntiles),
        out_shape=jax.ShapeDtypeStruct((nj, T, PROJ_TN), BF16),
        grid=(ntiles + 1,),
        in_specs=[pl.BlockSpec(memory_space=pl.ANY),
                  pl.BlockSpec((None, D, PROJ_TN), lambda s: (l, 0, mm(s) // ni)),
                  pl.BlockSpec((None, tm, 2 * LANES), tab_map)],
        out_specs=pl.BlockSpec((None, tm, PROJ_TN), lambda s: (ep(s) // ni, lax.rem(ep(s), ni), 0)),
        scratch_shapes=[pltpu.VMEM((D, PROJ_TN), BF16), pltpu.VMEM((2, tm, PROJ_TN), F32),
                        pltpu.VMEM((T, D), BF16), pltpu.SemaphoreType.DMA((ni,))],
        compiler_params=_cparams(("arbitrary",)),
        name="in_projection",
    )(h, w_in, tab)


def _attn_kernel(sinks_ref, q0_ref, q1_ref, kvc_ref, kvp_ref, o_ref, *, tq):
    W = WINDOW
    nsub = tq // W
    is_first = pl.program_id(1) == 0
    qi = lax.broadcasted_iota(I32, (W, 2 * W), 0)
    kj = lax.broadcasted_iota(I32, (W, 2 * W), 1)
    rel = qi + W - kj
    band = (rel >= 0) & (rel < WINDOW)
    band0 = band & (jnp.logical_not(is_first) | (kj >= W))
    lane = lax.broadcasted_iota(I32, (tq + W, LANES), 1)
    lo = lane < ATT_HEAD_DIM

    def split_pair(raw, parity):
        x = raw.astype(F32)
        xr = pltpu.roll(x, ATT_HEAD_DIM, 1)
        if parity == 0:
            return jnp.where(lo, x, 0.0), jnp.where(lo, 0.0, xr)
        return jnp.where(lo, xr, 0.0), jnp.where(lo, 0.0, x)

    units = []
    for hk in range(ATT_KV_HEADS):
        grp, par = hk // 2, hk % 2
        kc = slice(grp * LANES, (grp + 1) * LANES)
        vc = slice(ATT_KV_W + grp * LANES, ATT_KV_W + (grp + 1) * LANES)
        k_pair = split_pair(jnp.concatenate([kvp_ref[:, kc], kvc_ref[:, kc]], axis=0), par)
        v_pair = split_pair(jnp.concatenate([kvp_ref[:, vc], kvc_ref[:, vc]], axis=0), par)
        kt_pair = [k.T.astype(BF16) for k in k_pair]
        v_pair = [v.astype(BF16) for v in v_pair]
        qcols = [slice(hk * 4 * ATT_HEAD_DIM + jq * LANES, hk * 4 * ATT_HEAD_DIM + (jq + 1) * LANES)
                 for jq in range(2)]
        for m in range(nsub):
            units.append((hk, m, kt_pair, v_pair, qcols))

    def scores(unit):
        hk, m, kt_pair, v_pair, qcols = unit
        rows = slice(m * W, (m + 1) * W)
        keys = slice(m * W, m * W + 2 * W)
        kcat = jnp.concatenate([kt_pair[0][:, keys], kt_pair[1][:, keys]], axis=1)
        q_ref = (q0_ref, q1_ref)[hk // 2]
        c0 = (hk % 2) * 4 * ATT_HEAD_DIM
        q = jnp.concatenate([q_ref[rows, c0:c0 + LANES], q_ref[rows, c0 + LANES:c0 + 2 * LANES]], axis=0)
        return jnp.dot(q, kcat, preferred_element_type=F32)

    def finish(unit, s_all):
        hk, m, kt_pair, v_pair, qcols = unit
        rows = slice(m * W, (m + 1) * W)
        keys = slice(m * W, m * W + 2 * W)
        mask = band0 if m == 0 else band
        vcat = jnp.concatenate([v_pair[0][keys], v_pair[1][keys]], axis=0)
        p_rows = []
        for jq in range(2):
            p_cols = []
            for e in range(2):
                s = s_all[jq * W:(jq + 1) * W, e * 2 * W:(e + 1) * 2 * W]
                s = jnp.where(mask, s, -1e30)
                sink = sinks_ref[hk * 4 + jq * 2 + e]
                mx = jnp.maximum(jnp.max(s, axis=1, keepdims=True), sink)
                p = jnp.exp(s - mx)
                den = jnp.sum(p, axis=1, keepdims=True) + jnp.exp(sink - mx)
                p_cols.append((p * (1.0 / den)).astype(BF16))
            p_rows.append(jnp.concatenate(p_cols, axis=1))
        o = jnp.dot(jnp.concatenate(p_rows, axis=0), vcat, preferred_element_type=F32)
        o_ref[rows, qcols[0]] = o[0:W].astype(BF16)
        o_ref[rows, qcols[1]] = o[W:2 * W].astype(BF16)

    pending = [scores(u) for u in units[:ATTN_LOOKAHEAD]]
    for idx, unit in enumerate(units):
        if idx + ATTN_LOOKAHEAD < len(units):
            pending.append(scores(units[idx + ATTN_LOOKAHEAD]))
        finish(unit, pending.pop(0))


def swa_attention(proj, sinks, B, S):
    T = proj.shape[1]
    tq = _pick(S, (512, 384, 256, 128))
    nq = S // tq
    assert ATT_Q_W == 2 * PROJ_TN and 2 * ATT_KV_W == PROJ_TN
    kv_tile = OFF_KA // PROJ_TN
    tile = lambda j: pl.BlockSpec((None, tq, PROJ_TN), lambda b, i: (j, b * nq + i, 0))

    def prev_map(b, i):
        return (kv_tile, jnp.maximum(b * (S // WINDOW) + i * (tq // WINDOW) - 1, 0), 0)

    return pl.pallas_call(
        functools.partial(_attn_kernel, tq=tq),
        out_shape=jax.ShapeDtypeStruct((T, ATT_Q_W), BF16),
        grid=(B, nq),
        in_specs=[pl.BlockSpec(memory_space=pltpu.SMEM),
                  tile(0), tile(1), tile(kv_tile),
                  pl.BlockSpec((None, WINDOW, PROJ_TN), prev_map)],
        out_specs=pl.BlockSpec((tq, ATT_Q_W), lambda b, i: (b * nq + i, 0)),
        compiler_params=_cparams(("arbitrary", "arbitrary")),
        name="swa_attention",
    )(sinks, proj, proj, proj, proj)


def _ret_consts():
    C = RET_CHUNK
    log_g = jnp.log1p(-jnp.exp2(-5.0 - jnp.arange(RET_HEADS, dtype=F32)))
    i = jnp.arange(C, dtype=F32)
    diff = i[:, None] - i[None, :]
    dm = jnp.where(diff[None] >= 0, jnp.exp(jnp.maximum(diff, 0.0)[None] * log_g[:, None, None]), 0.0)
    qd = jnp.exp((i + 1.0)[None, :] * log_g[:, None])
    kd = jnp.exp((C - 1.0 - i)[None, :] * log_g[:, None])
    cd = jnp.exp(C * log_g)
    bc = lambda v: jnp.broadcast_to(v[:, :, None], (RET_HEADS, C, LANES))
    return dm, bc(qd), bc(kd), jnp.broadcast_to(cd[:, None, None], (RET_HEADS, 8, LANES))


def _ret_kernel(q_ref, k_ref, v_ref, g_ref, gn_ref, dm_ref, qd_ref, kd_ref, cd_ref, o_ref, *, nchunk):
    C = RET_CHUNK
    heads = range(RET_HEADS_PER_STEP)
    lanes = [slice(hh * LANES, (hh + 1) * LANES) for hh in heads]
    rows = [slice(c * C, (c + 1) * C) for c in range(nchunk)]

    def independent(c, hh):
        q, k, v = q_ref[rows[c], lanes[hh]], k_ref[rows[c], lanes[hh]], v_ref[rows[c], lanes[hh]]
        s = lax.dot_general(q, k, (((1,), (1,)), ((), ())), preferred_element_type=F32) * dm_ref[hh]
        vk = (v.astype(F32) * kd_ref[hh]).astype(BF16)
        kv = lax.dot_general(k, vk, (((0,), (0,)), ((), ())), preferred_element_type=F32)
        return s.astype(BF16), kv

    def finish(c, hh, s, state):
        q, v = q_ref[rows[c], lanes[hh]], v_ref[rows[c], lanes[hh]]
        y = jnp.dot(s, v, preferred_element_type=F32)
        y = y + jnp.dot(q, state.astype(BF16), preferred_element_type=F32) * qd_ref[hh]
        ms = jnp.mean(y * y, axis=-1, keepdims=True)
        yn = y * lax.rsqrt(ms + EPS) * gn_ref[:, lanes[hh]]
        o_ref[rows[c], lanes[hh]] = (g_ref[rows[c], lanes[hh]].astype(F32) * yn).astype(BF16)

    state = [jnp.zeros((RET_HEAD_DIM, RET_HEAD_DIM), F32) for _ in heads]
    ahead = [independent(0, hh) for hh in heads]
    for c in range(nchunk):
        cur = ahead
        if c + 1 < nchunk:
            ahead = [independent(c + 1, hh) for hh in heads]
        for hh in heads:
            s, kv = cur[hh]
            finish(c, hh, s, state[hh])
            state[hh] = cd_ref[hh, 0:1, :] * state[hh] + kv


def retention(proj, ret_norm_g, l, B, S):
    T = proj.shape[1]
    dm, qd, kd, cd = _ret_consts()
    hp = RET_HEADS_PER_STEP
    wid = hp * LANES
    per_tile = PROJ_TN // wid

    def part(off):
        return pl.BlockSpec((None, S, wid), lambda b, h: (off // PROJ_TN + h // per_tile, b, lax.rem(h, per_tile)))
    hconst = lambda rows: pl.BlockSpec((hp, rows, LANES), lambda b, h: (h, 0, 0))
    return pl.pallas_call(
        functools.partial(_ret_kernel, nchunk=S // RET_CHUNK),
        out_shape=jax.ShapeDtypeStruct((T, RET_W), BF16),
        grid=(B, RET_HEADS // hp),
        in_specs=[part(OFF_QR), part(OFF_KR), part(OFF_VR), part(OFF_GR),
                  pl.BlockSpec((None, 1, wid), lambda b, h: (l, 0, h)),
                  pl.BlockSpec((hp, RET_CHUNK, RET_CHUNK), lambda b, h: (h, 0, 0)),
                  hconst(RET_CHUNK), hconst(RET_CHUNK), hconst(8)],
        out_specs=pl.BlockSpec((S, wid), lambda b, h: (b, h)),
        compiler_params=_cparams(("arbitrary", "arbitrary")),
        name="retention",
    )(proj, proj, proj, proj, ret_norm_g.reshape(-1, 1, RET_W), dm, qd, kd, cd)


def _load_cast(w_hbm, dst, stage, sem):
    rows = stage.shape[1]
    n = w_hbm.shape[0] // rows

    def copy(i, slot):
        r0 = pl.multiple_of(i * rows, rows)
        return pltpu.make_async_copy(w_hbm.at[pl.ds(r0, rows), :], stage.at[slot], sem.at[slot])

    copy(0, 0).start()

    def body(i, carry):
        slot = lax.rem(i, 2)

        @pl.when(i + 1 < n)
        def _():
            copy(i + 1, 1 - slot).start()
        copy(i, slot).wait()
        dst[pl.ds(pl.multiple_of(i * rows, rows), rows), :] = stage[slot].astype(BF16)
        return carry
    lax.fori_loop(0, n, body, 0)


def _mixer_out_kernel(att_ref, ret_ref, ga0, ga1, ga2, ga3, gb0, gb1, gb2, gb3, x_ref,
                      wa_hbm, wr_hbm, wo_hbm, gate1_ref, g2_ref, sh2_ref, sc2_ref,
                      wrt_ref, brt_ref, tril_ref,
                      xn_ref, h2_ref, rti_ref, rtw_ref, cnt_ref,
                      wa, wr, wo, wrt2, mrg, carry, hs, stage, sem, *, tm, l):
    i = pl.program_id(0)
    nt = pl.num_programs(0) - 1
    D = D_MODEL

    @pl.when(i == 0)
    def _():
        _load_cast(wa_hbm.at[l], wa, stage, sem)
        _load_cast(wr_hbm.at[l], wr, stage, sem)
        _load_cast(wo_hbm.at[l], wo, stage, sem)
        w = wrt_ref[...]
        hi = w.astype(BF16)
        wrt2[:, 0:LANES] = hi
        wrt2[:, LANES:2 * LANES] = (w - hi.astype(F32)).astype(BF16)
        carry[...] = jnp.zeros_like(carry)
        hs[...] = jnp.zeros_like(hs)

    def main_stage():
        a = att_ref[...]
        r = ret_ref[...]
        ga = (ga0, ga1, ga2, ga3)
        gb = (gb0, gb1, gb2, gb3)
        for n in range(D // PROJ_TN):
            cols = slice(n * PROJ_TN, (n + 1) * PROJ_TN)
            ya = jnp.dot(a, wa[:, cols], preferred_element_type=F32)
            yr = jnp.dot(r, wr[:, cols], preferred_element_type=F32)
            mrg[:, cols] = (ga[n][...].astype(F32) * ya + gb[n][...].astype(F32) * yr).astype(BF16)
        o = jnp.dot(mrg[...], wo[...], preferred_element_type=F32)
        xn = x_ref[...] + gate1_ref[...] * o
        xn_ref[...] = xn
        h2 = _norm_mod(xn, g2_ref[...], sh2_ref[...], sc2_ref[...])
        h2b = h2.astype(BF16)
        h2_ref[...] = h2b
        hs[0:tm, :] = h2b
        hs[tm:2 * tm, :] = (h2 - h2b.astype(F32)).astype(BF16)

    def router_select():
        r = jnp.dot(hs[...], wrt2[...], preferred_element_type=F32)
        logits = ((r[0:tm, 0:LANES] + r[tm:2 * tm, 0:LANES])
                  + (r[0:tm, LANES:2 * LANES] + r[tm:2 * tm, LANES:2 * LANES])) + brt_ref[...]
        lane = lax.broadcasted_iota(I32, (tm, LANES), 1)
        valid = lane < N_EXPERTS
        mx = jnp.max(logits, axis=-1, keepdims=True)
        p = jnp.where(valid, jnp.exp(logits - mx), 0.0)
        pos_in_grp = lane % EXPERTS_PER_GROUP
        grp_of = lane // EXPERTS_PER_GROUP

        def member(k):
            wrapped = pos_in_grp + k >= EXPERTS_PER_GROUP
            return jnp.where(wrapped, pltpu.roll(p, EXPERTS_PER_GROUP - k, 1), pltpu.roll(p, LANES - k, 1)), wrapped

        (b1, w1), (b2, w2), (b3, w3) = member(1), member(2), member(3)
        m_ab, n_ab = jnp.maximum(p, b1), jnp.minimum(p, b1)
        m_cd, n_cd = jnp.maximum(b2, b3), jnp.minimum(b2, b3)
        gscore = jnp.maximum(m_ab, m_cd) + jnp.maximum(jnp.minimum(m_ab, m_cd), jnp.maximum(n_ab, n_cd))
        gscore = jnp.where(valid, gscore, -1.0)
        gmax = jnp.max(gscore, axis=-1, keepdims=True)
        gsel = jnp.min(jnp.where(gscore == gmax, grp_of, N_GROUPS), axis=-1, keepdims=True)
        in_sel = grp_of == gsel
        beats = lambda b, w: ((b > p) | ((b == p) & w)).astype(I32)
        rank_in_grp = beats(b1, w1) + beats(b2, w2) + beats(b3, w3)
        sel0 = in_sel & (rank_in_grp == 0)
        sel1 = in_sel & (rank_in_grp == 1)
        lsum = lambda m, v: jnp.sum(jnp.where(m, v, 0.0), axis=-1, keepdims=True)
        v0, v1 = lsum(sel0, p), lsum(sel1, p)
        lanef = lane.astype(F32)
        e0, e1 = lsum(sel0, lanef), lsum(sel1, lanef)
        inv = 1.0 / (v0 + v1)
        return lane, sel0, sel1, e0, e1, v0 * inv, v1 * inv

    def router_finish(live, lane, sel0, sel1, e0, e1, w0, w1):
        lsum = lambda m, v: jnp.sum(jnp.where(m, v, 0.0), axis=-1, keepdims=True)
        onehot = jnp.where(sel0 | sel1, 1.0, 0.0)
        prefix = jnp.dot(tril_ref[...], onehot.astype(BF16), preferred_element_type=F32) + carry[0:1, :]
        r0, r1 = lsum(sel0, prefix), lsum(sel1, prefix)
        carry[...] = carry[...] + live * jnp.sum(onehot, axis=0, keepdims=True)
        cnt_ref[...] = carry[...].astype(I32)
        sel4 = lambda a0, a1, a2, a3: jnp.where(lane == 0, a0, jnp.where(lane == 1, a1, jnp.where(lane == 2, a2, a3)))
        rti_ref[...] = sel4(e0, e1, r0, r1).astype(I32)
        rtw_ref[...] = jnp.where(lane == 0, w0, jnp.where(lane == 1, w1, 0.0))

    @pl.when(i < nt)
    def _():
        picked = router_select()
        main_stage()
        router_finish(jnp.where(i > 0, 1.0, 0.0), *picked)

    @pl.when(i == nt)
    def _():
        router_finish(1.0, *router_select())


def mixer_out(att, ret, proj, x2, w_attn_out, w_ret_out, w_o, modr, norm2_g, w_router, b_router, l, S):
    T, D = x2.shape
    tm = 256
    assert S % tm == 0
    nga, ngb = OFF_GA // PROJ_TN, OFF_GB // PROJ_TN
    nt = T // tm
    cur = lambda i: jnp.minimum(i, nt - 1)
    prev = lambda i: jnp.maximum(i - 1, 0)
    gate_spec = lambda blk: pl.BlockSpec((None, tm, PROJ_TN), lambda i: (blk, cur(i), 0))
    row = lambda: pl.BlockSpec((tm, D), lambda i: (cur(i), 0))
    mod_spec = lambda k: pl.BlockSpec((None, None, None, 1, D), lambda i: (l, k, (cur(i) * tm) // S, 0, 0))
    wrt = jnp.pad(w_router, ((0, 0), (0, LANES - N_EXPERTS)))
    brt = jnp.pad(b_router.astype(F32), (0, LANES - N_EXPERTS), constant_values=-1e30).reshape(1, LANES)
    tril = jnp.asarray(np.tril(np.ones((tm, tm), np.float32), -1), BF16)
    any_spec = pl.BlockSpec(memory_space=pl.ANY)
    outs = pl.pallas_call(
        functools.partial(_mixer_out_kernel, tm=tm, l=l),
        out_shape=(jax.ShapeDtypeStruct((T, D), F32),
                   jax.ShapeDtypeStruct((T, D), BF16),
                   jax.ShapeDtypeStruct((T, LANES), I32),
                   jax.ShapeDtypeStruct((T, LANES), F32),
                   jax.ShapeDtypeStruct((8, LANES), I32)),
        grid=(nt + 1,),
        in_specs=[pl.BlockSpec((tm, ATT_Q_W), lambda i: (cur(i), 0)),
                  pl.BlockSpec((tm, RET_W), lambda i: (cur(i), 0)),
                  *[gate_spec(nga + n) for n in range(4)],
                  *[gate_spec(ngb + n) for n in range(4)],
                  row(), any_spec, any_spec, any_spec,
                  mod_spec(2),
                  pl.BlockSpec((None, 1, D), lambda i: (l, 0, 0)),
                  mod_spec(3), mod_spec(4),
                  pl.BlockSpec((D, LANES), lambda i: (0, 0)),
                  pl.BlockSpec((1, LANES), lambda i: (0, 0)),
                  pl.BlockSpec((tm, tm), lambda i: (0, 0))],
        out_specs=(row(),
                   row(),
                   pl.BlockSpec((tm, LANES), lambda i: (prev(i), 0)),
                   pl.BlockSpec((tm, LANES), lambda i: (prev(i), 0)),
                   pl.BlockSpec((8, LANES), lambda i: (0, 0))),
        scratch_shapes=[pltpu.VMEM((ATT_Q_W, D), BF16), pltpu.VMEM((RET_W, D), BF16),
                        pltpu.VMEM((D, D), BF16),
                        pltpu.VMEM((D, 2 * LANES), BF16),
                        pltpu.VMEM((tm, D), BF16), pltpu.VMEM((8, LANES), F32),
                        pltpu.VMEM((2 * tm, D), BF16),
                        pltpu.VMEM((2, 256, D), F32), pltpu.SemaphoreType.DMA((2,))],
        compiler_params=_cparams(("arbitrary",)),
        name="mixer_out",
    )(att, ret, *([proj] * 8), x2, w_attn_out, w_ret_out, w_o,
      modr, norm2_g.reshape(-1, 1, D), modr, modr, wrt, brt, tril)
    return outs


def moe_plan(counts, n_work):
    sub_per = MOE_ROWS // MOE_SUB
    seg = ((counts + MOE_SUB - 1) // MOE_SUB) * MOE_SUB
    off = jnp.cumsum(seg) - seg
    off17 = jnp.concatenate([off, off[-1:] + seg[-1:]]).astype(I32)
    nb = (counts + MOE_ROWS - 1) // MOE_ROWS
    cum = jnp.cumsum(nb)
    total = cum[-1]
    w = jnp.arange(n_work, dtype=I32)
    wc = jnp.minimum(w, total - 1)
    e_w = jnp.sum((cum[None, :] <= wc[:, None]).astype(I32), axis=1)
    blk = wc - (cum[e_w] - nb[e_w])
    start = off[e_w] + blk * MOE_ROWS
    nsub = jnp.clip(seg[e_w] // MOE_SUB - blk * sub_per, 0, sub_per)
    nsub = jnp.where(w < total, nsub, 0)
    return off17, e_w, start.astype(I32), nsub.astype(I32)


def _row_dma_loops(n_rows, make_copy):
    def run(op):
        def body(g, carry):
            for u in range(ROW_GROUP):
                for k in range(2):
                    getattr(make_copy(g, u, k), op)()
            return carry
        lax.fori_loop(0, n_rows // ROW_GROUP, body, 0)
    run("start")
    run("wait")


def _dispatch_kernel(pos_ref, off_ref, h_ref, xs_ref, rows_ref, zero_ref, sem, *, td, T):
    i = pl.program_id(0)

    @pl.when(i == 0)
    def _():
        zero_ref[...] = jnp.zeros_like(zero_ref)
        used = off_ref[N_EXPERTS]

        def zero_copy(row):
            row = pl.multiple_of(row, MOE_SUB)
            return pltpu.make_async_copy(zero_ref, xs_ref.at[pl.ds(row, MOE_SUB), :], sem)

        def fill(op):
            def seg_tail(e, carry):
                @pl.when(off_ref[e + 1] > off_ref[e])
                def _():
                    getattr(zero_copy(off_ref[e + 1] - MOE_SUB), op)()
                return carry

            def buf_tail(n, carry):
                getattr(zero_copy(used + n * MOE_SUB), op)()
                return carry
            lax.fori_loop(0, N_EXPERTS, seg_tail, 0)
            lax.fori_loop(0, (xs_ref.shape[0] - used) // MOE_SUB, buf_tail, 0)
        fill("start")
        fill("wait")

    rows_ref[...] = h_ref[...].astype(F32).reshape(rows_ref.shape)

    def copy(g, u, k):
        p = pos_ref[k * T + i * td + g * ROW_GROUP + u]
        return pltpu.make_async_copy(rows_ref.at[g, pl.ds(u, 1), :], xs_ref.at[pl.ds(p, 1), :], sem)
    _row_dma_loops(td, copy)


def dispatch(h2, pos_flat, off17, n_rows):
    T, D = h2.shape
    td = MOE_SUB
    return pl.pallas_call(
        functools.partial(_dispatch_kernel, td=td, T=T),
        out_shape=jax.ShapeDtypeStruct((n_rows, D), F32),
        grid_spec=pltpu.PrefetchScalarGridSpec(
            num_scalar_prefetch=2, grid=(T // td,),
            in_specs=[pl.BlockSpec((td, D), lambda i, *_: (i, 0))],
            out_specs=pl.BlockSpec(memory_space=pl.ANY),
            scratch_shapes=[pltpu.VMEM((td // ROW_GROUP, ROW_GROUP, D), F32), pltpu.VMEM((MOE_SUB, D), F32),
                            pltpu.SemaphoreType.DMA]),
        compiler_params=_cparams(("arbitrary",)),
        name="moe_dispatch",
    )(pos_flat, off17, h2)


def _moe_kernel(we_ref, ws_ref, wn_ref, xs_ref, wg_ref, wu_ref, wd_ref, ys_ref,
                xb, acc, wgb, wub, wdb, stage, ostage, sem_in, sem_out, *, nfc, n_work):
    w = pl.program_id(0)
    c = pl.program_id(1)
    n = wn_ref[w]
    s0 = ws_ref[w]

    def in_copy(start_row, i, slot):
        g0 = pl.multiple_of(start_row + i * MOE_SUB, MOE_SUB)
        return pltpu.make_async_copy(xs_ref.at[pl.ds(g0, MOE_SUB), :], stage.at[slot], sem_in.at[slot])

    def out_copy(i, slot):
        g0 = pl.multiple_of(s0 + i * MOE_SUB, MOE_SUB)
        return pltpu.make_async_copy(ostage.at[slot], ys_ref.at[pl.ds(g0, MOE_SUB), :], sem_out.at[slot])

    def request_first_two(start_row, count):
        @pl.when(count > 0)
        def _():
            in_copy(start_row, 0, 0).start()

        @pl.when(count > 1)
        def _():
            in_copy(start_row, 1, 1).start()

    @pl.when((w == 0) & (c == 0))
    def _():
        request_first_two(s0, n)

    def for_subs(fn):
        def body(p, carry):
            fn(p * 2, 2)
            return carry
        lax.fori_loop(0, n // 2, body, 0)

        @pl.when(lax.rem(n, 2) == 1)
        def _():
            fn(n - 1, 1)

    def rows_of(i):
        return pl.ds(pl.multiple_of(i * MOE_SUB, MOE_SUB), MOE_SUB)

    def ffn(i0, count):
        xs_ = [xb[rows_of(i0 + a), :] for a in range(count)]
        gu = [(jnp.dot(x, wgb[...], preferred_element_type=F32), jnp.dot(x, wub[...], preferred_element_type=F32))
              for x in xs_]
        acts = [(g * _sigmoid(g) * u).astype(BF16) for g, u in gu]
        return [jnp.dot(a, wdb[...], preferred_element_type=F32) for a in acts]

    @pl.when(n > 0)
    def _():
        wgb[...] = wg_ref[...].astype(BF16)
        wub[...] = wu_ref[...].astype(BF16)
        wdb[...] = wd_ref[...].astype(BF16)

        @pl.when(c == 0)
        def _():
            def first(i0, count):
                for a in range(count):
                    in_copy(s0, i0 + a, a).wait()
                    xb[rows_of(i0 + a), :] = stage[a].astype(BF16)
                for a in range(count):
                    @pl.when(i0 + a + 2 < n)
                    def _():
                        in_copy(s0, i0 + a + 2, a).start()
                for a, y in enumerate(ffn(i0, count)):
                    acc[rows_of(i0 + a), :] = y
            for_subs(first)

        @pl.when((c > 0) & (c < nfc - 1))
        def _():
            def middle(i0, count):
                for a, y in enumerate(ffn(i0, count)):
                    acc[rows_of(i0 + a), :] += y
            for_subs(middle)

        @pl.when(c == nfc - 1)
        def _():
            def last(i0, count):
                for a in range(count):
                    @pl.when(i0 + a >= 2)
                    def _():
                        out_copy(i0 + a - 2, a).wait()
                for a, y in enumerate(ffn(i0, count)):
                    ostage[a] = acc[rows_of(i0 + a), :] + y
                    out_copy(i0 + a, a).start()
            for_subs(last)

            @pl.when(n >= 2)
            def _():
                out_copy(n - 2, lax.rem(n, 2)).wait()
            out_copy(n - 1, lax.rem(n - 1, 2)).wait()

            @pl.when(w + 1 < n_work)
            def _():
                nxt = jnp.minimum(w + 1, n_work - 1)
                request_first_two(ws_ref[nxt], wn_ref[nxt])


def moe_experts(xs, w_gate, w_up, w_down, l, e_w, start_w, nsub_w):
    n_rows, D = xs.shape
    nfc = D_FF // MOE_FC
    assert nfc >= 2
    n_work = e_w.shape[0]

    def chunk(c, wn, w):
        return jnp.where(wn[w] > 0, c, nfc - 1)

    return pl.pallas_call(
        functools.partial(_moe_kernel, nfc=nfc, n_work=n_work),
        out_shape=jax.ShapeDtypeStruct((n_rows, D), F32),
        grid_spec=pltpu.PrefetchScalarGridSpec(
            num_scalar_prefetch=3, grid=(n_work, nfc),
            in_specs=[pl.BlockSpec(memory_space=pl.ANY),
                      pl.BlockSpec((None, None, D, MOE_FC), lambda w, c, we, ws, wn: (l, we[w], 0, chunk(c, wn, w))),
                      pl.BlockSpec((None, None, D, MOE_FC), lambda w, c, we, ws, wn: (l, we[w], 0, chunk(c, wn, w))),
                      pl.BlockSpec((None, None, MOE_FC, D), lambda w, c, we, ws, wn: (l, we[w], chunk(c, wn, w), 0))],
            out_specs=pl.BlockSpec(memory_space=pl.ANY),
            scratch_shapes=[pltpu.VMEM((MOE_ROWS, D), BF16), pltpu.VMEM((MOE_ROWS, D), F32),
                            pltpu.VMEM((D, MOE_FC), BF16), pltpu.VMEM((D, MOE_FC), BF16),
                            pltpu.VMEM((MOE_FC, D), BF16),
                            pltpu.VMEM((2, MOE_SUB, D), F32), pltpu.VMEM((2, MOE_SUB, D), F32),
                            pltpu.SemaphoreType.DMA((2,)), pltpu.SemaphoreType.DMA((2,))]),
        input_output_aliases={3: 0},
        compiler_params=_cparams(("arbitrary", "arbitrary")),
        name="moe_experts",
    )(e_w, start_w, nsub_w, xs, w_gate, w_up, w_down)


def _combine_kernel(pos_ref, ys_ref, x_ref, rtw_ref, gate2_ref, g_ref, *rest, tc, T, final):
    if final:
        out_ref, ybuf, sem = rest
    else:
        sh_ref, sc_ref, xo_ref, h_ref, ybuf, sem = rest
    i = pl.program_id(0)
    slot = lax.rem(i, 2)
    ngroups = tc // COMBINE_GROUP

    def copy(tile, g, u, k, sl):
        p = pos_ref[k * T + tile * tc + g * ROW_GROUP + u]
        return pltpu.make_async_copy(ys_ref.at[pl.ds(p, 1), :], ybuf.at[sl, k, g, pl.ds(u, 1), :], sem.at[sl])

    def for_rows(tile, sl, r0, op):
        g0 = r0 // ROW_GROUP
        for gg in range(COMBINE_GROUP // ROW_GROUP):
            for u in range(ROW_GROUP):
                for k in range(2):
                    getattr(copy(tile, g0 + gg, u, k, sl), op)()

    def groups(fn):
        def body(g, carry):
            fn(pl.multiple_of(g * COMBINE_GROUP, COMBINE_GROUP))
            return carry
        lax.fori_loop(0, ngroups, body, 0)

    def compute(r0):
        rows = pl.ds(r0, COMBINE_GROUP)
        grp = pl.ds(r0 // ROW_GROUP, COMBINE_GROUP // ROW_GROUP)
        y = [ybuf[slot, k, grp, :, :].reshape(COMBINE_GROUP, D_MODEL) for k in range(2)]
        moe = rtw_ref[rows, 0:1] * y[0] + rtw_ref[rows, 1:2] * y[1]
        xo = x_ref[rows, :] + gate2_ref[...] * moe
        if final:
            ms = jnp.mean(xo * xo, axis=-1, keepdims=True)
            out_ref[rows, :] = xo * lax.rsqrt(ms + EPS) * g_ref[...]
        else:
            xo_ref[rows, :] = xo
            h_ref[rows, :] = _norm_mod(xo, g_ref[...], sh_ref[...], sc_ref[...]).astype(BF16)

    @pl.when(i == 0)
    def _():
        groups(lambda r0: for_rows(0, 0, r0, "start"))
    groups(lambda r0: for_rows(i, slot, r0, "wait"))

    @pl.when(i + 1 < pl.num_programs(0))
    def _():
        def both(r0):
            compute(r0)
            for_rows(i + 1, 1 - slot, r0, "start")
        groups(both)

    @pl.when(i + 1 >= pl.num_programs(0))
    def _():
        groups(compute)


def combine(ys, xn, rtw, pos_flat, modr, l, S, g_next, final):
    T, D = xn.shape
    tc = 256
    assert S % tc == 0
    row = lambda: pl.BlockSpec((tc, D), lambda i, *_: (i, 0))
    in_specs = [pl.BlockSpec(memory_space=pl.ANY), row(),
                pl.BlockSpec((tc, LANES), lambda i, *_: (i, 0)),
                _mod_spec(l, 5, S, tc)]
    args = [ys, xn, rtw, modr]
    if final:
        in_specs.append(pl.BlockSpec((1, D), lambda i, *_: (0, 0)))
        args.append(g_next.reshape(1, D))
        out_shape = jax.ShapeDtypeStruct((T, D), F32)
        out_specs = row()
    else:
        in_specs += [pl.BlockSpec((None, 1, D), lambda i, *_: (l + 1, 0, 0)),
                     _mod_spec(l + 1, 0, S, tc), _mod_spec(l + 1, 1, S, tc)]
        args += [g_next.reshape(-1, 1, D), modr, modr]
        out_shape = (jax.ShapeDtypeStruct((T, D), F32), jax.ShapeDtypeStruct((T, D), BF16))
        out_specs = (row(), row())
    return pl.pallas_call(
        functools.partial(_combine_kernel, tc=tc, T=T, final=final),
        out_shape=out_shape,
        grid_spec=pltpu.PrefetchScalarGridSpec(
            num_scalar_prefetch=1, grid=(T // tc,),
            in_specs=in_specs, out_specs=out_specs,
            scratch_shapes=[pltpu.VMEM((2, 2, tc // ROW_GROUP, ROW_GROUP, D), F32), pltpu.SemaphoreType.DMA((2,))]),
        compiler_params=_cparams(("arbitrary",)),
        name="moe_combine_final" if final else "moe_combine",
    )(pos_flat, *args)


def kernel(x, c, positions, w_ada, b_ada, norm1_g, norm2_g, w_in, attn_sinks, w_attn_out, ret_norm_g,
           w_ret_out, w_o, w_router, b_router, w_gate, w_up, w_down, final_g):
    B, S, D = x.shape
    L = w_ada.shape[0]
    T = B * S
    assert D == D_MODEL and w_in.shape[-1] == IN_W and S % RET_CHUNK == 0

    tab = rope_tables(positions)
    mod = adaln_mod(c, w_ada, b_ada)
    modr = mod[:, :B].reshape(L, B, 6, D).transpose(0, 2, 1, 3).reshape(L, 6, B, 1, D)

    x2 = x.reshape(T, D)
    h = norm_modulate(x2, norm1_g.reshape(L, 1, D), modr, 0, S)
    n_work = N_EXPERTS + (2 * T + MOE_ROWS - 1) // MOE_ROWS
    n_rows = 2 * T + N_EXPERTS * MOE_SUB
    out = None
    for l in range(L):
        proj = in_projection(h, w_in, l, tab)
        att = swa_attention(proj, attn_sinks[l], B, S)
        ret = retention(proj, ret_norm_g, l, B, S)
        xn, h2, rti, rtw, cnt = mixer_out(att, ret, proj, x2, w_attn_out, w_ret_out, w_o, modr,
                                          norm2_g, w_router, b_router, l, S)
        off17, e_w, start_w, nsub_w = moe_plan(cnt[0, :N_EXPERTS], n_work)
        seg_off = jnp.sum(jnp.where(rti[:, 0:2, None] == jnp.arange(N_EXPERTS, dtype=I32),
                                    off17[:N_EXPERTS], 0), axis=-1)
        pos_flat = (seg_off + rti[:, 2:4]).T.reshape(2 * T)
        xs = dispatch(h2, pos_flat, off17, n_rows)
        ys = moe_experts(xs, w_gate, w_up, w_down, l, e_w, start_w, nsub_w)
        if l + 1 < L:
            x2, h = combine(ys, xn, rtw, pos_flat, modr, l, S, norm1_g, final=False)
        else:
            out = combine(ys, xn, rtw, pos_flat, modr, l, S, final_g, final=True)
    return out.reshape(B, S, D)
```

```python
import functools

import numpy as np
import jax
import jax.numpy as jnp
from jax import lax
from jax.experimental import pallas as pl
from jax.experimental.pallas import tpu as pltpu

F32 = jnp.float32
BF16 = jnp.bfloat16
I32 = jnp.int32

D_MODEL = 2048
ATT_HEAD_DIM = 64
ATT_Q_HEADS = 16
ATT_KV_HEADS = 4
WINDOW = 128
RET_HEADS = 8
RET_HEAD_DIM = 128
RET_CHUNK = 256
ROPE_THETA = 10000.0
N_GROUPS = 4
EXPERTS_PER_GROUP = 4
N_EXPERTS = 16
D_FF = 1024
EPS = 1e-6

ATT_Q_W = ATT_Q_HEADS * ATT_HEAD_DIM
ATT_KV_W = ATT_KV_HEADS * ATT_HEAD_DIM
RET_W = RET_HEADS * RET_HEAD_DIM
OFF_QA = 0
OFF_KA = OFF_QA + ATT_Q_W
OFF_VA = OFF_KA + ATT_KV_W
OFF_QR = OFF_VA + ATT_KV_W
OFF_KR = OFF_QR + RET_W
OFF_VR = OFF_KR + RET_W
OFF_GR = OFF_VR + RET_W
OFF_GA = OFF_GR + RET_W
OFF_GB = OFF_GA + D_MODEL
IN_W = OFF_GB + D_MODEL

LANES = 128
PROJ_TN = 512
ROW_GROUP = 8
ATTN_LOOKAHEAD = 2
COMBINE_GROUP = 64
RET_HEADS_PER_STEP = 4
MOE_SUB = 256
MOE_ROWS = 2048
MOE_FC = 256
VMEM_LIMIT = 56 * 1024 * 1024


def _pick(n, cands):
    for c in cands:
        if n % c == 0:
            return c
    raise ValueError(f"no tile in {cands} divides {n}")


def _sigmoid(x):
    return 1.0 / (1.0 + jnp.exp(-x))


def _cparams(sem, vmem=VMEM_LIMIT):
    return pltpu.CompilerParams(dimension_semantics=sem, vmem_limit_bytes=vmem)


def _rope_consts():
    def inv_freq(head_dim):
        return ROPE_THETA ** (-2.0 * jnp.arange(head_dim // 2, dtype=F32) / head_dim)
    n_att, n_ret = ATT_HEAD_DIM // 2, RET_HEAD_DIM // 2
    inv = jnp.concatenate([inv_freq(ATT_HEAD_DIM), inv_freq(RET_HEAD_DIM), jnp.zeros((LANES - n_att - n_ret,), F32)])
    lane = np.arange(LANES)
    sgn_att = np.where((lane % ATT_HEAD_DIM) < n_att, -1.0, 1.0)
    sgn_ret = np.where(lane < n_ret, -1.0, 1.0)
    sgn = np.stack([np.tile(sgn_att, (8, 1)), np.tile(sgn_ret, (8, 1))]).astype(np.float32)
    return jnp.broadcast_to(inv, (8, LANES)), jnp.asarray(sgn)


def _rope_kernel(pos_ref, inv_ref, sgn_ref, tab_ref):
    ang = pos_ref[...] * inv_ref[0:1, :]
    lane = lax.broadcasted_iota(I32, ang.shape, 1)
    q = ATT_HEAD_DIM // 2
    for col, base in ((0, jnp.cos(ang)), (LANES, jnp.sin(ang))):
        r1, r2, r3 = pltpu.roll(base, q, 1), pltpu.roll(base, 2 * q, 1), pltpu.roll(base, 3 * q, 1)
        att = jnp.where(lane < q, base, jnp.where(lane < 2 * q, r1, jnp.where(lane < 3 * q, r2, r3)))
        ret = jnp.where(lane < RET_HEAD_DIM // 2, r3, r1)
        if col:
            att, ret = att * sgn_ref[0, 0:1, :], ret * sgn_ref[1, 0:1, :]
        tab_ref[0, :, col:col + LANES] = att
        tab_ref[1, :, col:col + LANES] = ret


def rope_tables(positions):
    T = positions.size
    posb = jnp.broadcast_to(positions.reshape(T, 1).astype(F32), (T, LANES))
    inv, sgn = _rope_consts()
    tm = _pick(T, (1024, 512, 256, 128))
    return pl.pallas_call(
        _rope_kernel,
        out_shape=jax.ShapeDtypeStruct((2, T, 2 * LANES), F32),
        grid=(T // tm,),
        in_specs=[pl.BlockSpec((tm, LANES), lambda i: (i, 0)),
                  pl.BlockSpec((8, LANES), lambda i: (0, 0)),
                  pl.BlockSpec((2, 8, LANES), lambda i: (0, 0, 0))],
        out_specs=pl.BlockSpec((2, tm, 2 * LANES), lambda i: (0, i, 0)),
        compiler_params=_cparams(("arbitrary",)),
        name="rope_tables",
    )(posb, inv, sgn)


def _adaln_kernel(c_ref, w_ref, b_ref, o_ref):
    c = c_ref[...]
    ca = c * _sigmoid(c)
    hi = ca.astype(BF16).astype(F32)
    lhs = jnp.concatenate([hi, ca - hi], axis=0).astype(BF16)
    r = jnp.dot(lhs, w_ref[...].astype(BF16), preferred_element_type=F32)
    o_ref[...] = r[0:8] + r[8:16] + b_ref[...]


def adaln_mod(c, w_ada, b_ada):
    L, D, N = w_ada.shape
    B = c.shape[0]
    cp = jnp.pad(c, ((0, 8 - B), (0, 0)))
    tn = _pick(N, (2048, 1024, 512))
    return pl.pallas_call(
        _adaln_kernel,
        out_shape=jax.ShapeDtypeStruct((L, 8, N), F32),
        grid=(L, N // tn),
        in_specs=[pl.BlockSpec((8, D), lambda l, j: (0, 0)),
                  pl.BlockSpec((None, D, tn), lambda l, j: (l, 0, j)),
                  pl.BlockSpec((None, 1, tn), lambda l, j: (l, 0, j))],
        out_specs=pl.BlockSpec((None, 8, tn), lambda l, j: (l, 0, j)),
        compiler_params=_cparams(("arbitrary", "arbitrary")),
        name="adaln_mod",
    )(cp, w_ada, b_ada.reshape(L, 1, N))


def _norm_mod(x, g, shift, scale):
    ms = jnp.mean(x * x, axis=-1, keepdims=True)
    return (x * lax.rsqrt(ms + EPS) * g) * (1.0 + scale) + shift


def _norm_mod_kernel(x_ref, g_ref, sh_ref, sc_ref, h_ref):
    h_ref[...] = _norm_mod(x_ref[...], g_ref[...], sh_ref[...], sc_ref[...]).astype(BF16)


def _mod_spec(l, k, S, tm):
    return pl.BlockSpec((None, None, None, 1, D_MODEL), lambda i, *_: (l, k, (i * tm) // S, 0, 0))


def norm_modulate(x2, g, modr, l, S):
    T, D = x2.shape
    tm = _pick(S, (512, 256, 128))
    return pl.pallas_call(
        _norm_mod_kernel,
        out_shape=jax.ShapeDtypeStruct((T, D), BF16),
        grid=(T // tm,),
        in_specs=[pl.BlockSpec((tm, D), lambda i: (i, 0)),
                  pl.BlockSpec((None, 1, D), lambda i: (l, 0, 0)),
                  _mod_spec(l, 0, S, tm), _mod_spec(l, 1, S, tm)],
        out_specs=pl.BlockSpec((tm, D), lambda i: (i, 0)),
        compiler_params=_cparams(("arbitrary",)),
        name="norm1_modulate",
    )(x2, g, modr, modr)


def _rope_att(blk, cos, sin, scale):
    lane = lax.broadcasted_iota(I32, blk.shape, 1)
    rot = jnp.where((lane % 64) < 32, pltpu.roll(blk, 96, 1), pltpu.roll(blk, 32, 1))
    out = blk * cos + rot * sin
    return out * scale if scale != 1.0 else out


def _rope_ret(blk, cos, sin, scale):
    out = blk * cos + pltpu.roll(blk, 64, 1) * sin
    return out * scale if scale != 1.0 else out


def _inproj_kernel(h_hbm, w_ref, tab_ref, o_ref, wbf_ref, acc_ref, h_ref, h_sem, *, ni, ntiles):
    s = pl.program_id(0)
    j = jnp.maximum(s - 1, 0) // ni
    tm = o_ref.shape[0]
    h_rows = pl.ds(pl.multiple_of(lax.rem(jnp.minimum(s, ntiles - 1), ni) * tm, tm), tm)

    def h_copy(i):
        rows = pl.ds(pl.multiple_of(i * tm, tm), tm)
        return pltpu.make_async_copy(h_hbm.at[rows, :], h_ref.at[rows, :], h_sem.at[i])

    @pl.when(s == 0)
    def _():
        acc_ref[1] = jnp.zeros(acc_ref.shape[1:], F32)
        for i in range(ni):
            h_copy(i).start()

    @pl.when(s < ni)
    def _():
        h_copy(s).wait()

    @pl.when((lax.rem(s, ni) == 0) & (s < ntiles))
    def _():
        wbf_ref[...] = w_ref[...].astype(BF16)

    ngrp = PROJ_TN // LANES
    t = lambda off: off // PROJ_TN
    q_scale = float(ATT_HEAD_DIM) ** -0.5
    k_scale = float(RET_HEAD_DIM) ** -0.5

    def step(cur):
        def run(epilogue):
            cos = tab_ref[:, 0:LANES]
            sin = tab_ref[:, LANES:2 * LANES]
            for g in range(ngrp):
                blk = acc_ref[1 - cur, :, g * LANES:(g + 1) * LANES]
                o_ref[:, g * LANES:(g + 1) * LANES] = epilogue(g, blk, cos, sin).astype(BF16)
            acc_ref[cur] = jnp.dot(h_ref[h_rows, :], wbf_ref[...], preferred_element_type=F32)

        @pl.when(j < t(OFF_KA))
        def _():
            run(lambda g, b, c, s: _rope_att(b, c, s, q_scale))

        @pl.when(j == t(OFF_KA))
        def _():
            run(lambda g, b, c, s: _rope_att(b, c, s, 1.0) if g < ATT_KV_W // LANES else b)

        @pl.when((j >= t(OFF_QR)) & (j < t(OFF_KR)))
        def _():
            run(lambda g, b, c, s: _rope_ret(b, c, s, 1.0))

        @pl.when((j >= t(OFF_KR)) & (j < t(OFF_VR)))
        def _():
            run(lambda g, b, c, s: _rope_ret(b, c, s, k_scale))

        @pl.when((j >= t(OFF_VR)) & (j < t(OFF_GR)))
        def _():
            run(lambda g, b, c, s: b)

        @pl.when((j >= t(OFF_GR)) & (j < t(OFF_GA)))
        def _():
            run(lambda g, b, c, s: b * _sigmoid(b))

        @pl.when(j >= t(OFF_GA))
        def _():
            run(lambda g, b, c, s: _sigmoid(b))

    for parity in range(2):
        pl.when(lax.rem(s, 2) == parity)(functools.partial(step, parity))


def in_projection(h, w_in, l, tab):
    T, D = h.shape
    tm = _pick(T, (1024, 768, 512, 256))
    nj, ni = IN_W // PROJ_TN, T // tm
    ntiles = nj * ni
    t_qr, t_vr = OFF_QR // PROJ_TN, OFF_VR // PROJ_TN
    mm = lambda s: jnp.minimum(s, ntiles - 1)
    ep = lambda s: jnp.maximum(s - 1, 0)

    def tab_map(s):
        j, i = ep(s) // ni, lax.rem(ep(s), ni)
        return (jnp.where(j < t_qr, 0, 1), jnp.where(j < t_vr, i, 0), 0)

    return pl.pallas_call(
        functools.partial(_inproj_kernel, ni=ni, ntiles=ntiles),
        out_shape=jax.ShapeDtypeStruct((nj, T, PROJ_TN), BF16),
        grid=(ntiles + 1,),
        in_specs=[pl.BlockSpec(memory_space=pl.ANY),
                  pl.BlockSpec((None, D, PROJ_TN), lambda s: (l, 0, mm(s) // ni)),
                  pl.BlockSpec((None, tm, 2 * LANES), tab_map)],
        out_specs=pl.BlockSpec((None, tm, PROJ_TN), lambda s: (ep(s) // ni, lax.rem(ep(s), ni), 0)),
        scratch_shapes=[pltpu.VMEM((D, PROJ_TN), BF16), pltpu.VMEM((2, tm, PROJ_TN), F32),
                        pltpu.VMEM((T, D), BF16), pltpu.SemaphoreType.DMA((ni,))],
        compiler_params=_cparams(("arbitrary",)),
        name="in_projection",
    )(h, w_in, tab)


def _attn_kernel(sinks_ref, q0_ref, q1_ref, kvc_ref, kvp_ref, o_ref, *, tq):
    W = WINDOW
    nsub = tq // W
    is_first = pl.program_id(1) == 0
    qi = lax.broadcasted_iota(I32, (W, 2 * W), 0)
    kj = lax.broadcasted_iota(I32, (W, 2 * W), 1)
    rel = qi + W - kj
    band = (rel >= 0) & (rel < WINDOW)
    band0 = band & (jnp.logical_not(is_first) | (kj >= W))
    lane = lax.broadcasted_iota(I32, (tq + W, LANES), 1)
    lo = lane < ATT_HEAD_DIM

    def split_pair(raw, parity):
        x = raw.astype(F32)
        xr = pltpu.roll(x, ATT_HEAD_DIM, 1)
        if parity == 0:
            return jnp.where(lo, x, 0.0), jnp.where(lo, 0.0, xr)
        return jnp.where(lo, xr, 0.0), jnp.where(lo, 0.0, x)

    units = []
    for hk in range(ATT_KV_HEADS):
        grp, par = hk // 2, hk % 2
        kc = slice(grp * LANES, (grp + 1) * LANES)
        vc = slice(ATT_KV_W + grp * LANES, ATT_KV_W + (grp + 1) * LANES)
        k_pair = split_pair(jnp.concatenate([kvp_ref[:, kc], kvc_ref[:, kc]], axis=0), par)
        v_pair = split_pair(jnp.concatenate([kvp_ref[:, vc], kvc_ref[:, vc]], axis=0), par)
        kt_pair = [k.T.astype(BF16) for k in k_pair]
        v_pair = [v.astype(BF16) for v in v_pair]
        qcols = [slice(hk * 4 * ATT_HEAD_DIM + jq * LANES, hk * 4 * ATT_HEAD_DIM + (jq + 1) * LANES)
                 for jq in range(2)]
        for m in range(nsub):
            units.append((hk, m, kt_pair, v_pair, qcols))

    def scores(unit):
        hk, m, kt_pair, v_pair, qcols = unit
        rows = slice(m * W, (m + 1) * W)
        keys = slice(m * W, m * W + 2 * W)
        kcat = jnp.concatenate([kt_pair[0][:, keys], kt_pair[1][:, keys]], axis=1)
        q_ref = (q0_ref, q1_ref)[hk // 2]
        c0 = (hk % 2) * 4 * ATT_HEAD_DIM
        q = jnp.concatenate([q_ref[rows, c0:c0 + LANES], q_ref[rows, c0 + LANES:c0 + 2 * LANES]], axis=0)
        return jnp.dot(q, kcat, preferred_element_type=F32)

    def finish(unit, s_all):
        hk, m, kt_pair, v_pair, qcols = unit
        rows = slice(m * W, (m + 1) * W)
        keys = slice(m * W, m * W + 2 * W)
        mask = band0 if m == 0 else band
        vcat = jnp.concatenate([v_pair[0][keys], v_pair[1][keys]], axis=0)
        p_rows = []
        for jq in range(2):
            p_cols = []
            for e in range(2):
                s = s_all[jq * W:(jq + 1) * W, e * 2 * W:(e + 1) * 2 * W]
                s = jnp.where(mask, s, -1e30)
                sink = sinks_ref[hk * 4 + jq * 2 + e]
                mx = jnp.maximum(jnp.max(s, axis=1, keepdims=True), sink)
                p = jnp.exp(s - mx)
                den = jnp.sum(p, axis=1, keepdims=True) + jnp.exp(sink - mx)
                p_cols.append((p * (1.0 / den)).astype(BF16))
            p_rows.append(jnp.concatenate(p_cols, axis=1))
        o = jnp.dot(jnp.concatenate(p_rows, axis=0), vcat, preferred_element_type=F32)
        o_ref[rows, qcols[0]] = o[0:W].astype(BF16)
        o_ref[rows, qcols[1]] = o[W:2 * W].astype(BF16)

    pending = [scores(u) for u in units[:ATTN_LOOKAHEAD]]
    for idx, unit in enumerate(units):
        if idx + ATTN_LOOKAHEAD < len(units):
            pending.append(scores(units[idx + ATTN_LOOKAHEAD]))
        finish(unit, pending.pop(0))


def swa_attention(proj, sinks, B, S):
    T = proj.shape[1]
    tq = _pick(S, (512, 384, 256, 128))
    nq = S // tq
    assert ATT_Q_W == 2 * PROJ_TN and 2 * ATT_KV_W == PROJ_TN
    kv_tile = OFF_KA // PROJ_TN
    tile = lambda j: pl.BlockSpec((None, tq, PROJ_TN), lambda b, i: (j, b * nq + i, 0))

    def prev_map(b, i):
        return (kv_tile, jnp.maximum(b * (S // WINDOW) + i * (tq // WINDOW) - 1, 0), 0)

    return pl.pallas_call(
        functools.partial(_attn_kernel, tq=tq),
        out_shape=jax.ShapeDtypeStruct((T, ATT_Q_W), BF16),
        grid=(B, nq),
        in_specs=[pl.BlockSpec(memory_space=pltpu.SMEM),
                  tile(0), tile(1), tile(kv_tile),
                  pl.BlockSpec((None, WINDOW, PROJ_TN), prev_map)],
        out_specs=pl.BlockSpec((tq, ATT_Q_W), lambda b, i: (b * nq + i, 0)),
        compiler_params=_cparams(("arbitrary", "arbitrary")),
        name="swa_attention",
    )(sinks, proj, proj, proj, proj)


def _ret_consts():
    C = RET_CHUNK
    log_g = jnp.log1p(-jnp.exp2(-5.0 - jnp.arange(RET_HEADS, dtype=F32)))
    i = jnp.arange(C, dtype=F32)
    diff = i[:, None] - i[None, :]
    dm = jnp.where(diff[None] >= 0, jnp.exp(jnp.maximum(diff, 0.0)[None] * log_g[:, None, None]), 0.0)
    qd = jnp.exp((i + 1.0)[None, :] * log_g[:, None])
    kd = jnp.exp((C - 1.0 - i)[None, :] * log_g[:, None])
    cd = jnp.exp(C * log_g)
    bc = lambda v: jnp.broadcast_to(v[:, :, None], (RET_HEADS, C, LANES))
    return dm, bc(qd), bc(kd), jnp.broadcast_to(cd[:, None, None], (RET_HEADS, 8, LANES))


def _ret_kernel(q_ref, k_ref, v_ref, g_ref, gn_ref, dm_ref, qd_ref, kd_ref, cd_ref, o_ref, *, nchunk):
    C = RET_CHUNK
    heads = range(RET_HEADS_PER_STEP)
    lanes = [slice(hh * LANES, (hh + 1) * LANES) for hh in heads]
    rows = [slice(c * C, (c + 1) * C) for c in range(nchunk)]

    def independent(c, hh):
        q, k, v = q_ref[rows[c], lanes[hh]], k_ref[rows[c], lanes[hh]], v_ref[rows[c], lanes[hh]]
        s = lax.dot_general(q, k, (((1,), (1,)), ((), ())), preferred_element_type=F32) * dm_ref[hh]
        vk = (v.astype(F32) * kd_ref[hh]).astype(BF16)
        kv = lax.dot_general(k, vk, (((0,), (0,)), ((), ())), preferred_element_type=F32)
        return s.astype(BF16), kv

    def finish(c, hh, s, state):
        q, v = q_ref[rows[c], lanes[hh]], v_ref[rows[c], lanes[hh]]
        y = jnp.dot(s, v, preferred_element_type=F32)
        y = y + jnp.dot(q, state.astype(BF16), preferred_element_type=F32) * qd_ref[hh]
        ms = jnp.mean(y * y, axis=-1, keepdims=True)
        yn = y * lax.rsqrt(ms + EPS) * gn_ref[:, lanes[hh]]
        o_ref[rows[c], lanes[hh]] = (g_ref[rows[c], lanes[hh]].astype(F32) * yn).astype(BF16)

    state = [jnp.zeros((RET_HEAD_DIM, RET_HEAD_DIM), F32) for _ in heads]
    ahead = [independent(0, hh) for hh in heads]
    for c in range(nchunk):
        cur = ahead
        if c + 1 < nchunk:
            ahead = [independent(c + 1, hh) for hh in heads]
        for hh in heads:
            s, kv = cur[hh]
            finish(c, hh, s, state[hh])
            state[hh] = cd_ref[hh, 0:1, :] * state[hh] + kv


def retention(proj, ret_norm_g, l, B, S):
    T = proj.shape[1]
    dm, qd, kd, cd = _ret_consts()
    hp = RET_HEADS_PER_STEP
    wid = hp * LANES
    per_tile = PROJ_TN // wid

    def part(off):
        return pl.BlockSpec((None, S, wid), lambda b, h: (off // PROJ_TN + h // per_tile, b, lax.rem(h, per_tile)))
    hconst = lambda rows: pl.BlockSpec((hp, rows, LANES), lambda b, h: (h, 0, 0))
    return pl.pallas_call(
        functools.partial(_ret_kernel, nchunk=S // RET_CHUNK),
        out_shape=jax.ShapeDtypeStruct((T, RET_W), BF16),
        grid=(B, RET_HEADS // hp),
        in_specs=[part(OFF_QR), part(OFF_KR), part(OFF_VR), part(OFF_GR),
                  pl.BlockSpec((None, 1, wid), lambda b, h: (l, 0, h)),
                  pl.BlockSpec((hp, RET_CHUNK, RET_CHUNK), lambda b, h: (h, 0, 0)),
                  hconst(RET_CHUNK), hconst(RET_CHUNK), hconst(8)],
        out_specs=pl.BlockSpec((S, wid), lambda b, h: (b, h)),
        compiler_params=_cparams(("arbitrary", "arbitrary")),
        name="retention",
    )(proj, proj, proj, proj, ret_norm_g.reshape(-1, 1, RET_W), dm, qd, kd, cd)


def _load_cast(w_hbm, dst, stage, sem):
    rows = stage.shape[1]
    n = w_hbm.shape[0] // rows

    def copy(i, slot):
        r0 = pl.multiple_of(i * rows, rows)
        return pltpu.make_async_copy(w_hbm.at[pl.ds(r0, rows), :], stage.at[slot], sem.at[slot])

    copy(0, 0).start()

    def body(i, carry):
        slot = lax.rem(i, 2)

        @pl.when(i + 1 < n)
        def _():
            copy(i + 1, 1 - slot).start()
        copy(i, slot).wait()
        dst[pl.ds(pl.multiple_of(i * rows, rows), rows), :] = stage[slot].astype(BF16)
        return carry
    lax.fori_loop(0, n, body, 0)


def _mixer_out_kernel(att_ref, ret_ref, ga0, ga1, ga2, ga3, gb0, gb1, gb2, gb3, x_ref,
                      wa_hbm, wr_hbm, wo_hbm, gate1_ref, g2_ref, sh2_ref, sc2_ref,
                      wrt_ref, brt_ref, tril_ref,
                      xn_ref, h2_ref, rti_ref, rtw_ref, cnt_ref,
                      wa, wr, wo, wrt2, mrg, carry, hs, stage, sem, *, tm, l):
    i = pl.program_id(0)
    nt = pl.num_programs(0) - 1
    D = D_MODEL

    @pl.when(i == 0)
    def _():
        _load_cast(wa_hbm.at[l], wa, stage, sem)
        _load_cast(wr_hbm.at[l], wr, stage, sem)
        _load_cast(wo_hbm.at[l], wo, stage, sem)
        w = wrt_ref[...]
        hi = w.astype(BF16)
        wrt2[:, 0:LANES] = hi
        wrt2[:, LANES:2 * LANES] = (w - hi.astype(F32)).astype(BF16)
        carry[...] = jnp.zeros_like(carry)
        hs[...] = jnp.zeros_like(hs)

    def main_stage():
        a = att_ref[...]
        r = ret_ref[...]
        ga = (ga0, ga1, ga2, ga3)
        gb = (gb0, gb1, gb2, gb3)
        for n in range(D // PROJ_TN):
            cols = slice(n * PROJ_TN, (n + 1) * PROJ_TN)
            ya = jnp.dot(a, wa[:, cols], preferred_element_type=F32)
            yr = jnp.dot(r, wr[:, cols], preferred_element_type=F32)
            mrg[:, cols] = (ga[n][...].astype(F32) * ya + gb[n][...].astype(F32) * yr).astype(BF16)
        o = jnp.dot(mrg[...], wo[...], preferred_element_type=F32)
        xn = x_ref[...] + gate1_ref[...] * o
        xn_ref[...] = xn
        h2 = _norm_mod(xn, g2_ref[...], sh2_ref[...], sc2_ref[...])
        h2b = h2.astype(BF16)
        h2_ref[...] = h2b
        hs[0:tm, :] = h2b
        hs[tm:2 * tm, :] = (h2 - h2b.astype(F32)).astype(BF16)

    def router_select():
        r = jnp.dot(hs[...], wrt2[...], preferred_element_type=F32)
        logits = ((r[0:tm, 0:LANES] + r[tm:2 * tm, 0:LANES])
                  + (r[0:tm, LANES:2 * LANES] + r[tm:2 * tm, LANES:2 * LANES])) + brt_ref[...]
        lane = lax.broadcasted_iota(I32, (tm, LANES), 1)
        valid = lane < N_EXPERTS
        mx = jnp.max(logits, axis=-1, keepdims=True)
        p = jnp.where(valid, jnp.exp(logits - mx), 0.0)
        pos_in_grp = lane % EXPERTS_PER_GROUP
        grp_of = lane // EXPERTS_PER_GROUP

        def member(k):
            wrapped = pos_in_grp + k >= EXPERTS_PER_GROUP
            return jnp.where(wrapped, pltpu.roll(p, EXPERTS_PER_GROUP - k, 1), pltpu.roll(p, LANES - k, 1)), wrapped

        (b1, w1), (b2, w2), (b3, w3) = member(1), member(2), member(3)
        m_ab, n_ab = jnp.maximum(p, b1), jnp.minimum(p, b1)
        m_cd, n_cd = jnp.maximum(b2, b3), jnp.minimum(b2, b3)
        gscore = jnp.maximum(m_ab, m_cd) + jnp.maximum(jnp.minimum(m_ab, m_cd), jnp.maximum(n_ab, n_cd))
        gscore = jnp.where(valid, gscore, -1.0)
        gmax = jnp.max(gscore, axis=-1, keepdims=True)
        gsel = jnp.min(jnp.where(gscore == gmax, grp_of, N_GROUPS), axis=-1, keepdims=True)
        in_sel = grp_of == gsel
        beats = lambda b, w: ((b > p) | ((b == p) & w)).astype(I32)
        rank_in_grp = beats(b1, w1) + beats(b2, w2) + beats(b3, w3)
        sel0 = in_sel & (rank_in_grp == 0)
        sel1 = in_sel & (rank_in_grp == 1)
        lsum = lambda m, v: jnp.sum(jnp.where(m, v, 0.0), axis=-1, keepdims=True)
        v0, v1 = lsum(sel0, p), lsum(sel1, p)
        lanef = lane.astype(F32)
        e0, e1 = lsum(sel0, lanef), lsum(sel1, lanef)
        inv = 1.0 / (v0 + v1)
        return lane, sel0, sel1, e0, e1, v0 * inv, v1 * inv

    def router_finish(live, lane, sel0, sel1, e0, e1, w0, w1):
        lsum = lambda m, v: jnp.sum(jnp.where(m, v, 0.0), axis=-1, keepdims=True)
        onehot = jnp.where(sel0 | sel1, 1.0, 0.0)
        prefix = jnp.dot(tril_ref[...], onehot.astype(BF16), preferred_element_type=F32) + carry[0:1, :]
        r0, r1 = lsum(sel0, prefix), lsum(sel1, prefix)
        carry[...] = carry[...] + live * jnp.sum(onehot, axis=0, keepdims=True)
        cnt_ref[...] = carry[...].astype(I32)
        sel4 = lambda a0, a1, a2, a3: jnp.where(lane == 0, a0, jnp.where(lane == 1, a1, jnp.where(lane == 2, a2, a3)))
        rti_ref[...] = sel4(e0, e1, r0, r1).astype(I32)
        rtw_ref[...] = jnp.where(lane == 0, w0, jnp.where(lane == 1, w1, 0.0))

    @pl.when(i < nt)
    def _():
        picked = router_select()
        main_stage()
        router_finish(jnp.where(i > 0, 1.0, 0.0), *picked)

    @pl.when(i == nt)
    def _():
        router_finish(1.0, *router_select())


def mixer_out(att, ret, proj, x2, w_attn_out, w_ret_out, w_o, modr, norm2_g, w_router, b_router, l, S):
    T, D = x2.shape
    tm = 256
    assert S % tm == 0
    nga, ngb = OFF_GA // PROJ_TN, OFF_GB // PROJ_TN
    nt = T // tm
    cur = lambda i: jnp.minimum(i, nt - 1)
    prev = lambda i: jnp.maximum(i - 1, 0)
    gate_spec = lambda blk: pl.BlockSpec((None, tm, PROJ_TN), lambda i: (blk, cur(i), 0))
    row = lambda: pl.BlockSpec((tm, D), lambda i: (cur(i), 0))
    mod_spec = lambda k: pl.BlockSpec((None, None, None, 1, D), lambda i: (l, k, (cur(i) * tm) // S, 0, 0))
    wrt = jnp.pad(w_router, ((0, 0), (0, LANES - N_EXPERTS)))
    brt = jnp.pad(b_router.astype(F32), (0, LANES - N_EXPERTS), constant_values=-1e30).reshape(1, LANES)
    tril = jnp.asarray(np.tril(np.ones((tm, tm), np.float32), -1), BF16)
    any_spec = pl.BlockSpec(memory_space=pl.ANY)
    outs = pl.pallas_call(
        functools.partial(_mixer_out_kernel, tm=tm, l=l),
        out_shape=(jax.ShapeDtypeStruct((T, D), F32),
                   jax.ShapeDtypeStruct((T, D), BF16),
                   jax.ShapeDtypeStruct((T, LANES), I32),
                   jax.ShapeDtypeStruct((T, LANES), F32),
                   jax.ShapeDtypeStruct((8, LANES), I32)),
        grid=(nt + 1,),
        in_specs=[pl.BlockSpec((tm, ATT_Q_W), lambda i: (cur(i), 0)),
                  pl.BlockSpec((tm, RET_W), lambda i: (cur(i), 0)),
                  *[gate_spec(nga + n) for n in range(4)],
                  *[gate_spec(ngb + n) for n in range(4)],
                  row(), any_spec, any_spec, any_spec,
                  mod_spec(2),
                  pl.BlockSpec((None, 1, D), lambda i: (l, 0, 0)),
                  mod_spec(3), mod_spec(4),
                  pl.BlockSpec((D, LANES), lambda i: (0, 0)),
                  pl.BlockSpec((1, LANES), lambda i: (0, 0)),
                  pl.BlockSpec((tm, tm), lambda i: (0, 0))],
        out_specs=(row(),
                   row(),
                   pl.BlockSpec((tm, LANES), lambda i: (prev(i), 0)),
                   pl.BlockSpec((tm, LANES), lambda i: (prev(i), 0)),
                   pl.BlockSpec((8, LANES), lambda i: (0, 0))),
        scratch_shapes=[pltpu.VMEM((ATT_Q_W, D), BF16), pltpu.VMEM((RET_W, D), BF16),
                        pltpu.VMEM((D, D), BF16),
                        pltpu.VMEM((D, 2 * LANES), BF16),
                        pltpu.VMEM((tm, D), BF16), pltpu.VMEM((8, LANES), F32),
                        pltpu.VMEM((2 * tm, D), BF16),
                        pltpu.VMEM((2, 256, D), F32), pltpu.SemaphoreType.DMA((2,))],
        compiler_params=_cparams(("arbitrary",)),
        name="mixer_out",
    )(att, ret, *([proj] * 8), x2, w_attn_out, w_ret_out, w_o,
      modr, norm2_g.reshape(-1, 1, D), modr, modr, wrt, brt, tril)
    return outs


def moe_plan(counts, n_work):
    sub_per = MOE_ROWS // MOE_SUB
    seg = ((counts + MOE_SUB - 1) // MOE_SUB) * MOE_SUB
    off = jnp.cumsum(seg) - seg
    off17 = jnp.concatenate([off, off[-1:] + seg[-1:]]).astype(I32)
    nb = (counts + MOE_ROWS - 1) // MOE_ROWS
    cum = jnp.cumsum(nb)
    total = cum[-1]
    w = jnp.arange(n_work, dtype=I32)
    wc = jnp.minimum(w, total - 1)
    e_w = jnp.sum((cum[None, :] <= wc[:, None]).astype(I32), axis=1)
    blk = wc - (cum[e_w] - nb[e_w])
    start = off[e_w] + blk * MOE_ROWS
    nsub = jnp.clip(seg[e_w] // MOE_SUB - blk * sub_per, 0, sub_per)
    nsub = jnp.where(w < total, nsub, 0)
    return off17, e_w, start.astype(I32), nsub.astype(I32)


def _row_dma_loops(n_rows, make_copy):
    def run(op):
        def body(g, carry):
            for u in range(ROW_GROUP):
                for k in range(2):
                    getattr(make_copy(g, u, k), op)()
            return carry
        lax.fori_loop(0, n_rows // ROW_GROUP, body, 0)
    run("start")
    run("wait")


def _dispatch_kernel(pos_ref, off_ref, h_ref, xs_ref, rows_ref, zero_ref, sem, *, td, T):
    i = pl.program_id(0)

    @pl.when(i == 0)
    def _():
        zero_ref[...] = jnp.zeros_like(zero_ref)
        used = off_ref[N_EXPERTS]

        def zero_copy(row):
            row = pl.multiple_of(row, MOE_SUB)
            return pltpu.make_async_copy(zero_ref, xs_ref.at[pl.ds(row, MOE_SUB), :], sem)

        def fill(op):
            def seg_tail(e, carry):
                @pl.when(off_ref[e + 1] > off_ref[e])
                def _():
                    getattr(zero_copy(off_ref[e + 1] - MOE_SUB), op)()
                return carry

            def buf_tail(n, carry):
                getattr(zero_copy(used + n * MOE_SUB), op)()
                return carry
            lax.fori_loop(0, N_EXPERTS, seg_tail, 0)
            lax.fori_loop(0, (xs_ref.shape[0] - used) // MOE_SUB, buf_tail, 0)
        fill("start")
        fill("wait")

    rows_ref[...] = h_ref[...].astype(F32).reshape(rows_ref.shape)

    def copy(g, u, k):
        p = pos_ref[k * T + i * td + g * ROW_GROUP + u]
        return pltpu.make_async_copy(rows_ref.at[g, pl.ds(u, 1), :], xs_ref.at[pl.ds(p, 1), :], sem)
    _row_dma_loops(td, copy)


def dispatch(h2, pos_flat, off17, n_rows):
    T, D = h2.shape
    td = MOE_SUB
    return pl.pallas_call(
        functools.partial(_dispatch_kernel, td=td, T=T),
        out_shape=jax.ShapeDtypeStruct((n_rows, D), F32),
        grid_spec=pltpu.PrefetchScalarGridSpec(
            num_scalar_prefetch=2, grid=(T // td,),
            in_specs=[pl.BlockSpec((td, D), lambda i, *_: (i, 0))],
            out_specs=pl.BlockSpec(memory_space=pl.ANY),
            scratch_shapes=[pltpu.VMEM((td // ROW_GROUP, ROW_GROUP, D), F32), pltpu.VMEM((MOE_SUB, D), F32),
                            pltpu.SemaphoreType.DMA]),
        compiler_params=_cparams(("arbitrary",)),
        name="moe_dispatch",
    )(pos_flat, off17, h2)


def _moe_kernel(we_ref, ws_ref, wn_ref, xs_ref, wg_ref, wu_ref, wd_ref, ys_ref,
                xb, acc, wgb, wub, wdb, stage, ostage, sem_in, sem_out, *, nfc, n_work):
    w = pl.program_id(0)
    c = pl.program_id(1)
    n = wn_ref[w]
    s0 = ws_ref[w]

    def in_copy(start_row, i, slot):
        g0 = pl.multiple_of(start_row + i * MOE_SUB, MOE_SUB)
        return pltpu.make_async_copy(xs_ref.at[pl.ds(g0, MOE_SUB), :], stage.at[slot], sem_in.at[slot])

    def out_copy(i, slot):
        g0 = pl.multiple_of(s0 + i * MOE_SUB, MOE_SUB)
        return pltpu.make_async_copy(ostage.at[slot], ys_ref.at[pl.ds(g0, MOE_SUB), :], sem_out.at[slot])

    def request_first_two(start_row, count):
        @pl.when(count > 0)
        def _():
            in_copy(start_row, 0, 0).start()

        @pl.when(count > 1)
        def _():
            in_copy(start_row, 1, 1).start()

    @pl.when((w == 0) & (c == 0))
    def _():
        request_first_two(s0, n)

    def for_subs(fn):
        def body(p, carry):
            fn(p * 2, 2)
            return carry
        lax.fori_loop(0, n // 2, body, 0)

        @pl.when(lax.rem(n, 2) == 1)
        def _():
            fn(n - 1, 1)

    def rows_of(i):
        return pl.ds(pl.multiple_of(i * MOE_SUB, MOE_SUB), MOE_SUB)

    def ffn(i0, count):
        xs_ = [xb[rows_of(i0 + a), :] for a in range(count)]
        gu = [(jnp.dot(x, wgb[...], preferred_element_type=F32), jnp.dot(x, wub[...], preferred_element_type=F32))
              for x in xs_]
        acts = [(g * _sigmoid(g) * u).astype(BF16) for g, u in gu]
        return [jnp.dot(a, wdb[...], preferred_element_type=F32) for a in acts]

    @pl.when(n > 0)
    def _():
        wgb[...] = wg_ref[...].astype(BF16)
        wub[...] = wu_ref[...].astype(BF16)
        wdb[...] = wd_ref[...].astype(BF16)

        @pl.when(c == 0)
        def _():
            def first(i0, count):
                for a in range(count):
                    in_copy(s0, i0 + a, a).wait()
                    xb[rows_of(i0 + a), :] = stage[a].astype(BF16)
                for a in range(count):
                    @pl.when(i0 + a + 2 < n)
                    def _():
                        in_copy(s0, i0 + a + 2, a).start()
                for a, y in enumerate(ffn(i0, count)):
                    acc[rows_of(i0 + a), :] = y
            for_subs(first)

        @pl.when((c > 0) & (c < nfc - 1))
        def _():
            def middle(i0, count):
                for a, y in enumerate(ffn(i0, count)):
                    acc[rows_of(i0 + a), :] += y
            for_subs(middle)

        @pl.when(c == nfc - 1)
        def _():
            def last(i0, count):
                for a in range(count):
                    @pl.when(i0 + a >= 2)
                    def _():
                        out_copy(i0 + a - 2, a).wait()
                for a, y in enumerate(ffn(i0, count)):
                    ostage[a] = acc[rows_of(i0 + a), :] + y
                    out_copy(i0 + a, a).start()
            for_subs(last)

            @pl.when(n >= 2)
            def _():
                out_copy(n - 2, lax.rem(n, 2)).wait()
            out_copy(n - 1, lax.rem(n - 1, 2)).wait()

            @pl.when(w + 1 < n_work)
            def _():
                nxt = jnp.minimum(w + 1, n_work - 1)
                request_first_two(ws_ref[nxt], wn_ref[nxt])


def moe_experts(xs, w_gate, w_up, w_down, l, e_w, start_w, nsub_w):
    n_rows, D = xs.shape
    nfc = D_FF // MOE_FC
    assert nfc >= 2
    n_work = e_w.shape[0]

    def chunk(c, wn, w):
        return jnp.where(wn[w] > 0, c, nfc - 1)

    return pl.pallas_call(
        functools.partial(_moe_kernel, nfc=nfc, n_work=n_work),
        out_shape=jax.ShapeDtypeStruct((n_rows, D), F32),
        grid_spec=pltpu.PrefetchScalarGridSpec(
            num_scalar_prefetch=3, grid=(n_work, nfc),
            in_specs=[pl.BlockSpec(memory_space=pl.ANY),
                      pl.BlockSpec((None, None, D, MOE_FC), lambda w, c, we, ws, wn: (l, we[w], 0, chunk(c, wn, w))),
                      pl.BlockSpec((None, None, D, MOE_FC), lambda w, c, we, ws, wn: (l, we[w], 0, chunk(c, wn, w))),
                      pl.BlockSpec((None, None, MOE_FC, D), lambda w, c, we, ws, wn: (l, we[w], chunk(c, wn, w), 0))],
            out_specs=pl.BlockSpec(memory_space=pl.ANY),
            scratch_shapes=[pltpu.VMEM((MOE_ROWS, D), BF16), pltpu.VMEM((MOE_ROWS, D), F32),
                            pltpu.VMEM((D, MOE_FC), BF16), pltpu.VMEM((D, MOE_FC), BF16),
                            pltpu.VMEM((MOE_FC, D), BF16),
                            pltpu.VMEM((2, MOE_SUB, D), F32), pltpu.VMEM((2, MOE_SUB, D), F32),
                            pltpu.SemaphoreType.DMA((2,)), pltpu.SemaphoreType.DMA((2,))]),
        input_output_aliases={3: 0},
        compiler_params=_cparams(("arbitrary", "arbitrary")),
        name="moe_experts",
    )(e_w, start_w, nsub_w, xs, w_gate, w_up, w_down)


def _combine_kernel(pos_ref, ys_ref, x_ref, rtw_ref, gate2_ref, g_ref, *rest, tc, T, final):
    if final:
        out_ref, ybuf, sem = rest
    else:
        sh_ref, sc_ref, xo_ref, h_ref, ybuf, sem = rest
    i = pl.program_id(0)
    slot = lax.rem(i, 2)
    ngroups = tc // COMBINE_GROUP

    def copy(tile, g, u, k, sl):
        p = pos_ref[k * T + tile * tc + g * ROW_GROUP + u]
        return pltpu.make_async_copy(ys_ref.at[pl.ds(p, 1), :], ybuf.at[sl, k, g, pl.ds(u, 1), :], sem.at[sl])

    def for_rows(tile, sl, r0, op):
        g0 = r0 // ROW_GROUP
        for gg in range(COMBINE_GROUP // ROW_GROUP):
            for u in range(ROW_GROUP):
                for k in range(2):
                    getattr(copy(tile, g0 + gg, u, k, sl), op)()

    def groups(fn):
        def body(g, carry):
            fn(pl.multiple_of(g * COMBINE_GROUP, COMBINE_GROUP))
            return carry
        lax.fori_loop(0, ngroups, body, 0)

    def compute(r0):
        rows = pl.ds(r0, COMBINE_GROUP)
        grp = pl.ds(r0 // ROW_GROUP, COMBINE_GROUP // ROW_GROUP)
        y = [ybuf[slot, k, grp, :, :].reshape(COMBINE_GROUP, D_MODEL) for k in range(2)]
        moe = rtw_ref[rows, 0:1] * y[0] + rtw_ref[rows, 1:2] * y[1]
        xo = x_ref[rows, :] + gate2_ref[...] * moe
        if final:
            ms = jnp.mean(xo * xo, axis=-1, keepdims=True)
            out_ref[rows, :] = xo * lax.rsqrt(ms + EPS) * g_ref[...]
        else:
            xo_ref[rows, :] = xo
            h_ref[rows, :] = _norm_mod(xo, g_ref[...], sh_ref[...], sc_ref[...]).astype(BF16)

    @pl.when(i == 0)
    def _():
        groups(lambda r0: for_rows(0, 0, r0, "start"))
    groups(lambda r0: for_rows(i, slot, r0, "wait"))

    @pl.when(i + 1 < pl.num_programs(0))
    def _():
        def both(r0):
            compute(r0)
            for_rows(i + 1, 1 - slot, r0, "start")
        groups(both)

    @pl.when(i + 1 >= pl.num_programs(0))
    def _():
        groups(compute)


def combine(ys, xn, rtw, pos_flat, modr, l, S, g_next, final):
    T, D = xn.shape
    tc = 256
    assert S % tc == 0
    row = lambda: pl.BlockSpec((tc, D), lambda i, *_: (i, 0))
    in_specs = [pl.BlockSpec(memory_space=pl.ANY), row(),
                pl.BlockSpec((tc, LANES), lambda i, *_: (i, 0)),
                _mod_spec(l, 5, S, tc)]
    args = [ys, xn, rtw, modr]
    if final:
        in_specs.append(pl.BlockSpec((1, D), lambda i, *_: (0, 0)))
        args.append(g_next.reshape(1, D))
        out_shape = jax.ShapeDtypeStruct((T, D), F32)
        out_specs = row()
    else:
        in_specs += [pl.BlockSpec((None, 1, D), lambda i, *_: (l + 1, 0, 0)),
                     _mod_spec(l + 1, 0, S, tc), _mod_spec(l + 1, 1, S, tc)]
        args += [g_next.reshape(-1, 1, D), modr, modr]
        out_shape = (jax.ShapeDtypeStruct((T, D), F32), jax.ShapeDtypeStruct((T, D), BF16))
        out_specs = (row(), row())
    return pl.pallas_call(
        functools.partial(_combine_kernel, tc=tc, T=T, final=final),
        out_shape=out_shape,
        grid_spec=pltpu.PrefetchScalarGridSpec(
            num_scalar_prefetch=1, grid=(T // tc,),
            in_specs=in_specs, out_specs=out_specs,
            scratch_shapes=[pltpu.VMEM((2, 2, tc // ROW_GROUP, ROW_GROUP, D), F32), pltpu.SemaphoreType.DMA((2,))]),
        compiler_params=_cparams(("arbitrary",)),
        name="moe_combine_final" if final else "moe_combine",
    )(pos_flat, *args)


def kernel(x, c, positions, w_ada, b_ada, norm1_g, norm2_g, w_in, attn_sinks, w_attn_out, ret_norm_g,
           w_ret_out, w_o, w_router, b_router, w_gate, w_up, w_down, final_g):
    B, S, D = x.shape
    L = w_ada.shape[0]
    T = B * S
    assert D == D_MODEL and w_in.shape[-1] == IN_W and S % RET_CHUNK == 0

    tab = rope_tables(positions)
    mod = adaln_mod(c, w_ada, b_ada)
    modr = mod[:, :B].reshape(L, B, 6, D).transpose(0, 2, 1, 3).reshape(L, 6, B, 1, D)

    x2 = x.reshape(T, D)
    h = norm_modulate(x2, norm1_g.reshape(L, 1, D), modr, 0, S)
    n_work = N_EXPERTS + (2 * T + MOE_ROWS - 1) // MOE_ROWS
    n_rows = 2 * T + N_EXPERTS * MOE_SUB
    out = None
    for l in range(L):
        proj = in_projection(h, w_in, l, tab)
        att = swa_attention(proj, attn_sinks[l], B, S)
        ret = retention(proj, ret_norm_g, l, B, S)
        xn, h2, rti, rtw, cnt = mixer_out(att, ret, proj, x2, w_attn_out, w_ret_out, w_o, modr,
                                          norm2_g, w_router, b_router, l, S)
        off17, e_w, start_w, nsub_w = moe_plan(cnt[0, :N_EXPERTS], n_work)
        seg_off = jnp.sum(jnp.where(rti[:, 0:2, None] == jnp.arange(N_EXPERTS, dtype=I32),
                                    off17[:N_EXPERTS], 0), axis=-1)
        pos_flat = (seg_off + rti[:, 2:4]).T.reshape(2 * T)
        xs = dispatch(h2, pos_flat, off17, n_rows)
        ys = moe_experts(xs, w_gate, w_up, w_down, l, e_w, start_w, nsub_w)
        if l + 1 < L:
            x2, h = combine(ys, xn, rtw, pos_flat, modr, l, S, norm1_g, final=False)
        else:
            out = combine(ys, xn, rtw, pos_flat, modr, l, S, final_g, final=True)
    return out.reshape(B, S, D)
```
